```python
import jax, jax.numpy as jnp
from jax import lax
import numpy as np

D_MODEL = 1024
BATCH = 4
SEQ = 4096
DEPTH = 1

CHUNK = 64
Q_BLOCK = 128
HEAD_DIM = 64
ROPE_THETA = 10000.0
EPS = 1e-6

N_DIFF_HEADS = 4
DIFF_V_DIM = 2 * HEAD_DIM
DIFF_QK = N_DIFF_HEADS * 2 * HEAD_DIM
DIFF_V = N_DIFF_HEADS * DIFF_V_DIM
N_FOX_HEADS = 8
FOX_W = N_FOX_HEADS * HEAD_DIM
MIX_W = DIFF_V + FOX_W
PROJ_WIDTHS = (DIFF_QK, DIFF_QK, DIFF_V, FOX_W, FOX_W, FOX_W, N_FOX_HEADS)
D_IN = sum(PROJ_WIDTHS)

N_GROUPS = 4
EXPERTS_PER_GROUP = 4
N_EXPERTS = N_GROUPS * EXPERTS_PER_GROUP
TOP_K_IN_GROUP = 2
D_EXPERT = D_MODEL // 2

kernel_name = "hybrid_diffattn_fox_hiermoe_chunk_causal"


def _rmsnorm(x, g):
    xf = x.astype(jnp.float32)
    y = xf * lax.rsqrt(jnp.mean(xf * xf, axis=-1, keepdims=True) + EPS)
    return (y * g.astype(jnp.float32)).astype(x.dtype)


def _rotate_half(x):
    x1, x2 = jnp.split(x, 2, axis=-1)
    return jnp.concatenate([-x2, x1], axis=-1)


def _rope(x, cos, sin):
    return (x * cos + _rotate_half(x) * sin).astype(x.dtype)


def _diff_attention(q1, q2, k1, k2, v, lam):
    S = q1.shape[2]
    scale = HEAD_DIM ** -0.5
    outs = []
    for blk in range(S // Q_BLOCK):
        q0, q1_end = blk * Q_BLOCK, (blk + 1) * Q_BLOCK
        t = np.arange(q0, q1_end)
        s = np.arange(q1_end)
        mask = (s[None, :] // CHUNK) <= (t[:, None] // CHUNK)

        def probs(q, k):
            logits = jnp.einsum('bhqd,bhkd->bhqk', q[:, :, q0:q1_end], k[:, :, :q1_end]).astype(jnp.float32) * scale
            return jax.nn.softmax(jnp.where(mask, logits, -jnp.inf), axis=-1)

        w = probs(q1, k1) - lam * probs(q2, k2)
        outs.append(jnp.einsum('bhqk,bhkd->bhqd', w.astype(v.dtype), v[:, :, :q1_end]))
    return jnp.concatenate(outs, axis=2)


def _forgetting_attention(q, k, v, cum_logf):
    S = q.shape[2]
    scale = HEAD_DIM ** -0.5
    outs = []
    for blk in range(S // Q_BLOCK):
        q0, q_end = blk * Q_BLOCK, (blk + 1) * Q_BLOCK
        t = np.arange(q0, q_end)
        s = np.arange(q_end)
        mask = s[None, :] <= t[:, None]
        logits = jnp.einsum('bhqd,bhkd->bhqk', q[:, :, q0:q_end], k[:, :, :q_end]).astype(jnp.float32) * scale
        logits = logits + cum_logf[:, :, q0:q_end, None] - cum_logf[:, :, None, :q_end]
        p = jax.nn.softmax(jnp.where(mask, logits, -jnp.inf), axis=-1)
        outs.append(jnp.einsum('bhqk,bhkd->bhqd', p.astype(v.dtype), v[:, :, :q_end]))
    return jnp.concatenate(outs, axis=2)


def _hier_moe(h, wg_r, bg_r, we_r, be_r, w_gate, w_up, w_down):
    g_logits = (h @ wg_r).astype(jnp.float32) + bg_r.astype(jnp.float32)
    g_probs = jax.nn.softmax(g_logits, axis=-1)
    g_w, g_idx = lax.top_k(g_probs, 1)
    e_logits = (h @ we_r).astype(jnp.float32) + be_r.astype(jnp.float32)
    e_logits = e_logits.reshape(-1, N_GROUPS, EXPERTS_PER_GROUP)
    sel = jnp.take_along_axis(e_logits, g_idx[:, :, None], axis=1)[:, 0]
    top_v, top_i = lax.top_k(sel, TOP_K_IN_GROUP)
    weights = g_w * jax.nn.softmax(top_v, axis=-1)
    expert_ids = g_idx * EXPERTS_PER_GROUP + top_i
    combine = jnp.sum(jax.nn.one_hot(expert_ids, N_EXPERTS, dtype=jnp.float32) * weights[:, :, None], axis=1)
    y = jnp.zeros_like(h)
    for e in range(N_EXPERTS):
        a = jax.nn.silu(h @ w_gate[e]) * (h @ w_up[e])
        y = y + combine[:, e:e + 1].astype(h.dtype) * (a @ w_down[e])
    return y


def setup_inputs(seed: int = 0) -> dict:
    key = jax.random.key(seed)
    ks = jax.random.split(key, 20)
    f32 = jnp.float32
    nrm = lambda k, shape, s: jax.random.normal(k, shape, f32) * s
    gain = lambda k, shape: 1.0 + 0.02 * jax.random.normal(k, shape, f32)
    return {
        "x": jax.random.normal(ks[0], (BATCH, SEQ, D_MODEL), f32),
        "norm_attn_g": gain(ks[1], (DEPTH, D_MODEL)),
        "w_in": nrm(ks[2], (DEPTH, D_MODEL, D_IN), D_MODEL ** -0.5),
        "b_forget": jax.random.uniform(ks[3], (DEPTH, N_FOX_HEADS), f32, 1.0, 4.0),
        "lambda_q1": nrm(ks[4], (DEPTH, HEAD_DIM), 0.1),
        "lambda_k1": nrm(ks[5], (DEPTH, HEAD_DIM), 0.1),
        "lambda_q2": nrm(ks[6], (DEPTH, HEAD_DIM), 0.1),
        "lambda_k2": nrm(ks[7], (DEPTH, HEAD_DIM), 0.1),
        "diff_norm_g": gain(ks[8], (DEPTH, DIFF_V_DIM)),
        "w_out": nrm(ks[9], (DEPTH, MIX_W, D_MODEL), MIX_W ** -0.5),
        "norm_ffn_g": gain(ks[10], (DEPTH, D_MODEL)),
        "router_group_w": nrm(ks[11], (DEPTH, D_MODEL, N_GROUPS), D_MODEL ** -0.5),
        "router_group_b": nrm(ks[12], (DEPTH, N_GROUPS), 0.01),
        "router_expert_w": nrm(ks[13], (DEPTH, D_MODEL, N_EXPERTS), D_MODEL ** -0.5),
        "router_expert_b": nrm(ks[14], (DEPTH, N_EXPERTS), 0.01),
        "w_gate": nrm(ks[15], (DEPTH, N_EXPERTS, D_MODEL, D_EXPERT), D_MODEL ** -0.5),
        "w_up": nrm(ks[16], (DEPTH, N_EXPERTS, D_MODEL, D_EXPERT), D_MODEL ** -0.5),
        "w_down": nrm(ks[17], (DEPTH, N_EXPERTS, D_EXPERT, D_MODEL), D_EXPERT ** -0.5),
        "norm_final_g": gain(ks[18], (D_MODEL,)),
    }


def reference(x, norm_attn_g, w_in, b_forget, lambda_q1, lambda_k1, lambda_q2, lambda_k2,
              diff_norm_g, w_out, norm_ffn_g, router_group_w, router_group_b,
              router_expert_w, router_expert_b, w_gate, w_up, w_down, norm_final_g):
    B, S, D = x.shape
    pos = jnp.arange(S, dtype=jnp.float32)
    inv_freq = 1.0 / (ROPE_THETA ** (jnp.arange(0, HEAD_DIM, 2, dtype=jnp.float32) / HEAD_DIM))
    ang = pos[:, None] * inv_freq[None, :]
    ang = jnp.concatenate([ang, ang], axis=-1)
    cos, sin = jnp.cos(ang), jnp.sin(ang)
    splits = [int(i) for i in np.cumsum(PROJ_WIDTHS)[:-1]]

    for l in range(DEPTH):
        lam_init = 0.8 - 0.6 * float(np.exp(-0.3 * l))
        h = _rmsnorm(x, norm_attn_g[l])
        proj = h @ w_in[l]
        q_d, k_d, v_d, q_f, k_f, v_f, f_logit = jnp.split(proj, splits, axis=-1)

        q_d = _rope(q_d.reshape(B, S, N_DIFF_HEADS, 2, HEAD_DIM).transpose(3, 0, 2, 1, 4), cos, sin)
        k_d = _rope(k_d.reshape(B, S, N_DIFF_HEADS, 2, HEAD_DIM).transpose(3, 0, 2, 1, 4), cos, sin)
        v_d = v_d.reshape(B, S, N_DIFF_HEADS, DIFF_V_DIM).transpose(0, 2, 1, 3)
        lam = (jnp.exp(jnp.sum(lambda_q1[l].astype(jnp.float32) * lambda_k1[l].astype(jnp.float32)))
               - jnp.exp(jnp.sum(lambda_q2[l].astype(jnp.float32) * lambda_k2[l].astype(jnp.float32)))
               + lam_init)
        o_d = _diff_attention(q_d[0], q_d[1], k_d[0], k_d[1], v_d, lam)
        o_d = _rmsnorm(o_d, diff_norm_g[l]) * (1.0 - lam_init)
        o_d = o_d.transpose(0, 2, 1, 3).reshape(B, S, DIFF_V).astype(x.dtype)

        to_heads = lambda t: t.reshape(B, S, N_FOX_HEADS, HEAD_DIM).transpose(0, 2, 1, 3)
        log_f = jax.nn.log_sigmoid(f_logit.astype(jnp.float32) + b_forget[l].astype(jnp.float32))
        cum_logf = jnp.cumsum(log_f.transpose(0, 2, 1), axis=-1)
        o_f = _forgetting_attention(to_heads(q_f), to_heads(k_f), to_heads(v_f), cum_logf)
        o_f = o_f.transpose(0, 2, 1, 3).reshape(B, S, FOX_W).astype(x.dtype)

        x = x + jnp.concatenate([o_d, o_f], axis=-1) @ w_out[l]

        hm = _rmsnorm(x, norm_ffn_g[l]).reshape(B * S, D)
        y = _hier_moe(hm, router_group_w[l], router_group_b[l], router_expert_w[l], router_expert_b[l],
                      w_gate[l], w_up[l], w_down[l])
        x = x + y.reshape(B, S, D)

    return _rmsnorm(x, norm_final_g)
```

```python
import functools

import numpy as np
import jax
import jax.numpy as jnp
from jax import lax
from jax.experimental import pallas as pl
from jax.experimental.pallas import tpu as pltpu

CHUNK = 64
HEAD_DIM = 64
ROPE_THETA = 10000.0
EPS = 1e-6
N_DIFF_HEADS = 4
N_FOX_HEADS = 8
N_GROUPS = 4
EXPERTS_PER_GROUP = 4
N_EXPERTS = N_GROUPS * EXPERTS_PER_GROUP

LANES = 128
NEG_BIG = -1e30
VMEM_LIMIT = 56 * 1024 * 1024

BF16 = jnp.bfloat16
F32 = jnp.float32
_NT = (((1,), (1,)), ((), ()))


def _dot(a, b):
    return jnp.dot(a, b, preferred_element_type=F32)


def _dot_nt(a, b):
    return lax.dot_general(a, b, _NT, preferred_element_type=F32)


def _split2(x):
    hi = x.astype(BF16)
    lo = (x - hi.astype(F32)).astype(BF16)
    return hi, lo


def _split3(x):
    hi = x.astype(BF16)
    r = x - hi.astype(F32)
    mid = r.astype(BF16)
    lo = (r - mid.astype(F32)).astype(BF16)
    return hi, mid, lo


def _rms(x, g):
    return x * lax.rsqrt(jnp.mean(x * x, axis=-1, keepdims=True) + EPS) * g


def _in_proj_kernel(x_ref, g_ref, cos_ref, sin_ref, w_ref, wf_hi_ref, wf_lo_ref, bf_ref, tri_ref,
                    qd_ref, kd_ref, vd_ref, qf_ref, kf_ref, vf_ref, fc_ref, carry_ref,
                    *, tiles_per_seq):
    i = pl.program_id(0)
    h = _rms(x_ref[...], g_ref[...])
    hb, h_lo = _split2(h)
    cos = cos_ref[...]
    sin = sin_ref[...]
    scale = HEAD_DIM ** -0.5

    def rope(w_idx, out_ref, mul):
        a = _dot(hb, w_ref[w_idx])
        r = _dot(hb, w_ref[w_idx + 1])
        for c in range(a.shape[1] // LANES):
            sl = slice(c * LANES, (c + 1) * LANES)
            out_ref[:, sl] = ((a[:, sl] * cos + r[:, sl] * sin) * mul).astype(out_ref.dtype)

    rope(0, qd_ref, scale)
    rope(2, kd_ref, 1.0)
    vd_ref[...] = _dot(hb, w_ref[4]).astype(vd_ref.dtype)
    qf_ref[...] = (_dot(hb, w_ref[5]) * scale).astype(qf_ref.dtype)
    kf_ref[...] = _dot(hb, w_ref[6]).astype(kf_ref.dtype)
    vf_ref[...] = _dot(hb, w_ref[7]).astype(vf_ref.dtype)

    z = (_dot(hb, wf_hi_ref[...]) + _dot(hb, wf_lo_ref[...]) + _dot(h_lo, wf_hi_ref[...])) + bf_ref[...]
    log_f = jnp.minimum(z, 0.0) - jnp.log1p(jnp.exp(-jnp.abs(z)))
    lane = lax.broadcasted_iota(jnp.int32, log_f.shape, 1)
    log_f = jnp.where(lane < N_FOX_HEADS, log_f, 0.0)

    @pl.when(i % tiles_per_seq == 0)
    def _():
        carry_ref[...] = jnp.zeros_like(carry_ref)

    tri = tri_ref[...]
    p0, p1, p2 = _split3(log_f)
    cum = (_dot(tri, p0) + _dot(tri, p1)) + _dot(tri, p2) + carry_ref[...]
    fc_ref[...] = cum
    carry_ref[...] = cum[cum.shape[0] - 1:, :]


def _in_proj(x2, g, cos2, sin2, w_stack, wf_hi, wf_lo, bf_pad, *, seq, tm):
    n, d = x2.shape
    tiles_per_seq = seq // tm
    tri = jnp.tril(jnp.ones((tm, tm), F32)).astype(BF16)
    wcols = w_stack.shape[2]
    const = lambda *shape: pl.BlockSpec(shape, lambda i: (0,) * len(shape))
    row_blk = lambda cols: pl.BlockSpec((tm, cols), lambda i: (i, 0))
    pos_blk = pl.BlockSpec((tm, LANES), lambda i: (i % tiles_per_seq, 0))
    out_bf = jax.ShapeDtypeStruct((n, wcols), BF16)
    return pl.pallas_call(
        functools.partial(_in_proj_kernel, tiles_per_seq=tiles_per_seq),
        grid=(n // tm,),
        in_specs=[row_blk(d), const(1, d), pos_blk, pos_blk, const(*w_stack.shape),
                  const(d, LANES), const(d, LANES), const(1, LANES), const(tm, tm)],
        out_specs=[row_blk(wcols)] * 6 + [row_blk(LANES)],
        out_shape=[out_bf] * 6 + [jax.ShapeDtypeStruct((n, LANES), F32)],
        scratch_shapes=[pltpu.VMEM((1, LANES), F32)],
        compiler_params=pltpu.CompilerParams(dimension_semantics=("arbitrary",),
                                             vmem_limit_bytes=VMEM_LIMIT),
        name="in_proj",
    )(x2, g, cos2, sin2, w_stack, wf_hi, wf_lo, bf_pad, tri)


def _softmax_step(s, v, m, l, acc):
    m_new = jnp.maximum(m, jnp.max(s, axis=1, keepdims=True))
    alpha = jnp.exp(m - m_new)
    p = jnp.exp(s - m_new)
    l = alpha * l + jnp.sum(p, axis=1, keepdims=True)
    acc = alpha * acc + _dot(p.astype(v.dtype), v)
    return m_new, l, acc


def _diff_attn_kernel(q_ref, k_ref, v_ref, lam_ref, g_ref, o_ref, *, tq, lam_init):
    i = pl.program_id(2)
    q = q_ref[0]
    lane = lax.broadcasted_iota(jnp.int32, q.shape, 1)
    zero = jnp.zeros_like(q)
    qs = (jnp.where(lane < HEAD_DIM, q, zero), jnp.where(lane >= HEAD_DIM, q, zero))

    def init():
        return (jnp.full((tq, 1), NEG_BIG, F32), jnp.zeros((tq, 1), F32), jnp.zeros((tq, LANES), F32))

    def body(j, carry):
        start = pl.multiple_of(j * tq, tq)
        kt = k_ref[0, pl.ds(start, tq), :]
        vt = v_ref[0, pl.ds(start, tq), :]
        return tuple(_softmax_step(_dot_nt(qs[a], kt), vt, *carry[a]) for a in range(2))

    carry = lax.fori_loop(0, i, body, (init(), init()))

    start = pl.multiple_of(i * tq, tq)
    kt = k_ref[0, pl.ds(start, tq), :]
    vt = v_ref[0, pl.ds(start, tq), :]
    row = lax.broadcasted_iota(jnp.int32, (tq, tq), 0)
    col = lax.broadcasted_iota(jnp.int32, (tq, tq), 1)
    mask = (col // CHUNK) <= (row // CHUNK)
    outs = []
    for a in range(2):
        s = jnp.where(mask, _dot_nt(qs[a], kt), NEG_BIG)
        _, l, acc = _softmax_step(s, vt, *carry[a])
        outs.append(acc / l)

    lp = lam_ref[...]
    lam = (jnp.exp(jnp.sum(lp[0:1] * lp[1:2], axis=1, keepdims=True))
           - jnp.exp(jnp.sum(lp[2:3] * lp[3:4], axis=1, keepdims=True)) + lam_init)
    o = outs[0] - lam * outs[1]
    o_ref[0] = (_rms(o, g_ref[...]) * (1.0 - lam_init)).astype(o_ref.dtype)


def _diff_attn(qd, kd, vd, lam_params, gnorm, *, tq, lam_init):
    b, s, w = qd.shape
    nh = w // LANES
    q_blk = pl.BlockSpec((1, tq, LANES), lambda bi, h, i: (bi, i, h))
    kv_blk = pl.BlockSpec((1, s, LANES), lambda bi, h, i: (bi, 0, h))
    return pl.pallas_call(
        functools.partial(_diff_attn_kernel, tq=tq, lam_init=lam_init),
        grid=(b, nh, s // tq),
        in_specs=[q_blk, kv_blk, kv_blk,
                  pl.BlockSpec(lam_params.shape, lambda bi, h, i: (0, 0)),
                  pl.BlockSpec(gnorm.shape, lambda bi, h, i: (0, 0))],
        out_specs=q_blk,
        out_shape=jax.ShapeDtypeStruct((b, s, w), BF16),
        compiler_params=pltpu.CompilerParams(dimension_semantics=("arbitrary",) * 3,
                                             vmem_limit_bytes=VMEM_LIMIT),
        name="diff_attn",
    )(qd, kd, vd, lam_params, gnorm)


def _fox_attn_kernel(q_ref, k_ref, v_ref, fc_ref, ft_ref, o_ref, *, tq):
    pair = pl.program_id(1)
    i = pl.program_id(2)
    q = q_ref[0]
    lane = lax.broadcasted_iota(jnp.int32, q.shape, 1)
    zero = jnp.zeros_like(q)
    qs = (jnp.where(lane < HEAD_DIM, q, zero), jnp.where(lane >= HEAD_DIM, q, zero))
    fc = fc_ref[0]
    heads = (2 * pair, 2 * pair + 1)
    f_col = tuple(jnp.sum(jnp.where(lane == h, fc, 0.0), axis=1, keepdims=True) for h in heads)

    def logits(a, kt, start):
        f_row = ft_ref[0, pl.ds(heads[a], 1), pl.ds(start, tq)]
        return _dot_nt(qs[a], kt) + f_col[a] - f_row

    def init():
        return (jnp.full((tq, 1), NEG_BIG, F32), jnp.zeros((tq, 1), F32), jnp.zeros((tq, LANES), F32))

    def body(j, carry):
        start = pl.multiple_of(j * tq, tq)
        kt = k_ref[0, pl.ds(start, tq), :]
        vt = v_ref[0, pl.ds(start, tq), :]
        return tuple(_softmax_step(logits(a, kt, start), vt, *carry[a]) for a in range(2))

    carry = lax.fori_loop(0, i, body, (init(), init()))

    start = pl.multiple_of(i * tq, tq)
    kt = k_ref[0, pl.ds(start, tq), :]
    vt = v_ref[0, pl.ds(start, tq), :]
    row = lax.broadcasted_iota(jnp.int32, (tq, tq), 0)
    col = lax.broadcasted_iota(jnp.int32, (tq, tq), 1)
    mask = col <= row
    outs = []
    for a in range(2):
        s = jnp.where(mask, logits(a, kt, start), NEG_BIG)
        _, l, acc = _softmax_step(s, vt, *carry[a])
        outs.append(acc / l)
    o_ref[0] = jnp.where(lane < HEAD_DIM, outs[0], outs[1]).astype(o_ref.dtype)


def _fox_attn(qf, kf, vf, fcum, fcum_t, *, tq):
    b, s, w = qf.shape
    npairs = w // LANES
    q_blk = pl.BlockSpec((1, tq, LANES), lambda bi, p, i: (bi, i, p))
    kv_blk = pl.BlockSpec((1, s, LANES), lambda bi, p, i: (bi, 0, p))
    return pl.pallas_call(
        functools.partial(_fox_attn_kernel, tq=tq),
        grid=(b, npairs, s // tq),
        in_specs=[q_blk, kv_blk, kv_blk,
                  pl.BlockSpec((1, tq, LANES), lambda bi, p, i: (bi, i, 0)),
                  pl.BlockSpec((1,) + fcum_t.shape[1:], lambda bi, p, i: (bi, 0, 0))],
        out_specs=q_blk,
        out_shape=jax.ShapeDtypeStruct((b, s, w), BF16),
        compiler_params=pltpu.CompilerParams(dimension_semantics=("arbitrary",) * 3,
                                             vmem_limit_bytes=VMEM_LIMIT),
        name="fox_attn",
    )(qf, kf, vf, fcum, fcum_t)


def _out_proj_kernel(x_ref, od_ref, of_ref, wo_ref, g_ref, wr_hi_ref, wr_lo_ref, br_ref,
                     x1_ref, hm_ref, comb_ref):
    x1 = x_ref[...] + (_dot(od_ref[...], wo_ref[0]) + _dot(of_ref[...], wo_ref[1]))
    x1_ref[...] = x1
    hm = _rms(x1, g_ref[...])
    hb, h_lo = _split2(hm)
    hm_ref[...] = hb

    wr_hi = wr_hi_ref[...]
    lt = (_dot_nt(wr_hi, hb) + _dot_nt(wr_lo_ref[...], hb) + _dot_nt(wr_hi, h_lo)) + br_ref[...]
    tm = lt.shape[1]
    g8 = lt[0:8]
    r8 = lax.broadcasted_iota(jnp.int32, (8, tm), 0)
    g8 = jnp.where(r8 < N_GROUPS, g8, NEG_BIG)
    gmax = jnp.max(g8, axis=0, keepdims=True)
    gidx = jnp.min(jnp.where(g8 == gmax, r8, N_GROUPS), axis=0, keepdims=True)
    g_w = 1.0 / jnp.sum(jnp.exp(g8 - gmax), axis=0, keepdims=True)

    e16 = lt[8:8 + N_EXPERTS]
    r16 = lax.broadcasted_iota(jnp.int32, (N_EXPERTS, tm), 0)
    in_group = (r16 // EXPERTS_PER_GROUP) == gidx
    e_sel = jnp.where(in_group, e16, NEG_BIG)
    top1 = jnp.max(e_sel, axis=0, keepdims=True)
    id1 = jnp.min(jnp.where(e_sel == top1, r16, N_EXPERTS), axis=0, keepdims=True)
    e_rest = jnp.where(r16 == id1, NEG_BIG, e_sel)
    top2 = jnp.max(e_rest, axis=0, keepdims=True)
    id2 = jnp.min(jnp.where(e_rest == top2, r16, N_EXPERTS), axis=0, keepdims=True)
    t = jnp.exp(top2 - top1)
    w1 = g_w / (1.0 + t)
    w2 = w1 * t
    r128 = lax.broadcasted_iota(jnp.int32, (LANES, tm), 0)
    comb_t = jnp.where(r128 == id1, w1, 0.0) + jnp.where(r128 == id2, w2, 0.0)
    comb_ref[...] = comb_t.T


def _out_proj(x2, od, of, wo, g, wr_hi, wr_lo, br, *, tm):
    n, d = x2.shape
    const = lambda *shape: pl.BlockSpec(shape, lambda i: (0,) * len(shape))
    row_blk = lambda cols: pl.BlockSpec((tm, cols), lambda i: (i, 0))
    return pl.pallas_call(
        _out_proj_kernel,
        grid=(n // tm,),
        in_specs=[row_blk(d), row_blk(od.shape[1]), row_blk(of.shape[1]), const(*wo.shape), const(1, d),
                  const(*wr_hi.shape), const(*wr_lo.shape), const(*br.shape)],
        out_specs=[row_blk(d), row_blk(d), row_blk(LANES)],
        out_shape=[jax.ShapeDtypeStruct((n, d), F32), jax.ShapeDtypeStruct((n, d), BF16),
                   jax.ShapeDtypeStruct((n, LANES), F32)],
        compiler_params=pltpu.CompilerParams(dimension_semantics=("arbitrary",),
                                             vmem_limit_bytes=VMEM_LIMIT),
        name="out_proj",
    )(x2, od, of, wo, g, wr_hi, wr_lo, br)


def _moe_kernel(hm_ref, comb_ref, x1_ref, wg_ref, wu_ref, wd_ref, gf_ref, o_ref, acc_ref,
                *, apply_final):
    e = pl.program_id(1)

    @pl.when(e == 0)
    def _():
        acc_ref[...] = jnp.zeros_like(acc_ref)

    h = hm_ref[...]
    gate = _dot(h, wg_ref[0])
    a = gate * jax.nn.sigmoid(gate) * _dot(h, wu_ref[0])
    comb = comb_ref[...]
    lane = lax.broadcasted_iota(jnp.int32, comb.shape, 1)
    c = jnp.sum(jnp.where(lane == e, comb, 0.0), axis=1, keepdims=True)
    acc_ref[...] += c * _dot(a.astype(BF16), wd_ref[0])

    @pl.when(e == pl.num_programs(1) - 1)
    def _():
        x2 = x1_ref[...] + acc_ref[...]
        o_ref[...] = _rms(x2, gf_ref[...]) if apply_final else x2


def _moe(hm, comb, x1, wg, wu, wd, gfin, *, tm, apply_final):
    n, d = x1.shape
    ne, _, de = wg.shape
    row_blk = lambda cols: pl.BlockSpec((tm, cols), lambda i, e: (i, 0))
    return pl.pallas_call(
        functools.partial(_moe_kernel, apply_final=apply_final),
        grid=(n // tm, ne),
        in_specs=[row_blk(d), row_blk(LANES), row_blk(d),
                  pl.BlockSpec((1, d, de), lambda i, e: (e, 0, 0)),
                  pl.BlockSpec((1, d, de), lambda i, e: (e, 0, 0)),
                  pl.BlockSpec((1, de, d), lambda i, e: (e, 0, 0)),
                  pl.BlockSpec((1, d), lambda i, e: (0, 0))],
        out_specs=row_blk(d),
        out_shape=jax.ShapeDtypeStruct((n, d), F32),
        scratch_shapes=[pltpu.VMEM((tm, d), F32)],
        compiler_params=pltpu.CompilerParams(dimension_semantics=("arbitrary", "arbitrary"),
                                             vmem_limit_bytes=VMEM_LIMIT),
        name="moe",
    )(hm, comb, x1, wg, wu, wd, gfin)


def _rot_cols(w):
    d, c = w.shape
    w4 = w.reshape(d, c // HEAD_DIM, 2, HEAD_DIM // 2)
    return jnp.stack([-w4[:, :, 1], w4[:, :, 0]], axis=2).reshape(d, c)


def kernel(x, norm_attn_g, w_in, b_forget, lambda_q1, lambda_k1, lambda_q2, lambda_k2, diff_norm_g, w_out,
           norm_ffn_g, router_group_w, router_group_b, router_expert_w, router_expert_b, w_gate, w_up, w_down,
           norm_final_g):
    b, s, d = x.shape
    depth = w_in.shape[0]
    n = b * s
    diff_w = N_DIFF_HEADS * 2 * HEAD_DIM
    fox_w = N_FOX_HEADS * HEAD_DIM

    pos = jnp.arange(s, dtype=F32)
    inv_freq = 1.0 / (ROPE_THETA ** (jnp.arange(0, HEAD_DIM, 2, dtype=F32) / HEAD_DIM))
    ang = pos[:, None] * inv_freq[None, :]
    ang = jnp.concatenate([ang, ang, ang, ang], axis=-1)
    cos2, sin2 = jnp.cos(ang), jnp.sin(ang)

    x2 = x.reshape(n, d)
    for l in range(depth):
        lam_init = 0.8 - 0.6 * float(np.exp(-0.3 * l))
        w = w_in[l]
        offs = np.cumsum([0, diff_w, diff_w, diff_w, fox_w, fox_w, fox_w, N_FOX_HEADS])
        seg = [w[:, offs[k]:offs[k + 1]] for k in range(7)]
        w_stack = jnp.stack([seg[0], _rot_cols(seg[0]), seg[1], _rot_cols(seg[1]),
                             seg[2], seg[3], seg[4], seg[5]]).astype(BF16)
        wf = jnp.pad(seg[6], ((0, 0), (0, LANES - N_FOX_HEADS)))
        wf_hi, wf_lo = _split2(wf)
        bf_pad = jnp.pad(b_forget[l], (0, LANES - N_FOX_HEADS)).reshape(1, LANES)

        qd, kd, vd, qf, kf, vf, fcum = _in_proj(
            x2, norm_attn_g[l].reshape(1, d), cos2, sin2, w_stack, wf_hi, wf_lo, bf_pad, seq=s, tm=512)

        to3 = lambda t: t.reshape(b, s, t.shape[-1])
        lam_params = jnp.stack([lambda_q1[l], lambda_k1[l], lambda_q2[l], lambda_k2[l]])
        od = _diff_attn(to3(qd), to3(kd), to3(vd), lam_params, diff_norm_g[l].reshape(1, -1),
                        tq=512, lam_init=lam_init)
        fcum3 = to3(fcum)
        fcum_t = jnp.swapaxes(fcum3[:, :, :N_FOX_HEADS], 1, 2)
        of = _fox_attn(to3(qf), to3(kf), to3(vf), fcum3, fcum_t, tq=512)

        wo = w_out[l].astype(BF16).reshape(2, -1, d)
        wr = jnp.zeros((d, LANES), F32)
        wr = wr.at[:, :N_GROUPS].set(router_group_w[l]).at[:, 8:8 + N_EXPERTS].set(router_expert_w[l])
        wr_hi, wr_lo = _split2(wr.T)
        br = jnp.zeros((LANES,), F32)
        br = br.at[:N_GROUPS].set(router_group_b[l]).at[8:8 + N_EXPERTS].set(router_expert_b[l]).reshape(LANES, 1)
        x1, hm, comb = _out_proj(x2, od.reshape(n, -1), of.reshape(n, -1), wo, norm_ffn_g[l].reshape(1, d),
                                 wr_hi, wr_lo, br, tm=512)

        x2 = _moe(hm, comb, x1, w_gate[l].astype(BF16), w_up[l].astype(BF16), w_down[l].astype(BF16),
                  norm_final_g.reshape(1, d), tm=1024, apply_final=(l == depth - 1))
    return x2.reshape(b, s, d)
```

```python
import functools

import numpy as np
import jax
import jax.numpy as jnp
from jax import lax
from jax.experimental import pallas as pl
from jax.experimental.pallas import tpu as pltpu

CHUNK = 64
HEAD_DIM = 64
ROPE_THETA = 10000.0
EPS = 1e-6
N_DIFF_HEADS = 4
N_FOX_HEADS = 8
N_GROUPS = 4
EXPERTS_PER_GROUP = 4
N_EXPERTS = N_GROUPS * EXPERTS_PER_GROUP

LANES = 128
NEG_BIG = -1e30
VMEM_LIMIT = 56 * 1024 * 1024
N_BIAS = 6

BF16 = jnp.bfloat16
F32 = jnp.float32
_NT = (((1,), (1,)), ((), ()))


def _dot(a, b):
    return jnp.dot(a, b, preferred_element_type=F32)


def _dot_nt(a, b):
    return lax.dot_general(a, b, _NT, preferred_element_type=F32)


def _split2(x):
    hi = x.astype(BF16)
    lo = (x - hi.astype(F32)).astype(BF16)
    return hi, lo


def _split3(x):
    hi = x.astype(BF16)
    r = x - hi.astype(F32)
    mid = r.astype(BF16)
    lo = (r - mid.astype(F32)).astype(BF16)
    return hi, mid, lo


def _rms(x, g):
    return x * lax.rsqrt(jnp.mean(x * x, axis=-1, keepdims=True) + EPS) * g


def _in_proj_kernel(x_ref, g_ref, cos_ref, sin_ref, w_ref, wt_ref, wf_hi_ref, wf_lo_ref, bf_ref, tri_ref,
                    eq_ref, ek_ref, cq_ref, ck_ref,
                    qd_ref, kd_ref, vdt_ref, qf_ref, kf_ref, vft_ref, qb_ref, kb_ref, carry_ref,
                    *, tiles_per_seq):
    i = pl.program_id(0)
    h = _rms(x_ref[...], g_ref[...])
    hb, h_lo = _split2(h)
    cos = cos_ref[...]
    sin = sin_ref[...]
    scale = HEAD_DIM ** -0.5

    def rope(w_idx, out_ref, mul):
        a = _dot(hb, w_ref[w_idx])
        r = _dot(hb, w_ref[w_idx + 1])
        for c in range(a.shape[1] // LANES):
            sl = slice(c * LANES, (c + 1) * LANES)
            out_ref[:, sl] = ((a[:, sl] * cos + r[:, sl] * sin) * mul).astype(out_ref.dtype)

    rope(0, qd_ref, scale)
    rope(2, kd_ref, 1.0)
    qf_ref[...] = (_dot(hb, w_ref[4]) * scale).astype(qf_ref.dtype)
    kf_ref[...] = _dot(hb, w_ref[5]).astype(kf_ref.dtype)
    vdt_ref[0] = _dot_nt(wt_ref[0], hb).astype(vdt_ref.dtype)
    vft_ref[0] = _dot_nt(wt_ref[1], hb).astype(vft_ref.dtype)

    z = (_dot(hb, wf_hi_ref[...]) + _dot(hb, wf_lo_ref[...]) + _dot(h_lo, wf_hi_ref[...])) + bf_ref[...]
    log_f = jnp.minimum(z, 0.0) - jnp.log1p(jnp.exp(-jnp.abs(z)))
    lane = lax.broadcasted_iota(jnp.int32, log_f.shape, 1)
    log_f = jnp.where(lane < N_FOX_HEADS, log_f, 0.0)

    @pl.when(i % tiles_per_seq == 0)
    def _():
        carry_ref[...] = jnp.zeros_like(carry_ref)

    tri = tri_ref[...]
    p0, p1, p2 = _split3(log_f)
    cum = (_dot(tri, p0) + _dot(tri, p1)) + _dot(tri, p2) + carry_ref[...]
    carry_ref[...] = cum[cum.shape[0] - 1:, :]

    f_parts = _split3(cum)
    qb = cq_ref[...]
    kb = ck_ref[...]
    for part in range(3):
        qb = qb + _dot(f_parts[part], eq_ref[part])
        kb = kb + _dot(f_parts[part], ek_ref[part])
    qb_ref[...] = qb.astype(qb_ref.dtype)
    kb_ref[...] = kb.astype(kb_ref.dtype)


def _bias_placement():
    width = (N_FOX_HEADS // 2) * LANES
    eq = np.zeros((3, LANES, width), np.float32)
    ek = np.zeros((3, LANES, width), np.float32)
    cq = np.zeros((1, width), np.float32)
    ck = np.zeros((1, width), np.float32)
    for h in range(N_FOX_HEADS):
        base = (h // 2) * LANES + (h % 2) * N_BIAS
        for part in range(3):
            eq[part, h, base + part] = 1.0
            ek[part, h, base + 3 + part] = -1.0
        cq[0, base + 3:base + 6] = 1.0
        ck[0, base:base + 3] = 1.0
    return jnp.asarray(eq, BF16), jnp.asarray(ek, BF16), jnp.asarray(cq), jnp.asarray(ck)


def _in_proj(x2, g, cos2, sin2, w_stack, wt_stack, wf_hi, wf_lo, bf_pad, *, batch, seq, tm):
    n, d = x2.shape
    tiles_per_seq = seq // tm
    tri = jnp.tril(jnp.ones((tm, tm), F32)).astype(BF16)
    eq, ek, cq, ck = _bias_placement()
    wcols = w_stack.shape[2]
    const = lambda *shape: pl.BlockSpec(shape, lambda i: (0,) * len(shape))
    row_blk = lambda cols: pl.BlockSpec((tm, cols), lambda i: (i, 0))
    pos_blk = pl.BlockSpec((tm, LANES), lambda i: (i % tiles_per_seq, 0))
    vt_blk = pl.BlockSpec((1, wcols, tm), lambda i: (i // tiles_per_seq, 0, i % tiles_per_seq))
    out_bf = jax.ShapeDtypeStruct((n, wcols), BF16)
    out_vt = jax.ShapeDtypeStruct((batch, wcols, seq), BF16)
    return pl.pallas_call(
        functools.partial(_in_proj_kernel, tiles_per_seq=tiles_per_seq),
        grid=(n // tm,),
        in_specs=[row_blk(d), const(1, d), pos_blk, pos_blk, const(*w_stack.shape), const(*wt_stack.shape),
                  const(d, LANES), const(d, LANES), const(1, LANES), const(tm, tm),
                  const(*eq.shape), const(*ek.shape), const(*cq.shape), const(*ck.shape)],
        out_specs=[row_blk(wcols), row_blk(wcols), vt_blk, row_blk(wcols), row_blk(wcols), vt_blk,
                   row_blk(eq.shape[2]), row_blk(ek.shape[2])],
        out_shape=[out_bf, out_bf, out_vt, out_bf, out_bf, out_vt,
                   jax.ShapeDtypeStruct((n, eq.shape[2]), BF16), jax.ShapeDtypeStruct((n, ek.shape[2]), BF16)],
        scratch_shapes=[pltpu.VMEM((1, LANES), F32)],
        compiler_params=pltpu.CompilerParams(dimension_semantics=("arbitrary",),
                                             vmem_limit_bytes=VMEM_LIMIT),
        name="in_proj",
    )(x2, g, cos2, sin2, w_stack, wt_stack, wf_hi, wf_lo, bf_pad, tri, eq, ek, cq, ck)


def _flash_tiles(i, tq, logits_fn, values_fn, mask, scratch):
    s0, s1, c0, c1, m_ref, l_ref, acc_ref = scratch
    n_maps = m_ref.shape[0]
    m_ref[...] = jnp.full(m_ref.shape, NEG_BIG, F32)
    l_ref[...] = jnp.zeros(l_ref.shape, F32)
    acc_ref[...] = jnp.zeros(acc_ref.shape, F32)

    def stage_a(tile, s_buf, c_buf, masked):
        sts = logits_fn(tile)
        for a in range(n_maps):
            st = jnp.where(mask, sts[a], NEG_BIG) if masked else sts[a]
            s_buf[a] = st
            c_buf[a] = jnp.max(st, axis=0, keepdims=True)

    def stage_b(tile, s_buf, c_buf):
        vts = values_fn(tile)
        for a in range(n_maps):
            m = m_ref[a]
            m_new = jnp.maximum(m, c_buf[a])
            alpha = jnp.exp(m - m_new)
            p = jnp.exp(s_buf[a] - m_new)
            l_ref[a] = alpha * l_ref[a] + jnp.sum(p, axis=0, keepdims=True)
            acc_ref[a] = alpha * acc_ref[a] + _dot(vts[a], p.astype(vts[a].dtype))
            m_ref[a] = m_new

    stage_a(i, s0, c0, True)

    def body(jj, _):
        t = 2 * jj
        stage_a(t, s1, c1, False)
        stage_b(jnp.where(jj == 0, i, t - 1), s0, c0)
        stage_a(t + 1, s0, c0, False)
        stage_b(t, s1, c1)
        return 0

    lax.fori_loop(0, lax.shift_right_logical(i, 1), body, 0)
    odd = (i & 1) == 1

    @pl.when(odd)
    def _():
        stage_a(i - 1, s1, c1, False)
        stage_b(jnp.where(i == 1, i, i - 2), s0, c0)
        stage_b(i - 1, s1, c1)

    @pl.when(jnp.logical_not(odd))
    def _():
        stage_b(jnp.where(i == 0, i, i - 1), s0, c0)


def _flash_scratch(n_maps, dv, tq):
    s_buf = pltpu.VMEM((n_maps, tq, tq), F32)
    c_buf = pltpu.VMEM((n_maps, 1, tq), F32)
    return [s_buf, s_buf, c_buf, c_buf, c_buf, c_buf, pltpu.VMEM((n_maps, dv, tq), F32)]


def _diff_attn_kernel(q_ref, k_ref, vt_ref, lam_ref, g_ref, o_ref, *scratch, tq, lam_init):
    i = pl.program_id(2)
    q = q_ref[0]
    lane = lax.broadcasted_iota(jnp.int32, q.shape, 1)
    zero = jnp.zeros_like(q)
    qs = (jnp.where(lane < HEAD_DIM, q, zero), jnp.where(lane >= HEAD_DIM, q, zero))

    def logits(tile):
        kt = k_ref[0, pl.ds(pl.multiple_of(tile * tq, tq), tq), :]
        return [_dot_nt(kt, qs[a]) for a in range(2)]

    def values(tile):
        vt = vt_ref[0, :, pl.ds(pl.multiple_of(tile * tq, tq), tq)]
        return [vt, vt]

    key = lax.broadcasted_iota(jnp.int32, (tq, tq), 0)
    qry = lax.broadcasted_iota(jnp.int32, (tq, tq), 1)
    _flash_tiles(i, tq, logits, values, (key // CHUNK) <= (qry // CHUNK), scratch)
    l_ref, acc_ref = scratch[5], scratch[6]

    lp = lam_ref[...]
    lam = (jnp.exp(jnp.sum(lp[0:1] * lp[1:2], axis=1, keepdims=True))
           - jnp.exp(jnp.sum(lp[2:3] * lp[3:4], axis=1, keepdims=True)) + lam_init)
    o = acc_ref[0] / l_ref[0] - lam * (acc_ref[1] / l_ref[1])
    y = o * lax.rsqrt(jnp.mean(o * o, axis=0, keepdims=True) + EPS) * g_ref[...] * (1.0 - lam_init)
    o_ref[0] = y.T.astype(o_ref.dtype)


def _diff_attn(qd, kd, vdt, lam_params, gnorm_col, *, tq, lam_init):
    b, s, w = qd.shape
    nh = w // LANES
    q_blk = pl.BlockSpec((1, tq, LANES), lambda bi, h, i: (bi, i, h))
    k_blk = pl.BlockSpec((1, s, LANES), lambda bi, h, i: (bi, 0, h))
    vt_blk = pl.BlockSpec((1, LANES, s), lambda bi, h, i: (bi, h, 0))
    return pl.pallas_call(
        functools.partial(_diff_attn_kernel, tq=tq, lam_init=lam_init),
        grid=(b, nh, s // tq),
        in_specs=[q_blk, k_blk, vt_blk,
                  pl.BlockSpec(lam_params.shape, lambda bi, h, i: (0, 0)),
                  pl.BlockSpec(gnorm_col.shape, lambda bi, h, i: (0, 0))],
        out_specs=q_blk,
        out_shape=jax.ShapeDtypeStruct((b, s, w), BF16),
        scratch_shapes=_flash_scratch(2, LANES, tq),
        compiler_params=pltpu.CompilerParams(dimension_semantics=("arbitrary",) * 3,
                                             vmem_limit_bytes=VMEM_LIMIT),
        name="diff_attn",
    )(qd, kd, vdt, lam_params, gnorm_col)


def _fox_attn_kernel(q_ref, k_ref, vt_ref, qb_ref, kb_ref, o_ref, *scratch, tq):
    i = pl.program_id(2)
    q = q_ref[0]
    qb = qb_ref[0]
    lane = lax.broadcasted_iota(jnp.int32, q.shape, 1)
    zero = jnp.zeros_like(q)
    qs = (jnp.concatenate([jnp.where(lane < HEAD_DIM, q, zero), jnp.where(lane < N_BIAS, qb, zero)], axis=1),
          jnp.concatenate([jnp.where(lane >= HEAD_DIM, q, zero),
                           jnp.where((lane >= N_BIAS) & (lane < 2 * N_BIAS), qb, zero)], axis=1))

    def logits(tile):
        start = pl.multiple_of(tile * tq, tq)
        kt = jnp.concatenate([k_ref[0, pl.ds(start, tq), :], kb_ref[0, pl.ds(start, tq), :]], axis=1)
        return [_dot_nt(kt, qs[a]) for a in range(2)]

    def values(tile):
        vt = vt_ref[0, :, pl.ds(pl.multiple_of(tile * tq, tq), tq)]
        return [vt[:HEAD_DIM], vt[HEAD_DIM:]]

    key = lax.broadcasted_iota(jnp.int32, (tq, tq), 0)
    qry = lax.broadcasted_iota(jnp.int32, (tq, tq), 1)
    _flash_tiles(i, tq, logits, values, key <= qry, scratch)
    l_ref, acc_ref = scratch[5], scratch[6]
    o = jnp.concatenate([acc_ref[0] / l_ref[0], acc_ref[1] / l_ref[1]], axis=0)
    o_ref[0] = o.T.astype(o_ref.dtype)


def _fox_attn(qf, kf, vft, qb, kb, *, tq):
    b, s, w = qf.shape
    npairs = w // LANES
    q_blk = pl.BlockSpec((1, tq, LANES), lambda bi, p, i: (bi, i, p))
    k_blk = pl.BlockSpec((1, s, LANES), lambda bi, p, i: (bi, 0, p))
    vt_blk = pl.BlockSpec((1, LANES, s), lambda bi, p, i: (bi, p, 0))
    return pl.pallas_call(
        functools.partial(_fox_attn_kernel, tq=tq),
        grid=(b, npairs, s // tq),
        in_specs=[q_blk, k_blk, vt_blk, q_blk, k_blk],
        out_specs=q_blk,
        out_shape=jax.ShapeDtypeStruct((b, s, w), BF16),
        scratch_shapes=_flash_scratch(2, HEAD_DIM, tq),
        compiler_params=pltpu.CompilerParams(dimension_semantics=("arbitrary",) * 3,
                                             vmem_limit_bytes=VMEM_LIMIT),
        name="fox_attn",
    )(qf, kf, vft, qb, kb)


def _out_proj_kernel(x_ref, od_ref, of_ref, wo_ref, g_ref, wr_hi_ref, wr_lo_ref, br_ref,
                     x1_ref, hm_ref, comb_ref):
    x1 = x_ref[...] + (_dot(od_ref[...], wo_ref[0]) + _dot(of_ref[...], wo_ref[1]))
    x1_ref[...] = x1
    hm = _rms(x1, g_ref[...])
    hb, h_lo = _split2(hm)
    hm_ref[...] = hb

    wr_hi = wr_hi_ref[...]
    lt = (_dot_nt(wr_hi, hb) + _dot_nt(wr_lo_ref[...], hb) + _dot_nt(wr_hi, h_lo)) + br_ref[...]
    tm = lt.shape[1]
    g8 = lt[0:8]
    r8 = lax.broadcasted_iota(jnp.int32, (8, tm), 0)
    g8 = jnp.where(r8 < N_GROUPS, g8, NEG_BIG)
    gmax = jnp.max(g8, axis=0, keepdims=True)
    gidx = jnp.min(jnp.where(g8 == gmax, r8, N_GROUPS), axis=0, keepdims=True)
    g_w = 1.0 / jnp.sum(jnp.exp(g8 - gmax), axis=0, keepdims=True)

    e16 = lt[8:8 + N_EXPERTS]
    r16 = lax.broadcasted_iota(jnp.int32, (N_EXPERTS, tm), 0)
    in_group = (r16 // EXPERTS_PER_GROUP) == gidx
    e_sel = jnp.where(in_group, e16, NEG_BIG)
    top1 = jnp.max(e_sel, axis=0, keepdims=True)
    id1 = jnp.min(jnp.where(e_sel == top1, r16, N_EXPERTS), axis=0, keepdims=True)
    e_rest = jnp.where(r16 == id1, NEG_BIG, e_sel)
    top2 = jnp.max(e_rest, axis=0, keepdims=True)
    id2 = jnp.min(jnp.where(e_rest == top2, r16, N_EXPERTS), axis=0, keepdims=True)
    t = jnp.exp(top2 - top1)
    w1 = g_w / (1.0 + t)
    w2 = w1 * t
    r128 = lax.broadcasted_iota(jnp.int32, (LANES, tm), 0)
    comb_t = jnp.where(r128 == id1, w1, 0.0) + jnp.where(r128 == id2, w2, 0.0)
    comb_ref[...] = comb_t.T


def _out_proj(x2, od, of, wo, g, wr_hi, wr_lo, br, *, tm):
    n, d = x2.shape
    const = lambda *shape: pl.BlockSpec(shape, lambda i: (0,) * len(shape))
    row_blk = lambda cols: pl.BlockSpec((tm, cols), lambda i: (i, 0))
    return pl.pallas_call(
        _out_proj_kernel,
        grid=(n // tm,),
        in_specs=[row_blk(d), row_blk(od.shape[1]), row_blk(of.shape[1]), const(*wo.shape), const(1, d),
                  const(*wr_hi.shape), const(*wr_lo.shape), const(*br.shape)],
        out_specs=[row_blk(d), row_blk(d), row_blk(LANES)],
        out_shape=[jax.ShapeDtypeStruct((n, d), F32), jax.ShapeDtypeStruct((n, d), BF16),
                   jax.ShapeDtypeStruct((n, LANES), F32)],
        compiler_params=pltpu.CompilerParams(dimension_semantics=("arbitrary",),
                                             vmem_limit_bytes=VMEM_LIMIT),
        name="out_proj",
    )(x2, od, of, wo, g, wr_hi, wr_lo, br)


def _moe_kernel(hm_ref, comb_ref, x1_ref, wg_ref, wu_ref, wd_ref, gf_ref, o_ref, acc_ref,
                *, apply_final):
    e = pl.program_id(1)

    @pl.when(e == 0)
    def _():
        acc_ref[...] = jnp.zeros_like(acc_ref)

    h = hm_ref[...]
    gate = _dot(h, wg_ref[0])
    a = gate * jax.nn.sigmoid(gate) * _dot(h, wu_ref[0])
    comb = comb_ref[...]
    lane = lax.broadcasted_iota(jnp.int32, comb.shape, 1)
    c = jnp.sum(jnp.where(lane == e, comb, 0.0), axis=1, keepdims=True)
    acc_ref[...] += c * _dot(a.astype(BF16), wd_ref[0])

    @pl.when(e == pl.num_programs(1) - 1)
    def _():
        x2 = x1_ref[...] + acc_ref[...]
        o_ref[...] = _rms(x2, gf_ref[...]) if apply_final else x2


def _moe(hm, comb, x1, wg, wu, wd, gfin, *, tm, apply_final):
    n, d = x1.shape
    ne, _, de = wg.shape
    row_blk = lambda cols: pl.BlockSpec((tm, cols), lambda i, e: (i, 0))
    return pl.pallas_call(
        functools.partial(_moe_kernel, apply_final=apply_final),
        grid=(n // tm, ne),
        in_specs=[row_blk(d), row_blk(LANES), row_blk(d),
                  pl.BlockSpec((1, d, de), lambda i, e: (e, 0, 0)),
                  pl.BlockSpec((1, d, de), lambda i, e: (e, 0, 0)),
                  pl.BlockSpec((1, de, d), lambda i, e: (e, 0, 0)),
                  pl.BlockSpec((1, d), lambda i, e: (0, 0))],
        out_specs=row_blk(d),
        out_shape=jax.ShapeDtypeStruct((n, d), F32),
        scratch_shapes=[pltpu.VMEM((tm, d), F32)],
        compiler_params=pltpu.CompilerParams(dimension_semantics=("arbitrary", "arbitrary"),
                                             vmem_limit_bytes=VMEM_LIMIT),
        name="moe",
    )(hm, comb, x1, wg, wu, wd, gfin)


def _rot_cols(w):
    d, c = w.shape
    w4 = w.reshape(d, c // HEAD_DIM, 2, HEAD_DIM // 2)
    return jnp.stack([-w4[:, :, 1], w4[:, :, 0]], axis=2).reshape(d, c)


def kernel(x, norm_attn_g, w_in, b_forget, lambda_q1, lambda_k1, lambda_q2, lambda_k2, diff_norm_g, w_out,
           norm_ffn_g, router_group_w, router_group_b, router_expert_w, router_expert_b, w_gate, w_up, w_down,
           norm_final_g):
    b, s, d = x.shape
    depth = w_in.shape[0]
    n = b * s
    diff_w = N_DIFF_HEADS * 2 * HEAD_DIM
    fox_w = N_FOX_HEADS * HEAD_DIM

    pos = jnp.arange(s, dtype=F32)
    inv_freq = 1.0 / (ROPE_THETA ** (jnp.arange(0, HEAD_DIM, 2, dtype=F32) / HEAD_DIM))
    ang = pos[:, None] * inv_freq[None, :]
    ang = jnp.concatenate([ang, ang, ang, ang], axis=-1)
    cos2, sin2 = jnp.cos(ang), jnp.sin(ang)

    x2 = x.reshape(n, d)
    for l in range(depth):
        lam_init = 0.8 - 0.6 * float(np.exp(-0.3 * l))
        w = w_in[l]
        offs = np.cumsum([0, diff_w, diff_w, diff_w, fox_w, fox_w, fox_w, N_FOX_HEADS])
        seg = [w[:, offs[k]:offs[k + 1]] for k in range(7)]
        w_stack = jnp.stack([seg[0], _rot_cols(seg[0]), seg[1], _rot_cols(seg[1]), seg[3], seg[4]]).astype(BF16)
        wt_stack = jnp.stack([seg[2].T, seg[5].T]).astype(BF16)
        wf = jnp.pad(seg[6], ((0, 0), (0, LANES - N_FOX_HEADS)))
        wf_hi, wf_lo = _split2(wf)
        bf_pad = jnp.pad(b_forget[l], (0, LANES - N_FOX_HEADS)).reshape(1, LANES)

        qd, kd, vdt, qf, kf, vft, qb, kb = _in_proj(
            x2, norm_attn_g[l].reshape(1, d), cos2, sin2, w_stack, wt_stack, wf_hi, wf_lo, bf_pad,
            batch=b, seq=s, tm=512)

        to3 = lambda t: t.reshape(b, s, t.shape[-1])
        lam_params = jnp.stack([lambda_q1[l], lambda_k1[l], lambda_q2[l], lambda_k2[l]])
        od = _diff_attn(to3(qd), to3(kd), vdt, lam_params, diff_norm_g[l].reshape(-1, 1),
                        tq=512, lam_init=lam_init)
        of = _fox_attn(to3(qf), to3(kf), vft, to3(qb), to3(kb), tq=512)

        wo = w_out[l].astype(BF16).reshape(2, -1, d)
        wr = jnp.zeros((d, LANES), F32)
        wr = wr.at[:, :N_GROUPS].set(router_group_w[l]).at[:, 8:8 + N_EXPERTS].set(router_expert_w[l])
        wr_hi, wr_lo = _split2(wr.T)
        br = jnp.zeros((LANES,), F32)
        br = br.at[:N_GROUPS].set(router_group_b[l]).at[8:8 + N_EXPERTS].set(router_expert_b[l]).reshape(LANES, 1)
        x1, hm, comb = _out_proj(x2, od.reshape(n, -1), of.reshape(n, -1), wo, norm_ffn_g[l].reshape(1, d),
                                 wr_hi, wr_lo, br, tm=512)

        x2 = _moe(hm, comb, x1, w_gate[l].astype(BF16), w_up[l].astype(BF16), w_down[l].astype(BF16),
                  norm_final_g.reshape(1, d), tm=1024, apply_final=(l == depth - 1))
    return x2.reshape(b, s, d)
```

```python
import functools

import numpy as np
import jax
import jax.numpy as jnp
from jax import lax
from jax.experimental import pallas as pl
from jax.experimental.pallas import tpu as pltpu

CHUNK = 64
HEAD_DIM = 64
ROPE_THETA = 10000.0
EPS = 1e-6
N_DIFF_HEADS = 4
N_FOX_HEADS = 8
N_GROUPS = 4
EXPERTS_PER_GROUP = 4
N_EXPERTS = N_GROUPS * EXPERTS_PER_GROUP

LANES = 128
NEG_BIG = -1e30
VMEM_LIMIT = 56 * 1024 * 1024
N_BIAS = 6
SUBLANES = 8
GID_LANE = N_EXPERTS
ROW_BLOCK = 128

BF16 = jnp.bfloat16
F32 = jnp.float32
_NT = (((1,), (1,)), ((), ()))


def _dot(a, b):
    return jnp.dot(a, b, preferred_element_type=F32)


def _dot_nt(a, b):
    return lax.dot_general(a, b, _NT, preferred_element_type=F32)


def _split2(x):
    hi = x.astype(BF16)
    lo = (x - hi.astype(F32)).astype(BF16)
    return hi, lo


def _split3(x):
    hi = x.astype(BF16)
    r = x - hi.astype(F32)
    mid = r.astype(BF16)
    lo = (r - mid.astype(F32)).astype(BF16)
    return hi, mid, lo


def _rms(x, g):
    return x * lax.rsqrt(jnp.mean(x * x, axis=-1, keepdims=True) + EPS) * g


def _in_proj_kernel(x_ref, g_ref, cos_ref, sin_ref, w_ref, wt_ref, wf_hi_ref, wf_lo_ref, bf_ref, tri_ref,
                    eq_ref, ek_ref, cq_ref, ck_ref,
                    qd_ref, kd_ref, vdt_ref, qf_ref, kf_ref, vft_ref, qb_ref, kb_ref, carry_ref,
                    *, tiles_per_seq):
    i = pl.program_id(0)
    h = _rms(x_ref[...], g_ref[...])
    hb, h_lo = _split2(h)
    cos = cos_ref[...]
    sin = sin_ref[...]
    scale = HEAD_DIM ** -0.5

    def rope(w_idx, out_ref, mul):
        a = _dot(hb, w_ref[w_idx])
        r = _dot(hb, w_ref[w_idx + 1])
        for c in range(a.shape[1] // LANES):
            sl = slice(c * LANES, (c + 1) * LANES)
            out_ref[:, sl] = ((a[:, sl] * cos + r[:, sl] * sin) * mul).astype(out_ref.dtype)

    rope(0, qd_ref, scale)
    rope(2, kd_ref, 1.0)
    qf_ref[...] = (_dot(hb, w_ref[4]) * scale).astype(qf_ref.dtype)
    kf_ref[...] = _dot(hb, w_ref[5]).astype(kf_ref.dtype)
    vdt_ref[0] = _dot_nt(wt_ref[0], hb).astype(vdt_ref.dtype)
    vft_ref[0] = _dot_nt(wt_ref[1], hb).astype(vft_ref.dtype)

    z = (_dot(hb, wf_hi_ref[...]) + _dot(hb, wf_lo_ref[...]) + _dot(h_lo, wf_hi_ref[...])) + bf_ref[...]
    log_f = jnp.minimum(z, 0.0) - jnp.log1p(jnp.exp(-jnp.abs(z)))
    lane = lax.broadcasted_iota(jnp.int32, log_f.shape, 1)
    log_f = jnp.where(lane < N_FOX_HEADS, log_f, 0.0)

    @pl.when(i % tiles_per_seq == 0)
    def _():
        carry_ref[...] = jnp.zeros_like(carry_ref)

    tri = tri_ref[...]
    p0, p1, p2 = _split3(log_f)
    cum = (_dot(tri, p0) + _dot(tri, p1)) + _dot(tri, p2) + carry_ref[...]
    carry_ref[...] = cum[cum.shape[0] - 1:, :]

    f_parts = _split3(cum)
    qb = cq_ref[...]
    kb = ck_ref[...]
    for part in range(3):
        qb = qb + _dot(f_parts[part], eq_ref[part])
        kb = kb + _dot(f_parts[part], ek_ref[part])
    qb_ref[...] = qb.astype(qb_ref.dtype)
    kb_ref[...] = kb.astype(kb_ref.dtype)


def _bias_placement():
    width = (N_FOX_HEADS // 2) * LANES
    eq = np.zeros((3, LANES, width), np.float32)
    ek = np.zeros((3, LANES, width), np.float32)
    cq = np.zeros((1, width), np.float32)
    ck = np.zeros((1, width), np.float32)
    for h in range(N_FOX_HEADS):
        base = (h // 2) * LANES + (h % 2) * N_BIAS
        for part in range(3):
            eq[part, h, base + part] = 1.0
            ek[part, h, base + 3 + part] = -1.0
        cq[0, base + 3:base + 6] = 1.0
        ck[0, base:base + 3] = 1.0
    return jnp.asarray(eq, BF16), jnp.asarray(ek, BF16), jnp.asarray(cq), jnp.asarray(ck)


def _in_proj(x2, g, cos2, sin2, w_stack, wt_stack, wf_hi, wf_lo, bf_pad, *, batch, seq, tm):
    n, d = x2.shape
    tiles_per_seq = seq // tm
    tri = jnp.tril(jnp.ones((tm, tm), F32)).astype(BF16)
    eq, ek, cq, ck = _bias_placement()
    wcols = w_stack.shape[2]
    const = lambda *shape: pl.BlockSpec(shape, lambda i: (0,) * len(shape))
    row_blk = lambda cols: pl.BlockSpec((tm, cols), lambda i: (i, 0))
    pos_blk = pl.BlockSpec((tm, LANES), lambda i: (i % tiles_per_seq, 0))
    vt_blk = pl.BlockSpec((1, wcols, tm), lambda i: (i // tiles_per_seq, 0, i % tiles_per_seq))
    out_bf = jax.ShapeDtypeStruct((n, wcols), BF16)
    out_vt = jax.ShapeDtypeStruct((batch, wcols, seq), BF16)
    return pl.pallas_call(
        functools.partial(_in_proj_kernel, tiles_per_seq=tiles_per_seq),
        grid=(n // tm,),
        in_specs=[row_blk(d), const(1, d), pos_blk, pos_blk, const(*w_stack.shape), const(*wt_stack.shape),
                  const(d, LANES), const(d, LANES), const(1, LANES), const(tm, tm),
                  const(*eq.shape), const(*ek.shape), const(*cq.shape), const(*ck.shape)],
        out_specs=[row_blk(wcols), row_blk(wcols), vt_blk, row_blk(wcols), row_blk(wcols), vt_blk,
                   row_blk(eq.shape[2]), row_blk(ek.shape[2])],
        out_shape=[out_bf, out_bf, out_vt, out_bf, out_bf, out_vt,
                   jax.ShapeDtypeStruct((n, eq.shape[2]), BF16), jax.ShapeDtypeStruct((n, ek.shape[2]), BF16)],
        scratch_shapes=[pltpu.VMEM((1, LANES), F32)],
        compiler_params=pltpu.CompilerParams(dimension_semantics=("arbitrary",),
                                             vmem_limit_bytes=VMEM_LIMIT),
        name="in_proj",
    )(x2, g, cos2, sin2, w_stack, wt_stack, wf_hi, wf_lo, bf_pad, tri, eq, ek, cq, ck)


def _flash_tiles(i, tq, logits_fn, values_fn, mask, scratch):
    s0, s1, c0, c1, m_ref, l_ref, acc_ref = scratch
    n_maps = m_ref.shape[0]
    m_ref[...] = jnp.full(m_ref.shape, NEG_BIG, F32)
    l_ref[...] = jnp.zeros(l_ref.shape, F32)
    acc_ref[...] = jnp.zeros(acc_ref.shape, F32)

    def stage_a(tile, s_buf, c_buf, masked):
        sts = logits_fn(tile)
        for a in range(n_maps):
            st = jnp.where(mask, sts[a], NEG_BIG) if masked else sts[a]
            s_buf[a] = st
            c_buf[a] = jnp.max(st, axis=0, keepdims=True)

    def stage_b(tile, s_buf, c_buf):
        vts = values_fn(tile)
        for a in range(n_maps):
            m = m_ref[a]
            m_new = jnp.maximum(m, c_buf[a])
            alpha = jnp.exp(m - m_new)
            p = jnp.exp(s_buf[a] - m_new)
            l_ref[a] = alpha * l_ref[a] + jnp.sum(p, axis=0, keepdims=True)
            acc_ref[a] = alpha * acc_ref[a] + _dot(vts[a], p.astype(vts[a].dtype))
            m_ref[a] = m_new

    stage_a(i, s0, c0, True)

    def body(jj, _):
        t = 2 * jj
        stage_a(t, s1, c1, False)
        stage_b(jnp.where(jj == 0, i, t - 1), s0, c0)
        stage_a(t + 1, s0, c0, False)
        stage_b(t, s1, c1)
        return 0

    lax.fori_loop(0, lax.shift_right_logical(i, 1), body, 0)
    odd = (i & 1) == 1

    @pl.when(odd)
    def _():
        stage_a(i - 1, s1, c1, False)
        stage_b(jnp.where(i == 1, i, i - 2), s0, c0)
        stage_b(i - 1, s1, c1)

    @pl.when(jnp.logical_not(odd))
    def _():
        stage_b(jnp.where(i == 0, i, i - 1), s0, c0)


def _flash_scratch(n_maps, dv, tq):
    s_buf = pltpu.VMEM((n_maps, tq, tq), F32)
    c_buf = pltpu.VMEM((n_maps, 1, tq), F32)
    return [s_buf, s_buf, c_buf, c_buf, c_buf, c_buf, pltpu.VMEM((n_maps, dv, tq), F32)]


def _diff_attn_kernel(q_ref, k_ref, vt_ref, lam_ref, g_ref, o_ref, *scratch, tq, lam_init):
    i = pl.program_id(2)
    q = q_ref[0]
    lane = lax.broadcasted_iota(jnp.int32, q.shape, 1)
    zero = jnp.zeros_like(q)
    qs = (jnp.where(lane < HEAD_DIM, q, zero), jnp.where(lane >= HEAD_DIM, q, zero))

    def logits(tile):
        kt = k_ref[0, pl.ds(pl.multiple_of(tile * tq, tq), tq), :]
        return [_dot_nt(kt, qs[a]) for a in range(2)]

    def values(tile):
        vt = vt_ref[0, :, pl.ds(pl.multiple_of(tile * tq, tq), tq)]
        return [vt, vt]

    key = lax.broadcasted_iota(jnp.int32, (tq, tq), 0)
    qry = lax.broadcasted_iota(jnp.int32, (tq, tq), 1)
    _flash_tiles(i, tq, logits, values, (key // CHUNK) <= (qry // CHUNK), scratch)
    l_ref, acc_ref = scratch[5], scratch[6]

    lp = lam_ref[...]
    lam = (jnp.exp(jnp.sum(lp[0:1] * lp[1:2], axis=1, keepdims=True))
           - jnp.exp(jnp.sum(lp[2:3] * lp[3:4], axis=1, keepdims=True)) + lam_init)
    o = acc_ref[0] / l_ref[0] - lam * (acc_ref[1] / l_ref[1])
    y = o * lax.rsqrt(jnp.mean(o * o, axis=0, keepdims=True) + EPS) * g_ref[...] * (1.0 - lam_init)
    o_ref[0] = y.T.astype(o_ref.dtype)


def _diff_attn(qd, kd, vdt, lam_params, gnorm_col, *, tq, lam_init):
    b, s, w = qd.shape
    nh = w // LANES
    q_blk = pl.BlockSpec((1, tq, LANES), lambda bi, h, i: (bi, i, h))
    k_blk = pl.BlockSpec((1, s, LANES), lambda bi, h, i: (bi, 0, h))
    vt_blk = pl.BlockSpec((1, LANES, s), lambda bi, h, i: (bi, h, 0))
    return pl.pallas_call(
        functools.partial(_diff_attn_kernel, tq=tq, lam_init=lam_init),
        grid=(b, nh, s // tq),
        in_specs=[q_blk, k_blk, vt_blk,
                  pl.BlockSpec(lam_params.shape, lambda bi, h, i: (0, 0)),
                  pl.BlockSpec(gnorm_col.shape, lambda bi, h, i: (0, 0))],
        out_specs=q_blk,
        out_shape=jax.ShapeDtypeStruct((b, s, w), BF16),
        scratch_shapes=_flash_scratch(2, LANES, tq),
        compiler_params=pltpu.CompilerParams(dimension_semantics=("arbitrary",) * 3,
                                             vmem_limit_bytes=VMEM_LIMIT),
        name="diff_attn",
    )(qd, kd, vdt, lam_params, gnorm_col)


def _fox_attn_kernel(q_ref, k_ref, vt_ref, qb_ref, kb_ref, o_ref, *scratch, tq):
    i = pl.program_id(2)
    q = q_ref[0]
    qb = qb_ref[0]
    lane = lax.broadcasted_iota(jnp.int32, q.shape, 1)
    zero = jnp.zeros_like(q)
    qs = (jnp.concatenate([jnp.where(lane < HEAD_DIM, q, zero), jnp.where(lane < N_BIAS, qb, zero)], axis=1),
          jnp.concatenate([jnp.where(lane >= HEAD_DIM, q, zero),
                           jnp.where((lane >= N_BIAS) & (lane < 2 * N_BIAS), qb, zero)], axis=1))

    def logits(tile):
        start = pl.multiple_of(tile * tq, tq)
        kt = jnp.concatenate([k_ref[0, pl.ds(start, tq), :], kb_ref[0, pl.ds(start, tq), :]], axis=1)
        return [_dot_nt(kt, qs[a]) for a in range(2)]

    def values(tile):
        vt = vt_ref[0, :, pl.ds(pl.multiple_of(tile * tq, tq), tq)]
        return [vt[:HEAD_DIM], vt[HEAD_DIM:]]

    key = lax.broadcasted_iota(jnp.int32, (tq, tq), 0)
    qry = lax.broadcasted_iota(jnp.int32, (tq, tq), 1)
    _flash_tiles(i, tq, logits, values, key <= qry, scratch)
    l_ref, acc_ref = scratch[5], scratch[6]
    o = jnp.concatenate([acc_ref[0] / l_ref[0], acc_ref[1] / l_ref[1]], axis=0)
    o_ref[0] = o.T.astype(o_ref.dtype)


def _fox_attn(qf, kf, vft, qb, kb, *, tq):
    b, s, w = qf.shape
    npairs = w // LANES
    q_blk = pl.BlockSpec((1, tq, LANES), lambda bi, p, i: (bi, i, p))
    k_blk = pl.BlockSpec((1, s, LANES), lambda bi, p, i: (bi, 0, p))
    vt_blk = pl.BlockSpec((1, LANES, s), lambda bi, p, i: (bi, p, 0))
    return pl.pallas_call(
        functools.partial(_fox_attn_kernel, tq=tq),
        grid=(b, npairs, s // tq),
        in_specs=[q_blk, k_blk, vt_blk, q_blk, k_blk],
        out_specs=q_blk,
        out_shape=jax.ShapeDtypeStruct((b, s, w), BF16),
        scratch_shapes=_flash_scratch(2, HEAD_DIM, tq),
        compiler_params=pltpu.CompilerParams(dimension_semantics=("arbitrary",) * 3,
                                             vmem_limit_bytes=VMEM_LIMIT),
        name="fox_attn",
    )(qf, kf, vft, qb, kb)


def _out_proj_kernel(x_ref, od_ref, of_ref, wo_ref, g_ref, wr_hi_ref, wr_lo_ref, br_ref,
                     x1_ref, hm_ref, comb_ref, gid_ref):
    x1 = x_ref[...] + (_dot(od_ref[...], wo_ref[0]) + _dot(of_ref[...], wo_ref[1]))
    x1_ref[...] = x1
    hm = _rms(x1, g_ref[...])
    hb, h_lo = _split2(hm)
    hm_ref[...] = hb

    wr_hi = wr_hi_ref[...]
    lt = (_dot_nt(wr_hi, hb) + _dot_nt(wr_lo_ref[...], hb) + _dot_nt(wr_hi, h_lo)) + br_ref[...]
    tm = lt.shape[1]
    g8 = lt[0:8]
    r8 = lax.broadcasted_iota(jnp.int32, (8, tm), 0)
    g8 = jnp.where(r8 < N_GROUPS, g8, NEG_BIG)
    gmax = jnp.max(g8, axis=0, keepdims=True)
    gidx = jnp.min(jnp.where(g8 == gmax, r8, N_GROUPS), axis=0, keepdims=True)
    g_w = 1.0 / jnp.sum(jnp.exp(g8 - gmax), axis=0, keepdims=True)

    e16 = lt[8:8 + N_EXPERTS]
    r16 = lax.broadcasted_iota(jnp.int32, (N_EXPERTS, tm), 0)
    in_group = (r16 // EXPERTS_PER_GROUP) == gidx
    e_sel = jnp.where(in_group, e16, NEG_BIG)
    top1 = jnp.max(e_sel, axis=0, keepdims=True)
    id1 = jnp.min(jnp.where(e_sel == top1, r16, N_EXPERTS), axis=0, keepdims=True)
    e_rest = jnp.where(r16 == id1, NEG_BIG, e_sel)
    top2 = jnp.max(e_rest, axis=0, keepdims=True)
    id2 = jnp.min(jnp.where(e_rest == top2, r16, N_EXPERTS), axis=0, keepdims=True)
    t = jnp.exp(top2 - top1)
    w1 = g_w / (1.0 + t)
    w2 = w1 * t
    r128 = lax.broadcasted_iota(jnp.int32, (LANES, tm), 0)
    gid = gidx.astype(F32)
    comb_t = (jnp.where(r128 == id1, w1, 0.0) + jnp.where(r128 == id2, w2, 0.0)
              + jnp.where(r128 == GID_LANE, gid, 0.0))
    comb_ref[...] = comb_t.T
    gid_ref[...] = jnp.where(r8 == 0, gid, 0.0)


def _out_proj(x2, od, of, wo, g, wr_hi, wr_lo, br, *, tm):
    n, d = x2.shape
    const = lambda *shape: pl.BlockSpec(shape, lambda i: (0,) * len(shape))
    row_blk = lambda cols: pl.BlockSpec((tm, cols), lambda i: (i, 0))
    return pl.pallas_call(
        _out_proj_kernel,
        grid=(n // tm,),
        in_specs=[row_blk(d), row_blk(od.shape[1]), row_blk(of.shape[1]), const(*wo.shape), const(1, d),
                  const(*wr_hi.shape), const(*wr_lo.shape), const(*br.shape)],
        out_specs=[row_blk(d), row_blk(d), row_blk(LANES), pl.BlockSpec((SUBLANES, tm), lambda i: (0, i))],
        out_shape=[jax.ShapeDtypeStruct((n, d), F32), jax.ShapeDtypeStruct((n, d), BF16),
                   jax.ShapeDtypeStruct((n, LANES), F32), jax.ShapeDtypeStruct((SUBLANES, n), F32)],
        compiler_params=pltpu.CompilerParams(dimension_semantics=("arbitrary",),
                                             vmem_limit_bytes=VMEM_LIMIT),
        name="out_proj",
    )(x2, od, of, wo, g, wr_hi, wr_lo, br)


def _moe_kernel(hm_ref, comb_ref, gidr_ref, x1_ref, wg_ref, wu_ref, wd_ref, gf_ref, o_ref,
                tri_ref, hs_ref, cs_ref, ys_ref, posc_ref, tab_ref, *, apply_final):
    i = pl.program_id(0)
    e = pl.program_id(1)
    t = hm_ref.shape[0]
    t_pad = hs_ref.shape[0]
    rb = ROW_BLOCK

    @pl.when((i == 0) & (e == 0))
    def _():
        row = lax.broadcasted_iota(jnp.int32, (t, t), 0)
        col = lax.broadcasted_iota(jnp.int32, (t, t), 1)
        tri_ref[...] = jnp.where(row > col, 1.0, 0.0).astype(BF16)

    @pl.when(e == 0)
    def _():
        comb = comb_ref[...]
        lane = lax.broadcasted_iota(jnp.int32, comb.shape, 1)
        onehot_c = jnp.where(lane == comb[:, GID_LANE:GID_LANE + 1].astype(jnp.int32), 1.0, 0.0)
        sub = lax.broadcasted_iota(jnp.int32, (SUBLANES, t), 0)
        onehot_r = jnp.where(sub == gidr_ref[0:1, :].astype(jnp.int32), 1.0, 0.0)
        tri = tri_ref[...]
        rank_c = _dot(tri, onehot_c.astype(BF16))
        rank_r = _dot_nt(onehot_r.astype(BF16), tri)
        lane1 = lax.broadcasted_iota(jnp.int32, (1, LANES), 1)
        sub1 = lax.broadcasted_iota(jnp.int32, (SUBLANES, 1), 0)
        start_c = jnp.zeros((1, LANES), F32)
        start_r = jnp.zeros((SUBLANES, 1), F32)
        first = jnp.int32(0)
        for g in range(N_GROUPS):
            cnt = jnp.sum(onehot_r[g:g + 1, :]).astype(jnp.int32)
            nblk = lax.shift_right_logical(cnt + (rb - 1), int(np.log2(rb)))
            tab_ref[g] = first
            tab_ref[N_GROUPS + g] = nblk
            row0 = (first * rb).astype(F32)
            start_c = jnp.where(lane1 == g, row0, start_c)
            start_r = jnp.where(sub1 == g, row0, start_r)
            first = first + nblk
        tab_ref[2 * N_GROUPS] = first
        posc_ref[...] = jnp.sum(onehot_c * (rank_c + start_c), axis=1, keepdims=True).astype(jnp.int32)
        pos_r = jnp.sum(onehot_r * (rank_r + start_r), axis=0, keepdims=True).astype(jnp.int32)
        ys_ref[...] = jnp.zeros_like(ys_ref)
        c_parts = _split3(comb)

        def sort_block(b, _):
            off = pl.multiple_of(b * rb, rb)
            dst = lax.broadcasted_iota(jnp.int32, (rb, t), 0) + off
            perm = jnp.where(dst == pos_r, 1.0, 0.0).astype(BF16)
            hs_ref[pl.ds(off, rb), :] = _dot(perm, hm_ref[...]).astype(hs_ref.dtype)
            cs_ref[pl.ds(off, rb), :] = (_dot(perm, c_parts[0]) + _dot(perm, c_parts[1])) + _dot(perm, c_parts[2])
            return 0

        lax.fori_loop(0, first, sort_block, 0)

    g = lax.shift_right_logical(e, int(np.log2(EXPERTS_PER_GROUP)))
    b0 = tab_ref[g]

    def expert_block(b, _):
        off = pl.multiple_of(b * rb, rb)
        rows = hs_ref[pl.ds(off, rb), :]
        gate = _dot(rows, wg_ref[0])
        cblk = cs_ref[pl.ds(off, rb), :]
        lane = lax.broadcasted_iota(jnp.int32, cblk.shape, 1)
        c = jnp.sum(jnp.where(lane == e, cblk, 0.0), axis=1, keepdims=True)
        a = gate * jax.nn.sigmoid(gate) * _dot(rows, wu_ref[0]) * c
        ys_ref[pl.ds(off, rb), :] += _dot(a.astype(BF16), wd_ref[0])
        return 0

    lax.fori_loop(b0, b0 + tab_ref[N_GROUPS + g], expert_block, 0)

    @pl.when(e == pl.num_programs(1) - 1)
    def _():
        src = lax.broadcasted_iota(jnp.int32, (t, t_pad), 1)
        unperm = jnp.where(src == posc_ref[...], 1.0, 0.0).astype(BF16)
        x2 = x1_ref[...] + _dot(unperm, ys_ref[...].astype(BF16))
        o_ref[...] = _rms(x2, gf_ref[...]) if apply_final else x2


def _moe(hm, comb, gid_rows, x1, wg, wu, wd, gfin, *, tm, apply_final):
    n, d = x1.shape
    ne, _, de = wg.shape
    t_pad = tm + N_GROUPS * ROW_BLOCK
    row_blk = lambda cols: pl.BlockSpec((tm, cols), lambda i, e: (i, 0))
    return pl.pallas_call(
        functools.partial(_moe_kernel, apply_final=apply_final),
        grid=(n // tm, ne),
        in_specs=[row_blk(d), row_blk(LANES), pl.BlockSpec((SUBLANES, tm), lambda i, e: (0, i)), row_blk(d),
                  pl.BlockSpec((1, d, de), lambda i, e: (e, 0, 0)),
                  pl.BlockSpec((1, d, de), lambda i, e: (e, 0, 0)),
                  pl.BlockSpec((1, de, d), lambda i, e: (e, 0, 0)),
                  pl.BlockSpec((1, d), lambda i, e: (0, 0))],
        out_specs=row_blk(d),
        out_shape=jax.ShapeDtypeStruct((n, d), F32),
        scratch_shapes=[pltpu.VMEM((tm, tm), BF16), pltpu.VMEM((t_pad, d), BF16), pltpu.VMEM((t_pad, LANES), F32),
                        pltpu.VMEM((t_pad, d), F32), pltpu.VMEM((tm, 1), jnp.int32),
                        pltpu.SMEM((2 * N_GROUPS + 1,), jnp.int32)],
        compiler_params=pltpu.CompilerParams(dimension_semantics=("arbitrary", "arbitrary"),
                                             vmem_limit_bytes=VMEM_LIMIT),
        name="moe",
    )(hm, comb, gid_rows, x1, wg, wu, wd, gfin)


def _rot_cols(w):
    d, c = w.shape
    w4 = w.reshape(d, c // HEAD_DIM, 2, HEAD_DIM // 2)
    return jnp.stack([-w4[:, :, 1], w4[:, :, 0]], axis=2).reshape(d, c)


def kernel(x, norm_attn_g, w_in, b_forget, lambda_q1, lambda_k1, lambda_q2, lambda_k2, diff_norm_g, w_out,
           norm_ffn_g, router_group_w, router_group_b, router_expert_w, router_expert_b, w_gate, w_up, w_down,
           norm_final_g):
    b, s, d = x.shape
    depth = w_in.shape[0]
    n = b * s
    diff_w = N_DIFF_HEADS * 2 * HEAD_DIM
    fox_w = N_FOX_HEADS * HEAD_DIM

    pos = jnp.arange(s, dtype=F32)
    inv_freq = 1.0 / (ROPE_THETA ** (jnp.arange(0, HEAD_DIM, 2, dtype=F32) / HEAD_DIM))
    ang = pos[:, None] * inv_freq[None, :]
    ang = jnp.concatenate([ang, ang, ang, ang], axis=-1)
    cos2, sin2 = jnp.cos(ang), jnp.sin(ang)

    x2 = x.reshape(n, d)
    for l in range(depth):
        lam_init = 0.8 - 0.6 * float(np.exp(-0.3 * l))
        w = w_in[l]
        offs = np.cumsum([0, diff_w, diff_w, diff_w, fox_w, fox_w, fox_w, N_FOX_HEADS])
        seg = [w[:, offs[k]:offs[k + 1]] for k in range(7)]
        w_stack = jnp.stack([seg[0], _rot_cols(seg[0]), seg[1], _rot_cols(seg[1]), seg[3], seg[4]]).astype(BF16)
        wt_stack = jnp.stack([seg[2].T, seg[5].T]).astype(BF16)
        wf = jnp.pad(seg[6], ((0, 0), (0, LANES - N_FOX_HEADS)))
        wf_hi, wf_lo = _split2(wf)
        bf_pad = jnp.pad(b_forget[l], (0, LANES - N_FOX_HEADS)).reshape(1, LANES)

        qd, kd, vdt, qf, kf, vft, qb, kb = _in_proj(
            x2, norm_attn_g[l].reshape(1, d), cos2, sin2, w_stack, wt_stack, wf_hi, wf_lo, bf_pad,
            batch=b, seq=s, tm=512)

        to3 = lambda t: t.reshape(b, s, t.shape[-1])
        lam_params = jnp.stack([lambda_q1[l], lambda_k1[l], lambda_q2[l], lambda_k2[l]])
        od = _diff_attn(to3(qd), to3(kd), vdt, lam_params, diff_norm_g[l].reshape(-1, 1),
                        tq=512, lam_init=lam_init)
        of = _fox_attn(to3(qf), to3(kf), vft, to3(qb), to3(kb), tq=512)

        wo = w_out[l].astype(BF16).reshape(2, -1, d)
        wr = jnp.zeros((d, LANES), F32)
        wr = wr.at[:, :N_GROUPS].set(router_group_w[l]).at[:, 8:8 + N_EXPERTS].set(router_expert_w[l])
        wr_hi, wr_lo = _split2(wr.T)
        br = jnp.zeros((LANES,), F32)
        br = br.at[:N_GROUPS].set(router_group_b[l]).at[8:8 + N_EXPERTS].set(router_expert_b[l]).reshape(LANES, 1)
        x1, hm, comb, gid_rows = _out_proj(x2, od.reshape(n, -1), of.reshape(n, -1), wo, norm_ffn_g[l].reshape(1, d),
                                 wr_hi, wr_lo, br, tm=512)

        x2 = _moe(hm, comb, gid_rows, x1, w_gate[l].astype(BF16), w_up[l].astype(BF16), w_down[l].astype(BF16),
                  norm_final_g.reshape(1, d), tm=1024, apply_final=(l == depth - 1))
    return x2.reshape(b, s, d)
```

```python
import functools

import numpy as np
import jax
import jax.numpy as jnp
from jax import lax
from jax.experimental import pallas as pl
from jax.experimental.pallas import tpu as pltpu

CHUNK = 64
HEAD_DIM = 64
ROPE_THETA = 10000.0
EPS = 1e-6
N_DIFF_HEADS = 4
N_FOX_HEADS = 8
N_GROUPS = 4
EXPERTS_PER_GROUP = 4
N_EXPERTS = N_GROUPS * EXPERTS_PER_GROUP

LANES = 128
NEG_BIG = -1e30
VMEM_LIMIT = 56 * 1024 * 1024
N_BIAS = 6
SUBLANES = 8
GID_LANE = N_EXPERTS
ROW_BLOCK = 128
SORT_ROWS = 512

BF16 = jnp.bfloat16
F32 = jnp.float32
_NT = (((1,), (1,)), ((), ()))


def _dot(a, b):
    return jnp.dot(a, b, preferred_element_type=F32)


def _dot_nt(a, b):
    return lax.dot_general(a, b, _NT, preferred_element_type=F32)


def _split2(x):
    hi = x.astype(BF16)
    lo = (x - hi.astype(F32)).astype(BF16)
    return hi, lo


def _split3(x):
    hi = x.astype(BF16)
    r = x - hi.astype(F32)
    mid = r.astype(BF16)
    lo = (r - mid.astype(F32)).astype(BF16)
    return hi, mid, lo


def _rms(x, g):
    return x * lax.rsqrt(jnp.mean(x * x, axis=-1, keepdims=True) + EPS) * g


def _in_proj_kernel(x_ref, g_ref, cos_ref, sin_ref, w_ref, wt_ref, wf_hi_ref, wf_lo_ref, bf_ref, tri_ref,
                    eq_ref, ek_ref, cq_ref, ck_ref,
                    qd_ref, kd_ref, vdt_ref, qf_ref, kf_ref, vft_ref, qb_ref, kb_ref, carry_ref,
                    *, tiles_per_seq):
    i = pl.program_id(0)
    h = _rms(x_ref[...], g_ref[...])
    hb, h_lo = _split2(h)
    cos = cos_ref[...]
    sin = sin_ref[...]
    scale = HEAD_DIM ** -0.5

    def rope(w_idx, out_ref, mul):
        a = _dot(hb, w_ref[w_idx])
        r = _dot(hb, w_ref[w_idx + 1])
        for c in range(a.shape[1] // LANES):
            sl = slice(c * LANES, (c + 1) * LANES)
            out_ref[:, sl] = ((a[:, sl] * cos + r[:, sl] * sin) * mul).astype(out_ref.dtype)

    rope(0, qd_ref, scale)
    rope(2, kd_ref, 1.0)
    qf_ref[...] = (_dot(hb, w_ref[4]) * scale).astype(qf_ref.dtype)
    kf_ref[...] = _dot(hb, w_ref[5]).astype(kf_ref.dtype)
    vdt_ref[0] = _dot_nt(wt_ref[0], hb).astype(vdt_ref.dtype)
    vft_ref[0] = _dot_nt(wt_ref[1], hb).astype(vft_ref.dtype)

    z = (_dot(hb, wf_hi_ref[...]) + _dot(hb, wf_lo_ref[...]) + _dot(h_lo, wf_hi_ref[...])) + bf_ref[...]
    log_f = jnp.minimum(z, 0.0) - jnp.log1p(jnp.exp(-jnp.abs(z)))
    lane = lax.broadcasted_iota(jnp.int32, log_f.shape, 1)
    log_f = jnp.where(lane < N_FOX_HEADS, log_f, 0.0)

    @pl.when(i % tiles_per_seq == 0)
    def _():
        carry_ref[...] = jnp.zeros_like(carry_ref)

    tri = tri_ref[...]
    p0, p1, p2 = _split3(log_f)
    cum = (_dot(tri, p0) + _dot(tri, p1)) + _dot(tri, p2) + carry_ref[...]
    carry_ref[...] = cum[cum.shape[0] - 1:, :]

    f_parts = _split3(cum)
    qb = cq_ref[...]
    kb = ck_ref[...]
    for part in range(3):
        qb = qb + _dot(f_parts[part], eq_ref[part])
        kb = kb + _dot(f_parts[part], ek_ref[part])
    qb_ref[...] = qb.astype(qb_ref.dtype)
    kb_ref[...] = kb.astype(kb_ref.dtype)


def _bias_placement():
    width = (N_FOX_HEADS // 2) * LANES
    eq = np.zeros((3, LANES, width), np.float32)
    ek = np.zeros((3, LANES, width), np.float32)
    cq = np.zeros((1, width), np.float32)
    ck = np.zeros((1, width), np.float32)
    for h in range(N_FOX_HEADS):
        base = (h // 2) * LANES + (h % 2) * N_BIAS
        for part in range(3):
            eq[part, h, base + part] = 1.0
            ek[part, h, base + 3 + part] = -1.0
        cq[0, base + 3:base + 6] = 1.0
        ck[0, base:base + 3] = 1.0
    return jnp.asarray(eq, BF16), jnp.asarray(ek, BF16), jnp.asarray(cq), jnp.asarray(ck)


def _in_proj(x2, g, cos2, sin2, w_stack, wt_stack, wf_hi, wf_lo, bf_pad, *, batch, seq, tm):
    n, d = x2.shape
    tiles_per_seq = seq // tm
    tri = jnp.tril(jnp.ones((tm, tm), F32)).astype(BF16)
    eq, ek, cq, ck = _bias_placement()
    wcols = w_stack.shape[2]
    const = lambda *shape: pl.BlockSpec(shape, lambda i: (0,) * len(shape))
    row_blk = lambda cols: pl.BlockSpec((tm, cols), lambda i: (i, 0))
    pos_blk = pl.BlockSpec((tm, LANES), lambda i: (i % tiles_per_seq, 0))
    vt_blk = pl.BlockSpec((1, wcols, tm), lambda i: (i // tiles_per_seq, 0, i % tiles_per_seq))
    out_bf = jax.ShapeDtypeStruct((n, wcols), BF16)
    out_vt = jax.ShapeDtypeStruct((batch, wcols, seq), BF16)
    return pl.pallas_call(
        functools.partial(_in_proj_kernel, tiles_per_seq=tiles_per_seq),
        grid=(n // tm,),
        in_specs=[row_blk(d), const(1, d), pos_blk, pos_blk, const(*w_stack.shape), const(*wt_stack.shape),
                  const(d, LANES), const(d, LANES), const(1, LANES), const(tm, tm),
                  const(*eq.shape), const(*ek.shape), const(*cq.shape), const(*ck.shape)],
        out_specs=[row_blk(wcols), row_blk(wcols), vt_blk, row_blk(wcols), row_blk(wcols), vt_blk,
                   row_blk(eq.shape[2]), row_blk(ek.shape[2])],
        out_shape=[out_bf, out_bf, out_vt, out_bf, out_bf, out_vt,
                   jax.ShapeDtypeStruct((n, eq.shape[2]), BF16), jax.ShapeDtypeStruct((n, ek.shape[2]), BF16)],
        scratch_shapes=[pltpu.VMEM((1, LANES), F32)],
        compiler_params=pltpu.CompilerParams(dimension_semantics=("arbitrary",),
                                             vmem_limit_bytes=VMEM_LIMIT),
        name="in_proj",
    )(x2, g, cos2, sin2, w_stack, wt_stack, wf_hi, wf_lo, bf_pad, tri, eq, ek, cq, ck)


def _flash_tiles(i, tq, logits_fn, values_fn, mask, scratch):
    s0, s1, c0, c1, m_ref, l_ref, acc_ref = scratch
    n_maps = m_ref.shape[0]
    m_ref[...] = jnp.full(m_ref.shape, NEG_BIG, F32)
    l_ref[...] = jnp.zeros(l_ref.shape, F32)
    acc_ref[...] = jnp.zeros(acc_ref.shape, F32)

    def stage_a(tile, s_buf, c_buf, masked):
        sts = logits_fn(tile)
        for a in range(n_maps):
            st = jnp.where(mask, sts[a], NEG_BIG) if masked else sts[a]
            s_buf[a] = st
            c_buf[a] = jnp.max(st, axis=0, keepdims=True)

    def stage_b(tile, s_buf, c_buf):
        vts = values_fn(tile)
        for a in range(n_maps):
            m = m_ref[a]
            m_new = jnp.maximum(m, c_buf[a])
            alpha = jnp.exp(m - m_new)
            p = jnp.exp(s_buf[a] - m_new)
            l_ref[a] = alpha * l_ref[a] + jnp.sum(p, axis=0, keepdims=True)
            acc_ref[a] = alpha * acc_ref[a] + _dot(vts[a], p.astype(vts[a].dtype))
            m_ref[a] = m_new

    stage_a(i, s0, c0, True)

    def body(jj, _):
        t = 2 * jj
        stage_a(t, s1, c1, False)
        stage_b(jnp.where(jj == 0, i, t - 1), s0, c0)
        stage_a(t + 1, s0, c0, False)
        stage_b(t, s1, c1)
        return 0

    lax.fori_loop(0, lax.shift_right_logical(i, 1), body, 0)
    odd = (i & 1) == 1

    @pl.when(odd)
    def _():
        stage_a(i - 1, s1, c1, False)
        stage_b(jnp.where(i == 1, i, i - 2), s0, c0)
        stage_b(i - 1, s1, c1)

    @pl.when(jnp.logical_not(odd))
    def _():
        stage_b(jnp.where(i == 0, i, i - 1), s0, c0)


def _flash_scratch(n_maps, dv, tq):
    s_buf = pltpu.VMEM((n_maps, tq, tq), F32)
    c_buf = pltpu.VMEM((n_maps, 1, tq), F32)
    return [s_buf, s_buf, c_buf, c_buf, c_buf, c_buf, pltpu.VMEM((n_maps, dv, tq), F32)]


def _diff_attn_kernel(q_ref, k_ref, vt_ref, lam_ref, g_ref, o_ref, *scratch, tq, lam_init):
    i = pl.program_id(2)
    q = q_ref[0]
    lane = lax.broadcasted_iota(jnp.int32, q.shape, 1)
    zero = jnp.zeros_like(q)
    qs = (jnp.where(lane < HEAD_DIM, q, zero), jnp.where(lane >= HEAD_DIM, q, zero))

    def logits(tile):
        kt = k_ref[0, pl.ds(pl.multiple_of(tile * tq, tq), tq), :]
        return [_dot_nt(kt, qs[a]) for a in range(2)]

    def values(tile):
        vt = vt_ref[0, :, pl.ds(pl.multiple_of(tile * tq, tq), tq)]
        return [vt, vt]

    key = lax.broadcasted_iota(jnp.int32, (tq, tq), 0)
    qry = lax.broadcasted_iota(jnp.int32, (tq, tq), 1)
    _flash_tiles(i, tq, logits, values, (key // CHUNK) <= (qry // CHUNK), scratch)
    l_ref, acc_ref = scratch[5], scratch[6]

    lp = lam_ref[...]
    lam = (jnp.exp(jnp.sum(lp[0:1] * lp[1:2], axis=1, keepdims=True))
           - jnp.exp(jnp.sum(lp[2:3] * lp[3:4], axis=1, keepdims=True)) + lam_init)
    o = acc_ref[0] / l_ref[0] - lam * (acc_ref[1] / l_ref[1])
    y = o * lax.rsqrt(jnp.mean(o * o, axis=0, keepdims=True) + EPS) * g_ref[...] * (1.0 - lam_init)
    o_ref[0] = y.T.astype(o_ref.dtype)


def _diff_attn(qd, kd, vdt, lam_params, gnorm_col, *, tq, lam_init):
    b, s, w = qd.shape
    nh = w // LANES
    q_blk = pl.BlockSpec((1, tq, LANES), lambda bi, h, i: (bi, i, h))
    k_blk = pl.BlockSpec((1, s, LANES), lambda bi, h, i: (bi, 0, h))
    vt_blk = pl.BlockSpec((1, LANES, s), lambda bi, h, i: (bi, h, 0))
    return pl.pallas_call(
        functools.partial(_diff_attn_kernel, tq=tq, lam_init=lam_init),
        grid=(b, nh, s // tq),
        in_specs=[q_blk, k_blk, vt_blk,
                  pl.BlockSpec(lam_params.shape, lambda bi, h, i: (0, 0)),
                  pl.BlockSpec(gnorm_col.shape, lambda bi, h, i: (0, 0))],
        out_specs=q_blk,
        out_shape=jax.ShapeDtypeStruct((b, s, w), BF16),
        scratch_shapes=_flash_scratch(2, LANES, tq),
        compiler_params=pltpu.CompilerParams(dimension_semantics=("arbitrary",) * 3,
                                             vmem_limit_bytes=VMEM_LIMIT),
        name="diff_attn",
    )(qd, kd, vdt, lam_params, gnorm_col)


def _fox_attn_kernel(q_ref, k_ref, vt_ref, qb_ref, kb_ref, o_ref, *scratch, tq):
    i = pl.program_id(2)
    q = q_ref[0]
    qb = qb_ref[0]
    lane = lax.broadcasted_iota(jnp.int32, q.shape, 1)
    zero = jnp.zeros_like(q)
    qs = (jnp.concatenate([jnp.where(lane < HEAD_DIM, q, zero), jnp.where(lane < N_BIAS, qb, zero)], axis=1),
          jnp.concatenate([jnp.where(lane >= HEAD_DIM, q, zero),
                           jnp.where((lane >= N_BIAS) & (lane < 2 * N_BIAS), qb, zero)], axis=1))

    def logits(tile):
        start = pl.multiple_of(tile * tq, tq)
        kt = jnp.concatenate([k_ref[0, pl.ds(start, tq), :], kb_ref[0, pl.ds(start, tq), :]], axis=1)
        return [_dot_nt(kt, qs[a]) for a in range(2)]

    def values(tile):
        vt = vt_ref[0, :, pl.ds(pl.multiple_of(tile * tq, tq), tq)]
        return [vt[:HEAD_DIM], vt[HEAD_DIM:]]

    key = lax.broadcasted_iota(jnp.int32, (tq, tq), 0)
    qry = lax.broadcasted_iota(jnp.int32, (tq, tq), 1)
    _flash_tiles(i, tq, logits, values, key <= qry, scratch)
    l_ref, acc_ref = scratch[5], scratch[6]
    o = jnp.concatenate([acc_ref[0] / l_ref[0], acc_ref[1] / l_ref[1]], axis=0)
    o_ref[0] = o.T.astype(o_ref.dtype)


def _fox_attn(qf, kf, vft, qb, kb, *, tq):
    b, s, w = qf.shape
    npairs = w // LANES
    q_blk = pl.BlockSpec((1, tq, LANES), lambda bi, p, i: (bi, i, p))
    k_blk = pl.BlockSpec((1, s, LANES), lambda bi, p, i: (bi, 0, p))
    vt_blk = pl.BlockSpec((1, LANES, s), lambda bi, p, i: (bi, p, 0))
    return pl.pallas_call(
        functools.partial(_fox_attn_kernel, tq=tq),
        grid=(b, npairs, s // tq),
        in_specs=[q_blk, k_blk, vt_blk, q_blk, k_blk],
        out_specs=q_blk,
        out_shape=jax.ShapeDtypeStruct((b, s, w), BF16),
        scratch_shapes=_flash_scratch(2, HEAD_DIM, tq),
        compiler_params=pltpu.CompilerParams(dimension_semantics=("arbitrary",) * 3,
                                             vmem_limit_bytes=VMEM_LIMIT),
        name="fox_attn",
    )(qf, kf, vft, qb, kb)


def _out_proj_kernel(x_ref, od_ref, of_ref, wo_ref, g_ref, wr_hi_ref, wr_lo_ref, br_ref,
                     x1_ref, hm_ref, comb_ref, gid_ref):
    x1 = x_ref[...] + (_dot(od_ref[...], wo_ref[0]) + _dot(of_ref[...], wo_ref[1]))
    x1_ref[...] = x1
    hm = _rms(x1, g_ref[...])
    hb, h_lo = _split2(hm)
    hm_ref[...] = hb

    wr_hi = wr_hi_ref[...]
    lt = (_dot_nt(wr_hi, hb) + _dot_nt(wr_lo_ref[...], hb) + _dot_nt(wr_hi, h_lo)) + br_ref[...]
    tm = lt.shape[1]
    g8 = lt[0:8]
    r8 = lax.broadcasted_iota(jnp.int32, (8, tm), 0)
    g8 = jnp.where(r8 < N_GROUPS, g8, NEG_BIG)
    gmax = jnp.max(g8, axis=0, keepdims=True)
    gidx = jnp.min(jnp.where(g8 == gmax, r8, N_GROUPS), axis=0, keepdims=True)
    g_w = 1.0 / jnp.sum(jnp.exp(g8 - gmax), axis=0, keepdims=True)

    e16 = lt[8:8 + N_EXPERTS]
    r16 = lax.broadcasted_iota(jnp.int32, (N_EXPERTS, tm), 0)
    in_group = (r16 // EXPERTS_PER_GROUP) == gidx
    e_sel = jnp.where(in_group, e16, NEG_BIG)
    top1 = jnp.max(e_sel, axis=0, keepdims=True)
    id1 = jnp.min(jnp.where(e_sel == top1, r16, N_EXPERTS), axis=0, keepdims=True)
    e_rest = jnp.where(r16 == id1, NEG_BIG, e_sel)
    top2 = jnp.max(e_rest, axis=0, keepdims=True)
    id2 = jnp.min(jnp.where(e_rest == top2, r16, N_EXPERTS), axis=0, keepdims=True)
    t = jnp.exp(top2 - top1)
    w1 = g_w / (1.0 + t)
    w2 = w1 * t
    r128 = lax.broadcasted_iota(jnp.int32, (LANES, tm), 0)
    gid = gidx.astype(F32)
    comb_t = (jnp.where(r128 == id1, w1, 0.0) + jnp.where(r128 == id2, w2, 0.0)
              + jnp.where(r128 == GID_LANE, gid, 0.0))
    comb_ref[...] = comb_t.T
    gid_ref[...] = jnp.where(r8 == 0, gid, 0.0)


def _out_proj(x2, od, of, wo, g, wr_hi, wr_lo, br, *, tm):
    n, d = x2.shape
    const = lambda *shape: pl.BlockSpec(shape, lambda i: (0,) * len(shape))
    row_blk = lambda cols: pl.BlockSpec((tm, cols), lambda i: (i, 0))
    return pl.pallas_call(
        _out_proj_kernel,
        grid=(n // tm,),
        in_specs=[row_blk(d), row_blk(od.shape[1]), row_blk(of.shape[1]), const(*wo.shape), const(1, d),
                  const(*wr_hi.shape), const(*wr_lo.shape), const(*br.shape)],
        out_specs=[row_blk(d), row_blk(d), row_blk(LANES), pl.BlockSpec((SUBLANES, tm), lambda i: (0, i))],
        out_shape=[jax.ShapeDtypeStruct((n, d), F32), jax.ShapeDtypeStruct((n, d), BF16),
                   jax.ShapeDtypeStruct((n, LANES), F32), jax.ShapeDtypeStruct((SUBLANES, n), F32)],
        compiler_params=pltpu.CompilerParams(dimension_semantics=("arbitrary",),
                                             vmem_limit_bytes=VMEM_LIMIT),
        name="out_proj",
    )(x2, od, of, wo, g, wr_hi, wr_lo, br)


def _moe_kernel(hm_ref, comb_ref, gidr_ref, x1_ref, wg_ref, wu_ref, wd_ref, gf_ref, o_ref,
                tri_ref, hs_ref, cs_ref, ys_ref, posc_ref, tab_ref, *, apply_final):
    i = pl.program_id(0)
    g = pl.program_id(1)
    t = hm_ref.shape[0]
    t_pad = hs_ref.shape[0]
    rb = ROW_BLOCK

    @pl.when((i == 0) & (g == 0))
    def _():
        row = lax.broadcasted_iota(jnp.int32, (t, t), 0)
        col = lax.broadcasted_iota(jnp.int32, (t, t), 1)
        tri_ref[...] = jnp.where(row > col, 1.0, 0.0).astype(BF16)

    @pl.when(g == 0)
    def _():
        comb = comb_ref[...]
        lane = lax.broadcasted_iota(jnp.int32, comb.shape, 1)
        onehot_c = jnp.where(lane == comb[:, GID_LANE:GID_LANE + 1].astype(jnp.int32), 1.0, 0.0)
        sub = lax.broadcasted_iota(jnp.int32, (SUBLANES, t), 0)
        onehot_r = jnp.where(sub == gidr_ref[0:1, :].astype(jnp.int32), 1.0, 0.0)
        tri = tri_ref[...]
        rank_c = _dot(tri, onehot_c.astype(BF16))
        rank_r = _dot_nt(onehot_r.astype(BF16), tri)
        lane1 = lax.broadcasted_iota(jnp.int32, (1, LANES), 1)
        sub1 = lax.broadcasted_iota(jnp.int32, (SUBLANES, 1), 0)
        start_c = jnp.zeros((1, LANES), F32)
        start_r = jnp.zeros((SUBLANES, 1), F32)
        first = jnp.int32(0)
        for k in range(N_GROUPS):
            cnt = jnp.sum(onehot_r[k:k + 1, :]).astype(jnp.int32)
            nblk = lax.shift_right_logical(cnt + (rb - 1), int(np.log2(rb)))
            tab_ref[k] = first
            tab_ref[N_GROUPS + k] = nblk
            row0 = (first * rb).astype(F32)
            start_c = jnp.where(lane1 == k, row0, start_c)
            start_r = jnp.where(sub1 == k, row0, start_r)
            first = first + nblk
        posc_ref[...] = jnp.sum(onehot_c * (rank_c + start_c), axis=1, keepdims=True).astype(jnp.int32)
        pos_r = jnp.sum(onehot_r * (rank_r + start_r), axis=0, keepdims=True).astype(jnp.int32)
        ys_ref[...] = jnp.zeros_like(ys_ref)
        c_parts = _split3(jnp.where(lane < N_EXPERTS, comb, 0.0))
        c_pack = (c_parts[0].astype(F32) + pltpu.roll(c_parts[1].astype(F32), N_EXPERTS, axis=1)
                  + pltpu.roll(c_parts[2].astype(F32), 2 * N_EXPERTS, axis=1)).astype(BF16)
        hm = hm_ref[...]
        for r0 in range(0, t_pad, SORT_ROWS):
            dst = lax.broadcasted_iota(jnp.int32, (SORT_ROWS, t), 0) + r0
            perm = jnp.where(dst == pos_r, 1.0, 0.0).astype(BF16)
            hs_ref[r0:r0 + SORT_ROWS, :] = _dot(perm, hm).astype(hs_ref.dtype)
            cp = _dot(perm, c_pack)
            cs_ref[r0:r0 + SORT_ROWS, :] = (cp + pltpu.roll(cp, LANES - N_EXPERTS, axis=1)
                                            + pltpu.roll(cp, LANES - 2 * N_EXPERTS, axis=1))

    b0 = tab_ref[g]

    def expert_block(b, _):
        off = pl.multiple_of(b * rb, rb)
        rows = hs_ref[pl.ds(off, rb), :]
        cblk = cs_ref[pl.ds(off, rb), :]
        lane = lax.broadcasted_iota(jnp.int32, cblk.shape, 1)
        gates = [_dot(rows, wg_ref[j]) for j in range(EXPERTS_PER_GROUP)]
        ups = [_dot(rows, wu_ref[j]) for j in range(EXPERTS_PER_GROUP)]
        y = None
        for j in range(EXPERTS_PER_GROUP):
            c = jnp.sum(jnp.where(lane == g * EXPERTS_PER_GROUP + j, cblk, 0.0), axis=1, keepdims=True)
            a = gates[j] * jax.nn.sigmoid(gates[j]) * ups[j] * c
            d = _dot(a.astype(BF16), wd_ref[j])
            y = d if y is None else y + d
        ys_ref[pl.ds(off, rb), :] = y.astype(ys_ref.dtype)
        return 0

    lax.fori_loop(b0, b0 + tab_ref[N_GROUPS + g], expert_block, 0)

    @pl.when(g == pl.num_programs(1) - 1)
    def _():
        ys = ys_ref[...]
        for r0 in range(0, t, SORT_ROWS):
            src = lax.broadcasted_iota(jnp.int32, (SORT_ROWS, t_pad), 1)
            unperm = jnp.where(src == posc_ref[r0:r0 + SORT_ROWS, :], 1.0, 0.0).astype(BF16)
            x2 = x1_ref[r0:r0 + SORT_ROWS, :] + _dot(unperm, ys)
            o_ref[r0:r0 + SORT_ROWS, :] = _rms(x2, gf_ref[...]) if apply_final else x2


def _moe(hm, comb, gid_rows, x1, wg, wu, wd, gfin, *, tm, apply_final):
    n, d = x1.shape
    ne, _, de = wg.shape
    epg = EXPERTS_PER_GROUP
    t_pad = tm + N_GROUPS * ROW_BLOCK
    row_blk = lambda cols, **kw: pl.BlockSpec((tm, cols), lambda i, g: (i, 0), **kw)
    once = dict(pipeline_mode=pl.Buffered(1))
    return pl.pallas_call(
        functools.partial(_moe_kernel, apply_final=apply_final),
        grid=(n // tm, ne // epg),
        in_specs=[row_blk(d), row_blk(LANES), pl.BlockSpec((SUBLANES, tm), lambda i, g: (0, i)), row_blk(d, **once),
                  pl.BlockSpec((epg, d, de), lambda i, g: (g, 0, 0)),
                  pl.BlockSpec((epg, d, de), lambda i, g: (g, 0, 0)),
                  pl.BlockSpec((epg, de, d), lambda i, g: (g, 0, 0)),
                  pl.BlockSpec((1, d), lambda i, g: (0, 0))],
        out_specs=row_blk(d, **once),
        out_shape=jax.ShapeDtypeStruct((n, d), F32),
        scratch_shapes=[pltpu.VMEM((tm, tm), BF16), pltpu.VMEM((t_pad, d), BF16), pltpu.VMEM((t_pad, LANES), F32),
                        pltpu.VMEM((t_pad, d), BF16), pltpu.VMEM((tm, 1), jnp.int32),
                        pltpu.SMEM((2 * N_GROUPS,), jnp.int32)],
        compiler_params=pltpu.CompilerParams(dimension_semantics=("arbitrary", "arbitrary"),
                                             vmem_limit_bytes=VMEM_LIMIT),
        name="moe",
    )(hm, comb, gid_rows, x1, wg, wu, wd, gfin)


def _rot_cols(w):
    d, c = w.shape
    w4 = w.reshape(d, c // HEAD_DIM, 2, HEAD_DIM // 2)
    return jnp.stack([-w4[:, :, 1], w4[:, :, 0]], axis=2).reshape(d, c)


def kernel(x, norm_attn_g, w_in, b_forget, lambda_q1, lambda_k1, lambda_q2, lambda_k2, diff_norm_g, w_out,
           norm_ffn_g, router_group_w, router_group_b, router_expert_w, router_expert_b, w_gate, w_up, w_down,
           norm_final_g):
    b, s, d = x.shape
    depth = w_in.shape[0]
    n = b * s
    diff_w = N_DIFF_HEADS * 2 * HEAD_DIM
    fox_w = N_FOX_HEADS * HEAD_DIM

    pos = jnp.arange(s, dtype=F32)
    inv_freq = 1.0 / (ROPE_THETA ** (jnp.arange(0, HEAD_DIM, 2, dtype=F32) / HEAD_DIM))
    ang = pos[:, None] * inv_freq[None, :]
    ang = jnp.concatenate([ang, ang, ang, ang], axis=-1)
    cos2, sin2 = jnp.cos(ang), jnp.sin(ang)

    x2 = x.reshape(n, d)
    for l in range(depth):
        lam_init = 0.8 - 0.6 * float(np.exp(-0.3 * l))
        w = w_in[l]
        offs = np.cumsum([0, diff_w, diff_w, diff_w, fox_w, fox_w, fox_w, N_FOX_HEADS])
        seg = [w[:, offs[k]:offs[k + 1]] for k in range(7)]
        w_stack = jnp.stack([seg[0], _rot_cols(seg[0]), seg[1], _rot_cols(seg[1]), seg[3], seg[4]]).astype(BF16)
        wt_stack = jnp.stack([seg[2].T, seg[5].T]).astype(BF16)
        wf = jnp.pad(seg[6], ((0, 0), (0, LANES - N_FOX_HEADS)))
        wf_hi, wf_lo = _split2(wf)
        bf_pad = jnp.pad(b_forget[l], (0, LANES - N_FOX_HEADS)).reshape(1, LANES)

        qd, kd, vdt, qf, kf, vft, qb, kb = _in_proj(
            x2, norm_attn_g[l].reshape(1, d), cos2, sin2, w_stack, wt_stack, wf_hi, wf_lo, bf_pad,
            batch=b, seq=s, tm=512)

        to3 = lambda t: t.reshape(b, s, t.shape[-1])
        lam_params = jnp.stack([lambda_q1[l], lambda_k1[l], lambda_q2[l], lambda_k2[l]])
        od = _diff_attn(to3(qd), to3(kd), vdt, lam_params, diff_norm_g[l].reshape(-1, 1),
                        tq=512, lam_init=lam_init)
        of = _fox_attn(to3(qf), to3(kf), vft, to3(qb), to3(kb), tq=512)

        wo = w_out[l].astype(BF16).reshape(2, -1, d)
        wr = jnp.zeros((d, LANES), F32)
        wr = wr.at[:, :N_GROUPS].set(router_group_w[l]).at[:, 8:8 + N_EXPERTS].set(router_expert_w[l])
        wr_hi, wr_lo = _split2(wr.T)
        br = jnp.zeros((LANES,), F32)
        br = br.at[:N_GROUPS].set(router_group_b[l]).at[8:8 + N_EXPERTS].set(router_expert_b[l]).reshape(LANES, 1)
        x1, hm, comb, gid_rows = _out_proj(x2, od.reshape(n, -1), of.reshape(n, -1), wo, norm_ffn_g[l].reshape(1, d),
                                 wr_hi, wr_lo, br, tm=512)

        x2 = _moe(hm, comb, gid_rows, x1, w_gate[l].astype(BF16), w_up[l].astype(BF16), w_down[l].astype(BF16),
                  norm_final_g.reshape(1, d), tm=1024, apply_final=(l == depth - 1))
    return x2.reshape(b, s, d)
```

```python
import functools

import numpy as np
import jax
import jax.numpy as jnp
from jax import lax
from jax.experimental import pallas as pl
from jax.experimental.pallas import tpu as pltpu

CHUNK = 64
HEAD_DIM = 64
ROPE_THETA = 10000.0
EPS = 1e-6
N_DIFF_HEADS = 4
N_FOX_HEADS = 8
N_GROUPS = 4
EXPERTS_PER_GROUP = 4
N_EXPERTS = N_GROUPS * EXPERTS_PER_GROUP

LANES = 128
NEG_BIG = -1e30
VMEM_LIMIT = 56 * 1024 * 1024
N_BIAS = 6
SUBLANES = 8
GID_LANE = N_EXPERTS
ROW_BLOCK = 128
ONES_ROWS = 16
BLOCKS_PER_STEP = 2
SORT_ROWS = 512

BF16 = jnp.bfloat16
F32 = jnp.float32
_NT = (((1,), (1,)), ((), ()))


def _dot(a, b):
    return jnp.dot(a, b, preferred_element_type=F32)


def _dot_nt(a, b):
    return lax.dot_general(a, b, _NT, preferred_element_type=F32)


def _split2(x):
    hi = x.astype(BF16)
    lo = (x - hi.astype(F32)).astype(BF16)
    return hi, lo


def _split3(x):
    hi = x.astype(BF16)
    r = x - hi.astype(F32)
    mid = r.astype(BF16)
    lo = (r - mid.astype(F32)).astype(BF16)
    return hi, mid, lo


def _rms(x, g):
    return x * lax.rsqrt(jnp.mean(x * x, axis=-1, keepdims=True) + EPS) * g


def _in_proj_kernel(x_ref, g_ref, cos_ref, sin_ref, w_ref, wt_ref, wf_hi_ref, wf_lo_ref, bf_ref, tri_ref,
                    eq_ref, ek_ref, cq_ref, ck_ref,
                    qd_ref, kd_ref, vdt_ref, qf_ref, kf_ref, vft_ref, qb_ref, kb_ref, carry_ref,
                    *, tiles_per_seq):
    i = pl.program_id(0)
    h = _rms(x_ref[...], g_ref[...])
    hb, h_lo = _split2(h)
    cos = cos_ref[...]
    sin = sin_ref[...]
    scale = HEAD_DIM ** -0.5

    def rope(w_idx, out_ref, mul):
        a = _dot(hb, w_ref[w_idx])
        r = _dot(hb, w_ref[w_idx + 1])
        for c in range(a.shape[1] // LANES):
            sl = slice(c * LANES, (c + 1) * LANES)
            out_ref[:, sl] = ((a[:, sl] * cos + r[:, sl] * sin) * mul).astype(out_ref.dtype)

    rope(0, qd_ref, scale)
    rope(2, kd_ref, 1.0)
    qf_ref[...] = (_dot(hb, w_ref[4]) * scale).astype(qf_ref.dtype)
    kf_ref[...] = _dot(hb, w_ref[5]).astype(kf_ref.dtype)
    vdt_ref[0] = _dot_nt(wt_ref[0], hb).astype(vdt_ref.dtype)
    vft_ref[0] = _dot_nt(wt_ref[1], hb).astype(vft_ref.dtype)

    z = (_dot(hb, wf_hi_ref[...]) + _dot(hb, wf_lo_ref[...]) + _dot(h_lo, wf_hi_ref[...])) + bf_ref[...]
    log_f = jnp.minimum(z, 0.0) - jnp.log1p(jnp.exp(-jnp.abs(z)))
    lane = lax.broadcasted_iota(jnp.int32, log_f.shape, 1)
    log_f = jnp.where(lane < N_FOX_HEADS, log_f, 0.0)

    @pl.when(i % tiles_per_seq == 0)
    def _():
        carry_ref[...] = jnp.zeros_like(carry_ref)

    tri = tri_ref[...]
    p0, p1, p2 = _split3(log_f)
    cum = (_dot(tri, p0) + _dot(tri, p1)) + _dot(tri, p2) + carry_ref[...]
    carry_ref[...] = cum[cum.shape[0] - 1:, :]

    f_parts = _split3(cum)
    qb = cq_ref[...]
    kb = ck_ref[...]
    for part in range(3):
        qb = qb + _dot(f_parts[part], eq_ref[part])
        kb = kb + _dot(f_parts[part], ek_ref[part])
    qb_ref[...] = qb.astype(qb_ref.dtype)
    kb_ref[...] = kb.astype(kb_ref.dtype)


def _bias_placement():
    width = (N_FOX_HEADS // 2) * LANES
    eq = np.zeros((3, LANES, width), np.float32)
    ek = np.zeros((3, LANES, width), np.float32)
    cq = np.zeros((1, width), np.float32)
    ck = np.zeros((1, width), np.float32)
    for h in range(N_FOX_HEADS):
        base = (h // 2) * LANES + (h % 2) * N_BIAS
        for part in range(3):
            eq[part, h, base + part] = 1.0
            ek[part, h, base + 3 + part] = -1.0
        cq[0, base + 3:base + 6] = 1.0
        ck[0, base:base + 3] = 1.0
    return jnp.asarray(eq, BF16), jnp.asarray(ek, BF16), jnp.asarray(cq), jnp.asarray(ck)


def _in_proj(x2, g, cos2, sin2, w_stack, wt_stack, wf_hi, wf_lo, bf_pad, *, batch, seq, tm):
    n, d = x2.shape
    tiles_per_seq = seq // tm
    tri = jnp.tril(jnp.ones((tm, tm), F32)).astype(BF16)
    eq, ek, cq, ck = _bias_placement()
    wcols = w_stack.shape[2]
    const = lambda *shape: pl.BlockSpec(shape, lambda i: (0,) * len(shape))
    row_blk = lambda cols: pl.BlockSpec((tm, cols), lambda i: (i, 0))
    pos_blk = pl.BlockSpec((tm, LANES), lambda i: (i % tiles_per_seq, 0))
    vt_blk = pl.BlockSpec((1, wcols, tm), lambda i: (i // tiles_per_seq, 0, i % tiles_per_seq))
    out_bf = jax.ShapeDtypeStruct((n, wcols), BF16)
    out_vt = jax.ShapeDtypeStruct((batch, wcols, seq), BF16)
    return pl.pallas_call(
        functools.partial(_in_proj_kernel, tiles_per_seq=tiles_per_seq),
        grid=(n // tm,),
        in_specs=[row_blk(d), const(1, d), pos_blk, pos_blk, const(*w_stack.shape), const(*wt_stack.shape),
                  const(d, LANES), const(d, LANES), const(1, LANES), const(tm, tm),
                  const(*eq.shape), const(*ek.shape), const(*cq.shape), const(*ck.shape)],
        out_specs=[row_blk(wcols), row_blk(wcols), vt_blk, row_blk(wcols), row_blk(wcols), vt_blk,
                   row_blk(eq.shape[2]), row_blk(ek.shape[2])],
        out_shape=[out_bf, out_bf, out_vt, out_bf, out_bf, out_vt,
                   jax.ShapeDtypeStruct((n, eq.shape[2]), BF16), jax.ShapeDtypeStruct((n, ek.shape[2]), BF16)],
        scratch_shapes=[pltpu.VMEM((1, LANES), F32)],
        compiler_params=pltpu.CompilerParams(dimension_semantics=("arbitrary",),
                                             vmem_limit_bytes=VMEM_LIMIT),
        name="in_proj",
    )(x2, g, cos2, sin2, w_stack, wt_stack, wf_hi, wf_lo, bf_pad, tri, eq, ek, cq, ck)


def _flash_tiles(i, tq, logits_fn, values_fn, mask, scratch):
    s0, s1, c0, c1, m_ref, acc_ref = scratch
    n_maps = m_ref.shape[0]
    m_ref[...] = jnp.full(m_ref.shape, NEG_BIG, F32)
    acc_ref[...] = jnp.zeros(acc_ref.shape, F32)
    ones = jnp.ones((ONES_ROWS, tq), BF16)

    def stage_a(tile, s_buf, c_buf, masked):
        sts = logits_fn(tile)
        for a in range(n_maps):
            st = jnp.where(mask, sts[a], NEG_BIG) if masked else sts[a]
            s_buf[a] = st
            c_buf[a] = jnp.max(st, axis=0, keepdims=True)

    def stage_b(tile, s_buf, c_buf):
        vts = values_fn(tile)
        for a in range(n_maps):
            m = m_ref[a]
            m_new = jnp.maximum(m, c_buf[a])
            alpha = jnp.exp(m - m_new)
            p = jnp.exp(s_buf[a] - m_new).astype(BF16)
            vt = jnp.concatenate([vts[a], ones], axis=0)
            acc_ref[a] = alpha * acc_ref[a] + _dot(vt, p)
            m_ref[a] = m_new

    stage_a(i, s0, c0, True)

    def body(jj, _):
        t = 2 * jj
        stage_a(t, s1, c1, False)
        stage_b(jnp.where(jj == 0, i, t - 1), s0, c0)
        stage_a(t + 1, s0, c0, False)
        stage_b(t, s1, c1)
        return 0

    lax.fori_loop(0, lax.shift_right_logical(i, 1), body, 0)
    odd = (i & 1) == 1

    @pl.when(odd)
    def _():
        stage_a(i - 1, s1, c1, False)
        stage_b(jnp.where(i == 1, i, i - 2), s0, c0)
        stage_b(i - 1, s1, c1)

    @pl.when(jnp.logical_not(odd))
    def _():
        stage_b(jnp.where(i == 0, i, i - 1), s0, c0)


def _flash_scratch(n_maps, dv, tq):
    s_buf = pltpu.VMEM((n_maps, tq, tq), F32)
    c_buf = pltpu.VMEM((n_maps, 1, tq), F32)
    return [s_buf, s_buf, c_buf, c_buf, c_buf, pltpu.VMEM((n_maps, dv + ONES_ROWS, tq), F32)]


def _flash_outputs(acc_ref, dv):
    return [acc_ref[a, :dv, :] / acc_ref[a, dv:dv + 1, :] for a in range(acc_ref.shape[0])]


def _head_blocks(ref_block):
    return [ref_block[:, k * LANES:(k + 1) * LANES] for k in range(BLOCKS_PER_STEP)]


def _diff_attn_kernel(q_ref, k_ref, vt_ref, lam_ref, g_ref, o_ref, *scratch, tq, lam_init):
    i = pl.program_id(2)
    qs = []
    for q in _head_blocks(q_ref[0]):
        lane = lax.broadcasted_iota(jnp.int32, q.shape, 1)
        zero = jnp.zeros_like(q)
        qs += [jnp.where(lane < HEAD_DIM, q, zero), jnp.where(lane >= HEAD_DIM, q, zero)]

    def logits(tile):
        kts = _head_blocks(k_ref[0, pl.ds(pl.multiple_of(tile * tq, tq), tq), :])
        return [_dot_nt(kts[a // 2], qs[a]) for a in range(len(qs))]

    def values(tile):
        vt = vt_ref[0, :, pl.ds(pl.multiple_of(tile * tq, tq), tq)]
        return [vt[(a // 2) * LANES:(a // 2 + 1) * LANES] for a in range(len(qs))]

    key = lax.broadcasted_iota(jnp.int32, (tq, tq), 0)
    qry = lax.broadcasted_iota(jnp.int32, (tq, tq), 1)
    _flash_tiles(i, tq, logits, values, (key // CHUNK) <= (qry // CHUNK), scratch)
    outs = _flash_outputs(scratch[-1], LANES)

    lp = lam_ref[...]
    lam = (jnp.exp(jnp.sum(lp[0:1] * lp[1:2], axis=1, keepdims=True))
           - jnp.exp(jnp.sum(lp[2:3] * lp[3:4], axis=1, keepdims=True)) + lam_init)
    for k in range(BLOCKS_PER_STEP):
        o = outs[2 * k] - lam * outs[2 * k + 1]
        y = o * lax.rsqrt(jnp.mean(o * o, axis=0, keepdims=True) + EPS) * g_ref[...] * (1.0 - lam_init)
        o_ref[0, :, k * LANES:(k + 1) * LANES] = y.T.astype(o_ref.dtype)


def _attn_specs(s, tq):
    w = BLOCKS_PER_STEP * LANES
    q_blk = pl.BlockSpec((1, tq, w), lambda bi, h, i: (bi, i, h))
    k_blk = pl.BlockSpec((1, s, w), lambda bi, h, i: (bi, 0, h))
    vt_blk = pl.BlockSpec((1, w, s), lambda bi, h, i: (bi, h, 0))
    return q_blk, k_blk, vt_blk


def _diff_attn(qd, kd, vdt, lam_params, gnorm_col, *, tq, lam_init):
    b, s, w = qd.shape
    q_blk, k_blk, vt_blk = _attn_specs(s, tq)
    return pl.pallas_call(
        functools.partial(_diff_attn_kernel, tq=tq, lam_init=lam_init),
        grid=(b, w // (BLOCKS_PER_STEP * LANES), s // tq),
        in_specs=[q_blk, k_blk, vt_blk,
                  pl.BlockSpec(lam_params.shape, lambda bi, h, i: (0, 0)),
                  pl.BlockSpec(gnorm_col.shape, lambda bi, h, i: (0, 0))],
        out_specs=q_blk,
        out_shape=jax.ShapeDtypeStruct((b, s, w), BF16),
        scratch_shapes=_flash_scratch(2 * BLOCKS_PER_STEP, LANES, tq),
        compiler_params=pltpu.CompilerParams(dimension_semantics=("arbitrary",) * 3,
                                             vmem_limit_bytes=VMEM_LIMIT),
        name="diff_attn",
    )(qd, kd, vdt, lam_params, gnorm_col)


def _fox_attn_kernel(q_ref, k_ref, vt_ref, qb_ref, kb_ref, o_ref, *scratch, tq):
    i = pl.program_id(2)
    qs = []
    for q, qb in zip(_head_blocks(q_ref[0]), _head_blocks(qb_ref[0])):
        lane = lax.broadcasted_iota(jnp.int32, q.shape, 1)
        zero = jnp.zeros_like(q)
        qs += [jnp.concatenate([jnp.where(lane < HEAD_DIM, q, zero), jnp.where(lane < N_BIAS, qb, zero)], axis=1),
               jnp.concatenate([jnp.where(lane >= HEAD_DIM, q, zero),
                                jnp.where((lane >= N_BIAS) & (lane < 2 * N_BIAS), qb, zero)], axis=1)]

    def logits(tile):
        start = pl.multiple_of(tile * tq, tq)
        kts = [jnp.concatenate([k, kb], axis=1) for k, kb in zip(_head_blocks(k_ref[0, pl.ds(start, tq), :]),
                                                                   _head_blocks(kb_ref[0, pl.ds(start, tq), :]))]
        return [_dot_nt(kts[a // 2], qs[a]) for a in range(len(qs))]

    def values(tile):
        vt = vt_ref[0, :, pl.ds(pl.multiple_of(tile * tq, tq), tq)]
        return [vt[a * HEAD_DIM:(a + 1) * HEAD_DIM] for a in range(len(qs))]

    key = lax.broadcasted_iota(jnp.int32, (tq, tq), 0)
    qry = lax.broadcasted_iota(jnp.int32, (tq, tq), 1)
    _flash_tiles(i, tq, logits, values, key <= qry, scratch)
    o = jnp.concatenate(_flash_outputs(scratch[-1], HEAD_DIM), axis=0)
    o_ref[0] = o.T.astype(o_ref.dtype)


def _fox_attn(qf, kf, vft, qb, kb, *, tq):
    b, s, w = qf.shape
    q_blk, k_blk, vt_blk = _attn_specs(s, tq)
    return pl.pallas_call(
        functools.partial(_fox_attn_kernel, tq=tq),
        grid=(b, w // (BLOCKS_PER_STEP * LANES), s // tq),
        in_specs=[q_blk, k_blk, vt_blk, q_blk, k_blk],
        out_specs=q_blk,
        out_shape=jax.ShapeDtypeStruct((b, s, w), BF16),
        scratch_shapes=_flash_scratch(2 * BLOCKS_PER_STEP, HEAD_DIM, tq),
        compiler_params=pltpu.CompilerParams(dimension_semantics=("arbitrary",) * 3,
                                             vmem_limit_bytes=VMEM_LIMIT),
        name="fox_attn",
    )(qf, kf, vft, qb, kb)


def _out_proj_kernel(x_ref, od_ref, of_ref, wo_ref, g_ref, wr_hi_ref, wr_lo_ref, br_ref,
                     x1_ref, hm_ref, comb_ref, gid_ref):
    x1 = x_ref[...] + (_dot(od_ref[...], wo_ref[0]) + _dot(of_ref[...], wo_ref[1]))
    x1_ref[...] = x1
    hm = _rms(x1, g_ref[...])
    hb, h_lo = _split2(hm)
    hm_ref[...] = hb

    wr_hi = wr_hi_ref[...]
    lt = (_dot_nt(wr_hi, hb) + _dot_nt(wr_lo_ref[...], hb) + _dot_nt(wr_hi, h_lo)) + br_ref[...]
    tm = lt.shape[1]
    g8 = lt[0:8]
    r8 = lax.broadcasted_iota(jnp.int32, (8, tm), 0)
    g8 = jnp.where(r8 < N_GROUPS, g8, NEG_BIG)
    gmax = jnp.max(g8, axis=0, keepdims=True)
    gidx = jnp.min(jnp.where(g8 == gmax, r8, N_GROUPS), axis=0, keepdims=True)
    g_w = 1.0 / jnp.sum(jnp.exp(g8 - gmax), axis=0, keepdims=True)

    e16 = lt[8:8 + N_EXPERTS]
    r16 = lax.broadcasted_iota(jnp.int32, (N_EXPERTS, tm), 0)
    in_group = (r16 // EXPERTS_PER_GROUP) == gidx
    e_sel = jnp.where(in_group, e16, NEG_BIG)
    top1 = jnp.max(e_sel, axis=0, keepdims=True)
    id1 = jnp.min(jnp.where(e_sel == top1, r16, N_EXPERTS), axis=0, keepdims=True)
    e_rest = jnp.where(r16 == id1, NEG_BIG, e_sel)
    top2 = jnp.max(e_rest, axis=0, keepdims=True)
    id2 = jnp.min(jnp.where(e_rest == top2, r16, N_EXPERTS), axis=0, keepdims=True)
    t = jnp.exp(top2 - top1)
    w1 = g_w / (1.0 + t)
    w2 = w1 * t
    r128 = lax.broadcasted_iota(jnp.int32, (LANES, tm), 0)
    gid = gidx.astype(F32)
    comb_t = (jnp.where(r128 == id1, w1, 0.0) + jnp.where(r128 == id2, w2, 0.0)
              + jnp.where(r128 == GID_LANE, gid, 0.0))
    comb_ref[...] = comb_t.T
    gid_ref[...] = jnp.where(r8 == 0, gid, 0.0)


def _out_proj(x2, od, of, wo, g, wr_hi, wr_lo, br, *, tm):
    n, d = x2.shape
    const = lambda *shape: pl.BlockSpec(shape, lambda i: (0,) * len(shape))
    row_blk = lambda cols: pl.BlockSpec((tm, cols), lambda i: (i, 0))
    return pl.pallas_call(
        _out_proj_kernel,
        grid=(n // tm,),
        in_specs=[row_blk(d), row_blk(od.shape[1]), row_blk(of.shape[1]), const(*wo.shape), const(1, d),
                  const(*wr_hi.shape), const(*wr_lo.shape), const(*br.shape)],
        out_specs=[row_blk(d), row_blk(d), row_blk(LANES), pl.BlockSpec((SUBLANES, tm), lambda i: (0, i))],
        out_shape=[jax.ShapeDtypeStruct((n, d), F32), jax.ShapeDtypeStruct((n, d), BF16),
                   jax.ShapeDtypeStruct((n, LANES), F32), jax.ShapeDtypeStruct((SUBLANES, n), F32)],
        compiler_params=pltpu.CompilerParams(dimension_semantics=("arbitrary",),
                                             vmem_limit_bytes=VMEM_LIMIT),
        name="out_proj",
    )(x2, od, of, wo, g, wr_hi, wr_lo, br)


def _moe_kernel(hm_ref, comb_ref, gidr_ref, x1_ref, wg_ref, wu_ref, wd_ref, gf_ref, o_ref,
                tri_ref, hs_ref, cs_ref, ys_ref, posc_ref, tab_ref, *, apply_final):
    i = pl.program_id(0)
    g = pl.program_id(1)
    t = hm_ref.shape[0]
    t_pad = hs_ref.shape[0]
    rb = ROW_BLOCK

    @pl.when((i == 0) & (g == 0))
    def _():
        row = lax.broadcasted_iota(jnp.int32, (t, t), 0)
        col = lax.broadcasted_iota(jnp.int32, (t, t), 1)
        tri_ref[...] = jnp.where(row > col, 1.0, 0.0).astype(BF16)

    @pl.when(g == 0)
    def _():
        comb = comb_ref[...]
        lane = lax.broadcasted_iota(jnp.int32, comb.shape, 1)
        onehot_c = jnp.where(lane == comb[:, GID_LANE:GID_LANE + 1].astype(jnp.int32), 1.0, 0.0)
        sub = lax.broadcasted_iota(jnp.int32, (SUBLANES, t), 0)
        onehot_r = jnp.where(sub == gidr_ref[0:1, :].astype(jnp.int32), 1.0, 0.0)
        tri = tri_ref[...]
        rank_c = _dot(tri, onehot_c.astype(BF16))
        rank_r = _dot_nt(onehot_r.astype(BF16), tri)
        lane1 = lax.broadcasted_iota(jnp.int32, (1, LANES), 1)
        sub1 = lax.broadcasted_iota(jnp.int32, (SUBLANES, 1), 0)
        start_c = jnp.zeros((1, LANES), F32)
        start_r = jnp.zeros((SUBLANES, 1), F32)
        first = jnp.int32(0)
        for k in range(N_GROUPS):
            cnt = jnp.sum(onehot_r[k:k + 1, :]).astype(jnp.int32)
            nblk = lax.shift_right_logical(cnt + (rb - 1), int(np.log2(rb)))
            tab_ref[k] = first
            tab_ref[N_GROUPS + k] = nblk
            row0 = (first * rb).astype(F32)
            start_c = jnp.where(lane1 == k, row0, start_c)
            start_r = jnp.where(sub1 == k, row0, start_r)
            first = first + nblk
        posc_ref[...] = jnp.sum(onehot_c * (rank_c + start_c), axis=1, keepdims=True).astype(jnp.int32)
        pos_r = jnp.sum(onehot_r * (rank_r + start_r), axis=0, keepdims=True).astype(jnp.int32)
        ys_ref[...] = jnp.zeros_like(ys_ref)
        c_parts = _split3(jnp.where(lane < N_EXPERTS, comb, 0.0))
        c_pack = (c_parts[0].astype(F32) + pltpu.roll(c_parts[1].astype(F32), N_EXPERTS, axis=1)
                  + pltpu.roll(c_parts[2].astype(F32), 2 * N_EXPERTS, axis=1)).astype(BF16)
        hm = hm_ref[...]
        for r0 in range(0, t_pad, SORT_ROWS):
            dst = lax.broadcasted_iota(jnp.int32, (SORT_ROWS, t), 0) + r0
            perm = jnp.where(dst == pos_r, 1.0, 0.0).astype(BF16)
            hs_ref[r0:r0 + SORT_ROWS, :] = _dot(perm, hm).astype(hs_ref.dtype)
            cp = _dot(perm, c_pack)
            cs_ref[r0:r0 + SORT_ROWS, :] = (cp + pltpu.roll(cp, LANES - N_EXPERTS, axis=1)
                                            + pltpu.roll(cp, LANES - 2 * N_EXPERTS, axis=1))

    b0 = tab_ref[g]

    def expert_block(b, _):
        off = pl.multiple_of(b * rb, rb)
        rows = hs_ref[pl.ds(off, rb), :]
        cblk = cs_ref[pl.ds(off, rb), :]
        lane = lax.broadcasted_iota(jnp.int32, cblk.shape, 1)
        gates = [_dot(rows, wg_ref[j]) for j in range(EXPERTS_PER_GROUP)]
        ups = [_dot(rows, wu_ref[j]) for j in range(EXPERTS_PER_GROUP)]
        y = None
        for j in range(EXPERTS_PER_GROUP):
            c = jnp.sum(jnp.where(lane == g * EXPERTS_PER_GROUP + j, cblk, 0.0), axis=1, keepdims=True)
            a = gates[j] * jax.nn.sigmoid(gates[j]) * ups[j] * c
            d = _dot(a.astype(BF16), wd_ref[j])
            y = d if y is None else y + d
        ys_ref[pl.ds(off, rb), :] = y.astype(ys_ref.dtype)
        return 0

    lax.fori_loop(b0, b0 + tab_ref[N_GROUPS + g], expert_block, 0)

    @pl.when(g == pl.num_programs(1) - 1)
    def _():
        ys = ys_ref[...]
        for r0 in range(0, t, SORT_ROWS):
            src = lax.broadcasted_iota(jnp.int32, (SORT_ROWS, t_pad), 1)
            unperm = jnp.where(src == posc_ref[r0:r0 + SORT_ROWS, :], 1.0, 0.0).astype(BF16)
            x2 = x1_ref[r0:r0 + SORT_ROWS, :] + _dot(unperm, ys)
            o_ref[r0:r0 + SORT_ROWS, :] = _rms(x2, gf_ref[...]) if apply_final else x2


def _moe(hm, comb, gid_rows, x1, wg, wu, wd, gfin, *, tm, apply_final):
    n, d = x1.shape
    ne, _, de = wg.shape
    epg = EXPERTS_PER_GROUP
    t_pad = tm + N_GROUPS * ROW_BLOCK
    row_blk = lambda cols, **kw: pl.BlockSpec((tm, cols), lambda i, g: (i, 0), **kw)
    once = dict(pipeline_mode=pl.Buffered(1))
    return pl.pallas_call(
        functools.partial(_moe_kernel, apply_final=apply_final),
        grid=(n // tm, ne // epg),
        in_specs=[row_blk(d), row_blk(LANES), pl.BlockSpec((SUBLANES, tm), lambda i, g: (0, i)), row_blk(d, **once),
                  pl.BlockSpec((epg, d, de), lambda i, g: (g, 0, 0)),
                  pl.BlockSpec((epg, d, de), lambda i, g: (g, 0, 0)),
                  pl.BlockSpec((epg, de, d), lambda i, g: (g, 0, 0)),
                  pl.BlockSpec((1, d), lambda i, g: (0, 0))],
        out_specs=row_blk(d, **once),
        out_shape=jax.ShapeDtypeStruct((n, d), F32),
        scratch_shapes=[pltpu.VMEM((tm, tm), BF16), pltpu.VMEM((t_pad, d), BF16), pltpu.VMEM((t_pad, LANES), F32),
                        pltpu.VMEM((t_pad, d), BF16), pltpu.VMEM((tm, 1), jnp.int32),
                        pltpu.SMEM((2 * N_GROUPS,), jnp.int32)],
        compiler_params=pltpu.CompilerParams(dimension_semantics=("arbitrary", "arbitrary"),
                                             vmem_limit_bytes=VMEM_LIMIT),
        name="moe",
    )(hm, comb, gid_rows, x1, wg, wu, wd, gfin)


def _rot_cols(w):
    d, c = w.shape
    w4 = w.reshape(d, c // HEAD_DIM, 2, HEAD_DIM // 2)
    return jnp.stack([-w4[:, :, 1], w4[:, :, 0]], axis=2).reshape(d, c)


def kernel(x, norm_attn_g, w_in, b_forget, lambda_q1, lambda_k1, lambda_q2, lambda_k2, diff_norm_g, w_out,
           norm_ffn_g, router_group_w, router_group_b, router_expert_w, router_expert_b, w_gate, w_up, w_down,
           norm_final_g):
    b, s, d = x.shape
    depth = w_in.shape[0]
    n = b * s
    diff_w = N_DIFF_HEADS * 2 * HEAD_DIM
    fox_w = N_FOX_HEADS * HEAD_DIM

    pos = jnp.arange(s, dtype=F32)
    inv_freq = 1.0 / (ROPE_THETA ** (jnp.arange(0, HEAD_DIM, 2, dtype=F32) / HEAD_DIM))
    ang = pos[:, None] * inv_freq[None, :]
    ang = jnp.concatenate([ang, ang, ang, ang], axis=-1)
    cos2, sin2 = jnp.cos(ang), jnp.sin(ang)

    x2 = x.reshape(n, d)
    for l in range(depth):
        lam_init = 0.8 - 0.6 * float(np.exp(-0.3 * l))
        w = w_in[l]
        offs = np.cumsum([0, diff_w, diff_w, diff_w, fox_w, fox_w, fox_w, N_FOX_HEADS])
        seg = [w[:, offs[k]:offs[k + 1]] for k in range(7)]
        w_stack = jnp.stack([seg[0], _rot_cols(seg[0]), seg[1], _rot_cols(seg[1]), seg[3], seg[4]]).astype(BF16)
        wt_stack = jnp.stack([seg[2].T, seg[5].T]).astype(BF16)
        wf = jnp.pad(seg[6], ((0, 0), (0, LANES - N_FOX_HEADS)))
        wf_hi, wf_lo = _split2(wf)
        bf_pad = jnp.pad(b_forget[l], (0, LANES - N_FOX_HEADS)).reshape(1, LANES)

        qd, kd, vdt, qf, kf, vft, qb, kb = _in_proj(
            x2, norm_attn_g[l].reshape(1, d), cos2, sin2, w_stack, wt_stack, wf_hi, wf_lo, bf_pad,
            batch=b, seq=s, tm=512)

        to3 = lambda t: t.reshape(b, s, t.shape[-1])
        lam_params = jnp.stack([lambda_q1[l], lambda_k1[l], lambda_q2[l], lambda_k2[l]])
        od = _diff_attn(to3(qd), to3(kd), vdt, lam_params, diff_norm_g[l].reshape(-1, 1),
                        tq=512, lam_init=lam_init)
        of = _fox_attn(to3(qf), to3(kf), vft, to3(qb), to3(kb), tq=512)

        wo = w_out[l].astype(BF16).reshape(2, -1, d)
        wr = jnp.zeros((d, LANES), F32)
        wr = wr.at[:, :N_GROUPS].set(router_group_w[l]).at[:, 8:8 + N_EXPERTS].set(router_expert_w[l])
        wr_hi, wr_lo = _split2(wr.T)
        br = jnp.zeros((LANES,), F32)
        br = br.at[:N_GROUPS].set(router_group_b[l]).at[8:8 + N_EXPERTS].set(router_expert_b[l]).reshape(LANES, 1)
        x1, hm, comb, gid_rows = _out_proj(x2, od.reshape(n, -1), of.reshape(n, -1), wo, norm_ffn_g[l].reshape(1, d),
                                 wr_hi, wr_lo, br, tm=512)

        x2 = _moe(hm, comb, gid_rows, x1, w_gate[l].astype(BF16), w_up[l].astype(BF16), w_down[l].astype(BF16),
                  norm_final_g.reshape(1, d), tm=1024, apply_final=(l == depth - 1))
    return x2.reshape(b, s, d)
```

```python
import functools

import numpy as np
import jax
import jax.numpy as jnp
from jax import lax
from jax.experimental import pallas as pl
from jax.experimental.pallas import tpu as pltpu

CHUNK = 64
HEAD_DIM = 64
ROPE_THETA = 10000.0
EPS = 1e-6
LOG2E = 1.4426950408889634
N_DIFF_HEADS = 4
N_FOX_HEADS = 8
N_GROUPS = 4
EXPERTS_PER_GROUP = 4
N_EXPERTS = N_GROUPS * EXPERTS_PER_GROUP

LANES = 128
NEG_BIG = -1e30
VMEM_LIMIT = 56 * 1024 * 1024
N_BIAS = 6
SUBLANES = 8
GID_LANE = N_EXPERTS
ROW_BLOCK = 128
ONES_ROWS = 16
BLOCKS_PER_STEP = 2
SORT_ROWS = 512

BF16 = jnp.bfloat16
F32 = jnp.float32
_NT = (((1,), (1,)), ((), ()))


def _dot(a, b):
    return jnp.dot(a, b, preferred_element_type=F32)


def _dot_nt(a, b):
    return lax.dot_general(a, b, _NT, preferred_element_type=F32)


def _split2(x):
    hi = x.astype(BF16)
    lo = (x - hi.astype(F32)).astype(BF16)
    return hi, lo


def _split3(x):
    hi = x.astype(BF16)
    r = x - hi.astype(F32)
    mid = r.astype(BF16)
    lo = (r - mid.astype(F32)).astype(BF16)
    return hi, mid, lo


def _rms(x, g):
    return x * lax.rsqrt(jnp.mean(x * x, axis=-1, keepdims=True) + EPS) * g


def _pack_parts(parts):
    packed = parts[0].astype(F32)
    for k in (1, 2):
        packed = packed + pltpu.roll(parts[k].astype(F32), k * N_FOX_HEADS, axis=1)
    return packed.astype(BF16)


def _in_proj_kernel(x_ref, g_ref, cos_ref, sin_ref, w_ref, wt_ref, wf_cat_ref, bf_ref, tri_ref,
                    eq_ref, ek_ref, cq_ref, ck_ref,
                    qd_ref, kd_ref, vdt_ref, qf_ref, kf_ref, vft_ref, qb_ref, kb_ref, carry_ref,
                    *, tiles_per_seq):
    i = pl.program_id(0)
    h = _rms(x_ref[...], g_ref[...])
    hb, h_lo = _split2(h)
    cos = cos_ref[...]
    sin = sin_ref[...]
    scale = HEAD_DIM ** -0.5 * LOG2E
    lane = lax.broadcasted_iota(jnp.int32, cos.shape, 1)
    first_half = (lane % HEAD_DIM) < HEAD_DIM // 2

    def rope(w_idx, out_ref, mul):
        a = _dot(hb, w_ref[w_idx])
        for c in range(a.shape[1] // LANES):
            sl = slice(c * LANES, (c + 1) * LANES)
            x = a[:, sl]
            rot = jnp.where(first_half, pltpu.roll(x, LANES - HEAD_DIM // 2, axis=1),
                            pltpu.roll(x, HEAD_DIM // 2, axis=1))
            out_ref[:, sl] = ((x * cos + rot * sin) * mul).astype(out_ref.dtype)

    rope(0, qd_ref, scale)
    rope(1, kd_ref, 1.0)
    qf_ref[...] = (_dot(hb, w_ref[2]) * scale).astype(qf_ref.dtype)
    kf_ref[...] = _dot(hb, w_ref[3]).astype(kf_ref.dtype)
    vdt_ref[0] = _dot_nt(wt_ref[0], hb).astype(vdt_ref.dtype)
    vft_ref[0] = _dot_nt(wt_ref[1], hb).astype(vft_ref.dtype)

    wf_cat = wf_cat_ref[...]
    zz = _dot(hb, wf_cat)
    z = ((zz[:, :LANES] + zz[:, LANES:]) + _dot(h_lo, wf_cat[:, :LANES])) + bf_ref[...]
    log_f = jnp.minimum(z, 0.0) - jnp.log1p(jnp.exp(-jnp.abs(z)))
    valid = lane < N_FOX_HEADS
    log_f = jnp.where(valid, log_f, 0.0)

    @pl.when(i % tiles_per_seq == 0)
    def _():
        carry_ref[...] = jnp.zeros_like(carry_ref)

    r = _dot(tri_ref[...], _pack_parts(_split3(log_f)))
    cum = (r + pltpu.roll(r, LANES - N_FOX_HEADS, axis=1)) + pltpu.roll(r, LANES - 2 * N_FOX_HEADS, axis=1)
    cum = jnp.where(valid, cum + carry_ref[...], 0.0)
    carry_ref[...] = cum[cum.shape[0] - 1:, :]

    f_pack = _pack_parts(_split3(cum * LOG2E))
    qb_ref[...] = (cq_ref[...] + _dot(f_pack, eq_ref[...])).astype(qb_ref.dtype)
    kb_ref[...] = (ck_ref[...] + _dot(f_pack, ek_ref[...])).astype(kb_ref.dtype)


def _bias_placement():
    width = (N_FOX_HEADS // 2) * LANES
    eq = np.zeros((LANES, width), np.float32)
    ek = np.zeros((LANES, width), np.float32)
    cq = np.zeros((1, width), np.float32)
    ck = np.zeros((1, width), np.float32)
    for h in range(N_FOX_HEADS):
        base = (h // 2) * LANES + (h % 2) * N_BIAS
        for part in range(3):
            eq[part * N_FOX_HEADS + h, base + part] = 1.0
            ek[part * N_FOX_HEADS + h, base + 3 + part] = -1.0
        cq[0, base + 3:base + 6] = 1.0
        ck[0, base:base + 3] = 1.0
    return jnp.asarray(eq, BF16), jnp.asarray(ek, BF16), jnp.asarray(cq), jnp.asarray(ck)


def _in_proj(x2, g, cos2, sin2, w_stack, wt_stack, wf_cat, bf_pad, *, batch, seq, tm):
    n, d = x2.shape
    tiles_per_seq = seq // tm
    tri = jnp.tril(jnp.ones((tm, tm), F32)).astype(BF16)
    eq, ek, cq, ck = _bias_placement()
    wcols = w_stack.shape[2]
    const = lambda *shape: pl.BlockSpec(shape, lambda i: (0,) * len(shape))
    row_blk = lambda cols: pl.BlockSpec((tm, cols), lambda i: (i, 0))
    pos_blk = pl.BlockSpec((tm, LANES), lambda i: (i % tiles_per_seq, 0))
    vt_blk = pl.BlockSpec((1, wcols, tm), lambda i: (i // tiles_per_seq, 0, i % tiles_per_seq))
    out_bf = jax.ShapeDtypeStruct((n, wcols), BF16)
    out_vt = jax.ShapeDtypeStruct((batch, wcols, seq), BF16)
    return pl.pallas_call(
        functools.partial(_in_proj_kernel, tiles_per_seq=tiles_per_seq),
        grid=(n // tm,),
        in_specs=[row_blk(d), const(1, d), pos_blk, pos_blk, const(*w_stack.shape), const(*wt_stack.shape),
                  const(*wf_cat.shape), const(1, LANES), const(tm, tm),
                  const(*eq.shape), const(*ek.shape), const(*cq.shape), const(*ck.shape)],
        out_specs=[row_blk(wcols), row_blk(wcols), vt_blk, row_blk(wcols), row_blk(wcols), vt_blk,
                   row_blk(eq.shape[1]), row_blk(ek.shape[1])],
        out_shape=[out_bf, out_bf, out_vt, out_bf, out_bf, out_vt,
                   jax.ShapeDtypeStruct((n, eq.shape[1]), BF16), jax.ShapeDtypeStruct((n, ek.shape[1]), BF16)],
        scratch_shapes=[pltpu.VMEM((1, LANES), F32)],
        compiler_params=pltpu.CompilerParams(dimension_semantics=("arbitrary",),
                                             vmem_limit_bytes=VMEM_LIMIT),
        name="in_proj",
    )(x2, g, cos2, sin2, w_stack, wt_stack, wf_cat, bf_pad, tri, eq, ek, cq, ck)


def _flash_tiles(i, tq, logits_fn, values_fn, mask, scratch):
    s0, s1, c0, c1, m_ref, acc_ref = scratch
    n_maps = m_ref.shape[0]
    m_ref[...] = jnp.full(m_ref.shape, NEG_BIG, F32)
    acc_ref[...] = jnp.zeros(acc_ref.shape, F32)
    ones = jnp.ones((ONES_ROWS, tq), BF16)

    def stage_a(tile, s_buf, c_buf, masked):
        sts = logits_fn(tile)
        for a in range(n_maps):
            st = jnp.where(mask, sts[a], NEG_BIG) if masked else sts[a]
            s_buf[a] = st
            c_buf[a] = jnp.max(st, axis=0, keepdims=True)

    def stage_b(tile, s_buf, c_buf):
        vts = values_fn(tile)
        for a in range(n_maps):
            m = m_ref[a]
            m_new = jnp.maximum(m, c_buf[a])
            alpha = jnp.exp2(m - m_new)
            p = jnp.exp2(s_buf[a] - m_new).astype(BF16)
            vt = jnp.concatenate([vts[a], ones], axis=0)
            acc_ref[a] = alpha * acc_ref[a] + _dot(vt, p)
            m_ref[a] = m_new

    stage_a(i, s0, c0, True)

    def body(jj, _):
        t = 2 * jj
        stage_a(t, s1, c1, False)
        stage_b(jnp.where(jj == 0, i, t - 1), s0, c0)
        stage_a(t + 1, s0, c0, False)
        stage_b(t, s1, c1)
        return 0

    lax.fori_loop(0, lax.shift_right_logical(i, 1), body, 0)
    odd = (i & 1) == 1

    @pl.when(odd)
    def _():
        stage_a(i - 1, s1, c1, False)
        stage_b(jnp.where(i == 1, i, i - 2), s0, c0)
        stage_b(i - 1, s1, c1)

    @pl.when(jnp.logical_not(odd))
    def _():
        stage_b(jnp.where(i == 0, i, i - 1), s0, c0)


def _flash_scratch(n_maps, dv, tq):
    s_buf = pltpu.VMEM((n_maps, tq, tq), F32)
    c_buf = pltpu.VMEM((n_maps, 1, tq), F32)
    return [s_buf, s_buf, c_buf, c_buf, c_buf, pltpu.VMEM((n_maps, dv + ONES_ROWS, tq), F32)]


def _flash_outputs(acc_ref, dv):
    return [acc_ref[a, :dv, :] / acc_ref[a, dv:dv + 1, :] for a in range(acc_ref.shape[0])]


def _head_blocks(ref_block):
    return [ref_block[:, k * LANES:(k + 1) * LANES] for k in range(BLOCKS_PER_STEP)]


def _diff_attn_kernel(q_ref, k_ref, vt_ref, lam_ref, g_ref, o_ref, *scratch, tq, lam_init):
    i = pl.program_id(2)
    qs = []
    for q in _head_blocks(q_ref[0]):
        lane = lax.broadcasted_iota(jnp.int32, q.shape, 1)
        zero = jnp.zeros_like(q)
        qs += [jnp.where(lane < HEAD_DIM, q, zero), jnp.where(lane >= HEAD_DIM, q, zero)]

    def logits(tile):
        kts = _head_blocks(k_ref[0, pl.ds(pl.multiple_of(tile * tq, tq), tq), :])
        return [_dot_nt(kts[a // 2], qs[a]) for a in range(len(qs))]

    def values(tile):
        vt = vt_ref[0, :, pl.ds(pl.multiple_of(tile * tq, tq), tq)]
        return [vt[(a // 2) * LANES:(a // 2 + 1) * LANES] for a in range(len(qs))]

    key = lax.broadcasted_iota(jnp.int32, (tq, tq), 0)
    qry = lax.broadcasted_iota(jnp.int32, (tq, tq), 1)
    _flash_tiles(i, tq, logits, values, (key // CHUNK) <= (qry // CHUNK), scratch)
    outs = _flash_outputs(scratch[-1], LANES)

    lp = lam_ref[...]
    lam = (jnp.exp(jnp.sum(lp[0:1] * lp[1:2], axis=1, keepdims=True))
           - jnp.exp(jnp.sum(lp[2:3] * lp[3:4], axis=1, keepdims=True)) + lam_init)
    for k in range(BLOCKS_PER_STEP):
        o = outs[2 * k] - lam * outs[2 * k + 1]
        y = o * lax.rsqrt(jnp.mean(o * o, axis=0, keepdims=True) + EPS) * g_ref[...] * (1.0 - lam_init)
        o_ref[0, :, k * LANES:(k + 1) * LANES] = y.T.astype(o_ref.dtype)


def _attn_specs(s, tq):
    w = BLOCKS_PER_STEP * LANES
    q_blk = pl.BlockSpec((1, tq, w), lambda bi, h, i: (bi, i, h))
    k_blk = pl.BlockSpec((1, s, w), lambda bi, h, i: (bi, 0, h))
    vt_blk = pl.BlockSpec((1, w, s), lambda bi, h, i: (bi, h, 0))
    return q_blk, k_blk, vt_blk


def _diff_attn(qd, kd, vdt, lam_params, gnorm_col, *, tq, lam_init):
    b, s, w = qd.shape
    q_blk, k_blk, vt_blk = _attn_specs(s, tq)
    return pl.pallas_call(
        functools.partial(_diff_attn_kernel, tq=tq, lam_init=lam_init),
        grid=(b, w // (BLOCKS_PER_STEP * LANES), s // tq),
        in_specs=[q_blk, k_blk, vt_blk,
                  pl.BlockSpec(lam_params.shape, lambda bi, h, i: (0, 0)),
                  pl.BlockSpec(gnorm_col.shape, lambda bi, h, i: (0, 0))],
        out_specs=q_blk,
        out_shape=jax.ShapeDtypeStruct((b, s, w), BF16),
        scratch_shapes=_flash_scratch(2 * BLOCKS_PER_STEP, LANES, tq),
        compiler_params=pltpu.CompilerParams(dimension_semantics=("arbitrary",) * 3,
                                             vmem_limit_bytes=VMEM_LIMIT),
        name="diff_attn",
    )(qd, kd, vdt, lam_params, gnorm_col)


def _fox_attn_kernel(q_ref, k_ref, vt_ref, qb_ref, kb_ref, o_ref, *scratch, tq):
    i = pl.program_id(2)
    qs = []
    for q, qb in zip(_head_blocks(q_ref[0]), _head_blocks(qb_ref[0])):
        lane = lax.broadcasted_iota(jnp.int32, q.shape, 1)
        zero = jnp.zeros_like(q)
        qs += [jnp.concatenate([jnp.where(lane < HEAD_DIM, q, zero), jnp.where(lane < N_BIAS, qb, zero)], axis=1),
               jnp.concatenate([jnp.where(lane >= HEAD_DIM, q, zero),
                                jnp.where((lane >= N_BIAS) & (lane < 2 * N_BIAS), qb, zero)], axis=1)]

    def logits(tile):
        start = pl.multiple_of(tile * tq, tq)
        kts = [jnp.concatenate([k, kb], axis=1) for k, kb in zip(_head_blocks(k_ref[0, pl.ds(start, tq), :]),
                                                                   _head_blocks(kb_ref[0, pl.ds(start, tq), :]))]
        return [_dot_nt(kts[a // 2], qs[a]) for a in range(len(qs))]

    def values(tile):
        vt = vt_ref[0, :, pl.ds(pl.multiple_of(tile * tq, tq), tq)]
        return [vt[a * HEAD_DIM:(a + 1) * HEAD_DIM] for a in range(len(qs))]

    key = lax.broadcasted_iota(jnp.int32, (tq, tq), 0)
    qry = lax.broadcasted_iota(jnp.int32, (tq, tq), 1)
    _flash_tiles(i, tq, logits, values, key <= qry, scratch)
    o = jnp.concatenate(_flash_outputs(scratch[-1], HEAD_DIM), axis=0)
    o_ref[0] = o.T.astype(o_ref.dtype)


def _fox_attn(qf, kf, vft, qb, kb, *, tq):
    b, s, w = qf.shape
    q_blk, k_blk, vt_blk = _attn_specs(s, tq)
    return pl.pallas_call(
        functools.partial(_fox_attn_kernel, tq=tq),
        grid=(b, w // (BLOCKS_PER_STEP * LANES), s // tq),
        in_specs=[q_blk, k_blk, vt_blk, q_blk, k_blk],
        out_specs=q_blk,
        out_shape=jax.ShapeDtypeStruct((b, s, w), BF16),
        scratch_shapes=_flash_scratch(2 * BLOCKS_PER_STEP, HEAD_DIM, tq),
        compiler_params=pltpu.CompilerParams(dimension_semantics=("arbitrary",) * 3,
                                             vmem_limit_bytes=VMEM_LIMIT),
        name="fox_attn",
    )(qf, kf, vft, qb, kb)


def _out_proj_kernel(x_ref, od_ref, of_ref, wo_ref, g_ref, wr_hi_ref, wr_lo_ref, br_ref,
                     x1_ref, hm_ref, comb_ref, gid_ref):
    x1 = x_ref[...] + (_dot(od_ref[...], wo_ref[0]) + _dot(of_ref[...], wo_ref[1]))
    x1_ref[...] = x1
    hm = _rms(x1, g_ref[...])
    hb, h_lo = _split2(hm)
    hm_ref[...] = hb

    wr_hi = wr_hi_ref[...]
    lt = (_dot_nt(wr_hi, hb) + _dot_nt(wr_lo_ref[...], hb) + _dot_nt(wr_hi, h_lo)) + br_ref[...]
    tm = lt.shape[1]
    g8 = lt[0:8]
    r8 = lax.broadcasted_iota(jnp.int32, (8, tm), 0)
    g8 = jnp.where(r8 < N_GROUPS, g8, NEG_BIG)
    gmax = jnp.max(g8, axis=0, keepdims=True)
    gidx = jnp.min(jnp.where(g8 == gmax, r8, N_GROUPS), axis=0, keepdims=True)
    g_w = 1.0 / jnp.sum(jnp.exp(g8 - gmax), axis=0, keepdims=True)

    e16 = lt[8:8 + N_EXPERTS]
    r16 = lax.broadcasted_iota(jnp.int32, (N_EXPERTS, tm), 0)
    in_group = (r16 // EXPERTS_PER_GROUP) == gidx
    e_sel = jnp.where(in_group, e16, NEG_BIG)
    top1 = jnp.max(e_sel, axis=0, keepdims=True)
    id1 = jnp.min(jnp.where(e_sel == top1, r16, N_EXPERTS), axis=0, keepdims=True)
    e_rest = jnp.where(r16 == id1, NEG_BIG, e_sel)
    top2 = jnp.max(e_rest, axis=0, keepdims=True)
    id2 = jnp.min(jnp.where(e_rest == top2, r16, N_EXPERTS), axis=0, keepdims=True)
    t = jnp.exp(top2 - top1)
    w1 = g_w / (1.0 + t)
    w2 = w1 * t
    r128 = lax.broadcasted_iota(jnp.int32, (LANES, tm), 0)
    gid = gidx.astype(F32)
    comb_t = (jnp.where(r128 == id1, w1, 0.0) + jnp.where(r128 == id2, w2, 0.0)
              + jnp.where(r128 == GID_LANE, gid, 0.0))
    comb_ref[...] = comb_t.T
    gid_ref[...] = jnp.where(r8 == 0, gid, 0.0)


def _out_proj(x2, od, of, wo, g, wr_hi, wr_lo, br, *, tm):
    n, d = x2.shape
    const = lambda *shape: pl.BlockSpec(shape, lambda i: (0,) * len(shape))
    row_blk = lambda cols: pl.BlockSpec((tm, cols), lambda i: (i, 0))
    return pl.pallas_call(
        _out_proj_kernel,
        grid=(n // tm,),
        in_specs=[row_blk(d), row_blk(od.shape[1]), row_blk(of.shape[1]), const(*wo.shape), const(1, d),
                  const(*wr_hi.shape), const(*wr_lo.shape), const(*br.shape)],
        out_specs=[row_blk(d), row_blk(d), row_blk(LANES), pl.BlockSpec((SUBLANES, tm), lambda i: (0, i))],
        out_shape=[jax.ShapeDtypeStruct((n, d), F32), jax.ShapeDtypeStruct((n, d), BF16),
                   jax.ShapeDtypeStruct((n, LANES), F32), jax.ShapeDtypeStruct((SUBLANES, n), F32)],
        compiler_params=pltpu.CompilerParams(dimension_semantics=("arbitrary",),
                                             vmem_limit_bytes=VMEM_LIMIT),
        name="out_proj",
    )(x2, od, of, wo, g, wr_hi, wr_lo, br)


def _moe_kernel(hm_ref, comb_ref, gidr_ref, x1_ref, wg_ref, wu_ref, wd_ref, gf_ref, o_ref,
                tri_ref, hs_ref, cs_ref, ys_ref, posc_ref, tab_ref, *, apply_final):
    i = pl.program_id(0)
    g = pl.program_id(1)
    t = hm_ref.shape[0]
    t_pad = hs_ref.shape[0]
    rb = ROW_BLOCK

    @pl.when((i == 0) & (g == 0))
    def _():
        row = lax.broadcasted_iota(jnp.int32, (t, t), 0)
        col = lax.broadcasted_iota(jnp.int32, (t, t), 1)
        tri_ref[...] = jnp.where(row > col, 1.0, 0.0).astype(BF16)

    @pl.when(g == 0)
    def _():
        comb = comb_ref[...]
        lane = lax.broadcasted_iota(jnp.int32, comb.shape, 1)
        onehot_c = jnp.where(lane == comb[:, GID_LANE:GID_LANE + 1].astype(jnp.int32), 1.0, 0.0)
        sub = lax.broadcasted_iota(jnp.int32, (SUBLANES, t), 0)
        onehot_r = jnp.where(sub == gidr_ref[0:1, :].astype(jnp.int32), 1.0, 0.0)
        tri = tri_ref[...]
        rank_c = _dot(tri, onehot_c.astype(BF16))
        rank_r = _dot_nt(onehot_r.astype(BF16), tri)
        lane1 = lax.broadcasted_iota(jnp.int32, (1, LANES), 1)
        sub1 = lax.broadcasted_iota(jnp.int32, (SUBLANES, 1), 0)
        start_c = jnp.zeros((1, LANES), F32)
        start_r = jnp.zeros((SUBLANES, 1), F32)
        first = jnp.int32(0)
        for k in range(N_GROUPS):
            cnt = jnp.sum(onehot_r[k:k + 1, :]).astype(jnp.int32)
            nblk = lax.shift_right_logical(cnt + (rb - 1), int(np.log2(rb)))
            tab_ref[k] = first
            tab_ref[N_GROUPS + k] = nblk
            row0 = (first * rb).astype(F32)
            start_c = jnp.where(lane1 == k, row0, start_c)
            start_r = jnp.where(sub1 == k, row0, start_r)
            first = first + nblk
        posc_ref[...] = jnp.sum(onehot_c * (rank_c + start_c), axis=1, keepdims=True).astype(jnp.int32)
        pos_r = jnp.sum(onehot_r * (rank_r + start_r), axis=0, keepdims=True).astype(jnp.int32)
        ys_ref[...] = jnp.zeros_like(ys_ref)
        c_parts = _split3(jnp.where(lane < N_EXPERTS, comb, 0.0))
        c_pack = (c_parts[0].astype(F32) + pltpu.roll(c_parts[1].astype(F32), N_EXPERTS, axis=1)
                  + pltpu.roll(c_parts[2].astype(F32), 2 * N_EXPERTS, axis=1)).astype(BF16)
        hm = hm_ref[...]
        for r0 in range(0, t_pad, SORT_ROWS):
            dst = lax.broadcasted_iota(jnp.int32, (SORT_ROWS, t), 0) + r0
            perm = jnp.where(dst == pos_r, 1.0, 0.0).astype(BF16)
            hs_ref[r0:r0 + SORT_ROWS, :] = _dot(perm, hm).astype(hs_ref.dtype)
            cp = _dot(perm, c_pack)
            cs_ref[r0:r0 + SORT_ROWS, :] = (cp + pltpu.roll(cp, LANES - N_EXPERTS, axis=1)
                                            + pltpu.roll(cp, LANES - 2 * N_EXPERTS, axis=1))

    b0 = tab_ref[g]

    def expert_block(b, _):
        off = pl.multiple_of(b * rb, rb)
        rows = hs_ref[pl.ds(off, rb), :]
        cblk = cs_ref[pl.ds(off, rb), :]
        lane = lax.broadcasted_iota(jnp.int32, cblk.shape, 1)
        gates = [_dot(rows, wg_ref[j]) for j in range(EXPERTS_PER_GROUP)]
        ups = [_dot(rows, wu_ref[j]) for j in range(EXPERTS_PER_GROUP)]
        y = None
        for j in range(EXPERTS_PER_GROUP):
            c = jnp.sum(jnp.where(lane == g * EXPERTS_PER_GROUP + j, cblk, 0.0), axis=1, keepdims=True)
            a = gates[j] * jax.nn.sigmoid(gates[j]) * ups[j] * c
            d = _dot(a.astype(BF16), wd_ref[j])
            y = d if y is None else y + d
        ys_ref[pl.ds(off, rb), :] = y.astype(ys_ref.dtype)
        return 0

    lax.fori_loop(b0, b0 + tab_ref[N_GROUPS + g], expert_block, 0)

    @pl.when(g == pl.num_programs(1) - 1)
    def _():
        ys = ys_ref[...]
        for r0 in range(0, t, SORT_ROWS):
            src = lax.broadcasted_iota(jnp.int32, (SORT_ROWS, t_pad), 1)
            unperm = jnp.where(src == posc_ref[r0:r0 + SORT_ROWS, :], 1.0, 0.0).astype(BF16)
            x2 = x1_ref[r0:r0 + SORT_ROWS, :] + _dot(unperm, ys)
            o_ref[r0:r0 + SORT_ROWS, :] = _rms(x2, gf_ref[...]) if apply_final else x2


def _moe(hm, comb, gid_rows, x1, wg, wu, wd, gfin, *, tm, apply_final):
    n, d = x1.shape
    ne, _, de = wg.shape
    epg = EXPERTS_PER_GROUP
    t_pad = tm + N_GROUPS * ROW_BLOCK
    row_blk = lambda cols, **kw: pl.BlockSpec((tm, cols), lambda i, g: (i, 0), **kw)
    once = dict(pipeline_mode=pl.Buffered(1))
    return pl.pallas_call(
        functools.partial(_moe_kernel, apply_final=apply_final),
        grid=(n // tm, ne // epg),
        in_specs=[row_blk(d), row_blk(LANES), pl.BlockSpec((SUBLANES, tm), lambda i, g: (0, i)), row_blk(d, **once),
                  pl.BlockSpec((epg, d, de), lambda i, g: (g, 0, 0)),
                  pl.BlockSpec((epg, d, de), lambda i, g: (g, 0, 0)),
                  pl.BlockSpec((epg, de, d), lambda i, g: (g, 0, 0)),
                  pl.BlockSpec((1, d), lambda i, g: (0, 0))],
        out_specs=row_blk(d, **once),
        out_shape=jax.ShapeDtypeStruct((n, d), F32),
        scratch_shapes=[pltpu.VMEM((tm, tm), BF16), pltpu.VMEM((t_pad, d), BF16), pltpu.VMEM((t_pad, LANES), F32),
                        pltpu.VMEM((t_pad, d), BF16), pltpu.VMEM((tm, 1), jnp.int32),
                        pltpu.SMEM((2 * N_GROUPS,), jnp.int32)],
        compiler_params=pltpu.CompilerParams(dimension_semantics=("arbitrary", "arbitrary"),
                                             vmem_limit_bytes=VMEM_LIMIT),
        name="moe",
    )(hm, comb, gid_rows, x1, wg, wu, wd, gfin)


def kernel(x, norm_attn_g, w_in, b_forget, lambda_q1, lambda_k1, lambda_q2, lambda_k2, diff_norm_g, w_out,
           norm_ffn_g, router_group_w, router_group_b, router_expert_w, router_expert_b, w_gate, w_up, w_down,
           norm_final_g):
    b, s, d = x.shape
    depth = w_in.shape[0]
    n = b * s
    diff_w = N_DIFF_HEADS * 2 * HEAD_DIM
    fox_w = N_FOX_HEADS * HEAD_DIM

    pos = jnp.arange(s, dtype=F32)
    inv_freq = 1.0 / (ROPE_THETA ** (jnp.arange(0, HEAD_DIM, 2, dtype=F32) / HEAD_DIM))
    ang = pos[:, None] * inv_freq[None, :]
    ang = jnp.concatenate([ang, ang, ang, ang], axis=-1)
    first_half = (jnp.arange(LANES) % HEAD_DIM) < HEAD_DIM // 2
    cos2, sin2 = jnp.cos(ang), jnp.where(first_half, -jnp.sin(ang), jnp.sin(ang))

    x2 = x.reshape(n, d)
    for l in range(depth):
        lam_init = 0.8 - 0.6 * float(np.exp(-0.3 * l))
        w = w_in[l]
        offs = np.cumsum([0, diff_w, diff_w, diff_w, fox_w, fox_w, fox_w, N_FOX_HEADS])
        seg = [w[:, offs[k]:offs[k + 1]] for k in range(7)]
        w_stack = jnp.stack([seg[0], seg[1], seg[3], seg[4]]).astype(BF16)
        wt_stack = jnp.stack([seg[2].T, seg[5].T]).astype(BF16)
        wf = jnp.pad(seg[6], ((0, 0), (0, LANES - N_FOX_HEADS)))
        wf_cat = jnp.concatenate(_split2(wf), axis=1)
        bf_pad = jnp.pad(b_forget[l], (0, LANES - N_FOX_HEADS)).reshape(1, LANES)

        qd, kd, vdt, qf, kf, vft, qb, kb = _in_proj(
            x2, norm_attn_g[l].reshape(1, d), cos2, sin2, w_stack, wt_stack, wf_cat, bf_pad,
            batch=b, seq=s, tm=512)

        to3 = lambda t: t.reshape(b, s, t.shape[-1])
        lam_params = jnp.stack([lambda_q1[l], lambda_k1[l], lambda_q2[l], lambda_k2[l]])
        od = _diff_attn(to3(qd), to3(kd), vdt, lam_params, diff_norm_g[l].reshape(-1, 1),
                        tq=512, lam_init=lam_init)
        of = _fox_attn(to3(qf), to3(kf), vft, to3(qb), to3(kb), tq=512)

        wo = w_out[l].astype(BF16).reshape(2, -1, d)
        wr = jnp.zeros((d, LANES), F32)
        wr = wr.at[:, :N_GROUPS].set(router_group_w[l]).at[:, 8:8 + N_EXPERTS].set(router_expert_w[l])
        wr_hi, wr_lo = _split2(wr.T)
        br = jnp.zeros((LANES,), F32)
        br = br.at[:N_GROUPS].set(router_group_b[l]).at[8:8 + N_EXPERTS].set(router_expert_b[l]).reshape(LANES, 1)
        x1, hm, comb, gid_rows = _out_proj(x2, od.reshape(n, -1), of.reshape(n, -1), wo, norm_ffn_g[l].reshape(1, d),
                                 wr_hi, wr_lo, br, tm=512)

        x2 = _moe(hm, comb, gid_rows, x1, w_gate[l].astype(BF16), w_up[l].astype(BF16), w_down[l].astype(BF16),
                  norm_final_g.reshape(1, d), tm=1024, apply_final=(l == depth - 1))
    return x2.reshape(b, s, d)
```

```python
import functools

import numpy as np
import jax
import jax.numpy as jnp
from jax import lax
from jax.experimental import pallas as pl
from jax.experimental.pallas import tpu as pltpu

CHUNK = 64
HEAD_DIM = 64
ROPE_THETA = 10000.0
EPS = 1e-6
LOG2E = 1.4426950408889634
N_DIFF_HEADS = 4
N_FOX_HEADS = 8
N_GROUPS = 4
EXPERTS_PER_GROUP = 4
N_EXPERTS = N_GROUPS * EXPERTS_PER_GROUP

LANES = 128
NEG_BIG = -1e30
VMEM_LIMIT = 56 * 1024 * 1024
N_BIAS = 6
SUBLANES = 8
GID_LANE = N_EXPERTS
ROW_BLOCK = 128
ONES_ROWS = 16
BLOCKS_PER_STEP = 2
SORT_ROWS = 512

BF16 = jnp.bfloat16
F32 = jnp.float32
_NT = (((1,), (1,)), ((), ()))


def _dot(a, b):
    return jnp.dot(a, b, preferred_element_type=F32)


def _dot_nt(a, b):
    return lax.dot_general(a, b, _NT, preferred_element_type=F32)


def _split2(x):
    hi = x.astype(BF16)
    lo = (x - hi.astype(F32)).astype(BF16)
    return hi, lo


def _split3(x):
    hi = x.astype(BF16)
    r = x - hi.astype(F32)
    mid = r.astype(BF16)
    lo = (r - mid.astype(F32)).astype(BF16)
    return hi, mid, lo


def _rms(x, g):
    return x * lax.rsqrt(jnp.mean(x * x, axis=-1, keepdims=True) + EPS) * g


def _pack_parts(parts):
    packed = parts[0].astype(F32)
    for k in (1, 2):
        packed = packed + pltpu.roll(parts[k].astype(F32), k * N_FOX_HEADS, axis=1)
    return packed.astype(BF16)


def _in_proj_kernel(x_ref, g_ref, cos_ref, sin_ref, w_ref, wt_ref, wf_cat_ref, bf_ref, tri_ref,
                    eq_ref, ek_ref, cq_ref, ck_ref,
                    qd_ref, kd_ref, vdt_ref, qf_ref, kf_ref, vft_ref, qb_ref, kb_ref, carry_ref,
                    *, tiles_per_seq):
    i = pl.program_id(0)
    h = _rms(x_ref[...], g_ref[...])
    hb, h_lo = _split2(h)
    cos = cos_ref[...]
    sin = sin_ref[...]
    scale = HEAD_DIM ** -0.5 * LOG2E
    lane = lax.broadcasted_iota(jnp.int32, cos.shape, 1)
    first_half = (lane % HEAD_DIM) < HEAD_DIM // 2

    def rope(w_idx, out_ref, mul):
        a = _dot(hb, w_ref[w_idx])
        for c in range(a.shape[1] // LANES):
            sl = slice(c * LANES, (c + 1) * LANES)
            x = a[:, sl]
            rot = jnp.where(first_half, pltpu.roll(x, LANES - HEAD_DIM // 2, axis=1),
                            pltpu.roll(x, HEAD_DIM // 2, axis=1))
            out_ref[:, sl] = ((x * cos + rot * sin) * mul).astype(out_ref.dtype)

    rope(0, qd_ref, scale)
    rope(1, kd_ref, 1.0)
    qf_ref[...] = (_dot(hb, w_ref[2]) * scale).astype(qf_ref.dtype)
    kf_ref[...] = _dot(hb, w_ref[3]).astype(kf_ref.dtype)
    vdt_ref[0] = _dot_nt(wt_ref[0], hb).astype(vdt_ref.dtype)
    vft_ref[0] = _dot_nt(wt_ref[1], hb).astype(vft_ref.dtype)

    wf_cat = wf_cat_ref[...]
    zz = _dot(hb, wf_cat)
    z = ((zz[:, :LANES] + zz[:, LANES:]) + _dot(h_lo, wf_cat[:, :LANES])) + bf_ref[...]
    log_f = jnp.minimum(z, 0.0) - jnp.log1p(jnp.exp(-jnp.abs(z)))
    valid = lane < N_FOX_HEADS
    log_f = jnp.where(valid, log_f, 0.0)

    @pl.when(i % tiles_per_seq == 0)
    def _():
        carry_ref[...] = jnp.zeros_like(carry_ref)

    r = _dot(tri_ref[...], _pack_parts(_split3(log_f)))
    cum = (r + pltpu.roll(r, LANES - N_FOX_HEADS, axis=1)) + pltpu.roll(r, LANES - 2 * N_FOX_HEADS, axis=1)
    cum = jnp.where(valid, cum + carry_ref[...], 0.0)
    carry_ref[...] = cum[cum.shape[0] - 1:, :]

    f_pack = _pack_parts(_split3(cum * LOG2E))
    qb_ref[...] = (cq_ref[...] + _dot(f_pack, eq_ref[...])).astype(qb_ref.dtype)
    kb_ref[...] = (ck_ref[...] + _dot(f_pack, ek_ref[...])).astype(kb_ref.dtype)


def _bias_placement():
    width = (N_FOX_HEADS // 2) * LANES
    eq = np.zeros((LANES, width), np.float32)
    ek = np.zeros((LANES, width), np.float32)
    cq = np.zeros((1, width), np.float32)
    ck = np.zeros((1, width), np.float32)
    for h in range(N_FOX_HEADS):
        base = (h // 2) * LANES + (h % 2) * N_BIAS
        for part in range(3):
            eq[part * N_FOX_HEADS + h, base + part] = 1.0
            ek[part * N_FOX_HEADS + h, base + 3 + part] = -1.0
        cq[0, base + 3:base + 6] = 1.0
        ck[0, base:base + 3] = 1.0
    return jnp.asarray(eq, BF16), jnp.asarray(ek, BF16), jnp.asarray(cq), jnp.asarray(ck)


def _in_proj(x2, g, cos2, sin2, w_stack, wt_stack, wf_cat, bf_pad, *, batch, seq, tm):
    n, d = x2.shape
    tiles_per_seq = seq // tm
    tri = jnp.tril(jnp.ones((tm, tm), F32)).astype(BF16)
    eq, ek, cq, ck = _bias_placement()
    wcols = w_stack.shape[2]
    const = lambda *shape: pl.BlockSpec(shape, lambda i: (0,) * len(shape))
    row_blk = lambda cols: pl.BlockSpec((tm, cols), lambda i: (i, 0))
    pos_blk = pl.BlockSpec((tm, LANES), lambda i: (i % tiles_per_seq, 0))
    vt_blk = pl.BlockSpec((1, wcols, tm), lambda i: (i // tiles_per_seq, 0, i % tiles_per_seq))
    out_bf = jax.ShapeDtypeStruct((n, wcols), BF16)
    out_vt = jax.ShapeDtypeStruct((batch, wcols, seq), BF16)
    return pl.pallas_call(
        functools.partial(_in_proj_kernel, tiles_per_seq=tiles_per_seq),
        grid=(n // tm,),
        in_specs=[row_blk(d), const(1, d), pos_blk, pos_blk, const(*w_stack.shape), const(*wt_stack.shape),
                  const(*wf_cat.shape), const(1, LANES), const(tm, tm),
                  const(*eq.shape), const(*ek.shape), const(*cq.shape), const(*ck.shape)],
        out_specs=[row_blk(wcols), row_blk(wcols), vt_blk, row_blk(wcols), row_blk(wcols), vt_blk,
                   row_blk(eq.shape[1]), row_blk(ek.shape[1])],
        out_shape=[out_bf, out_bf, out_vt, out_bf, out_bf, out_vt,
                   jax.ShapeDtypeStruct((n, eq.shape[1]), BF16), jax.ShapeDtypeStruct((n, ek.shape[1]), BF16)],
        scratch_shapes=[pltpu.VMEM((1, LANES), F32)],
        compiler_params=pltpu.CompilerParams(dimension_semantics=("arbitrary",),
                                             vmem_limit_bytes=VMEM_LIMIT),
        name="in_proj",
    )(x2, g, cos2, sin2, w_stack, wt_stack, wf_cat, bf_pad, tri, eq, ek, cq, ck)


def _flash_sweep(nq, tq, tabs, logits_fn, values_fn, mask, scratch):
    s0, s1, c0, c1, m_ref, acc_ref = scratch
    row_tab, k_tab = tabs
    n_maps = m_ref.shape[1]
    n_off = nq * (nq - 1) // 2
    ones = jnp.ones((ONES_ROWS, tq), BF16)

    def stage_a(row, tile, s_buf, c_buf, masked):
        sts = logits_fn(row, tile)
        for a in range(n_maps):
            st = jnp.where(mask, sts[a], NEG_BIG) if masked else sts[a]
            s_buf[a] = st
            c_buf[a] = jnp.max(st, axis=0, keepdims=True)

    def stage_b(row, tile, s_buf, c_buf, first):
        vts = values_fn(tile)
        for a in range(n_maps):
            vt = jnp.concatenate([vts[a], ones], axis=0)
            if first:
                m_new = c_buf[a]
                acc_ref[row, a] = _dot(vt, jnp.exp2(s_buf[a] - m_new).astype(BF16))
            else:
                m = m_ref[row, a]
                m_new = jnp.maximum(m, c_buf[a])
                alpha = jnp.exp2(m - m_new)
                acc_ref[row, a] = alpha * acc_ref[row, a] + _dot(vt, jnp.exp2(s_buf[a] - m_new).astype(BF16))
            m_ref[row, a] = m_new

    stage_a(0, 0, s0, c0, True)

    def diag_pair(jj, _):
        r = 2 * jj
        stage_a(r + 1, r + 1, s1, c1, True)
        stage_b(r, r, s0, c0, True)
        stage_a(r + 2, r + 2, s0, c0, True)
        stage_b(r + 1, r + 1, s1, c1, True)
        return 0

    lax.fori_loop(0, (nq - 2) // 2, diag_pair, 0)
    stage_a(nq - 1, nq - 1, s1, c1, True)
    stage_b(nq - 2, nq - 2, s0, c0, True)

    stage_a(row_tab[0], k_tab[0], s0, c0, False)
    stage_b(nq - 1, nq - 1, s1, c1, True)

    def off_pair(jj, _):
        t = 2 * jj
        stage_a(row_tab[t + 1], k_tab[t + 1], s1, c1, False)
        stage_b(row_tab[t], k_tab[t], s0, c0, False)
        stage_a(row_tab[t + 2], k_tab[t + 2], s0, c0, False)
        stage_b(row_tab[t + 1], k_tab[t + 1], s1, c1, False)
        return 0

    lax.fori_loop(0, (n_off - 2) // 2, off_pair, 0)
    stage_a(row_tab[n_off - 1], k_tab[n_off - 1], s1, c1, False)
    stage_b(row_tab[n_off - 2], k_tab[n_off - 2], s0, c0, False)
    stage_b(row_tab[n_off - 1], k_tab[n_off - 1], s1, c1, False)


def _flash_scratch(nq, n_maps, dv, tq):
    s_buf = pltpu.VMEM((n_maps, tq, tq), F32)
    c_buf = pltpu.VMEM((n_maps, 1, tq), F32)
    return [s_buf, s_buf, c_buf, c_buf, pltpu.VMEM((nq, n_maps, 1, tq), F32),
            pltpu.VMEM((nq, n_maps, dv + ONES_ROWS, tq), F32)]


def _off_diagonal_tables(nq):
    pairs = [(r, k) for r in range(nq) for k in range(r)]
    return (jnp.asarray([p[0] for p in pairs], jnp.int32), jnp.asarray([p[1] for p in pairs], jnp.int32))


def _flash_outputs(acc_ref, row, dv):
    return [acc_ref[row, a, :dv, :] / acc_ref[row, a, dv:dv + 1, :] for a in range(acc_ref.shape[1])]


def _head_blocks(block):
    return [block[:, k * LANES:(k + 1) * LANES] for k in range(BLOCKS_PER_STEP)]


def _tile_rows(ref, tile, tq):
    return ref[0, pl.ds(pl.multiple_of(tile * tq, tq), tq), :]


def _diff_attn_kernel(row_tab, k_tab, q_ref, k_ref, vt_ref, lam_ref, g_ref, o_ref, *scratch, tq, lam_init):
    nq = q_ref.shape[1] // tq

    def logits(row, tile):
        qs = []
        for q in _head_blocks(_tile_rows(q_ref, row, tq)):
            lane = lax.broadcasted_iota(jnp.int32, q.shape, 1)
            zero = jnp.zeros_like(q)
            qs += [jnp.where(lane < HEAD_DIM, q, zero), jnp.where(lane >= HEAD_DIM, q, zero)]
        kts = _head_blocks(_tile_rows(k_ref, tile, tq))
        return [_dot_nt(kts[a // 2], qs[a]) for a in range(len(qs))]

    def values(tile):
        vt = vt_ref[0, :, pl.ds(pl.multiple_of(tile * tq, tq), tq)]
        return [vt[(a // 2) * LANES:(a // 2 + 1) * LANES] for a in range(2 * BLOCKS_PER_STEP)]

    key = lax.broadcasted_iota(jnp.int32, (tq, tq), 0)
    qry = lax.broadcasted_iota(jnp.int32, (tq, tq), 1)
    _flash_sweep(nq, tq, (row_tab, k_tab), logits, values, (key // CHUNK) <= (qry // CHUNK), scratch)

    lp = lam_ref[...]
    lam = (jnp.exp(jnp.sum(lp[0:1] * lp[1:2], axis=1, keepdims=True))
           - jnp.exp(jnp.sum(lp[2:3] * lp[3:4], axis=1, keepdims=True)) + lam_init)

    def finish(row, _):
        outs = _flash_outputs(scratch[-1], row, LANES)
        ys = []
        for k in range(BLOCKS_PER_STEP):
            o = outs[2 * k] - lam * outs[2 * k + 1]
            y = o * lax.rsqrt(jnp.mean(o * o, axis=0, keepdims=True) + EPS) * g_ref[...] * (1.0 - lam_init)
            ys.append(y.T.astype(o_ref.dtype))
        o_ref[0, pl.ds(pl.multiple_of(row * tq, tq), tq), :] = jnp.concatenate(ys, axis=1)
        return 0

    lax.fori_loop(0, nq, finish, 0)


def _attn_call(kernel_fn, name, n_maps, dv, inputs, in_specs, b, s, w, tq):
    nq = s // tq
    assert nq % 2 == 0 and nq >= 4, "the pipelined sweeps handle tiles in pairs"
    wb = BLOCKS_PER_STEP * LANES
    return pl.pallas_call(
        kernel_fn,
        grid_spec=pltpu.PrefetchScalarGridSpec(
            num_scalar_prefetch=2,
            grid=(b, w // wb),
            in_specs=in_specs,
            out_specs=pl.BlockSpec((1, s, wb), lambda bi, h, *_: (bi, 0, h)),
            scratch_shapes=_flash_scratch(nq, n_maps, dv, tq)),
        out_shape=jax.ShapeDtypeStruct((b, s, w), BF16),
        compiler_params=pltpu.CompilerParams(dimension_semantics=("arbitrary",) * 2,
                                             vmem_limit_bytes=VMEM_LIMIT),
        name=name,
    )(*_off_diagonal_tables(nq), *inputs)


def _attn_specs(s):
    wb = BLOCKS_PER_STEP * LANES
    row_blk = pl.BlockSpec((1, s, wb), lambda bi, h, *_: (bi, 0, h))
    vt_blk = pl.BlockSpec((1, wb, s), lambda bi, h, *_: (bi, h, 0))
    return row_blk, vt_blk


def _diff_attn(qd, kd, vdt, lam_params, gnorm_col, *, tq, lam_init):
    b, s, w = qd.shape
    row_blk, vt_blk = _attn_specs(s)
    small = lambda arr: pl.BlockSpec(arr.shape, lambda bi, h, *_: (0, 0))
    return _attn_call(functools.partial(_diff_attn_kernel, tq=tq, lam_init=lam_init), "diff_attn",
                      2 * BLOCKS_PER_STEP, LANES, (qd, kd, vdt, lam_params, gnorm_col),
                      [row_blk, row_blk, vt_blk, small(lam_params), small(gnorm_col)], b, s, w, tq)


def _fox_attn_kernel(row_tab, k_tab, q_ref, k_ref, vt_ref, qb_ref, kb_ref, o_ref, *scratch, tq):
    nq = q_ref.shape[1] // tq

    def logits(row, tile):
        qs = []
        for q, qb in zip(_head_blocks(_tile_rows(q_ref, row, tq)), _head_blocks(_tile_rows(qb_ref, row, tq))):
            lane = lax.broadcasted_iota(jnp.int32, q.shape, 1)
            zero = jnp.zeros_like(q)
            qs += [jnp.concatenate([jnp.where(lane < HEAD_DIM, q, zero), jnp.where(lane < N_BIAS, qb, zero)],
                                   axis=1),
                   jnp.concatenate([jnp.where(lane >= HEAD_DIM, q, zero),
                                    jnp.where((lane >= N_BIAS) & (lane < 2 * N_BIAS), qb, zero)], axis=1)]
        kts = [jnp.concatenate([k, kb], axis=1) for k, kb in zip(_head_blocks(_tile_rows(k_ref, tile, tq)),
                                                                   _head_blocks(_tile_rows(kb_ref, tile, tq)))]
        return [_dot_nt(kts[a // 2], qs[a]) for a in range(len(qs))]

    def values(tile):
        vt = vt_ref[0, :, pl.ds(pl.multiple_of(tile * tq, tq), tq)]
        return [vt[a * HEAD_DIM:(a + 1) * HEAD_DIM] for a in range(2 * BLOCKS_PER_STEP)]

    key = lax.broadcasted_iota(jnp.int32, (tq, tq), 0)
    qry = lax.broadcasted_iota(jnp.int32, (tq, tq), 1)
    _flash_sweep(nq, tq, (row_tab, k_tab), logits, values, key <= qry, scratch)

    def finish(row, _):
        o = jnp.concatenate(_flash_outputs(scratch[-1], row, HEAD_DIM), axis=0)
        o_ref[0, pl.ds(pl.multiple_of(row * tq, tq), tq), :] = o.T.astype(o_ref.dtype)
        return 0

    lax.fori_loop(0, nq, finish, 0)


def _fox_attn(qf, kf, vft, qb, kb, *, tq):
    b, s, w = qf.shape
    row_blk, vt_blk = _attn_specs(s)
    return _attn_call(functools.partial(_fox_attn_kernel, tq=tq), "fox_attn", 2 * BLOCKS_PER_STEP, HEAD_DIM,
                      (qf, kf, vft, qb, kb), [row_blk, row_blk, vt_blk, row_blk, row_blk], b, s, w, tq)


def _out_proj_kernel(x_ref, od_ref, of_ref, wo_ref, g_ref, wr_hi_ref, wr_lo_ref, br_ref,
                     x1_ref, hm_ref, comb_ref, gid_ref):
    x1 = x_ref[...] + (_dot(od_ref[...], wo_ref[0]) + _dot(of_ref[...], wo_ref[1]))
    x1_ref[...] = x1
    hm = _rms(x1, g_ref[...])
    hb, h_lo = _split2(hm)
    hm_ref[...] = hb

    wr_hi = wr_hi_ref[...]
    lt = (_dot_nt(wr_hi, hb) + _dot_nt(wr_lo_ref[...], hb) + _dot_nt(wr_hi, h_lo)) + br_ref[...]
    tm = lt.shape[1]
    g8 = lt[0:8]
    r8 = lax.broadcasted_iota(jnp.int32, (8, tm), 0)
    g8 = jnp.where(r8 < N_GROUPS, g8, NEG_BIG)
    gmax = jnp.max(g8, axis=0, keepdims=True)
    gidx = jnp.min(jnp.where(g8 == gmax, r8, N_GROUPS), axis=0, keepdims=True)
    g_w = 1.0 / jnp.sum(jnp.exp(g8 - gmax), axis=0, keepdims=True)

    e16 = lt[8:8 + N_EXPERTS]
    r16 = lax.broadcasted_iota(jnp.int32, (N_EXPERTS, tm), 0)
    in_group = (r16 // EXPERTS_PER_GROUP) == gidx
    e_sel = jnp.where(in_group, e16, NEG_BIG)
    top1 = jnp.max(e_sel, axis=0, keepdims=True)
    id1 = jnp.min(jnp.where(e_sel == top1, r16, N_EXPERTS), axis=0, keepdims=True)
    e_rest = jnp.where(r16 == id1, NEG_BIG, e_sel)
    top2 = jnp.max(e_rest, axis=0, keepdims=True)
    id2 = jnp.min(jnp.where(e_rest == top2, r16, N_EXPERTS), axis=0, keepdims=True)
    t = jnp.exp(top2 - top1)
    w1 = g_w / (1.0 + t)
    w2 = w1 * t
    r128 = lax.broadcasted_iota(jnp.int32, (LANES, tm), 0)
    gid = gidx.astype(F32)
    comb_t = (jnp.where(r128 == id1, w1, 0.0) + jnp.where(r128 == id2, w2, 0.0)
              + jnp.where(r128 == GID_LANE, gid, 0.0))
    comb_ref[...] = comb_t.T
    gid_ref[...] = jnp.where(r8 == 0, gid, 0.0)


def _out_proj(x2, od, of, wo, g, wr_hi, wr_lo, br, *, tm):
    n, d = x2.shape
    const = lambda *shape: pl.BlockSpec(shape, lambda i: (0,) * len(shape))
    row_blk = lambda cols: pl.BlockSpec((tm, cols), lambda i: (i, 0))
    return pl.pallas_call(
        _out_proj_kernel,
        grid=(n // tm,),
        in_specs=[row_blk(d), row_blk(od.shape[1]), row_blk(of.shape[1]), const(*wo.shape), const(1, d),
                  const(*wr_hi.shape), const(*wr_lo.shape), const(*br.shape)],
        out_specs=[row_blk(d), row_blk(d), row_blk(LANES), pl.BlockSpec((SUBLANES, tm), lambda i: (0, i))],
        out_shape=[jax.ShapeDtypeStruct((n, d), F32), jax.ShapeDtypeStruct((n, d), BF16),
                   jax.ShapeDtypeStruct((n, LANES), F32), jax.ShapeDtypeStruct((SUBLANES, n), F32)],
        compiler_params=pltpu.CompilerParams(dimension_semantics=("arbitrary",),
                                             vmem_limit_bytes=VMEM_LIMIT),
        name="out_proj",
    )(x2, od, of, wo, g, wr_hi, wr_lo, br)


def _moe_kernel(hm_ref, comb_ref, gidr_ref, x1_ref, wg_ref, wu_ref, wd_ref, gf_ref, o_ref,
                tri_ref, hs_ref, cs_ref, ys_ref, posc_ref, tab_ref, *, apply_final):
    i = pl.program_id(0)
    g = pl.program_id(1)
    t = hm_ref.shape[0]
    t_pad = hs_ref.shape[0]
    rb = ROW_BLOCK

    @pl.when((i == 0) & (g == 0))
    def _():
        row = lax.broadcasted_iota(jnp.int32, (t, t), 0)
        col = lax.broadcasted_iota(jnp.int32, (t, t), 1)
        tri_ref[...] = jnp.where(row > col, 1.0, 0.0).astype(BF16)

    @pl.when(g == 0)
    def _():
        comb = comb_ref[...]
        lane = lax.broadcasted_iota(jnp.int32, comb.shape, 1)
        onehot_c = jnp.where(lane == comb[:, GID_LANE:GID_LANE + 1].astype(jnp.int32), 1.0, 0.0)
        sub = lax.broadcasted_iota(jnp.int32, (SUBLANES, t), 0)
        onehot_r = jnp.where(sub == gidr_ref[0:1, :].astype(jnp.int32), 1.0, 0.0)
        tri = tri_ref[...]
        rank_c = _dot(tri, onehot_c.astype(BF16))
        rank_r = _dot_nt(onehot_r.astype(BF16), tri)
        lane1 = lax.broadcasted_iota(jnp.int32, (1, LANES), 1)
        sub1 = lax.broadcasted_iota(jnp.int32, (SUBLANES, 1), 0)
        start_c = jnp.zeros((1, LANES), F32)
        start_r = jnp.zeros((SUBLANES, 1), F32)
        first = jnp.int32(0)
        for k in range(N_GROUPS):
            cnt = jnp.sum(onehot_r[k:k + 1, :]).astype(jnp.int32)
            nblk = lax.shift_right_logical(cnt + (rb - 1), int(np.log2(rb)))
            tab_ref[k] = first
            tab_ref[N_GROUPS + k] = nblk
            row0 = (first * rb).astype(F32)
            start_c = jnp.where(lane1 == k, row0, start_c)
            start_r = jnp.where(sub1 == k, row0, start_r)
            first = first + nblk
        posc_ref[...] = jnp.sum(onehot_c * (rank_c + start_c), axis=1, keepdims=True).astype(jnp.int32)
        pos_r = jnp.sum(onehot_r * (rank_r + start_r), axis=0, keepdims=True).astype(jnp.int32)
        ys_ref[...] = jnp.zeros_like(ys_ref)
        c_parts = _split3(jnp.where(lane < N_EXPERTS, comb, 0.0))
        c_pack = (c_parts[0].astype(F32) + pltpu.roll(c_parts[1].astype(F32), N_EXPERTS, axis=1)
                  + pltpu.roll(c_parts[2].astype(F32), 2 * N_EXPERTS, axis=1)).astype(BF16)
        hm = hm_ref[...]
        for r0 in range(0, t_pad, SORT_ROWS):
            dst = lax.broadcasted_iota(jnp.int32, (SORT_ROWS, t), 0) + r0
            perm = jnp.where(dst == pos_r, 1.0, 0.0).astype(BF16)
            hs_ref[r0:r0 + SORT_ROWS, :] = _dot(perm, hm).astype(hs_ref.dtype)
            cp = _dot(perm, c_pack)
            cs_ref[r0:r0 + SORT_ROWS, :] = (cp + pltpu.roll(cp, LANES - N_EXPERTS, axis=1)
                                            + pltpu.roll(cp, LANES - 2 * N_EXPERTS, axis=1))

    b0 = tab_ref[g]

    def expert_block(b, _):
        off = pl.multiple_of(b * rb, rb)
        rows = hs_ref[pl.ds(off, rb), :]
        cblk = cs_ref[pl.ds(off, rb), :]
        lane = lax.broadcasted_iota(jnp.int32, cblk.shape, 1)
        gates = [_dot(rows, wg_ref[j]) for j in range(EXPERTS_PER_GROUP)]
        ups = [_dot(rows, wu_ref[j]) for j in range(EXPERTS_PER_GROUP)]
        y = None
        for j in range(EXPERTS_PER_GROUP):
            c = jnp.sum(jnp.where(lane == g * EXPERTS_PER_GROUP + j, cblk, 0.0), axis=1, keepdims=True)
            a = gates[j] * jax.nn.sigmoid(gates[j]) * ups[j] * c
            d = _dot(a.astype(BF16), wd_ref[j])
            y = d if y is None else y + d
        ys_ref[pl.ds(off, rb), :] = y.astype(ys_ref.dtype)
        return 0

    lax.fori_loop(b0, b0 + tab_ref[N_GROUPS + g], expert_block, 0)

    @pl.when(g == pl.num_programs(1) - 1)
    def _():
        ys = ys_ref[...]
        for r0 in range(0, t, SORT_ROWS):
            src = lax.broadcasted_iota(jnp.int32, (SORT_ROWS, t_pad), 1)
            unperm = jnp.where(src == posc_ref[r0:r0 + SORT_ROWS, :], 1.0, 0.0).astype(BF16)
            x2 = x1_ref[r0:r0 + SORT_ROWS, :] + _dot(unperm, ys)
            o_ref[r0:r0 + SORT_ROWS, :] = _rms(x2, gf_ref[...]) if apply_final else x2


def _moe(hm, comb, gid_rows, x1, wg, wu, wd, gfin, *, tm, apply_final):
    n, d = x1.shape
    ne, _, de = wg.shape
    epg = EXPERTS_PER_GROUP
    t_pad = tm + N_GROUPS * ROW_BLOCK
    row_blk = lambda cols, **kw: pl.BlockSpec((tm, cols), lambda i, g: (i, 0), **kw)
    once = dict(pipeline_mode=pl.Buffered(1))
    return pl.pallas_call(
        functools.partial(_moe_kernel, apply_final=apply_final),
        grid=(n // tm, ne // epg),
        in_specs=[row_blk(d), row_blk(LANES), pl.BlockSpec((SUBLANES, tm), lambda i, g: (0, i)), row_blk(d, **once),
                  pl.BlockSpec((epg, d, de), lambda i, g: (g, 0, 0)),
                  pl.BlockSpec((epg, d, de), lambda i, g: (g, 0, 0)),
                  pl.BlockSpec((epg, de, d), lambda i, g: (g, 0, 0)),
                  pl.BlockSpec((1, d), lambda i, g: (0, 0))],
        out_specs=row_blk(d, **once),
        out_shape=jax.ShapeDtypeStruct((n, d), F32),
        scratch_shapes=[pltpu.VMEM((tm, tm), BF16), pltpu.VMEM((t_pad, d), BF16), pltpu.VMEM((t_pad, LANES), F32),
                        pltpu.VMEM((t_pad, d), BF16), pltpu.VMEM((tm, 1), jnp.int32),
                        pltpu.SMEM((2 * N_GROUPS,), jnp.int32)],
        compiler_params=pltpu.CompilerParams(dimension_semantics=("arbitrary", "arbitrary"),
                                             vmem_limit_bytes=VMEM_LIMIT),
        name="moe",
    )(hm, comb, gid_rows, x1, wg, wu, wd, gfin)


def kernel(x, norm_attn_g, w_in, b_forget, lambda_q1, lambda_k1, lambda_q2, lambda_k2, diff_norm_g, w_out,
           norm_ffn_g, router_group_w, router_group_b, router_expert_w, router_expert_b, w_gate, w_up, w_down,
           norm_final_g):
    b, s, d = x.shape
    depth = w_in.shape[0]
    n = b * s
    diff_w = N_DIFF_HEADS * 2 * HEAD_DIM
    fox_w = N_FOX_HEADS * HEAD_DIM

    pos = jnp.arange(s, dtype=F32)
    inv_freq = 1.0 / (ROPE_THETA ** (jnp.arange(0, HEAD_DIM, 2, dtype=F32) / HEAD_DIM))
    ang = pos[:, None] * inv_freq[None, :]
    ang = jnp.concatenate([ang, ang, ang, ang], axis=-1)
    first_half = (jnp.arange(LANES) % HEAD_DIM) < HEAD_DIM // 2
    cos2, sin2 = jnp.cos(ang), jnp.where(first_half, -jnp.sin(ang), jnp.sin(ang))

    x2 = x.reshape(n, d)
    for l in range(depth):
        lam_init = 0.8 - 0.6 * float(np.exp(-0.3 * l))
        w = w_in[l]
        offs = np.cumsum([0, diff_w, diff_w, diff_w, fox_w, fox_w, fox_w, N_FOX_HEADS])
        seg = [w[:, offs[k]:offs[k + 1]] for k in range(7)]
        w_stack = jnp.stack([seg[0], seg[1], seg[3], seg[4]]).astype(BF16)
        wt_stack = jnp.stack([seg[2].T, seg[5].T]).astype(BF16)
        wf = jnp.pad(seg[6], ((0, 0), (0, LANES - N_FOX_HEADS)))
        wf_cat = jnp.concatenate(_split2(wf), axis=1)
        bf_pad = jnp.pad(b_forget[l], (0, LANES - N_FOX_HEADS)).reshape(1, LANES)

        qd, kd, vdt, qf, kf, vft, qb, kb = _in_proj(
            x2, norm_attn_g[l].reshape(1, d), cos2, sin2, w_stack, wt_stack, wf_cat, bf_pad,
            batch=b, seq=s, tm=512)

        to3 = lambda t: t.reshape(b, s, t.shape[-1])
        lam_params = jnp.stack([lambda_q1[l], lambda_k1[l], lambda_q2[l], lambda_k2[l]])
        od = _diff_attn(to3(qd), to3(kd), vdt, lam_params, diff_norm_g[l].reshape(-1, 1),
                        tq=512, lam_init=lam_init)
        of = _fox_attn(to3(qf), to3(kf), vft, to3(qb), to3(kb), tq=512)

        wo = w_out[l].astype(BF16).reshape(2, -1, d)
        wr = jnp.zeros((d, LANES), F32)
        wr = wr.at[:, :N_GROUPS].set(router_group_w[l]).at[:, 8:8 + N_EXPERTS].set(router_expert_w[l])
        wr_hi, wr_lo = _split2(wr.T)
        br = jnp.zeros((LANES,), F32)
        br = br.at[:N_GROUPS].set(router_group_b[l]).at[8:8 + N_EXPERTS].set(router_expert_b[l]).reshape(LANES, 1)
        x1, hm, comb, gid_rows = _out_proj(x2, od.reshape(n, -1), of.reshape(n, -1), wo, norm_ffn_g[l].reshape(1, d),
                                 wr_hi, wr_lo, br, tm=512)

        x2 = _moe(hm, comb, gid_rows, x1, w_gate[l].astype(BF16), w_up[l].astype(BF16), w_down[l].astype(BF16),
                  norm_final_g.reshape(1, d), tm=1024, apply_final=(l == depth - 1))
    return x2.reshape(b, s, d)
```

```python
import functools

import numpy as np
import jax
import jax.numpy as jnp
from jax import lax
from jax.experimental import pallas as pl
from jax.experimental.pallas import tpu as pltpu

CHUNK = 64
HEAD_DIM = 64
ROPE_THETA = 10000.0
EPS = 1e-6
LOG2E = 1.4426950408889634
N_DIFF_HEADS = 4
N_FOX_HEADS = 8
N_GROUPS = 4
EXPERTS_PER_GROUP = 4
N_EXPERTS = N_GROUPS * EXPERTS_PER_GROUP

LANES = 128
NEG_BIG = -1e30
VMEM_LIMIT = 56 * 1024 * 1024
N_BIAS = 6
SUBLANES = 8
GID_LANE = N_EXPERTS
ROW_BLOCK = 128
ONES_ROWS = 16
BLOCKS_PER_STEP = 2
MOE_TILES = 2
SORT_ROWS = 512

BF16 = jnp.bfloat16
F32 = jnp.float32
_NT = (((1,), (1,)), ((), ()))


def _dot(a, b):
    return jnp.dot(a, b, preferred_element_type=F32)


def _dot_nt(a, b):
    return lax.dot_general(a, b, _NT, preferred_element_type=F32)


def _split2(x):
    hi = x.astype(BF16)
    lo = (x - hi.astype(F32)).astype(BF16)
    return hi, lo


def _split3(x):
    hi = x.astype(BF16)
    r = x - hi.astype(F32)
    mid = r.astype(BF16)
    lo = (r - mid.astype(F32)).astype(BF16)
    return hi, mid, lo


def _rms(x, g):
    return x * lax.rsqrt(jnp.mean(x * x, axis=-1, keepdims=True) + EPS) * g


def _pack_parts(parts):
    packed = parts[0].astype(F32)
    for k in (1, 2):
        packed = packed + pltpu.roll(parts[k].astype(F32), k * N_FOX_HEADS, axis=1)
    return packed.astype(BF16)


def _in_proj_kernel(x_ref, g_ref, cos_ref, sin_ref, w_ref, wt_ref, wf_cat_ref, bf_ref, tri_ref,
                    eq_ref, ek_ref, cq_ref, ck_ref,
                    qd_ref, kd_ref, vdt_ref, qf_ref, kf_ref, vft_ref, qb_ref, kb_ref, carry_ref,
                    *, tiles_per_seq):
    i = pl.program_id(0)
    h = _rms(x_ref[...], g_ref[...])
    hb, h_lo = _split2(h)
    cos = cos_ref[...]
    sin = sin_ref[...]
    scale = HEAD_DIM ** -0.5 * LOG2E
    lane = lax.broadcasted_iota(jnp.int32, cos.shape, 1)
    first_half = (lane % HEAD_DIM) < HEAD_DIM // 2

    def rope(w_idx, out_ref, mul):
        a = _dot(hb, w_ref[w_idx])
        for c in range(a.shape[1] // LANES):
            sl = slice(c * LANES, (c + 1) * LANES)
            x = a[:, sl]
            rot = jnp.where(first_half, pltpu.roll(x, LANES - HEAD_DIM // 2, axis=1),
                            pltpu.roll(x, HEAD_DIM // 2, axis=1))
            out_ref[:, sl] = ((x * cos + rot * sin) * mul).astype(out_ref.dtype)

    rope(0, qd_ref, scale)
    rope(1, kd_ref, 1.0)
    qf_ref[...] = (_dot(hb, w_ref[2]) * scale).astype(qf_ref.dtype)
    kf_ref[...] = _dot(hb, w_ref[3]).astype(kf_ref.dtype)
    vdt_ref[0] = _dot_nt(wt_ref[0], hb).astype(vdt_ref.dtype)
    vft_ref[0] = _dot_nt(wt_ref[1], hb).astype(vft_ref.dtype)

    wf_cat = wf_cat_ref[...]
    zz = _dot(hb, wf_cat)
    z = ((zz[:, :LANES] + zz[:, LANES:]) + _dot(h_lo, wf_cat[:, :LANES])) + bf_ref[...]
    log_f = jnp.minimum(z, 0.0) - jnp.log1p(jnp.exp(-jnp.abs(z)))
    valid = lane < N_FOX_HEADS
    log_f = jnp.where(valid, log_f, 0.0)

    @pl.when(i % tiles_per_seq == 0)
    def _():
        carry_ref[...] = jnp.zeros_like(carry_ref)

    r = _dot(tri_ref[...], _pack_parts(_split3(log_f)))
    cum = (r + pltpu.roll(r, LANES - N_FOX_HEADS, axis=1)) + pltpu.roll(r, LANES - 2 * N_FOX_HEADS, axis=1)
    cum = jnp.where(valid, cum + carry_ref[...], 0.0)
    carry_ref[...] = cum[cum.shape[0] - 1:, :]

    f_pack = _pack_parts(_split3(cum * LOG2E))
    qb_ref[...] = (cq_ref[...] + _dot(f_pack, eq_ref[...])).astype(qb_ref.dtype)
    kb_ref[...] = (ck_ref[...] + _dot(f_pack, ek_ref[...])).astype(kb_ref.dtype)


def _bias_placement():
    width = (N_FOX_HEADS // 2) * LANES
    eq = np.zeros((LANES, width), np.float32)
    ek = np.zeros((LANES, width), np.float32)
    cq = np.zeros((1, width), np.float32)
    ck = np.zeros((1, width), np.float32)
    for h in range(N_FOX_HEADS):
        base = (h // 2) * LANES + (h % 2) * N_BIAS
        for part in range(3):
            eq[part * N_FOX_HEADS + h, base + part] = 1.0
            ek[part * N_FOX_HEADS + h, base + 3 + part] = -1.0
        cq[0, base + 3:base + 6] = 1.0
        ck[0, base:base + 3] = 1.0
    return jnp.asarray(eq, BF16), jnp.asarray(ek, BF16), jnp.asarray(cq), jnp.asarray(ck)


def _in_proj(x2, g, cos2, sin2, w_stack, wt_stack, wf_cat, bf_pad, *, batch, seq, tm):
    n, d = x2.shape
    tiles_per_seq = seq // tm
    tri = jnp.tril(jnp.ones((tm, tm), F32)).astype(BF16)
    eq, ek, cq, ck = _bias_placement()
    wcols = w_stack.shape[2]
    const = lambda *shape: pl.BlockSpec(shape, lambda i: (0,) * len(shape))
    row_blk = lambda cols: pl.BlockSpec((tm, cols), lambda i: (i, 0))
    pos_blk = pl.BlockSpec((tm, LANES), lambda i: (i % tiles_per_seq, 0))
    vt_blk = pl.BlockSpec((1, wcols, tm), lambda i: (i // tiles_per_seq, 0, i % tiles_per_seq))
    out_bf = jax.ShapeDtypeStruct((n, wcols), BF16)
    out_vt = jax.ShapeDtypeStruct((batch, wcols, seq), BF16)
    return pl.pallas_call(
        functools.partial(_in_proj_kernel, tiles_per_seq=tiles_per_seq),
        grid=(n // tm,),
        in_specs=[row_blk(d), const(1, d), pos_blk, pos_blk, const(*w_stack.shape), const(*wt_stack.shape),
                  const(*wf_cat.shape), const(1, LANES), const(tm, tm),
                  const(*eq.shape), const(*ek.shape), const(*cq.shape), const(*ck.shape)],
        out_specs=[row_blk(wcols), row_blk(wcols), vt_blk, row_blk(wcols), row_blk(wcols), vt_blk,
                   row_blk(eq.shape[1]), row_blk(ek.shape[1])],
        out_shape=[out_bf, out_bf, out_vt, out_bf, out_bf, out_vt,
                   jax.ShapeDtypeStruct((n, eq.shape[1]), BF16), jax.ShapeDtypeStruct((n, ek.shape[1]), BF16)],
        scratch_shapes=[pltpu.VMEM((1, LANES), F32)],
        compiler_params=pltpu.CompilerParams(dimension_semantics=("arbitrary",),
                                             vmem_limit_bytes=VMEM_LIMIT),
        name="in_proj",
    )(x2, g, cos2, sin2, w_stack, wt_stack, wf_cat, bf_pad, tri, eq, ek, cq, ck)


def _flash_sweep(nq, tq, tabs, logits_fn, values_fn, mask, scratch):
    s0, s1, c0, c1, m_ref, acc_ref = scratch
    row_tab, k_tab = tabs
    n_maps = m_ref.shape[1]
    n_off = nq * (nq - 1) // 2
    ones = jnp.ones((ONES_ROWS, tq), BF16)

    def stage_a(row, tile, s_buf, c_buf, masked):
        sts = logits_fn(row, tile)
        for a in range(n_maps):
            st = jnp.where(mask, sts[a], NEG_BIG) if masked else sts[a]
            s_buf[a] = st
            c_buf[a] = jnp.max(st, axis=0, keepdims=True)

    def stage_b(row, tile, s_buf, c_buf, first):
        vts = values_fn(tile)
        for a in range(n_maps):
            vt = jnp.concatenate([vts[a], ones], axis=0)
            if first:
                m_new = c_buf[a]
                acc_ref[row, a] = _dot(vt, jnp.exp2(s_buf[a] - m_new).astype(BF16))
            else:
                m = m_ref[row, a]
                m_new = jnp.maximum(m, c_buf[a])
                alpha = jnp.exp2(m - m_new)
                acc_ref[row, a] = alpha * acc_ref[row, a] + _dot(vt, jnp.exp2(s_buf[a] - m_new).astype(BF16))
            m_ref[row, a] = m_new

    stage_a(0, 0, s0, c0, True)

    def diag_pair(jj, _):
        r = 2 * jj
        stage_a(r + 1, r + 1, s1, c1, True)
        stage_b(r, r, s0, c0, True)
        stage_a(r + 2, r + 2, s0, c0, True)
        stage_b(r + 1, r + 1, s1, c1, True)
        return 0

    lax.fori_loop(0, (nq - 2) // 2, diag_pair, 0)
    stage_a(nq - 1, nq - 1, s1, c1, True)
    stage_b(nq - 2, nq - 2, s0, c0, True)

    stage_a(row_tab[0], k_tab[0], s0, c0, False)
    stage_b(nq - 1, nq - 1, s1, c1, True)

    def off_pair(jj, _):
        t = 2 * jj
        stage_a(row_tab[t + 1], k_tab[t + 1], s1, c1, False)
        stage_b(row_tab[t], k_tab[t], s0, c0, False)
        stage_a(row_tab[t + 2], k_tab[t + 2], s0, c0, False)
        stage_b(row_tab[t + 1], k_tab[t + 1], s1, c1, False)
        return 0

    lax.fori_loop(0, (n_off - 2) // 2, off_pair, 0)
    stage_a(row_tab[n_off - 1], k_tab[n_off - 1], s1, c1, False)
    stage_b(row_tab[n_off - 2], k_tab[n_off - 2], s0, c0, False)
    stage_b(row_tab[n_off - 1], k_tab[n_off - 1], s1, c1, False)


def _flash_scratch(nq, n_maps, dv, tq):
    s_buf = pltpu.VMEM((n_maps, tq, tq), F32)
    c_buf = pltpu.VMEM((n_maps, 1, tq), F32)
    return [s_buf, s_buf, c_buf, c_buf, pltpu.VMEM((nq, n_maps, 1, tq), F32),
            pltpu.VMEM((nq, n_maps, dv + ONES_ROWS, tq), F32)]


def _off_diagonal_tables(nq):
    pairs = [(r, k) for r in range(nq) for k in range(r)]
    return (jnp.asarray([p[0] for p in pairs], jnp.int32), jnp.asarray([p[1] for p in pairs], jnp.int32))


def _flash_outputs(acc_ref, row, dv):
    return [acc_ref[row, a, :dv, :] / acc_ref[row, a, dv:dv + 1, :] for a in range(acc_ref.shape[1])]


def _head_blocks(block):
    return [block[:, k * LANES:(k + 1) * LANES] for k in range(BLOCKS_PER_STEP)]


def _tile_rows(ref, tile, tq):
    return ref[0, pl.ds(pl.multiple_of(tile * tq, tq), tq), :]


def _diff_attn_kernel(row_tab, k_tab, q_ref, k_ref, vt_ref, lam_ref, g_ref, o_ref, *scratch, tq, lam_init):
    nq = q_ref.shape[1] // tq

    def logits(row, tile):
        qs = []
        for q in _head_blocks(_tile_rows(q_ref, row, tq)):
            lane = lax.broadcasted_iota(jnp.int32, q.shape, 1)
            zero = jnp.zeros_like(q)
            qs += [jnp.where(lane < HEAD_DIM, q, zero), jnp.where(lane >= HEAD_DIM, q, zero)]
        kts = _head_blocks(_tile_rows(k_ref, tile, tq))
        return [_dot_nt(kts[a // 2], qs[a]) for a in range(len(qs))]

    def values(tile):
        vt = vt_ref[0, :, pl.ds(pl.multiple_of(tile * tq, tq), tq)]
        return [vt[(a // 2) * LANES:(a // 2 + 1) * LANES] for a in range(2 * BLOCKS_PER_STEP)]

    key = lax.broadcasted_iota(jnp.int32, (tq, tq), 0)
    qry = lax.broadcasted_iota(jnp.int32, (tq, tq), 1)
    _flash_sweep(nq, tq, (row_tab, k_tab), logits, values, (key // CHUNK) <= (qry // CHUNK), scratch)

    lp = lam_ref[...]
    lam = (jnp.exp(jnp.sum(lp[0:1] * lp[1:2], axis=1, keepdims=True))
           - jnp.exp(jnp.sum(lp[2:3] * lp[3:4], axis=1, keepdims=True)) + lam_init)

    def finish(row, _):
        outs = _flash_outputs(scratch[-1], row, LANES)
        ys = []
        for k in range(BLOCKS_PER_STEP):
            o = outs[2 * k] - lam * outs[2 * k + 1]
            y = o * lax.rsqrt(jnp.mean(o * o, axis=0, keepdims=True) + EPS) * g_ref[...] * (1.0 - lam_init)
            ys.append(y.T.astype(o_ref.dtype))
        o_ref[0, pl.ds(pl.multiple_of(row * tq, tq), tq), :] = jnp.concatenate(ys, axis=1)
        return 0

    lax.fori_loop(0, nq, finish, 0)


def _attn_call(kernel_fn, name, n_maps, dv, inputs, in_specs, b, s, w, tq):
    nq = s // tq
    assert nq % 2 == 0 and nq >= 4, "the pipelined sweeps handle tiles in pairs"
    wb = BLOCKS_PER_STEP * LANES
    return pl.pallas_call(
        kernel_fn,
        grid_spec=pltpu.PrefetchScalarGridSpec(
            num_scalar_prefetch=2,
            grid=(b, w // wb),
            in_specs=in_specs,
            out_specs=pl.BlockSpec((1, s, wb), lambda bi, h, *_: (bi, 0, h)),
            scratch_shapes=_flash_scratch(nq, n_maps, dv, tq)),
        out_shape=jax.ShapeDtypeStruct((b, s, w), BF16),
        compiler_params=pltpu.CompilerParams(dimension_semantics=("arbitrary",) * 2,
                                             vmem_limit_bytes=VMEM_LIMIT),
        name=name,
    )(*_off_diagonal_tables(nq), *inputs)


def _attn_specs(s):
    wb = BLOCKS_PER_STEP * LANES
    row_blk = pl.BlockSpec((1, s, wb), lambda bi, h, *_: (bi, 0, h))
    vt_blk = pl.BlockSpec((1, wb, s), lambda bi, h, *_: (bi, h, 0))
    return row_blk, vt_blk


def _diff_attn(qd, kd, vdt, lam_params, gnorm_col, *, tq, lam_init):
    b, s, w = qd.shape
    row_blk, vt_blk = _attn_specs(s)
    small = lambda arr: pl.BlockSpec(arr.shape, lambda bi, h, *_: (0, 0))
    return _attn_call(functools.partial(_diff_attn_kernel, tq=tq, lam_init=lam_init), "diff_attn",
                      2 * BLOCKS_PER_STEP, LANES, (qd, kd, vdt, lam_params, gnorm_col),
                      [row_blk, row_blk, vt_blk, small(lam_params), small(gnorm_col)], b, s, w, tq)


def _fox_attn_kernel(row_tab, k_tab, q_ref, k_ref, vt_ref, qb_ref, kb_ref, o_ref, *scratch, tq):
    nq = q_ref.shape[1] // tq

    def logits(row, tile):
        qs = []
        for q, qb in zip(_head_blocks(_tile_rows(q_ref, row, tq)), _head_blocks(_tile_rows(qb_ref, row, tq))):
            lane = lax.broadcasted_iota(jnp.int32, q.shape, 1)
            zero = jnp.zeros_like(q)
            qs += [jnp.concatenate([jnp.where(lane < HEAD_DIM, q, zero), jnp.where(lane < N_BIAS, qb, zero)],
                                   axis=1),
                   jnp.concatenate([jnp.where(lane >= HEAD_DIM, q, zero),
                                    jnp.where((lane >= N_BIAS) & (lane < 2 * N_BIAS), qb, zero)], axis=1)]
        kts = [jnp.concatenate([k, kb], axis=1) for k, kb in zip(_head_blocks(_tile_rows(k_ref, tile, tq)),
                                                                   _head_blocks(_tile_rows(kb_ref, tile, tq)))]
        return [_dot_nt(kts[a // 2], qs[a]) for a in range(len(qs))]

    def values(tile):
        vt = vt_ref[0, :, pl.ds(pl.multiple_of(tile * tq, tq), tq)]
        return [vt[a * HEAD_DIM:(a + 1) * HEAD_DIM] for a in range(2 * BLOCKS_PER_STEP)]

    key = lax.broadcasted_iota(jnp.int32, (tq, tq), 0)
    qry = lax.broadcasted_iota(jnp.int32, (tq, tq), 1)
    _flash_sweep(nq, tq, (row_tab, k_tab), logits, values, key <= qry, scratch)

    def finish(row, _):
        o = jnp.concatenate(_flash_outputs(scratch[-1], row, HEAD_DIM), axis=0)
        o_ref[0, pl.ds(pl.multiple_of(row * tq, tq), tq), :] = o.T.astype(o_ref.dtype)
        return 0

    lax.fori_loop(0, nq, finish, 0)


def _fox_attn(qf, kf, vft, qb, kb, *, tq):
    b, s, w = qf.shape
    row_blk, vt_blk = _attn_specs(s)
    return _attn_call(functools.partial(_fox_attn_kernel, tq=tq), "fox_attn", 2 * BLOCKS_PER_STEP, HEAD_DIM,
                      (qf, kf, vft, qb, kb), [row_blk, row_blk, vt_blk, row_blk, row_blk], b, s, w, tq)


def _out_proj_kernel(x_ref, od_ref, of_ref, wo_ref, g_ref, wr_hi_ref, wr_lo_ref, br_ref,
                     x1_ref, hm_ref, comb_ref):
    x1 = x_ref[...] + (_dot(od_ref[...], wo_ref[0]) + _dot(of_ref[...], wo_ref[1]))
    x1_ref[...] = x1
    hm = _rms(x1, g_ref[...])
    hb, h_lo = _split2(hm)
    hm_ref[...] = hb

    wr_hi = wr_hi_ref[...]
    lt = (_dot_nt(wr_hi, hb) + _dot_nt(wr_lo_ref[...], hb) + _dot_nt(wr_hi, h_lo)) + br_ref[...]
    tm = lt.shape[1]
    g8 = lt[0:8]
    r8 = lax.broadcasted_iota(jnp.int32, (8, tm), 0)
    g8 = jnp.where(r8 < N_GROUPS, g8, NEG_BIG)
    gmax = jnp.max(g8, axis=0, keepdims=True)
    gidx = jnp.min(jnp.where(g8 == gmax, r8, N_GROUPS), axis=0, keepdims=True)
    g_w = 1.0 / jnp.sum(jnp.exp(g8 - gmax), axis=0, keepdims=True)

    e16 = lt[8:8 + N_EXPERTS]
    r16 = lax.broadcasted_iota(jnp.int32, (N_EXPERTS, tm), 0)
    in_group = (r16 // EXPERTS_PER_GROUP) == gidx
    e_sel = jnp.where(in_group, e16, NEG_BIG)
    top1 = jnp.max(e_sel, axis=0, keepdims=True)
    id1 = jnp.min(jnp.where(e_sel == top1, r16, N_EXPERTS), axis=0, keepdims=True)
    e_rest = jnp.where(r16 == id1, NEG_BIG, e_sel)
    top2 = jnp.max(e_rest, axis=0, keepdims=True)
    id2 = jnp.min(jnp.where(e_rest == top2, r16, N_EXPERTS), axis=0, keepdims=True)
    t = jnp.exp(top2 - top1)
    w1 = g_w / (1.0 + t)
    w2 = w1 * t
    r128 = lax.broadcasted_iota(jnp.int32, (LANES, tm), 0)
    gid = gidx.astype(F32)
    comb_t = (jnp.where(r128 == id1, w1, 0.0) + jnp.where(r128 == id2, w2, 0.0)
              + jnp.where(r128 == GID_LANE, gid, 0.0))
    comb_ref[...] = comb_t.T


def _out_proj(x2, od, of, wo, g, wr_hi, wr_lo, br, *, tm):
    n, d = x2.shape
    const = lambda *shape: pl.BlockSpec(shape, lambda i: (0,) * len(shape))
    row_blk = lambda cols: pl.BlockSpec((tm, cols), lambda i: (i, 0))
    return pl.pallas_call(
        _out_proj_kernel,
        grid=(n // tm,),
        in_specs=[row_blk(d), row_blk(od.shape[1]), row_blk(of.shape[1]), const(*wo.shape), const(1, d),
                  const(*wr_hi.shape), const(*wr_lo.shape), const(*br.shape)],
        out_specs=[row_blk(d), row_blk(d), row_blk(LANES)],
        out_shape=[jax.ShapeDtypeStruct((n, d), F32), jax.ShapeDtypeStruct((n, d), BF16),
                   jax.ShapeDtypeStruct((n, LANES), F32)],
        compiler_params=pltpu.CompilerParams(dimension_semantics=("arbitrary",),
                                             vmem_limit_bytes=VMEM_LIMIT),
        name="out_proj",
    )(x2, od, of, wo, g, wr_hi, wr_lo, br)


def _moe_kernel(hm_ref, comb_ref, x1_ref, wg_ref, wu_ref, wd_ref, gf_ref, o_ref,
                tri_ref, hs_ref, cs_ref, posc_ref, tab_ref, *, apply_final):
    g = pl.program_id(1)
    k = pl.program_id(2)
    t = hm_ref.shape[0]
    t_pad = hs_ref.shape[1]
    rb = ROW_BLOCK

    @pl.when((pl.program_id(0) == 0) & (g == 0) & (k == 0))
    def _():
        row = lax.broadcasted_iota(jnp.int32, (t, t), 0)
        col = lax.broadcasted_iota(jnp.int32, (t, t), 1)
        tri_ref[...] = jnp.where(row > col, 1.0, 0.0).astype(BF16)

    @pl.when(g == 0)
    def _():
        comb = comb_ref[...]
        lane = lax.broadcasted_iota(jnp.int32, comb.shape, 1)
        onehot_c = jnp.where(lane == comb[:, GID_LANE:GID_LANE + 1].astype(jnp.int32), 1.0, 0.0)
        rank_c = _dot(tri_ref[...], onehot_c.astype(BF16))
        counts = jnp.sum(onehot_c, axis=0, keepdims=True)
        lane1 = lax.broadcasted_iota(jnp.int32, (1, LANES), 1)
        start_c = jnp.zeros((1, LANES), F32)
        first = jnp.int32(0)
        for grp in range(N_GROUPS):
            cnt = jnp.sum(jnp.where(lane1 == grp, counts, 0.0)).astype(jnp.int32)
            nblk = lax.shift_right_logical(cnt + (rb - 1), int(np.log2(rb)))
            tab_ref[k, grp] = first
            tab_ref[k, N_GROUPS + grp] = nblk
            start_c = jnp.where(lane1 == grp, (first * rb).astype(F32), start_c)
            first = first + nblk
        dest = onehot_c * (rank_c + start_c)
        posc_ref[k] = jnp.sum(dest, axis=1, keepdims=True).astype(jnp.int32)
        pos_r = jnp.sum(dest.T, axis=0, keepdims=True).astype(jnp.int32)
        c_parts = _split3(jnp.where(lane < N_EXPERTS, comb, 0.0))
        c_pack = (c_parts[0].astype(F32) + pltpu.roll(c_parts[1].astype(F32), N_EXPERTS, axis=1)
                  + pltpu.roll(c_parts[2].astype(F32), 2 * N_EXPERTS, axis=1)).astype(BF16)
        hm = hm_ref[...]
        for r0 in range(0, t_pad, SORT_ROWS):
            dst = lax.broadcasted_iota(jnp.int32, (SORT_ROWS, t), 0) + r0
            perm = jnp.where(dst == pos_r, 1.0, 0.0).astype(BF16)
            hs_ref[k, r0:r0 + SORT_ROWS, :] = _dot(perm, hm).astype(hs_ref.dtype)
            cp = _dot(perm, c_pack)
            cs_ref[k, r0:r0 + SORT_ROWS, :] = (cp + pltpu.roll(cp, LANES - N_EXPERTS, axis=1)
                                               + pltpu.roll(cp, LANES - 2 * N_EXPERTS, axis=1))

    b0 = tab_ref[k, g]

    def expert_block(b, _):
        off = pl.multiple_of(b * rb, rb)
        rows = hs_ref[k, pl.ds(off, rb), :]
        cblk = cs_ref[k, pl.ds(off, rb), :]
        lane = lax.broadcasted_iota(jnp.int32, cblk.shape, 1)
        gates = [_dot(rows, wg_ref[j]) for j in range(EXPERTS_PER_GROUP)]
        ups = [_dot(rows, wu_ref[j]) for j in range(EXPERTS_PER_GROUP)]
        y = None
        for j in range(EXPERTS_PER_GROUP):
            c = jnp.sum(jnp.where(lane == g * EXPERTS_PER_GROUP + j, cblk, 0.0), axis=1, keepdims=True)
            a = gates[j] * jax.nn.sigmoid(gates[j]) * ups[j] * c
            d = _dot(a.astype(BF16), wd_ref[j])
            y = d if y is None else y + d
        hs_ref[k, pl.ds(off, rb), :] = y.astype(hs_ref.dtype)
        return 0

    lax.fori_loop(b0, b0 + tab_ref[k, N_GROUPS + g], expert_block, 0)

    @pl.when(g == pl.num_programs(1) - 1)
    def _():
        ys = hs_ref[k]
        for r0 in range(0, t, SORT_ROWS):
            src = lax.broadcasted_iota(jnp.int32, (SORT_ROWS, t_pad), 1)
            unperm = jnp.where(src == posc_ref[k, r0:r0 + SORT_ROWS, :], 1.0, 0.0).astype(BF16)
            x2 = x1_ref[r0:r0 + SORT_ROWS, :] + _dot(unperm, ys)
            o_ref[r0:r0 + SORT_ROWS, :] = _rms(x2, gf_ref[...]) if apply_final else x2


def _moe(hm, comb, x1, wg, wu, wd, gfin, *, tm, apply_final):
    n, d = x1.shape
    ne, _, de = wg.shape
    epg = EXPERTS_PER_GROUP
    n_groups = ne // epg
    t_pad = tm + N_GROUPS * ROW_BLOCK
    assert t_pad % SORT_ROWS == 0 and tm % SORT_ROWS == 0 and n % (MOE_TILES * tm) == 0
    first_blk = lambda cols: pl.BlockSpec(
        (tm, cols), lambda p, g, k: (MOE_TILES * p + jnp.where(g == 0, k, MOE_TILES - 1), 0))
    last_blk = lambda cols: pl.BlockSpec(
        (tm, cols), lambda p, g, k: (MOE_TILES * p + jnp.where(g == n_groups - 1, k, 0), 0),
        pipeline_mode=pl.Buffered(1))
    w_blk = lambda *shape: pl.BlockSpec((epg,) + shape, lambda p, g, k: (g, 0, 0))
    return pl.pallas_call(
        functools.partial(_moe_kernel, apply_final=apply_final),
        grid=(n // (MOE_TILES * tm), n_groups, MOE_TILES),
        in_specs=[first_blk(d), first_blk(LANES), last_blk(d), w_blk(d, de), w_blk(d, de), w_blk(de, d),
                  pl.BlockSpec((1, d), lambda p, g, k: (0, 0))],
        out_specs=last_blk(d),
        out_shape=jax.ShapeDtypeStruct((n, d), F32),
        scratch_shapes=[pltpu.VMEM((tm, tm), BF16), pltpu.VMEM((MOE_TILES, t_pad, d), BF16),
                        pltpu.VMEM((MOE_TILES, t_pad, LANES), F32), pltpu.VMEM((MOE_TILES, tm, 1), jnp.int32),
                        pltpu.SMEM((MOE_TILES, 2 * N_GROUPS), jnp.int32)],
        compiler_params=pltpu.CompilerParams(dimension_semantics=("arbitrary",) * 3,
                                             vmem_limit_bytes=VMEM_LIMIT),
        name="moe",
    )(hm, comb, x1, wg, wu, wd, gfin)


def kernel(x, norm_attn_g, w_in, b_forget, lambda_q1, lambda_k1, lambda_q2, lambda_k2, diff_norm_g, w_out,
           norm_ffn_g, router_group_w, router_group_b, router_expert_w, router_expert_b, w_gate, w_up, w_down,
           norm_final_g):
    b, s, d = x.shape
    depth = w_in.shape[0]
    n = b * s
    diff_w = N_DIFF_HEADS * 2 * HEAD_DIM
    fox_w = N_FOX_HEADS * HEAD_DIM

    pos = jnp.arange(s, dtype=F32)
    inv_freq = 1.0 / (ROPE_THETA ** (jnp.arange(0, HEAD_DIM, 2, dtype=F32) / HEAD_DIM))
    ang = pos[:, None] * inv_freq[None, :]
    ang = jnp.concatenate([ang, ang, ang, ang], axis=-1)
    first_half = (jnp.arange(LANES) % HEAD_DIM) < HEAD_DIM // 2
    cos2, sin2 = jnp.cos(ang), jnp.where(first_half, -jnp.sin(ang), jnp.sin(ang))

    x2 = x.reshape(n, d)
    for l in range(depth):
        lam_init = 0.8 - 0.6 * float(np.exp(-0.3 * l))
        w = w_in[l]
        offs = np.cumsum([0, diff_w, diff_w, diff_w, fox_w, fox_w, fox_w, N_FOX_HEADS])
        seg = [w[:, offs[k]:offs[k + 1]] for k in range(7)]
        w_stack = jnp.stack([seg[0], seg[1], seg[3], seg[4]]).astype(BF16)
        wt_stack = jnp.stack([seg[2].T, seg[5].T]).astype(BF16)
        wf = jnp.pad(seg[6], ((0, 0), (0, LANES - N_FOX_HEADS)))
        wf_cat = jnp.concatenate(_split2(wf), axis=1)
        bf_pad = jnp.pad(b_forget[l], (0, LANES - N_FOX_HEADS)).reshape(1, LANES)

        qd, kd, vdt, qf, kf, vft, qb, kb = _in_proj(
            x2, norm_attn_g[l].reshape(1, d), cos2, sin2, w_stack, wt_stack, wf_cat, bf_pad,
            batch=b, seq=s, tm=512)

        to3 = lambda t: t.reshape(b, s, t.shape[-1])
        lam_params = jnp.stack([lambda_q1[l], lambda_k1[l], lambda_q2[l], lambda_k2[l]])
        od = _diff_attn(to3(qd), to3(kd), vdt, lam_params, diff_norm_g[l].reshape(-1, 1),
                        tq=512, lam_init=lam_init)
        of = _fox_attn(to3(qf), to3(kf), vft, to3(qb), to3(kb), tq=512)

        wo = w_out[l].astype(BF16).reshape(2, -1, d)
        wr = jnp.zeros((d, LANES), F32)
        wr = wr.at[:, :N_GROUPS].set(router_group_w[l]).at[:, 8:8 + N_EXPERTS].set(router_expert_w[l])
        wr_hi, wr_lo = _split2(wr.T)
        br = jnp.zeros((LANES,), F32)
        br = br.at[:N_GROUPS].set(router_group_b[l]).at[8:8 + N_EXPERTS].set(router_expert_b[l]).reshape(LANES, 1)
        x1, hm, comb = _out_proj(x2, od.reshape(n, -1), of.reshape(n, -1), wo, norm_ffn_g[l].reshape(1, d),
                                 wr_hi, wr_lo, br, tm=512)

        x2 = _moe(hm, comb, x1, w_gate[l].astype(BF16), w_up[l].astype(BF16), w_down[l].astype(BF16),
                  norm_final_g.reshape(1, d), tm=1024, apply_final=(l == depth - 1))
    return x2.reshape(b, s, d)
```

```python
import functools

import numpy as np
import jax
import jax.numpy as jnp
from jax import lax
from jax.experimental import pallas as pl
from jax.experimental.pallas import tpu as pltpu

CHUNK = 64
HEAD_DIM = 64
ROPE_THETA = 10000.0
EPS = 1e-6
LOG2E = 1.4426950408889634
N_DIFF_HEADS = 4
N_FOX_HEADS = 8
N_GROUPS = 4
EXPERTS_PER_GROUP = 4
N_EXPERTS = N_GROUPS * EXPERTS_PER_GROUP

LANES = 128
NEG_BIG = -1e30
VMEM_LIMIT = 56 * 1024 * 1024
N_BIAS = 6
SUBLANES = 8
GID_LANE = N_EXPERTS
ROW_BLOCK = 128
ONES_ROWS = 16
BLOCKS_PER_STEP = 2
MOE_TILES = 2
SORT_ROWS = 512

BF16 = jnp.bfloat16
F32 = jnp.float32
_NT = (((1,), (1,)), ((), ()))


def _dot(a, b):
    return jnp.dot(a, b, preferred_element_type=F32)


def _dot_nt(a, b):
    return lax.dot_general(a, b, _NT, preferred_element_type=F32)


def _split2(x):
    hi = x.astype(BF16)
    lo = (x - hi.astype(F32)).astype(BF16)
    return hi, lo


def _split3(x):
    hi = x.astype(BF16)
    r = x - hi.astype(F32)
    mid = r.astype(BF16)
    lo = (r - mid.astype(F32)).astype(BF16)
    return hi, mid, lo


def _rms(x, g):
    return x * lax.rsqrt(jnp.mean(x * x, axis=-1, keepdims=True) + EPS) * g


def _pack_parts(parts):
    packed = parts[0].astype(F32)
    for k in (1, 2):
        packed = packed + pltpu.roll(parts[k].astype(F32), k * N_FOX_HEADS, axis=1)
    return packed.astype(BF16)


def _in_proj_kernel(x_ref, g_ref, cos_ref, sin_ref, w_ref, wt_ref, wf_cat_ref, bf_ref, tri_ref,
                    eq_ref, ek_ref, cq_ref, ck_ref,
                    qd_ref, kd_ref, vdt_ref, qf_ref, kf_ref, vft_ref, qb_ref, kb_ref, carry_ref,
                    *, tiles_per_seq):
    i = pl.program_id(0)
    h = _rms(x_ref[...], g_ref[...])
    hb = h.astype(BF16)
    cos = cos_ref[...]
    sin = sin_ref[...]
    scale = HEAD_DIM ** -0.5 * LOG2E
    lane = lax.broadcasted_iota(jnp.int32, cos.shape, 1)
    first_half = (lane % HEAD_DIM) < HEAD_DIM // 2

    def rope(w_idx, out_ref, mul):
        a = _dot(hb, w_ref[w_idx])
        for c in range(a.shape[1] // LANES):
            sl = slice(c * LANES, (c + 1) * LANES)
            x = a[:, sl]
            rot = jnp.where(first_half, pltpu.roll(x, LANES - HEAD_DIM // 2, axis=1),
                            pltpu.roll(x, HEAD_DIM // 2, axis=1))
            out_ref[:, sl] = ((x * cos + rot * sin) * mul).astype(out_ref.dtype)

    rope(0, qd_ref, scale)
    rope(1, kd_ref, 1.0)
    qf_ref[...] = (_dot(hb, w_ref[2]) * scale).astype(qf_ref.dtype)
    kf_ref[...] = _dot(hb, w_ref[3]).astype(kf_ref.dtype)
    vdt_ref[0] = _dot_nt(wt_ref[0], hb).astype(vdt_ref.dtype)
    vft_ref[0] = _dot_nt(wt_ref[1], hb).astype(vft_ref.dtype)

    zz = _dot(hb, wf_cat_ref[...])
    z = (zz[:, :LANES] + zz[:, LANES:]) + bf_ref[...]
    log_f = jnp.minimum(z, 0.0) - jnp.log1p(jnp.exp(-jnp.abs(z)))
    valid = lane < N_FOX_HEADS
    log_f = jnp.where(valid, log_f, 0.0)

    @pl.when(i % tiles_per_seq == 0)
    def _():
        carry_ref[...] = jnp.zeros_like(carry_ref)

    r = _dot(tri_ref[...], _pack_parts(_split3(log_f)))
    cum = (r + pltpu.roll(r, LANES - N_FOX_HEADS, axis=1)) + pltpu.roll(r, LANES - 2 * N_FOX_HEADS, axis=1)
    cum = jnp.where(valid, cum + carry_ref[...], 0.0)
    carry_ref[...] = cum[cum.shape[0] - 1:, :]

    f_pack = _pack_parts(_split3(cum * LOG2E))
    qb_ref[...] = (cq_ref[...] + _dot(f_pack, eq_ref[...])).astype(qb_ref.dtype)
    kb_ref[...] = (ck_ref[...] + _dot(f_pack, ek_ref[...])).astype(kb_ref.dtype)


def _bias_placement():
    width = (N_FOX_HEADS // 2) * LANES
    eq = np.zeros((LANES, width), np.float32)
    ek = np.zeros((LANES, width), np.float32)
    cq = np.zeros((1, width), np.float32)
    ck = np.zeros((1, width), np.float32)
    for h in range(N_FOX_HEADS):
        base = (h // 2) * LANES + (h % 2) * N_BIAS
        for part in range(3):
            eq[part * N_FOX_HEADS + h, base + part] = 1.0
            ek[part * N_FOX_HEADS + h, base + 3 + part] = -1.0
        cq[0, base + 3:base + 6] = 1.0
        ck[0, base:base + 3] = 1.0
    return jnp.asarray(eq, BF16), jnp.asarray(ek, BF16), jnp.asarray(cq), jnp.asarray(ck)


def _in_proj(x2, g, cos2, sin2, w_stack, wt_stack, wf_cat, bf_pad, *, batch, seq, tm):
    n, d = x2.shape
    tiles_per_seq = seq // tm
    tri = jnp.tril(jnp.ones((tm, tm), F32)).astype(BF16)
    eq, ek, cq, ck = _bias_placement()
    wcols = w_stack.shape[2]
    const = lambda *shape: pl.BlockSpec(shape, lambda i: (0,) * len(shape))
    row_blk = lambda cols: pl.BlockSpec((tm, cols), lambda i: (i, 0))
    pos_blk = pl.BlockSpec((tm, LANES), lambda i: (i % tiles_per_seq, 0))
    vt_blk = pl.BlockSpec((1, wcols, tm), lambda i: (i // tiles_per_seq, 0, i % tiles_per_seq))
    out_bf = jax.ShapeDtypeStruct((n, wcols), BF16)
    out_vt = jax.ShapeDtypeStruct((batch, wcols, seq), BF16)
    return pl.pallas_call(
        functools.partial(_in_proj_kernel, tiles_per_seq=tiles_per_seq),
        grid=(n // tm,),
        in_specs=[row_blk(d), const(1, d), pos_blk, pos_blk, const(*w_stack.shape), const(*wt_stack.shape),
                  const(*wf_cat.shape), const(1, LANES), const(tm, tm),
                  const(*eq.shape), const(*ek.shape), const(*cq.shape), const(*ck.shape)],
        out_specs=[row_blk(wcols), row_blk(wcols), vt_blk, row_blk(wcols), row_blk(wcols), vt_blk,
                   row_blk(eq.shape[1]), row_blk(ek.shape[1])],
        out_shape=[out_bf, out_bf, out_vt, out_bf, out_bf, out_vt,
                   jax.ShapeDtypeStruct((n, eq.shape[1]), BF16), jax.ShapeDtypeStruct((n, ek.shape[1]), BF16)],
        scratch_shapes=[pltpu.VMEM((1, LANES), F32)],
        compiler_params=pltpu.CompilerParams(dimension_semantics=("arbitrary",),
                                             vmem_limit_bytes=VMEM_LIMIT),
        name="in_proj",
    )(x2, g, cos2, sin2, w_stack, wt_stack, wf_cat, bf_pad, tri, eq, ek, cq, ck)


def _flash_sweep(nq, tq, tabs, logits_fn, values_fn, mask, scratch):
    s0, s1, c0, c1, m_ref, acc_ref = scratch
    row_tab, k_tab = tabs
    n_maps = m_ref.shape[1]
    n_off = nq * (nq - 1) // 2
    ones = jnp.ones((ONES_ROWS, tq), BF16)

    def stage_a(row, tile, s_buf, c_buf, masked):
        sts = logits_fn(row, tile)
        for a in range(n_maps):
            st = jnp.where(mask, sts[a], NEG_BIG) if masked else sts[a]
            s_buf[a] = st
            c_buf[a] = jnp.max(st, axis=0, keepdims=True)

    def stage_b(row, tile, s_buf, c_buf, first):
        vts = values_fn(tile)
        for a in range(n_maps):
            vt = jnp.concatenate([vts[a], ones], axis=0)
            if first:
                m_new = c_buf[a]
                acc_ref[row, a] = _dot(vt, jnp.exp2(s_buf[a] - m_new).astype(BF16))
            else:
                m = m_ref[row, a]
                m_new = jnp.maximum(m, c_buf[a])
                alpha = jnp.exp2(m - m_new)
                acc_ref[row, a] = alpha * acc_ref[row, a] + _dot(vt, jnp.exp2(s_buf[a] - m_new).astype(BF16))
            m_ref[row, a] = m_new

    stage_a(0, 0, s0, c0, True)

    def diag_pair(jj, _):
        r = 2 * jj
        stage_a(r + 1, r + 1, s1, c1, True)
        stage_b(r, r, s0, c0, True)
        stage_a(r + 2, r + 2, s0, c0, True)
        stage_b(r + 1, r + 1, s1, c1, True)
        return 0

    lax.fori_loop(0, (nq - 2) // 2, diag_pair, 0)
    stage_a(nq - 1, nq - 1, s1, c1, True)
    stage_b(nq - 2, nq - 2, s0, c0, True)

    stage_a(row_tab[0], k_tab[0], s0, c0, False)
    stage_b(nq - 1, nq - 1, s1, c1, True)

    def off_pair(jj, _):
        t = 2 * jj
        stage_a(row_tab[t + 1], k_tab[t + 1], s1, c1, False)
        stage_b(row_tab[t], k_tab[t], s0, c0, False)
        stage_a(row_tab[t + 2], k_tab[t + 2], s0, c0, False)
        stage_b(row_tab[t + 1], k_tab[t + 1], s1, c1, False)
        return 0

    lax.fori_loop(0, (n_off - 2) // 2, off_pair, 0)
    stage_a(row_tab[n_off - 1], k_tab[n_off - 1], s1, c1, False)
    stage_b(row_tab[n_off - 2], k_tab[n_off - 2], s0, c0, False)
    stage_b(row_tab[n_off - 1], k_tab[n_off - 1], s1, c1, False)


def _flash_scratch(nq, n_maps, dv, tq):
    s_buf = pltpu.VMEM((n_maps, tq, tq), F32)
    c_buf = pltpu.VMEM((n_maps, 1, tq), F32)
    return [s_buf, s_buf, c_buf, c_buf, pltpu.VMEM((nq, n_maps, 1, tq), F32),
            pltpu.VMEM((nq, n_maps, dv + ONES_ROWS, tq), F32)]


def _off_diagonal_tables(nq):
    pairs = [(r, k) for r in range(nq) for k in range(r)]
    return (jnp.asarray([p[0] for p in pairs], jnp.int32), jnp.asarray([p[1] for p in pairs], jnp.int32))


def _flash_outputs(acc_ref, row, dv):
    return [acc_ref[row, a, :dv, :] / acc_ref[row, a, dv:dv + 1, :] for a in range(acc_ref.shape[1])]


def _head_blocks(block):
    return [block[:, k * LANES:(k + 1) * LANES] for k in range(BLOCKS_PER_STEP)]


def _tile_rows(ref, tile, tq):
    return ref[0, pl.ds(pl.multiple_of(tile * tq, tq), tq), :]


def _diff_attn_kernel(row_tab, k_tab, q_ref, k_ref, vt_ref, lam_ref, g_ref, o_ref, *scratch, tq, lam_init):
    nq = q_ref.shape[1] // tq

    def logits(row, tile):
        qs = []
        for q in _head_blocks(_tile_rows(q_ref, row, tq)):
            lane = lax.broadcasted_iota(jnp.int32, q.shape, 1)
            zero = jnp.zeros_like(q)
            qs += [jnp.where(lane < HEAD_DIM, q, zero), jnp.where(lane >= HEAD_DIM, q, zero)]
        kts = _head_blocks(_tile_rows(k_ref, tile, tq))
        return [_dot_nt(kts[a // 2], qs[a]) for a in range(len(qs))]

    def values(tile):
        vt = vt_ref[0, :, pl.ds(pl.multiple_of(tile * tq, tq), tq)]
        return [vt[(a // 2) * LANES:(a // 2 + 1) * LANES] for a in range(2 * BLOCKS_PER_STEP)]

    key = lax.broadcasted_iota(jnp.int32, (tq, tq), 0)
    qry = lax.broadcasted_iota(jnp.int32, (tq, tq), 1)
    _flash_sweep(nq, tq, (row_tab, k_tab), logits, values, (key // CHUNK) <= (qry // CHUNK), scratch)

    lp = lam_ref[...]
    lam = (jnp.exp(jnp.sum(lp[0:1] * lp[1:2], axis=1, keepdims=True))
           - jnp.exp(jnp.sum(lp[2:3] * lp[3:4], axis=1, keepdims=True)) + lam_init)

    def finish(row, _):
        outs = _flash_outputs(scratch[-1], row, LANES)
        ys = []
        for k in range(BLOCKS_PER_STEP):
            o = outs[2 * k] - lam * outs[2 * k + 1]
            y = o * lax.rsqrt(jnp.mean(o * o, axis=0, keepdims=True) + EPS) * g_ref[...] * (1.0 - lam_init)
            ys.append(y.T.astype(o_ref.dtype))
        o_ref[0, pl.ds(pl.multiple_of(row * tq, tq), tq), :] = jnp.concatenate(ys, axis=1)
        return 0

    lax.fori_loop(0, nq, finish, 0)


def _attn_call(kernel_fn, name, n_maps, dv, inputs, in_specs, b, s, w, tq):
    nq = s // tq
    assert nq % 2 == 0 and nq >= 4, "the pipelined sweeps handle tiles in pairs"
    wb = BLOCKS_PER_STEP * LANES
    return pl.pallas_call(
        kernel_fn,
        grid_spec=pltpu.PrefetchScalarGridSpec(
            num_scalar_prefetch=2,
            grid=(b, w // wb),
            in_specs=in_specs,
            out_specs=pl.BlockSpec((1, s, wb), lambda bi, h, *_: (bi, 0, h)),
            scratch_shapes=_flash_scratch(nq, n_maps, dv, tq)),
        out_shape=jax.ShapeDtypeStruct((b, s, w), BF16),
        compiler_params=pltpu.CompilerParams(dimension_semantics=("arbitrary",) * 2,
                                             vmem_limit_bytes=VMEM_LIMIT),
        name=name,
    )(*_off_diagonal_tables(nq), *inputs)


def _attn_specs(s):
    wb = BLOCKS_PER_STEP * LANES
    row_blk = pl.BlockSpec((1, s, wb), lambda bi, h, *_: (bi, 0, h))
    vt_blk = pl.BlockSpec((1, wb, s), lambda bi, h, *_: (bi, h, 0))
    return row_blk, vt_blk


def _diff_attn(qd, kd, vdt, lam_params, gnorm_col, *, tq, lam_init):
    b, s, w = qd.shape
    row_blk, vt_blk = _attn_specs(s)
    small = lambda arr: pl.BlockSpec(arr.shape, lambda bi, h, *_: (0, 0))
    return _attn_call(functools.partial(_diff_attn_kernel, tq=tq, lam_init=lam_init), "diff_attn",
                      2 * BLOCKS_PER_STEP, LANES, (qd, kd, vdt, lam_params, gnorm_col),
                      [row_blk, row_blk, vt_blk, small(lam_params), small(gnorm_col)], b, s, w, tq)


def _fox_attn_kernel(row_tab, k_tab, q_ref, k_ref, vt_ref, qb_ref, kb_ref, o_ref, *scratch, tq):
    nq = q_ref.shape[1] // tq

    def logits(row, tile):
        qs = []
        for q, qb in zip(_head_blocks(_tile_rows(q_ref, row, tq)), _head_blocks(_tile_rows(qb_ref, row, tq))):
            lane = lax.broadcasted_iota(jnp.int32, q.shape, 1)
            zero = jnp.zeros_like(q)
            qs += [jnp.concatenate([jnp.where(lane < HEAD_DIM, q, zero), jnp.where(lane < N_BIAS, qb, zero)],
                                   axis=1),
                   jnp.concatenate([jnp.where(lane >= HEAD_DIM, q, zero),
                                    jnp.where((lane >= N_BIAS) & (lane < 2 * N_BIAS), qb, zero)], axis=1)]
        kts = [jnp.concatenate([k, kb], axis=1) for k, kb in zip(_head_blocks(_tile_rows(k_ref, tile, tq)),
                                                                   _head_blocks(_tile_rows(kb_ref, tile, tq)))]
        return [_dot_nt(kts[a // 2], qs[a]) for a in range(len(qs))]

    def values(tile):
        vt = vt_ref[0, :, pl.ds(pl.multiple_of(tile * tq, tq), tq)]
        return [vt[a * HEAD_DIM:(a + 1) * HEAD_DIM] for a in range(2 * BLOCKS_PER_STEP)]

    key = lax.broadcasted_iota(jnp.int32, (tq, tq), 0)
    qry = lax.broadcasted_iota(jnp.int32, (tq, tq), 1)
    _flash_sweep(nq, tq, (row_tab, k_tab), logits, values, key <= qry, scratch)

    def finish(row, _):
        o = jnp.concatenate(_flash_outputs(scratch[-1], row, HEAD_DIM), axis=0)
        o_ref[0, pl.ds(pl.multiple_of(row * tq, tq), tq), :] = o.T.astype(o_ref.dtype)
        return 0

    lax.fori_loop(0, nq, finish, 0)


def _fox_attn(qf, kf, vft, qb, kb, *, tq):
    b, s, w = qf.shape
    row_blk, vt_blk = _attn_specs(s)
    return _attn_call(functools.partial(_fox_attn_kernel, tq=tq), "fox_attn", 2 * BLOCKS_PER_STEP, HEAD_DIM,
                      (qf, kf, vft, qb, kb), [row_blk, row_blk, vt_blk, row_blk, row_blk], b, s, w, tq)


def _out_proj_kernel(x_ref, od_ref, of_ref, wo_ref, g_ref, wr_cat_ref, br_ref,
                     x1_ref, hm_ref, comb_ref):
    x1 = x_ref[...] + (_dot(od_ref[...], wo_ref[0]) + _dot(of_ref[...], wo_ref[1]))
    x1_ref[...] = x1
    hm = _rms(x1, g_ref[...])
    hb, h_lo = _split2(hm)
    hm_ref[...] = hb

    wr_cat = wr_cat_ref[...]
    both = _dot_nt(wr_cat, hb)
    lt = ((both[:LANES] + both[LANES:]) + _dot_nt(wr_cat[:LANES], h_lo)) + br_ref[...]
    tm = lt.shape[1]
    g8 = lt[0:8]
    r8 = lax.broadcasted_iota(jnp.int32, (8, tm), 0)
    g8 = jnp.where(r8 < N_GROUPS, g8, NEG_BIG)
    gmax = jnp.max(g8, axis=0, keepdims=True)
    gidx = jnp.min(jnp.where(g8 == gmax, r8, N_GROUPS), axis=0, keepdims=True)
    g_w = 1.0 / jnp.sum(jnp.exp(g8 - gmax), axis=0, keepdims=True)

    e16 = lt[8:8 + N_EXPERTS]
    r16 = lax.broadcasted_iota(jnp.int32, (N_EXPERTS, tm), 0)
    in_group = (r16 // EXPERTS_PER_GROUP) == gidx
    e_sel = jnp.where(in_group, e16, NEG_BIG)
    top1 = jnp.max(e_sel, axis=0, keepdims=True)
    id1 = jnp.min(jnp.where(e_sel == top1, r16, N_EXPERTS), axis=0, keepdims=True)
    e_rest = jnp.where(r16 == id1, NEG_BIG, e_sel)
    top2 = jnp.max(e_rest, axis=0, keepdims=True)
    id2 = jnp.min(jnp.where(e_rest == top2, r16, N_EXPERTS), axis=0, keepdims=True)
    t = jnp.exp(top2 - top1)
    w1 = g_w / (1.0 + t)
    w2 = w1 * t
    r128 = lax.broadcasted_iota(jnp.int32, (LANES, tm), 0)
    gid = gidx.astype(F32)
    comb_t = (jnp.where(r128 == id1, w1, 0.0) + jnp.where(r128 == id2, w2, 0.0)
              + jnp.where(r128 == GID_LANE, gid, 0.0))
    comb_ref[...] = comb_t.T


def _out_proj(x2, od, of, wo, g, wr_cat, br, *, tm):
    n, d = x2.shape
    const = lambda *shape: pl.BlockSpec(shape, lambda i: (0,) * len(shape))
    row_blk = lambda cols: pl.BlockSpec((tm, cols), lambda i: (i, 0))
    return pl.pallas_call(
        _out_proj_kernel,
        grid=(n // tm,),
        in_specs=[row_blk(d), row_blk(od.shape[1]), row_blk(of.shape[1]), const(*wo.shape), const(1, d),
                  const(*wr_cat.shape), const(*br.shape)],
        out_specs=[row_blk(d), row_blk(d), row_blk(LANES)],
        out_shape=[jax.ShapeDtypeStruct((n, d), F32), jax.ShapeDtypeStruct((n, d), BF16),
                   jax.ShapeDtypeStruct((n, LANES), F32)],
        compiler_params=pltpu.CompilerParams(dimension_semantics=("arbitrary",),
                                             vmem_limit_bytes=VMEM_LIMIT),
        name="out_proj",
    )(x2, od, of, wo, g, wr_cat, br)


def _moe_kernel(hm_ref, comb_ref, x1_ref, wg_ref, wu_ref, wd_ref, gf_ref, o_ref,
                tri_ref, hs_ref, cs_ref, posc_ref, tab_ref, *, apply_final):
    g = pl.program_id(1)
    k = pl.program_id(2)
    t = hm_ref.shape[0]
    t_pad = hs_ref.shape[1]
    rb = ROW_BLOCK

    @pl.when((pl.program_id(0) == 0) & (g == 0) & (k == 0))
    def _():
        row = lax.broadcasted_iota(jnp.int32, (t, t), 0)
        col = lax.broadcasted_iota(jnp.int32, (t, t), 1)
        tri_ref[...] = jnp.where(row > col, 1.0, 0.0).astype(BF16)

    @pl.when(g == 0)
    def _():
        comb = comb_ref[...]
        lane = lax.broadcasted_iota(jnp.int32, comb.shape, 1)
        onehot_c = jnp.where(lane == comb[:, GID_LANE:GID_LANE + 1].astype(jnp.int32), 1.0, 0.0)
        rank_c = _dot(tri_ref[...], onehot_c.astype(BF16))
        counts = jnp.sum(onehot_c, axis=0, keepdims=True)
        lane1 = lax.broadcasted_iota(jnp.int32, (1, LANES), 1)
        start_c = jnp.zeros((1, LANES), F32)
        first = jnp.int32(0)
        for grp in range(N_GROUPS):
            cnt = jnp.sum(jnp.where(lane1 == grp, counts, 0.0)).astype(jnp.int32)
            nblk = lax.shift_right_logical(cnt + (rb - 1), int(np.log2(rb)))
            tab_ref[k, grp] = first
            tab_ref[k, N_GROUPS + grp] = nblk
            start_c = jnp.where(lane1 == grp, (first * rb).astype(F32), start_c)
            first = first + nblk
        dest = onehot_c * (rank_c + start_c)
        posc_ref[k] = jnp.sum(dest, axis=1, keepdims=True).astype(jnp.int32)
        pos_r = jnp.sum(dest.T, axis=0, keepdims=True).astype(jnp.int32)
        c_parts = _split3(jnp.where(lane < N_EXPERTS, comb, 0.0))
        c_pack = (c_parts[0].astype(F32) + pltpu.roll(c_parts[1].astype(F32), N_EXPERTS, axis=1)
                  + pltpu.roll(c_parts[2].astype(F32), 2 * N_EXPERTS, axis=1)).astype(BF16)
        hm = hm_ref[...]
        for r0 in range(0, t_pad, SORT_ROWS):
            dst = lax.broadcasted_iota(jnp.int32, (SORT_ROWS, t), 0) + r0
            perm = jnp.where(dst == pos_r, 1.0, 0.0).astype(BF16)
            hs_ref[k, r0:r0 + SORT_ROWS, :] = _dot(perm, hm).astype(hs_ref.dtype)
            cp = _dot(perm, c_pack)
            cs_ref[k, r0:r0 + SORT_ROWS, :] = (cp + pltpu.roll(cp, LANES - N_EXPERTS, axis=1)
                                               + pltpu.roll(cp, LANES - 2 * N_EXPERTS, axis=1))

    b0 = tab_ref[k, g]

    def expert_block(b, _):
        off = pl.multiple_of(b * rb, rb)
        rows = hs_ref[k, pl.ds(off, rb), :]
        cblk = cs_ref[k, pl.ds(off, rb), :]
        lane = lax.broadcasted_iota(jnp.int32, cblk.shape, 1)
        gates = [_dot(rows, wg_ref[j]) for j in range(EXPERTS_PER_GROUP)]
        ups = [_dot(rows, wu_ref[j]) for j in range(EXPERTS_PER_GROUP)]
        y = None
        for j in range(EXPERTS_PER_GROUP):
            c = jnp.sum(jnp.where(lane == g * EXPERTS_PER_GROUP + j, cblk, 0.0), axis=1, keepdims=True)
            a = gates[j] * jax.nn.sigmoid(gates[j]) * ups[j] * c
            d = _dot(a.astype(BF16), wd_ref[j])
            y = d if y is None else y + d
        hs_ref[k, pl.ds(off, rb), :] = y.astype(hs_ref.dtype)
        return 0

    lax.fori_loop(b0, b0 + tab_ref[k, N_GROUPS + g], expert_block, 0)

    @pl.when(g == pl.num_programs(1) - 1)
    def _():
        ys = hs_ref[k]
        for r0 in range(0, t, SORT_ROWS):
            src = lax.broadcasted_iota(jnp.int32, (SORT_ROWS, t_pad), 1)
            unperm = jnp.where(src == posc_ref[k, r0:r0 + SORT_ROWS, :], 1.0, 0.0).astype(BF16)
            x2 = x1_ref[r0:r0 + SORT_ROWS, :] + _dot(unperm, ys)
            o_ref[r0:r0 + SORT_ROWS, :] = _rms(x2, gf_ref[...]) if apply_final else x2


def _moe(hm, comb, x1, wg, wu, wd, gfin, *, tm, apply_final):
    n, d = x1.shape
    ne, _, de = wg.shape
    epg = EXPERTS_PER_GROUP
    n_groups = ne // epg
    t_pad = tm + N_GROUPS * ROW_BLOCK
    assert t_pad % SORT_ROWS == 0 and tm % SORT_ROWS == 0 and n % (MOE_TILES * tm) == 0
    first_blk = lambda cols: pl.BlockSpec(
        (tm, cols), lambda p, g, k: (MOE_TILES * p + jnp.where(g == 0, k, MOE_TILES - 1), 0))
    last_blk = lambda cols: pl.BlockSpec(
        (tm, cols), lambda p, g, k: (MOE_TILES * p + jnp.where(g == n_groups - 1, k, 0), 0),
        pipeline_mode=pl.Buffered(1))
    w_blk = lambda *shape: pl.BlockSpec((epg,) + shape, lambda p, g, k: (g, 0, 0))
    return pl.pallas_call(
        functools.partial(_moe_kernel, apply_final=apply_final),
        grid=(n // (MOE_TILES * tm), n_groups, MOE_TILES),
        in_specs=[first_blk(d), first_blk(LANES), last_blk(d), w_blk(d, de), w_blk(d, de), w_blk(de, d),
                  pl.BlockSpec((1, d), lambda p, g, k: (0, 0))],
        out_specs=last_blk(d),
        out_shape=jax.ShapeDtypeStruct((n, d), F32),
        scratch_shapes=[pltpu.VMEM((tm, tm), BF16), pltpu.VMEM((MOE_TILES, t_pad, d), BF16),
                        pltpu.VMEM((MOE_TILES, t_pad, LANES), F32), pltpu.VMEM((MOE_TILES, tm, 1), jnp.int32),
                        pltpu.SMEM((MOE_TILES, 2 * N_GROUPS), jnp.int32)],
        compiler_params=pltpu.CompilerParams(dimension_semantics=("arbitrary",) * 3,
                                             vmem_limit_bytes=VMEM_LIMIT),
        name="moe",
    )(hm, comb, x1, wg, wu, wd, gfin)


def kernel(x, norm_attn_g, w_in, b_forget, lambda_q1, lambda_k1, lambda_q2, lambda_k2, diff_norm_g, w_out,
           norm_ffn_g, router_group_w, router_group_b, router_expert_w, router_expert_b, w_gate, w_up, w_down,
           norm_final_g):
    b, s, d = x.shape
    depth = w_in.shape[0]
    n = b * s
    diff_w = N_DIFF_HEADS * 2 * HEAD_DIM
    fox_w = N_FOX_HEADS * HEAD_DIM

    pos = jnp.arange(s, dtype=F32)
    inv_freq = 1.0 / (ROPE_THETA ** (jnp.arange(0, HEAD_DIM, 2, dtype=F32) / HEAD_DIM))
    ang = pos[:, None] * inv_freq[None, :]
    ang = jnp.concatenate([ang, ang, ang, ang], axis=-1)
    first_half = (jnp.arange(LANES) % HEAD_DIM) < HEAD_DIM // 2
    cos2, sin2 = jnp.cos(ang), jnp.where(first_half, -jnp.sin(ang), jnp.sin(ang))

    x2 = x.reshape(n, d)
    for l in range(depth):
        lam_init = 0.8 - 0.6 * float(np.exp(-0.3 * l))
        w = w_in[l]
        offs = np.cumsum([0, diff_w, diff_w, diff_w, fox_w, fox_w, fox_w, N_FOX_HEADS])
        seg = [w[:, offs[k]:offs[k + 1]] for k in range(7)]
        w_stack = jnp.stack([seg[0], seg[1], seg[3], seg[4]]).astype(BF16)
        wt_stack = jnp.stack([seg[2].T, seg[5].T]).astype(BF16)
        wf = jnp.pad(seg[6], ((0, 0), (0, LANES - N_FOX_HEADS)))
        wf_cat = jnp.concatenate(_split2(wf), axis=1)
        bf_pad = jnp.pad(b_forget[l], (0, LANES - N_FOX_HEADS)).reshape(1, LANES)

        qd, kd, vdt, qf, kf, vft, qb, kb = _in_proj(
            x2, norm_attn_g[l].reshape(1, d), cos2, sin2, w_stack, wt_stack, wf_cat, bf_pad,
            batch=b, seq=s, tm=512)

        to3 = lambda t: t.reshape(b, s, t.shape[-1])
        lam_params = jnp.stack([lambda_q1[l], lambda_k1[l], lambda_q2[l], lambda_k2[l]])
        od = _diff_attn(to3(qd), to3(kd), vdt, lam_params, diff_norm_g[l].reshape(-1, 1),
                        tq=512, lam_init=lam_init)
        of = _fox_attn(to3(qf), to3(kf), vft, to3(qb), to3(kb), tq=512)

        wo = w_out[l].astype(BF16).reshape(2, -1, d)
        wr = jnp.zeros((d, LANES), F32)
        wr = wr.at[:, :N_GROUPS].set(router_group_w[l]).at[:, 8:8 + N_EXPERTS].set(router_expert_w[l])
        wr_cat = jnp.concatenate(_split2(wr.T), axis=0)
        br = jnp.zeros((LANES,), F32)
        br = br.at[:N_GROUPS].set(router_group_b[l]).at[8:8 + N_EXPERTS].set(router_expert_b[l]).reshape(LANES, 1)
        x1, hm, comb = _out_proj(x2, od.reshape(n, -1), of.reshape(n, -1), wo, norm_ffn_g[l].reshape(1, d),
                                 wr_cat, br, tm=512)

        x2 = _moe(hm, comb, x1, w_gate[l].astype(BF16), w_up[l].astype(BF16), w_down[l].astype(BF16),
                  norm_final_g.reshape(1, d), tm=1024, apply_final=(l == depth - 1))
    return x2.reshape(b, s, d)
```

```python
import functools

import numpy as np
import jax
import jax.numpy as jnp
from jax import lax
from jax.experimental import pallas as pl
from jax.experimental.pallas import tpu as pltpu

CHUNK = 64
HEAD_DIM = 64
ROPE_THETA = 10000.0
EPS = 1e-6
LOG2E = 1.4426950408889634
N_DIFF_HEADS = 4
N_FOX_HEADS = 8
N_GROUPS = 4
EXPERTS_PER_GROUP = 4
N_EXPERTS = N_GROUPS * EXPERTS_PER_GROUP

LANES = 128
NEG_BIG = -1e30
VMEM_LIMIT = 56 * 1024 * 1024
N_BIAS = 6
SUBLANES = 8
GID_LANE = N_EXPERTS
ROW_BLOCK = 128
ONES_ROWS = 16
BLOCKS_PER_STEP = 2
MOE_TILES = 2
OUT_PROJ_SPLIT = 2
SORT_ROWS = 512

BF16 = jnp.bfloat16
F32 = jnp.float32
_NT = (((1,), (1,)), ((), ()))


def _dot(a, b):
    return jnp.dot(a, b, preferred_element_type=F32)


def _dot_nt(a, b):
    return lax.dot_general(a, b, _NT, preferred_element_type=F32)


def _split2(x):
    hi = x.astype(BF16)
    lo = (x - hi.astype(F32)).astype(BF16)
    return hi, lo


def _split3(x):
    hi = x.astype(BF16)
    r = x - hi.astype(F32)
    mid = r.astype(BF16)
    lo = (r - mid.astype(F32)).astype(BF16)
    return hi, mid, lo


def _rms(x, g):
    return x * lax.rsqrt(jnp.mean(x * x, axis=-1, keepdims=True) + EPS) * g


def _pack_parts(parts):
    packed = parts[0].astype(F32)
    for k in (1, 2):
        packed = packed + pltpu.roll(parts[k].astype(F32), k * N_FOX_HEADS, axis=1)
    return packed.astype(BF16)


def _in_proj_kernel(x_ref, g_ref, cos_ref, sin_ref, w_ref, wt_ref, wf_cat_ref, bf_ref, tri_ref,
                    eq_ref, ek_ref, cq_ref, ck_ref,
                    qd_ref, kd_ref, vdt_ref, qf_ref, kf_ref, vft_ref, qb_ref, kb_ref, carry_ref,
                    *, tiles_per_seq):
    i = pl.program_id(0)
    h = _rms(x_ref[...], g_ref[...])
    hb = h.astype(BF16)
    cos = cos_ref[...]
    sin = sin_ref[...]
    scale = HEAD_DIM ** -0.5 * LOG2E
    lane = lax.broadcasted_iota(jnp.int32, cos.shape, 1)
    first_half = (lane % HEAD_DIM) < HEAD_DIM // 2

    def rope(w_idx, out_ref, mul):
        a = _dot(hb, w_ref[w_idx])
        for c in range(a.shape[1] // LANES):
            sl = slice(c * LANES, (c + 1) * LANES)
            x = a[:, sl]
            rot = jnp.where(first_half, pltpu.roll(x, LANES - HEAD_DIM // 2, axis=1),
                            pltpu.roll(x, HEAD_DIM // 2, axis=1))
            out_ref[:, sl] = ((x * cos + rot * sin) * mul).astype(out_ref.dtype)

    @pl.when(i % tiles_per_seq == 0)
    def _():
        carry_ref[...] = jnp.zeros_like(carry_ref)


    zz = _dot(hb, wf_cat_ref[...])
    z = (zz[:, :LANES] + zz[:, LANES:]) + bf_ref[...]
    log_f = jnp.minimum(z, 0.0) - jnp.log1p(jnp.exp(-jnp.abs(z)))
    valid = lane < N_FOX_HEADS
    log_f = jnp.where(valid, log_f, 0.0)

    rope(0, qd_ref, scale)

    r = _dot(tri_ref[...], _pack_parts(_split3(log_f)))
    cum = (r + pltpu.roll(r, LANES - N_FOX_HEADS, axis=1)) + pltpu.roll(r, LANES - 2 * N_FOX_HEADS, axis=1)
    cum = jnp.where(valid, cum + carry_ref[...], 0.0)
    carry_ref[...] = cum[cum.shape[0] - 1:, :]

    rope(1, kd_ref, 1.0)

    f_pack = _pack_parts(_split3(cum * LOG2E))
    qb_ref[...] = (cq_ref[...] + _dot(f_pack, eq_ref[...])).astype(qb_ref.dtype)
    kb_ref[...] = (ck_ref[...] + _dot(f_pack, ek_ref[...])).astype(kb_ref.dtype)

    qf_ref[...] = (_dot(hb, w_ref[2]) * scale).astype(qf_ref.dtype)
    kf_ref[...] = _dot(hb, w_ref[3]).astype(kf_ref.dtype)
    vdt_ref[0] = _dot_nt(wt_ref[0], hb).astype(vdt_ref.dtype)
    vft_ref[0] = _dot_nt(wt_ref[1], hb).astype(vft_ref.dtype)


def _bias_placement():
    width = (N_FOX_HEADS // 2) * LANES
    eq = np.zeros((LANES, width), np.float32)
    ek = np.zeros((LANES, width), np.float32)
    cq = np.zeros((1, width), np.float32)
    ck = np.zeros((1, width), np.float32)
    for h in range(N_FOX_HEADS):
        base = (h // 2) * LANES + (h % 2) * N_BIAS
        for part in range(3):
            eq[part * N_FOX_HEADS + h, base + part] = 1.0
            ek[part * N_FOX_HEADS + h, base + 3 + part] = -1.0
        cq[0, base + 3:base + 6] = 1.0
        ck[0, base:base + 3] = 1.0
    return jnp.asarray(eq, BF16), jnp.asarray(ek, BF16), jnp.asarray(cq), jnp.asarray(ck)


def _in_proj(x2, g, cos2, sin2, w_stack, wt_stack, wf_cat, bf_pad, *, batch, seq, tm):
    n, d = x2.shape
    tiles_per_seq = seq // tm
    tri = jnp.tril(jnp.ones((tm, tm), F32)).astype(BF16)
    eq, ek, cq, ck = _bias_placement()
    wcols = w_stack.shape[2]
    const = lambda *shape: pl.BlockSpec(shape, lambda i: (0,) * len(shape))
    row_blk = lambda cols: pl.BlockSpec((tm, cols), lambda i: (i, 0))
    pos_blk = pl.BlockSpec((tm, LANES), lambda i: (i % tiles_per_seq, 0))
    vt_blk = pl.BlockSpec((1, wcols, tm), lambda i: (i // tiles_per_seq, 0, i % tiles_per_seq))
    out_bf = jax.ShapeDtypeStruct((n, wcols), BF16)
    out_vt = jax.ShapeDtypeStruct((batch, wcols, seq), BF16)
    return pl.pallas_call(
        functools.partial(_in_proj_kernel, tiles_per_seq=tiles_per_seq),
        grid=(n // tm,),
        in_specs=[row_blk(d), const(1, d), pos_blk, pos_blk, const(*w_stack.shape), const(*wt_stack.shape),
                  const(*wf_cat.shape), const(1, LANES), const(tm, tm),
                  const(*eq.shape), const(*ek.shape), const(*cq.shape), const(*ck.shape)],
        out_specs=[row_blk(wcols), row_blk(wcols), vt_blk, row_blk(wcols), row_blk(wcols), vt_blk,
                   row_blk(eq.shape[1]), row_blk(ek.shape[1])],
        out_shape=[out_bf, out_bf, out_vt, out_bf, out_bf, out_vt,
                   jax.ShapeDtypeStruct((n, eq.shape[1]), BF16), jax.ShapeDtypeStruct((n, ek.shape[1]), BF16)],
        scratch_shapes=[pltpu.VMEM((1, LANES), F32)],
        compiler_params=pltpu.CompilerParams(dimension_semantics=("arbitrary",),
                                             vmem_limit_bytes=VMEM_LIMIT),
        name="in_proj",
    )(x2, g, cos2, sin2, w_stack, wt_stack, wf_cat, bf_pad, tri, eq, ek, cq, ck)


def _flash_sweep(nq, tq, tabs, logits_fn, values_fn, mask, scratch):
    s0, s1, c0, c1, m_ref, acc_ref = scratch
    row_tab, k_tab = tabs
    n_maps = m_ref.shape[1]
    n_off = nq * (nq - 1) // 2
    ones = jnp.ones((ONES_ROWS, tq), BF16)

    def stage_a(row, tile, s_buf, c_buf, masked):
        sts = logits_fn(row, tile)
        for a in range(n_maps):
            st = jnp.where(mask, sts[a], NEG_BIG) if masked else sts[a]
            s_buf[a] = st
            c_buf[a] = jnp.max(st, axis=0, keepdims=True)

    def stage_b(row, tile, s_buf, c_buf, first):
        vts = values_fn(tile)
        for a in range(n_maps):
            vt = jnp.concatenate([vts[a], ones], axis=0)
            if first:
                m_new = c_buf[a]
                acc_ref[row, a] = _dot(vt, jnp.exp2(s_buf[a] - m_new).astype(BF16))
            else:
                m = m_ref[row, a]
                m_new = jnp.maximum(m, c_buf[a])
                alpha = jnp.exp2(m - m_new)
                acc_ref[row, a] = alpha * acc_ref[row, a] + _dot(vt, jnp.exp2(s_buf[a] - m_new).astype(BF16))
            m_ref[row, a] = m_new

    stage_a(0, 0, s0, c0, True)

    def diag_pair(jj, _):
        r = 2 * jj
        stage_a(r + 1, r + 1, s1, c1, True)
        stage_b(r, r, s0, c0, True)
        stage_a(r + 2, r + 2, s0, c0, True)
        stage_b(r + 1, r + 1, s1, c1, True)
        return 0

    lax.fori_loop(0, (nq - 2) // 2, diag_pair, 0)
    stage_a(nq - 1, nq - 1, s1, c1, True)
    stage_b(nq - 2, nq - 2, s0, c0, True)

    stage_a(row_tab[0], k_tab[0], s0, c0, False)
    stage_b(nq - 1, nq - 1, s1, c1, True)

    def off_pair(jj, _):
        t = 2 * jj
        stage_a(row_tab[t + 1], k_tab[t + 1], s1, c1, False)
        stage_b(row_tab[t], k_tab[t], s0, c0, False)
        stage_a(row_tab[t + 2], k_tab[t + 2], s0, c0, False)
        stage_b(row_tab[t + 1], k_tab[t + 1], s1, c1, False)
        return 0

    lax.fori_loop(0, (n_off - 2) // 2, off_pair, 0)
    stage_a(row_tab[n_off - 1], k_tab[n_off - 1], s1, c1, False)
    stage_b(row_tab[n_off - 2], k_tab[n_off - 2], s0, c0, False)
    stage_b(row_tab[n_off - 1], k_tab[n_off - 1], s1, c1, False)


def _flash_scratch(nq, n_maps, dv, tq):
    s_buf = pltpu.VMEM((n_maps, tq, tq), F32)
    c_buf = pltpu.VMEM((n_maps, 1, tq), F32)
    return [s_buf, s_buf, c_buf, c_buf, pltpu.VMEM((nq, n_maps, 1, tq), F32),
            pltpu.VMEM((nq, n_maps, dv + ONES_ROWS, tq), F32)]


def _off_diagonal_tables(nq):
    pairs = [(r, k) for r in range(nq) for k in range(r)]
    return (jnp.asarray([p[0] for p in pairs], jnp.int32), jnp.asarray([p[1] for p in pairs], jnp.int32))


def _flash_outputs(acc_ref, row, dv):
    return [acc_ref[row, a, :dv, :] / acc_ref[row, a, dv:dv + 1, :] for a in range(acc_ref.shape[1])]


def _head_blocks(block):
    return [block[:, k * LANES:(k + 1) * LANES] for k in range(BLOCKS_PER_STEP)]


def _tile_rows(ref, tile, tq):
    return ref[0, pl.ds(pl.multiple_of(tile * tq, tq), tq), :]


def _diff_attn_kernel(row_tab, k_tab, q_ref, k_ref, vt_ref, lam_ref, g_ref, o_ref, *scratch, tq, lam_init):
    nq = q_ref.shape[1] // tq

    def logits(row, tile):
        qs = []
        for q in _head_blocks(_tile_rows(q_ref, row, tq)):
            lane = lax.broadcasted_iota(jnp.int32, q.shape, 1)
            zero = jnp.zeros_like(q)
            qs += [jnp.where(lane < HEAD_DIM, q, zero), jnp.where(lane >= HEAD_DIM, q, zero)]
        kts = _head_blocks(_tile_rows(k_ref, tile, tq))
        return [_dot_nt(kts[a // 2], qs[a]) for a in range(len(qs))]

    def values(tile):
        vt = vt_ref[0, :, pl.ds(pl.multiple_of(tile * tq, tq), tq)]
        return [vt[(a // 2) * LANES:(a // 2 + 1) * LANES] for a in range(2 * BLOCKS_PER_STEP)]

    key = lax.broadcasted_iota(jnp.int32, (tq, tq), 0)
    qry = lax.broadcasted_iota(jnp.int32, (tq, tq), 1)
    _flash_sweep(nq, tq, (row_tab, k_tab), logits, values, (key // CHUNK) <= (qry // CHUNK), scratch)

    lp = lam_ref[...]
    lam = (jnp.exp(jnp.sum(lp[0:1] * lp[1:2], axis=1, keepdims=True))
           - jnp.exp(jnp.sum(lp[2:3] * lp[3:4], axis=1, keepdims=True)) + lam_init)

    def finish(row, _):
        outs = _flash_outputs(scratch[-1], row, LANES)
        ys = []
        for k in range(BLOCKS_PER_STEP):
            o = outs[2 * k] - lam * outs[2 * k + 1]
            y = o * lax.rsqrt(jnp.mean(o * o, axis=0, keepdims=True) + EPS) * g_ref[...] * (1.0 - lam_init)
            ys.append(y.T.astype(o_ref.dtype))
        o_ref[0, pl.ds(pl.multiple_of(row * tq, tq), tq), :] = jnp.concatenate(ys, axis=1)
        return 0

    lax.fori_loop(0, nq, finish, 0)


def _attn_call(kernel_fn, name, n_maps, dv, inputs, in_specs, b, s, w, tq):
    nq = s // tq
    assert nq % 2 == 0 and nq >= 4, "the pipelined sweeps handle tiles in pairs"
    wb = BLOCKS_PER_STEP * LANES
    return pl.pallas_call(
        kernel_fn,
        grid_spec=pltpu.PrefetchScalarGridSpec(
            num_scalar_prefetch=2,
            grid=(b, w // wb),
            in_specs=in_specs,
            out_specs=pl.BlockSpec((1, s, wb), lambda bi, h, *_: (bi, 0, h)),
            scratch_shapes=_flash_scratch(nq, n_maps, dv, tq)),
        out_shape=jax.ShapeDtypeStruct((b, s, w), BF16),
        compiler_params=pltpu.CompilerParams(dimension_semantics=("arbitrary",) * 2,
                                             vmem_limit_bytes=VMEM_LIMIT),
        name=name,
    )(*_off_diagonal_tables(nq), *inputs)


def _attn_specs(s):
    wb = BLOCKS_PER_STEP * LANES
    row_blk = pl.BlockSpec((1, s, wb), lambda bi, h, *_: (bi, 0, h))
    vt_blk = pl.BlockSpec((1, wb, s), lambda bi, h, *_: (bi, h, 0))
    return row_blk, vt_blk


def _diff_attn(qd, kd, vdt, lam_params, gnorm_col, *, tq, lam_init):
    b, s, w = qd.shape
    row_blk, vt_blk = _attn_specs(s)
    small = lambda arr: pl.BlockSpec(arr.shape, lambda bi, h, *_: (0, 0))
    return _attn_call(functools.partial(_diff_attn_kernel, tq=tq, lam_init=lam_init), "diff_attn",
                      2 * BLOCKS_PER_STEP, LANES, (qd, kd, vdt, lam_params, gnorm_col),
                      [row_blk, row_blk, vt_blk, small(lam_params), small(gnorm_col)], b, s, w, tq)


def _fox_attn_kernel(row_tab, k_tab, q_ref, k_ref, vt_ref, qb_ref, kb_ref, o_ref, *scratch, tq):
    nq = q_ref.shape[1] // tq

    def logits(row, tile):
        qs = []
        for q, qb in zip(_head_blocks(_tile_rows(q_ref, row, tq)), _head_blocks(_tile_rows(qb_ref, row, tq))):
            lane = lax.broadcasted_iota(jnp.int32, q.shape, 1)
            zero = jnp.zeros_like(q)
            qs += [jnp.concatenate([jnp.where(lane < HEAD_DIM, q, zero), jnp.where(lane < N_BIAS, qb, zero)],
                                   axis=1),
                   jnp.concatenate([jnp.where(lane >= HEAD_DIM, q, zero),
                                    jnp.where((lane >= N_BIAS) & (lane < 2 * N_BIAS), qb, zero)], axis=1)]
        kts = [jnp.concatenate([k, kb], axis=1) for k, kb in zip(_head_blocks(_tile_rows(k_ref, tile, tq)),
                                                                   _head_blocks(_tile_rows(kb_ref, tile, tq)))]
        return [_dot_nt(kts[a // 2], qs[a]) for a in range(len(qs))]

    def values(tile):
        vt = vt_ref[0, :, pl.ds(pl.multiple_of(tile * tq, tq), tq)]
        return [vt[a * HEAD_DIM:(a + 1) * HEAD_DIM] for a in range(2 * BLOCKS_PER_STEP)]

    key = lax.broadcasted_iota(jnp.int32, (tq, tq), 0)
    qry = lax.broadcasted_iota(jnp.int32, (tq, tq), 1)
    _flash_sweep(nq, tq, (row_tab, k_tab), logits, values, key <= qry, scratch)

    def finish(row, _):
        o = jnp.concatenate(_flash_outputs(scratch[-1], row, HEAD_DIM), axis=0)
        o_ref[0, pl.ds(pl.multiple_of(row * tq, tq), tq), :] = o.T.astype(o_ref.dtype)
        return 0

    lax.fori_loop(0, nq, finish, 0)


def _fox_attn(qf, kf, vft, qb, kb, *, tq):
    b, s, w = qf.shape
    row_blk, vt_blk = _attn_specs(s)
    return _attn_call(functools.partial(_fox_attn_kernel, tq=tq), "fox_attn", 2 * BLOCKS_PER_STEP, HEAD_DIM,
                      (qf, kf, vft, qb, kb), [row_blk, row_blk, vt_blk, row_blk, row_blk], b, s, w, tq)


def _route(lt):
    tm = lt.shape[1]
    g8 = lt[0:8]
    r8 = lax.broadcasted_iota(jnp.int32, (8, tm), 0)
    g8 = jnp.where(r8 < N_GROUPS, g8, NEG_BIG)
    gmax = jnp.max(g8, axis=0, keepdims=True)
    gidx = jnp.min(jnp.where(g8 == gmax, r8, N_GROUPS), axis=0, keepdims=True)
    g_w = 1.0 / jnp.sum(jnp.exp(g8 - gmax), axis=0, keepdims=True)

    e16 = lt[8:8 + N_EXPERTS]
    r16 = lax.broadcasted_iota(jnp.int32, (N_EXPERTS, tm), 0)
    in_group = (r16 // EXPERTS_PER_GROUP) == gidx
    e_sel = jnp.where(in_group, e16, NEG_BIG)
    top1 = jnp.max(e_sel, axis=0, keepdims=True)
    id1 = jnp.min(jnp.where(e_sel == top1, r16, N_EXPERTS), axis=0, keepdims=True)
    e_rest = jnp.where(r16 == id1, NEG_BIG, e_sel)
    top2 = jnp.max(e_rest, axis=0, keepdims=True)
    id2 = jnp.min(jnp.where(e_rest == top2, r16, N_EXPERTS), axis=0, keepdims=True)
    t = jnp.exp(top2 - top1)
    w1 = g_w / (1.0 + t)
    w2 = w1 * t
    r128 = lax.broadcasted_iota(jnp.int32, (LANES, tm), 0)
    return (jnp.where(r128 == id1, w1, 0.0) + jnp.where(r128 == id2, w2, 0.0)
            + jnp.where(r128 == GID_LANE, gidx.astype(F32), 0.0))


def _out_proj_kernel(x_ref, od_ref, of_ref, wo_ref, g_ref, wr_cat_ref, br_ref,
                     x1_ref, hm_ref, comb_ref):
    tm = x_ref.shape[0]
    halves = [slice(k * tm // OUT_PROJ_SPLIT, (k + 1) * tm // OUT_PROJ_SPLIT) for k in range(OUT_PROJ_SPLIT)]
    x1s = []
    for rows in halves:
        x1 = x_ref[rows, :] + (_dot(od_ref[rows, :], wo_ref[0]) + _dot(of_ref[rows, :], wo_ref[1]))
        x1_ref[rows, :] = x1
        x1s.append(x1)
    wr_cat = wr_cat_ref[...]
    for rows, x1 in zip(halves, x1s):
        hb, h_lo = _split2(_rms(x1, g_ref[...]))
        hm_ref[rows, :] = hb
        both = _dot_nt(wr_cat, hb)
        lt = ((both[:LANES] + both[LANES:]) + _dot_nt(wr_cat[:LANES], h_lo)) + br_ref[...]
        comb_ref[rows, :] = _route(lt).T


def _out_proj(x2, od, of, wo, g, wr_cat, br, *, tm):
    n, d = x2.shape
    const = lambda *shape: pl.BlockSpec(shape, lambda i: (0,) * len(shape))
    row_blk = lambda cols: pl.BlockSpec((tm, cols), lambda i: (i, 0))
    return pl.pallas_call(
        _out_proj_kernel,
        grid=(n // tm,),
        in_specs=[row_blk(d), row_blk(od.shape[1]), row_blk(of.shape[1]), const(*wo.shape), const(1, d),
                  const(*wr_cat.shape), const(*br.shape)],
        out_specs=[row_blk(d), row_blk(d), row_blk(LANES)],
        out_shape=[jax.ShapeDtypeStruct((n, d), F32), jax.ShapeDtypeStruct((n, d), BF16),
                   jax.ShapeDtypeStruct((n, LANES), F32)],
        compiler_params=pltpu.CompilerParams(dimension_semantics=("arbitrary",),
                                             vmem_limit_bytes=VMEM_LIMIT),
        name="out_proj",
    )(x2, od, of, wo, g, wr_cat, br)


def _moe_kernel(hm_ref, comb_ref, x1_ref, wg_ref, wu_ref, wd_ref, gf_ref, o_ref,
                tri_ref, hs_ref, cs_ref, posc_ref, tab_ref, *, apply_final):
    g = pl.program_id(1)
    k = pl.program_id(2)
    t = hm_ref.shape[0]
    t_pad = hs_ref.shape[1]
    rb = ROW_BLOCK

    @pl.when((pl.program_id(0) == 0) & (g == 0) & (k == 0))
    def _():
        row = lax.broadcasted_iota(jnp.int32, (t, t), 0)
        col = lax.broadcasted_iota(jnp.int32, (t, t), 1)
        tri_ref[...] = jnp.where(row > col, 1.0, 0.0).astype(BF16)

    @pl.when(g == 0)
    def _():
        comb = comb_ref[...]
        lane = lax.broadcasted_iota(jnp.int32, comb.shape, 1)
        onehot_c = jnp.where(lane == comb[:, GID_LANE:GID_LANE + 1].astype(jnp.int32), 1.0, 0.0)
        rank_c = _dot(tri_ref[...], onehot_c.astype(BF16))
        counts = jnp.sum(onehot_c, axis=0, keepdims=True)
        lane1 = lax.broadcasted_iota(jnp.int32, (1, LANES), 1)
        start_c = jnp.zeros((1, LANES), F32)
        first = jnp.int32(0)
        for grp in range(N_GROUPS):
            cnt = jnp.sum(jnp.where(lane1 == grp, counts, 0.0)).astype(jnp.int32)
            nblk = lax.shift_right_logical(cnt + (rb - 1), int(np.log2(rb)))
            tab_ref[k, grp] = first
            tab_ref[k, N_GROUPS + grp] = nblk
            start_c = jnp.where(lane1 == grp, (first * rb).astype(F32), start_c)
            first = first + nblk
        dest = onehot_c * (rank_c + start_c)
        posc_ref[k] = jnp.sum(dest, axis=1, keepdims=True).astype(jnp.int32)
        pos_r = jnp.sum(dest.T, axis=0, keepdims=True).astype(jnp.int32)
        c_parts = _split3(jnp.where(lane < N_EXPERTS, comb, 0.0))
        c_pack = (c_parts[0].astype(F32) + pltpu.roll(c_parts[1].astype(F32), N_EXPERTS, axis=1)
                  + pltpu.roll(c_parts[2].astype(F32), 2 * N_EXPERTS, axis=1)).astype(BF16)
        hm = hm_ref[...]
        for r0 in range(0, t_pad, SORT_ROWS):
            dst = lax.broadcasted_iota(jnp.int32, (SORT_ROWS, t), 0) + r0
            perm = jnp.where(dst == pos_r, 1.0, 0.0).astype(BF16)
            hs_ref[k, r0:r0 + SORT_ROWS, :] = _dot(perm, hm).astype(hs_ref.dtype)
            cp = _dot(perm, c_pack)
            cs_ref[k, r0:r0 + SORT_ROWS, :] = (cp + pltpu.roll(cp, LANES - N_EXPERTS, axis=1)
                                               + pltpu.roll(cp, LANES - 2 * N_EXPERTS, axis=1))

    b0 = tab_ref[k, g]

    def expert_block(b, _):
        off = pl.multiple_of(b * rb, rb)
        rows = hs_ref[k, pl.ds(off, rb), :]
        cblk = cs_ref[k, pl.ds(off, rb), :]
        lane = lax.broadcasted_iota(jnp.int32, cblk.shape, 1)
        gates = [_dot(rows, wg_ref[j]) for j in range(EXPERTS_PER_GROUP)]
        ups = [_dot(rows, wu_ref[j]) for j in range(EXPERTS_PER_GROUP)]
        y = None
        for j in range(EXPERTS_PER_GROUP):
            c = jnp.sum(jnp.where(lane == g * EXPERTS_PER_GROUP + j, cblk, 0.0), axis=1, keepdims=True)
            a = gates[j] * jax.nn.sigmoid(gates[j]) * ups[j] * c
            d = _dot(a.astype(BF16), wd_ref[j])
            y = d if y is None else y + d
        hs_ref[k, pl.ds(off, rb), :] = y.astype(hs_ref.dtype)
        return 0

    lax.fori_loop(b0, b0 + tab_ref[k, N_GROUPS + g], expert_block, 0)

    @pl.when(g == pl.num_programs(1) - 1)
    def _():
        ys = hs_ref[k]
        for r0 in range(0, t, SORT_ROWS):
            src = lax.broadcasted_iota(jnp.int32, (SORT_ROWS, t_pad), 1)
            unperm = jnp.where(src == posc_ref[k, r0:r0 + SORT_ROWS, :], 1.0, 0.0).astype(BF16)
            x2 = x1_ref[r0:r0 + SORT_ROWS, :] + _dot(unperm, ys)
            o_ref[r0:r0 + SORT_ROWS, :] = _rms(x2, gf_ref[...]) if apply_final else x2


def _moe(hm, comb, x1, wg, wu, wd, gfin, *, tm, apply_final):
    n, d = x1.shape
    ne, _, de = wg.shape
    epg = EXPERTS_PER_GROUP
    n_groups = ne // epg
    t_pad = tm + N_GROUPS * ROW_BLOCK
    assert t_pad % SORT_ROWS == 0 and tm % SORT_ROWS == 0 and n % (MOE_TILES * tm) == 0
    first_blk = lambda cols: pl.BlockSpec(
        (tm, cols), lambda p, g, k: (MOE_TILES * p + jnp.where(g == 0, k, MOE_TILES - 1), 0))
    last_blk = lambda cols: pl.BlockSpec(
        (tm, cols), lambda p, g, k: (MOE_TILES * p + jnp.where(g == n_groups - 1, k, 0), 0),
        pipeline_mode=pl.Buffered(1))
    w_blk = lambda *shape: pl.BlockSpec((epg,) + shape, lambda p, g, k: (g, 0, 0))
    return pl.pallas_call(
        functools.partial(_moe_kernel, apply_final=apply_final),
        grid=(n // (MOE_TILES * tm), n_groups, MOE_TILES),
        in_specs=[first_blk(d), first_blk(LANES), last_blk(d), w_blk(d, de), w_blk(d, de), w_blk(de, d),
                  pl.BlockSpec((1, d), lambda p, g, k: (0, 0))],
        out_specs=last_blk(d),
        out_shape=jax.ShapeDtypeStruct((n, d), F32),
        scratch_shapes=[pltpu.VMEM((tm, tm), BF16), pltpu.VMEM((MOE_TILES, t_pad, d), BF16),
                        pltpu.VMEM((MOE_TILES, t_pad, LANES), F32), pltpu.VMEM((MOE_TILES, tm, 1), jnp.int32),
                        pltpu.SMEM((MOE_TILES, 2 * N_GROUPS), jnp.int32)],
        compiler_params=pltpu.CompilerParams(dimension_semantics=("arbitrary",) * 3,
                                             vmem_limit_bytes=VMEM_LIMIT),
        name="moe",
    )(hm, comb, x1, wg, wu, wd, gfin)


def kernel(x, norm_attn_g, w_in, b_forget, lambda_q1, lambda_k1, lambda_q2, lambda_k2, diff_norm_g, w_out,
           norm_ffn_g, router_group_w, router_group_b, router_expert_w, router_expert_b, w_gate, w_up, w_down,
           norm_final_g):
    b, s, d = x.shape
    depth = w_in.shape[0]
    n = b * s
    diff_w = N_DIFF_HEADS * 2 * HEAD_DIM
    fox_w = N_FOX_HEADS * HEAD_DIM

    pos = jnp.arange(s, dtype=F32)
    inv_freq = 1.0 / (ROPE_THETA ** (jnp.arange(0, HEAD_DIM, 2, dtype=F32) / HEAD_DIM))
    ang = pos[:, None] * inv_freq[None, :]
    ang = jnp.concatenate([ang, ang, ang, ang], axis=-1)
    first_half = (jnp.arange(LANES) % HEAD_DIM) < HEAD_DIM // 2
    cos2, sin2 = jnp.cos(ang), jnp.where(first_half, -jnp.sin(ang), jnp.sin(ang))

    x2 = x.reshape(n, d)
    for l in range(depth):
        lam_init = 0.8 - 0.6 * float(np.exp(-0.3 * l))
        w = w_in[l]
        offs = np.cumsum([0, diff_w, diff_w, diff_w, fox_w, fox_w, fox_w, N_FOX_HEADS])
        seg = [w[:, offs[k]:offs[k + 1]] for k in range(7)]
        w_stack = jnp.stack([seg[0], seg[1], seg[3], seg[4]]).astype(BF16)
        wt_stack = jnp.stack([seg[2].T, seg[5].T]).astype(BF16)
        wf = jnp.pad(seg[6], ((0, 0), (0, LANES - N_FOX_HEADS)))
        wf_cat = jnp.concatenate(_split2(wf), axis=1)
        bf_pad = jnp.pad(b_forget[l], (0, LANES - N_FOX_HEADS)).reshape(1, LANES)

        qd, kd, vdt, qf, kf, vft, qb, kb = _in_proj(
            x2, norm_attn_g[l].reshape(1, d), cos2, sin2, w_stack, wt_stack, wf_cat, bf_pad,
            batch=b, seq=s, tm=512)

        to3 = lambda t: t.reshape(b, s, t.shape[-1])
        lam_params = jnp.stack([lambda_q1[l], lambda_k1[l], lambda_q2[l], lambda_k2[l]])
        od = _diff_attn(to3(qd), to3(kd), vdt, lam_params, diff_norm_g[l].reshape(-1, 1),
                        tq=512, lam_init=lam_init)
        of = _fox_attn(to3(qf), to3(kf), vft, to3(qb), to3(kb), tq=512)

        wo = w_out[l].astype(BF16).reshape(2, -1, d)
        wr = jnp.zeros((d, LANES), F32)
        wr = wr.at[:, :N_GROUPS].set(router_group_w[l]).at[:, 8:8 + N_EXPERTS].set(router_expert_w[l])
        wr_cat = jnp.concatenate(_split2(wr.T), axis=0)
        br = jnp.zeros((LANES,), F32)
        br = br.at[:N_GROUPS].set(router_group_b[l]).at[8:8 + N_EXPERTS].set(router_expert_b[l]).reshape(LANES, 1)
        x1, hm, comb = _out_proj(x2, od.reshape(n, -1), of.reshape(n, -1), wo, norm_ffn_g[l].reshape(1, d),
                                 wr_cat, br, tm=1024)

        x2 = _moe(hm, comb, x1, w_gate[l].astype(BF16), w_up[l].astype(BF16), w_down[l].astype(BF16),
                  norm_final_g.reshape(1, d), tm=1024, apply_final=(l == depth - 1))
    return x2.reshape(b, s, d)
```

```python
import functools

import numpy as np
import jax
import jax.numpy as jnp
from jax import lax
from jax.experimental import pallas as pl
from jax.experimental.pallas import tpu as pltpu

CHUNK = 64
HEAD_DIM = 64
ROPE_THETA = 10000.0
EPS = 1e-6
LOG2E = 1.4426950408889634
N_DIFF_HEADS = 4
N_FOX_HEADS = 8
N_GROUPS = 4
EXPERTS_PER_GROUP = 4
N_EXPERTS = N_GROUPS * EXPERTS_PER_GROUP

LANES = 128
NEG_BIG = -1e30
VMEM_LIMIT = 56 * 1024 * 1024
N_BIAS = 6
SUBLANES = 8
GID_LANE = N_EXPERTS
ROW_BLOCK = 128
ONES_ROWS = 16
BLOCKS_PER_STEP = 2
MOE_TILES = 2
OUT_PROJ_SPLIT = 2
SORT_ROWS = 512

BF16 = jnp.bfloat16
F32 = jnp.float32
_NT = (((1,), (1,)), ((), ()))


def _dot(a, b):
    return jnp.dot(a, b, preferred_element_type=F32)


def _dot_nt(a, b):
    return lax.dot_general(a, b, _NT, preferred_element_type=F32)


def _split2(x):
    hi = x.astype(BF16)
    lo = (x - hi.astype(F32)).astype(BF16)
    return hi, lo


def _split3(x):
    hi = x.astype(BF16)
    r = x - hi.astype(F32)
    mid = r.astype(BF16)
    lo = (r - mid.astype(F32)).astype(BF16)
    return hi, mid, lo


def _rms(x, g):
    return x * lax.rsqrt(jnp.mean(x * x, axis=-1, keepdims=True) + EPS) * g


def _pack_parts(parts):
    packed = parts[0].astype(F32)
    for k in (1, 2):
        packed = packed + pltpu.roll(parts[k].astype(F32), k * N_FOX_HEADS, axis=1)
    return packed.astype(BF16)


def _in_proj_kernel(x_ref, g_ref, cos_ref, sin_ref, w_ref, wt_ref, wf_cat_ref, bf_ref, tri_ref,
                    eq_ref, ek_ref, cq_ref, ck_ref,
                    qd_ref, kd_ref, vdt_ref, qf_ref, kf_ref, vft_ref, qb_ref, kb_ref, carry_ref,
                    *, tiles_per_seq):
    i = pl.program_id(0)
    h = _rms(x_ref[...], g_ref[...])
    hb = h.astype(BF16)
    cos = cos_ref[...]
    sin = sin_ref[...]
    scale = HEAD_DIM ** -0.5 * LOG2E
    lane = lax.broadcasted_iota(jnp.int32, cos.shape, 1)
    first_half = (lane % HEAD_DIM) < HEAD_DIM // 2

    def rope(w_idx, out_ref, mul):
        a = _dot(hb, w_ref[w_idx])
        for c in range(a.shape[1] // LANES):
            sl = slice(c * LANES, (c + 1) * LANES)
            x = a[:, sl]
            rot = jnp.where(first_half, pltpu.roll(x, LANES - HEAD_DIM // 2, axis=1),
                            pltpu.roll(x, HEAD_DIM // 2, axis=1))
            out_ref[:, sl] = ((x * cos + rot * sin) * mul).astype(out_ref.dtype)

    @pl.when(i % tiles_per_seq == 0)
    def _():
        carry_ref[...] = jnp.zeros_like(carry_ref)


    zz = _dot(hb, wf_cat_ref[...])
    z = (zz[:, :LANES] + zz[:, LANES:]) + bf_ref[...]
    log_f = jnp.minimum(z, 0.0) - jnp.log1p(jnp.exp(-jnp.abs(z)))
    valid = lane < N_FOX_HEADS
    log_f = jnp.where(valid, log_f, 0.0)

    rope(0, qd_ref, scale)

    r = _dot(tri_ref[...], _pack_parts(_split3(log_f)))
    cum = (r + pltpu.roll(r, LANES - N_FOX_HEADS, axis=1)) + pltpu.roll(r, LANES - 2 * N_FOX_HEADS, axis=1)
    cum = jnp.where(valid, cum + carry_ref[...], 0.0)
    carry_ref[...] = cum[cum.shape[0] - 1:, :]

    rope(1, kd_ref, 1.0)

    f_pack = _pack_parts(_split3(cum * LOG2E))
    qb_ref[...] = (cq_ref[...] + _dot(f_pack, eq_ref[...])).astype(qb_ref.dtype)
    kb_ref[...] = (ck_ref[...] + _dot(f_pack, ek_ref[...])).astype(kb_ref.dtype)

    qf_ref[...] = (_dot(hb, w_ref[2]) * scale).astype(qf_ref.dtype)
    kf_ref[...] = _dot(hb, w_ref[3]).astype(kf_ref.dtype)
    vdt_ref[0] = _dot_nt(wt_ref[0], hb).astype(vdt_ref.dtype)
    vft_ref[0] = _dot_nt(wt_ref[1], hb).astype(vft_ref.dtype)


def _bias_placement():
    width = (N_FOX_HEADS // 2) * LANES
    eq = np.zeros((LANES, width), np.float32)
    ek = np.zeros((LANES, width), np.float32)
    cq = np.zeros((1, width), np.float32)
    ck = np.zeros((1, width), np.float32)
    for h in range(N_FOX_HEADS):
        base = (h // 2) * LANES + (h % 2) * N_BIAS
        for part in range(3):
            eq[part * N_FOX_HEADS + h, base + part] = 1.0
            ek[part * N_FOX_HEADS + h, base + 3 + part] = -1.0
        cq[0, base + 3:base + 6] = 1.0
        ck[0, base:base + 3] = 1.0
    return jnp.asarray(eq, BF16), jnp.asarray(ek, BF16), jnp.asarray(cq), jnp.asarray(ck)


def _in_proj(x2, g, cos2, sin2, w_stack, wt_stack, wf_cat, bf_pad, *, batch, seq, tm):
    n, d = x2.shape
    tiles_per_seq = seq // tm
    tri = jnp.tril(jnp.ones((tm, tm), F32)).astype(BF16)
    eq, ek, cq, ck = _bias_placement()
    wcols = w_stack.shape[2]
    const = lambda *shape: pl.BlockSpec(shape, lambda i: (0,) * len(shape))
    row_blk = lambda cols: pl.BlockSpec((tm, cols), lambda i: (i, 0))
    pos_blk = pl.BlockSpec((tm, LANES), lambda i: (i % tiles_per_seq, 0))
    vt_blk = pl.BlockSpec((1, wcols, tm), lambda i: (i // tiles_per_seq, 0, i % tiles_per_seq))
    out_bf = jax.ShapeDtypeStruct((n, wcols), BF16)
    out_vt = jax.ShapeDtypeStruct((batch, wcols, seq), BF16)
    return pl.pallas_call(
        functools.partial(_in_proj_kernel, tiles_per_seq=tiles_per_seq),
        grid=(n // tm,),
        in_specs=[row_blk(d), const(1, d), pos_blk, pos_blk, const(*w_stack.shape), const(*wt_stack.shape),
                  const(*wf_cat.shape), const(1, LANES), const(tm, tm),
                  const(*eq.shape), const(*ek.shape), const(*cq.shape), const(*ck.shape)],
        out_specs=[row_blk(wcols), row_blk(wcols), vt_blk, row_blk(wcols), row_blk(wcols), vt_blk,
                   row_blk(eq.shape[1]), row_blk(ek.shape[1])],
        out_shape=[out_bf, out_bf, out_vt, out_bf, out_bf, out_vt,
                   jax.ShapeDtypeStruct((n, eq.shape[1]), BF16), jax.ShapeDtypeStruct((n, ek.shape[1]), BF16)],
        scratch_shapes=[pltpu.VMEM((1, LANES), F32)],
        compiler_params=pltpu.CompilerParams(dimension_semantics=("arbitrary",),
                                             vmem_limit_bytes=VMEM_LIMIT),
        name="in_proj",
    )(x2, g, cos2, sin2, w_stack, wt_stack, wf_cat, bf_pad, tri, eq, ek, cq, ck)


def _flash_sweep(nq, tq, tabs, logits_fn, values_fn, mask, scratch):
    s0, s1, c0, c1, m_ref, acc_ref = scratch
    row_tab, k_tab = tabs
    n_maps = m_ref.shape[1]
    n_off = nq * (nq - 1) // 2
    ones = jnp.ones((ONES_ROWS, tq), BF16)

    def stage_a(row, tile, s_buf, c_buf, masked):
        sts = logits_fn(row, tile)
        for a in range(n_maps):
            st = jnp.where(mask, sts[a], NEG_BIG) if masked else sts[a]
            s_buf[a] = st
            c_buf[a] = jnp.max(st, axis=0, keepdims=True)

    def stage_b(row, tile, s_buf, c_buf, first):
        vts = values_fn(tile)
        for a in range(n_maps):
            vt = jnp.concatenate([vts[a], ones], axis=0)
            if first:
                m_new = c_buf[a]
                acc_ref[row, a] = _dot(vt, jnp.exp2(s_buf[a] - m_new).astype(BF16))
            else:
                m = m_ref[row, a]
                m_new = jnp.maximum(m, c_buf[a])
                alpha = jnp.exp2(m - m_new)
                acc_ref[row, a] = alpha * acc_ref[row, a] + _dot(vt, jnp.exp2(s_buf[a] - m_new).astype(BF16))
            m_ref[row, a] = m_new

    stage_a(0, 0, s0, c0, True)

    def diag_pair(jj, _):
        r = 2 * jj
        stage_a(r + 1, r + 1, s1, c1, True)
        stage_b(r, r, s0, c0, True)
        stage_a(r + 2, r + 2, s0, c0, True)
        stage_b(r + 1, r + 1, s1, c1, True)
        return 0

    lax.fori_loop(0, (nq - 2) // 2, diag_pair, 0)
    stage_a(nq - 1, nq - 1, s1, c1, True)
    stage_b(nq - 2, nq - 2, s0, c0, True)

    stage_a(row_tab[0], k_tab[0], s0, c0, False)
    stage_b(nq - 1, nq - 1, s1, c1, True)

    def off_pair(jj, _):
        t = 2 * jj
        stage_a(row_tab[t + 1], k_tab[t + 1], s1, c1, False)
        stage_b(row_tab[t], k_tab[t], s0, c0, False)
        stage_a(row_tab[t + 2], k_tab[t + 2], s0, c0, False)
        stage_b(row_tab[t + 1], k_tab[t + 1], s1, c1, False)
        return 0

    lax.fori_loop(0, (n_off - 2) // 2, off_pair, 0)
    stage_a(row_tab[n_off - 1], k_tab[n_off - 1], s1, c1, False)
    stage_b(row_tab[n_off - 2], k_tab[n_off - 2], s0, c0, False)
    stage_b(row_tab[n_off - 1], k_tab[n_off - 1], s1, c1, False)


def _flash_scratch(nq, n_maps, dv, tq):
    s_buf = pltpu.VMEM((n_maps, tq, tq), F32)
    c_buf = pltpu.VMEM((n_maps, 1, tq), F32)
    return [s_buf, s_buf, c_buf, c_buf, pltpu.VMEM((nq, n_maps, 1, tq), F32),
            pltpu.VMEM((nq, n_maps, dv + ONES_ROWS, tq), F32)]


def _off_diagonal_tables(nq):
    pairs = [(r, k) for r in range(nq) for k in range(r)]
    return (jnp.asarray([p[0] for p in pairs], jnp.int32), jnp.asarray([p[1] for p in pairs], jnp.int32))


def _flash_outputs(acc_ref, row, dv):
    return [acc_ref[row, a, :dv, :] / acc_ref[row, a, dv:dv + 1, :] for a in range(acc_ref.shape[1])]


def _head_blocks(block):
    return [block[:, k * LANES:(k + 1) * LANES] for k in range(BLOCKS_PER_STEP)]


def _tile_rows(ref, tile, tq):
    return ref[0, pl.ds(pl.multiple_of(tile * tq, tq), tq), :]


def _diff_attn_kernel(row_tab, k_tab, q_ref, k_ref, vt_ref, lam_ref, g_ref, o_ref, *scratch, tq, lam_init):
    nq = q_ref.shape[1] // tq

    def logits(row, tile):
        qs = []
        for q in _head_blocks(_tile_rows(q_ref, row, tq)):
            lane = lax.broadcasted_iota(jnp.int32, q.shape, 1)
            zero = jnp.zeros_like(q)
            qs += [jnp.where(lane < HEAD_DIM, q, zero), jnp.where(lane >= HEAD_DIM, q, zero)]
        kts = _head_blocks(_tile_rows(k_ref, tile, tq))
        return [_dot_nt(kts[a // 2], qs[a]) for a in range(len(qs))]

    def values(tile):
        vt = vt_ref[0, :, pl.ds(pl.multiple_of(tile * tq, tq), tq)]
        return [vt[(a // 2) * LANES:(a // 2 + 1) * LANES] for a in range(2 * BLOCKS_PER_STEP)]

    key = lax.broadcasted_iota(jnp.int32, (tq, tq), 0)
    qry = lax.broadcasted_iota(jnp.int32, (tq, tq), 1)
    _flash_sweep(nq, tq, (row_tab, k_tab), logits, values, (key // CHUNK) <= (qry // CHUNK), scratch)

    lp = lam_ref[...]
    lam = (jnp.exp(jnp.sum(lp[0:1] * lp[1:2], axis=1, keepdims=True))
           - jnp.exp(jnp.sum(lp[2:3] * lp[3:4], axis=1, keepdims=True)) + lam_init)

    def finish(row, _):
        outs = _flash_outputs(scratch[-1], row, LANES)
        ys = []
        for k in range(BLOCKS_PER_STEP):
            o = outs[2 * k] - lam * outs[2 * k + 1]
            y = o * lax.rsqrt(jnp.mean(o * o, axis=0, keepdims=True) + EPS) * g_ref[...] * (1.0 - lam_init)
            ys.append(y.T.astype(o_ref.dtype))
        o_ref[0, pl.ds(pl.multiple_of(row * tq, tq), tq), :] = jnp.concatenate(ys, axis=1)
        return 0

    lax.fori_loop(0, nq, finish, 0)


def _attn_call(kernel_fn, name, n_maps, dv, inputs, in_specs, b, s, w, tq):
    nq = s // tq
    assert nq % 2 == 0 and nq >= 4, "the pipelined sweeps handle tiles in pairs"
    wb = BLOCKS_PER_STEP * LANES
    return pl.pallas_call(
        kernel_fn,
        grid_spec=pltpu.PrefetchScalarGridSpec(
            num_scalar_prefetch=2,
            grid=(b, w // wb),
            in_specs=in_specs,
            out_specs=pl.BlockSpec((1, s, wb), lambda bi, h, *_: (bi, 0, h)),
            scratch_shapes=_flash_scratch(nq, n_maps, dv, tq)),
        out_shape=jax.ShapeDtypeStruct((b, s, w), BF16),
        compiler_params=pltpu.CompilerParams(dimension_semantics=("arbitrary",) * 2,
                                             vmem_limit_bytes=VMEM_LIMIT),
        name=name,
    )(*_off_diagonal_tables(nq), *inputs)


def _attn_specs(s):
    wb = BLOCKS_PER_STEP * LANES
    row_blk = pl.BlockSpec((1, s, wb), lambda bi, h, *_: (bi, 0, h))
    vt_blk = pl.BlockSpec((1, wb, s), lambda bi, h, *_: (bi, h, 0))
    return row_blk, vt_blk


def _diff_attn(qd, kd, vdt, lam_params, gnorm_col, *, tq, lam_init):
    b, s, w = qd.shape
    row_blk, vt_blk = _attn_specs(s)
    small = lambda arr: pl.BlockSpec(arr.shape, lambda bi, h, *_: (0, 0))
    return _attn_call(functools.partial(_diff_attn_kernel, tq=tq, lam_init=lam_init), "diff_attn",
                      2 * BLOCKS_PER_STEP, LANES, (qd, kd, vdt, lam_params, gnorm_col),
                      [row_blk, row_blk, vt_blk, small(lam_params), small(gnorm_col)], b, s, w, tq)


def _fox_attn_kernel(row_tab, k_tab, q_ref, k_ref, vt_ref, qb_ref, kb_ref, o_ref, *scratch, tq):
    nq = q_ref.shape[1] // tq

    def logits(row, tile):
        qs = []
        for q, qb in zip(_head_blocks(_tile_rows(q_ref, row, tq)), _head_blocks(_tile_rows(qb_ref, row, tq))):
            lane = lax.broadcasted_iota(jnp.int32, q.shape, 1)
            zero = jnp.zeros_like(q)
            qs += [jnp.concatenate([jnp.where(lane < HEAD_DIM, q, zero), jnp.where(lane < N_BIAS, qb, zero)],
                                   axis=1),
                   jnp.concatenate([jnp.where(lane >= HEAD_DIM, q, zero),
                                    jnp.where((lane >= N_BIAS) & (lane < 2 * N_BIAS), qb, zero)], axis=1)]
        kts = [jnp.concatenate([k, kb], axis=1) for k, kb in zip(_head_blocks(_tile_rows(k_ref, tile, tq)),
                                                                   _head_blocks(_tile_rows(kb_ref, tile, tq)))]
        return [_dot_nt(kts[a // 2], qs[a]) for a in range(len(qs))]

    def values(tile):
        vt = vt_ref[0, :, pl.ds(pl.multiple_of(tile * tq, tq), tq)]
        return [vt[a * HEAD_DIM:(a + 1) * HEAD_DIM] for a in range(2 * BLOCKS_PER_STEP)]

    key = lax.broadcasted_iota(jnp.int32, (tq, tq), 0)
    qry = lax.broadcasted_iota(jnp.int32, (tq, tq), 1)
    _flash_sweep(nq, tq, (row_tab, k_tab), logits, values, key <= qry, scratch)

    def finish(row, _):
        o = jnp.concatenate(_flash_outputs(scratch[-1], row, HEAD_DIM), axis=0)
        o_ref[0, pl.ds(pl.multiple_of(row * tq, tq), tq), :] = o.T.astype(o_ref.dtype)
        return 0

    lax.fori_loop(0, nq, finish, 0)


def _fox_attn(qf, kf, vft, qb, kb, *, tq):
    b, s, w = qf.shape
    row_blk, vt_blk = _attn_specs(s)
    return _attn_call(functools.partial(_fox_attn_kernel, tq=tq), "fox_attn", 2 * BLOCKS_PER_STEP, HEAD_DIM,
                      (qf, kf, vft, qb, kb), [row_blk, row_blk, vt_blk, row_blk, row_blk], b, s, w, tq)


def _route(lt):
    tm = lt.shape[1]
    g8 = lt[0:8]
    r8 = lax.broadcasted_iota(jnp.int32, (8, tm), 0)
    g8 = jnp.where(r8 < N_GROUPS, g8, NEG_BIG)
    gmax = jnp.max(g8, axis=0, keepdims=True)
    gidx = jnp.min(jnp.where(g8 == gmax, r8, N_GROUPS), axis=0, keepdims=True)
    g_w = 1.0 / jnp.sum(jnp.exp(g8 - gmax), axis=0, keepdims=True)

    e16 = lt[8:8 + N_EXPERTS]
    r16 = lax.broadcasted_iota(jnp.int32, (N_EXPERTS, tm), 0)
    in_group = (r16 // EXPERTS_PER_GROUP) == gidx
    e_sel = jnp.where(in_group, e16, NEG_BIG)
    top1 = jnp.max(e_sel, axis=0, keepdims=True)
    id1 = jnp.min(jnp.where(e_sel == top1, r16, N_EXPERTS), axis=0, keepdims=True)
    e_rest = jnp.where(r16 == id1, NEG_BIG, e_sel)
    top2 = jnp.max(e_rest, axis=0, keepdims=True)
    id2 = jnp.min(jnp.where(e_rest == top2, r16, N_EXPERTS), axis=0, keepdims=True)
    t = jnp.exp(top2 - top1)
    w1 = g_w / (1.0 + t)
    w2 = w1 * t
    r128 = lax.broadcasted_iota(jnp.int32, (LANES, tm), 0)
    return (jnp.where(r128 == id1, w1, 0.0) + jnp.where(r128 == id2, w2, 0.0)
            + jnp.where(r128 == GID_LANE, gidx.astype(F32), 0.0))


def _out_proj_kernel(x_ref, od_ref, of_ref, wo_ref, g_ref, wr_cat_ref, br_ref,
                     x1_ref, hm_ref, comb_ref):
    tm = x_ref.shape[0]
    halves = [slice(k * tm // OUT_PROJ_SPLIT, (k + 1) * tm // OUT_PROJ_SPLIT) for k in range(OUT_PROJ_SPLIT)]
    x1s = []
    for rows in halves:
        x1 = x_ref[rows, :] + (_dot(od_ref[rows, :], wo_ref[0]) + _dot(of_ref[rows, :], wo_ref[1]))
        x1_ref[rows, :] = x1
        x1s.append(x1)
    wr_cat = wr_cat_ref[...]
    for rows, x1 in zip(halves, x1s):
        hb, h_lo = _split2(_rms(x1, g_ref[...]))
        hm_ref[rows, :] = hb
        both = _dot_nt(wr_cat, hb)
        lt = ((both[:LANES] + both[LANES:]) + _dot_nt(wr_cat[:LANES], h_lo)) + br_ref[...]
        comb_ref[rows, :] = _route(lt).T


def _out_proj(x2, od, of, wo, g, wr_cat, br, *, tm):
    n, d = x2.shape
    const = lambda *shape: pl.BlockSpec(shape, lambda i: (0,) * len(shape))
    row_blk = lambda cols: pl.BlockSpec((tm, cols), lambda i: (i, 0))
    return pl.pallas_call(
        _out_proj_kernel,
        grid=(n // tm,),
        in_specs=[row_blk(d), row_blk(od.shape[1]), row_blk(of.shape[1]), const(*wo.shape), const(1, d),
                  const(*wr_cat.shape), const(*br.shape)],
        out_specs=[row_blk(d), row_blk(d), row_blk(LANES)],
        out_shape=[jax.ShapeDtypeStruct((n, d), F32), jax.ShapeDtypeStruct((n, d), BF16),
                   jax.ShapeDtypeStruct((n, LANES), F32)],
        compiler_params=pltpu.CompilerParams(dimension_semantics=("arbitrary",),
                                             vmem_limit_bytes=VMEM_LIMIT),
        name="out_proj",
    )(x2, od, of, wo, g, wr_cat, br)


def _moe_kernel(hm_ref, comb_ref, x1_hbm, wg_ref, wu_ref, wd_ref, gf_ref, o_hbm,
                tri_ref, hs_ref, cs_ref, posc_ref, tab_ref, xbuf_ref, obuf_ref, sem_ref, *, apply_final):
    g = pl.program_id(1)
    k = pl.program_id(2)
    t = hm_ref.shape[0]
    t_pad = hs_ref.shape[1]
    rb = ROW_BLOCK

    @pl.when((pl.program_id(0) == 0) & (g == 0) & (k == 0))
    def _():
        row = lax.broadcasted_iota(jnp.int32, (t, t), 0)
        col = lax.broadcasted_iota(jnp.int32, (t, t), 1)
        tri_ref[...] = jnp.where(row > col, 1.0, 0.0).astype(BF16)

    @pl.when(g == 0)
    def _():
        comb = comb_ref[...]
        lane = lax.broadcasted_iota(jnp.int32, comb.shape, 1)
        onehot_c = jnp.where(lane == comb[:, GID_LANE:GID_LANE + 1].astype(jnp.int32), 1.0, 0.0)
        rank_c = _dot(tri_ref[...], onehot_c.astype(BF16))
        counts = jnp.sum(onehot_c, axis=0, keepdims=True)
        lane1 = lax.broadcasted_iota(jnp.int32, (1, LANES), 1)
        start_c = jnp.zeros((1, LANES), F32)
        first = jnp.int32(0)
        for grp in range(N_GROUPS):
            cnt = jnp.sum(jnp.where(lane1 == grp, counts, 0.0)).astype(jnp.int32)
            nblk = lax.shift_right_logical(cnt + (rb - 1), int(np.log2(rb)))
            tab_ref[k, grp] = first
            tab_ref[k, N_GROUPS + grp] = nblk
            start_c = jnp.where(lane1 == grp, (first * rb).astype(F32), start_c)
            first = first + nblk
        dest = onehot_c * (rank_c + start_c)
        posc_ref[k] = jnp.sum(dest, axis=1, keepdims=True).astype(jnp.int32)
        pos_r = jnp.sum(dest.T, axis=0, keepdims=True).astype(jnp.int32)
        c_parts = _split3(jnp.where(lane < N_EXPERTS, comb, 0.0))
        c_pack = (c_parts[0].astype(F32) + pltpu.roll(c_parts[1].astype(F32), N_EXPERTS, axis=1)
                  + pltpu.roll(c_parts[2].astype(F32), 2 * N_EXPERTS, axis=1)).astype(BF16)
        hm = hm_ref[...]
        for r0 in range(0, t_pad, SORT_ROWS):
            dst = lax.broadcasted_iota(jnp.int32, (SORT_ROWS, t), 0) + r0
            perm = jnp.where(dst == pos_r, 1.0, 0.0).astype(BF16)
            hs_ref[k, r0:r0 + SORT_ROWS, :] = _dot(perm, hm).astype(hs_ref.dtype)
            cp = _dot(perm, c_pack)
            cs_ref[k, r0:r0 + SORT_ROWS, :] = (cp + pltpu.roll(cp, LANES - N_EXPERTS, axis=1)
                                               + pltpu.roll(cp, LANES - 2 * N_EXPERTS, axis=1))

    b0 = tab_ref[k, g]

    def expert_block(b, _):
        off = pl.multiple_of(b * rb, rb)
        rows = hs_ref[k, pl.ds(off, rb), :]
        cblk = cs_ref[k, pl.ds(off, rb), :]
        lane = lax.broadcasted_iota(jnp.int32, cblk.shape, 1)
        gates = [_dot(rows, wg_ref[j]) for j in range(EXPERTS_PER_GROUP)]
        ups = [_dot(rows, wu_ref[j]) for j in range(EXPERTS_PER_GROUP)]
        y = None
        for j in range(EXPERTS_PER_GROUP):
            c = jnp.sum(jnp.where(lane == g * EXPERTS_PER_GROUP + j, cblk, 0.0), axis=1, keepdims=True)
            a = gates[j] * jax.nn.sigmoid(gates[j]) * ups[j] * c
            d = _dot(a.astype(BF16), wd_ref[j])
            y = d if y is None else y + d
        hs_ref[k, pl.ds(off, rb), :] = y.astype(hs_ref.dtype)
        return 0

    last_group = g == pl.num_programs(1) - 1
    tile = pl.program_id(0) * pl.num_programs(2) + k
    n_tiles = pl.num_programs(0) * pl.num_programs(2)

    def x1_copy():
        return pltpu.make_async_copy(x1_hbm.at[pl.ds(pl.multiple_of(tile * t, t), t), :], xbuf_ref, sem_ref.at[0])

    def out_copy(which):
        return pltpu.make_async_copy(obuf_ref, o_hbm.at[pl.ds(pl.multiple_of(which * t, t), t), :], sem_ref.at[1])

    @pl.when(last_group)
    def _():
        x1_copy().start()

    lax.fori_loop(b0, b0 + tab_ref[k, N_GROUPS + g], expert_block, 0)

    @pl.when(last_group)
    def _():
        x1_copy().wait()

        @pl.when(tile > 0)
        def _():
            out_copy(tile - 1).wait()

        ys = hs_ref[k]
        for r0 in range(0, t, SORT_ROWS):
            src = lax.broadcasted_iota(jnp.int32, (SORT_ROWS, t_pad), 1)
            unperm = jnp.where(src == posc_ref[k, r0:r0 + SORT_ROWS, :], 1.0, 0.0).astype(BF16)
            x2 = xbuf_ref[r0:r0 + SORT_ROWS, :] + _dot(unperm, ys)
            obuf_ref[r0:r0 + SORT_ROWS, :] = _rms(x2, gf_ref[...]) if apply_final else x2
        out_copy(tile).start()

        @pl.when(tile == n_tiles - 1)
        def _():
            out_copy(tile).wait()


def _moe(hm, comb, x1, wg, wu, wd, gfin, *, tm, apply_final):
    n, d = x1.shape
    ne, _, de = wg.shape
    epg = EXPERTS_PER_GROUP
    n_groups = ne // epg
    t_pad = tm + N_GROUPS * ROW_BLOCK
    assert t_pad % SORT_ROWS == 0 and tm % SORT_ROWS == 0 and n % (MOE_TILES * tm) == 0
    first_blk = lambda cols: pl.BlockSpec(
        (tm, cols), lambda p, g, k: (MOE_TILES * p + jnp.where(g == 0, k, MOE_TILES - 1), 0))
    w_blk = lambda *shape: pl.BlockSpec((epg,) + shape, lambda p, g, k: (g, 0, 0))
    hbm = pl.BlockSpec(memory_space=pl.ANY)
    return pl.pallas_call(
        functools.partial(_moe_kernel, apply_final=apply_final),
        grid=(n // (MOE_TILES * tm), n_groups, MOE_TILES),
        in_specs=[first_blk(d), first_blk(LANES), hbm, w_blk(d, de), w_blk(d, de), w_blk(de, d),
                  pl.BlockSpec((1, d), lambda p, g, k: (0, 0))],
        out_specs=hbm,
        out_shape=jax.ShapeDtypeStruct((n, d), F32),
        scratch_shapes=[pltpu.VMEM((tm, tm), BF16), pltpu.VMEM((MOE_TILES, t_pad, d), BF16),
                        pltpu.VMEM((MOE_TILES, t_pad, LANES), F32), pltpu.VMEM((MOE_TILES, tm, 1), jnp.int32),
                        pltpu.SMEM((MOE_TILES, 2 * N_GROUPS), jnp.int32),
                        pltpu.VMEM((tm, d), F32), pltpu.VMEM((tm, d), F32), pltpu.SemaphoreType.DMA((2,))],
        compiler_params=pltpu.CompilerParams(dimension_semantics=("arbitrary",) * 3,
                                             vmem_limit_bytes=VMEM_LIMIT),
        name="moe",
    )(hm, comb, x1, wg, wu, wd, gfin)


def kernel(x, norm_attn_g, w_in, b_forget, lambda_q1, lambda_k1, lambda_q2, lambda_k2, diff_norm_g, w_out,
           norm_ffn_g, router_group_w, router_group_b, router_expert_w, router_expert_b, w_gate, w_up, w_down,
           norm_final_g):
    b, s, d = x.shape
    depth = w_in.shape[0]
    n = b * s
    diff_w = N_DIFF_HEADS * 2 * HEAD_DIM
    fox_w = N_FOX_HEADS * HEAD_DIM

    pos = jnp.arange(s, dtype=F32)
    inv_freq = 1.0 / (ROPE_THETA ** (jnp.arange(0, HEAD_DIM, 2, dtype=F32) / HEAD_DIM))
    ang = pos[:, None] * inv_freq[None, :]
    ang = jnp.concatenate([ang, ang, ang, ang], axis=-1)
    first_half = (jnp.arange(LANES) % HEAD_DIM) < HEAD_DIM // 2
    cos2, sin2 = jnp.cos(ang), jnp.where(first_half, -jnp.sin(ang), jnp.sin(ang))

    x2 = x.reshape(n, d)
    for l in range(depth):
        lam_init = 0.8 - 0.6 * float(np.exp(-0.3 * l))
        w = w_in[l]
        offs = np.cumsum([0, diff_w, diff_w, diff_w, fox_w, fox_w, fox_w, N_FOX_HEADS])
        seg = [w[:, offs[k]:offs[k + 1]] for k in range(7)]
        w_stack = jnp.stack([seg[0], seg[1], seg[3], seg[4]]).astype(BF16)
        wt_stack = jnp.stack([seg[2].T, seg[5].T]).astype(BF16)
        wf = jnp.pad(seg[6], ((0, 0), (0, LANES - N_FOX_HEADS)))
        wf_cat = jnp.concatenate(_split2(wf), axis=1)
        bf_pad = jnp.pad(b_forget[l], (0, LANES - N_FOX_HEADS)).reshape(1, LANES)

        qd, kd, vdt, qf, kf, vft, qb, kb = _in_proj(
            x2, norm_attn_g[l].reshape(1, d), cos2, sin2, w_stack, wt_stack, wf_cat, bf_pad,
            batch=b, seq=s, tm=512)

        to3 = lambda t: t.reshape(b, s, t.shape[-1])
        lam_params = jnp.stack([lambda_q1[l], lambda_k1[l], lambda_q2[l], lambda_k2[l]])
        od = _diff_attn(to3(qd), to3(kd), vdt, lam_params, diff_norm_g[l].reshape(-1, 1),
                        tq=512, lam_init=lam_init)
        of = _fox_attn(to3(qf), to3(kf), vft, to3(qb), to3(kb), tq=512)

        wo = w_out[l].astype(BF16).reshape(2, -1, d)
        wr = jnp.zeros((d, LANES), F32)
        wr = wr.at[:, :N_GROUPS].set(router_group_w[l]).at[:, 8:8 + N_EXPERTS].set(router_expert_w[l])
        wr_cat = jnp.concatenate(_split2(wr.T), axis=0)
        br = jnp.zeros((LANES,), F32)
        br = br.at[:N_GROUPS].set(router_group_b[l]).at[8:8 + N_EXPERTS].set(router_expert_b[l]).reshape(LANES, 1)
        x1, hm, comb = _out_proj(x2, od.reshape(n, -1), of.reshape(n, -1), wo, norm_ffn_g[l].reshape(1, d),
                                 wr_cat, br, tm=1024)

        x2 = _moe(hm, comb, x1, w_gate[l].astype(BF16), w_up[l].astype(BF16), w_down[l].astype(BF16),
                  norm_final_g.reshape(1, d), tm=1024, apply_final=(l == depth - 1))
    return x2.reshape(b, s, d)
```

```python
import functools

import numpy as np
import jax
import jax.numpy as jnp
from jax import lax
from jax.experimental import pallas as pl
from jax.experimental.pallas import tpu as pltpu

CHUNK = 64
HEAD_DIM = 64
ROPE_THETA = 10000.0
EPS = 1e-6
LOG2E = 1.4426950408889634
N_DIFF_HEADS = 4
N_FOX_HEADS = 8
N_GROUPS = 4
EXPERTS_PER_GROUP = 4
N_EXPERTS = N_GROUPS * EXPERTS_PER_GROUP

LANES = 128
NEG_BIG = -1e30
VMEM_LIMIT = 56 * 1024 * 1024
IN_PROJ_ROWS = 512
ATTN_TILE = 512
OUT_PROJ_ROWS = 1024
MOE_ROWS = 1024
N_BIAS = 6
SUBLANES = 8
GID_LANE = N_EXPERTS
ROW_BLOCK = 128
ONES_ROWS = 16
BLOCKS_PER_STEP = 2
MOE_TILES = 2
OUT_PROJ_SPLIT = 2
SORT_ROWS = 512

BF16 = jnp.bfloat16
F32 = jnp.float32
_NT = (((1,), (1,)), ((), ()))


def _dot(a, b):
    return jnp.dot(a, b, preferred_element_type=F32)


def _dot_nt(a, b):
    return lax.dot_general(a, b, _NT, preferred_element_type=F32)


def _split2(x):
    hi = x.astype(BF16)
    lo = (x - hi.astype(F32)).astype(BF16)
    return hi, lo


def _split3(x):
    hi = x.astype(BF16)
    r = x - hi.astype(F32)
    mid = r.astype(BF16)
    lo = (r - mid.astype(F32)).astype(BF16)
    return hi, mid, lo


def _rms(x, g):
    return x * lax.rsqrt(jnp.mean(x * x, axis=-1, keepdims=True) + EPS) * g


def _pack_parts(parts):
    packed = parts[0].astype(F32)
    for k in (1, 2):
        packed = packed + pltpu.roll(parts[k].astype(F32), k * N_FOX_HEADS, axis=1)
    return packed.astype(BF16)


def _in_proj_kernel(x_ref, g_ref, cos_ref, sin_ref, w_ref, wt_ref, wf_cat_ref, bf_ref, tri_ref,
                    eq_ref, ek_ref, cq_ref, ck_ref,
                    qd_ref, kd_ref, vdt_ref, qf_ref, kf_ref, vft_ref, qb_ref, kb_ref, carry_ref,
                    *, tiles_per_seq):
    i = pl.program_id(0)
    h = _rms(x_ref[...], g_ref[...])
    hb = h.astype(BF16)
    cos = cos_ref[...]
    sin = sin_ref[...]
    scale = HEAD_DIM ** -0.5 * LOG2E
    lane = lax.broadcasted_iota(jnp.int32, cos.shape, 1)
    first_half = (lane % HEAD_DIM) < HEAD_DIM // 2

    def rope(w_idx, out_ref, mul):
        a = _dot(hb, w_ref[w_idx])
        for c in range(a.shape[1] // LANES):
            sl = slice(c * LANES, (c + 1) * LANES)
            x = a[:, sl]
            rot = jnp.where(first_half, pltpu.roll(x, LANES - HEAD_DIM // 2, axis=1),
                            pltpu.roll(x, HEAD_DIM // 2, axis=1))
            out_ref[:, sl] = ((x * cos + rot * sin) * mul).astype(out_ref.dtype)

    @pl.when(i % tiles_per_seq == 0)
    def _():
        carry_ref[...] = jnp.zeros_like(carry_ref)


    zz = _dot(hb, wf_cat_ref[...])
    z = (zz[:, :LANES] + zz[:, LANES:]) + bf_ref[...]
    log_f = jnp.minimum(z, 0.0) - jnp.log1p(jnp.exp(-jnp.abs(z)))
    valid = lane < N_FOX_HEADS
    log_f = jnp.where(valid, log_f, 0.0)

    rope(0, qd_ref, scale)

    r = _dot(tri_ref[...], _pack_parts(_split3(log_f)))
    cum = (r + pltpu.roll(r, LANES - N_FOX_HEADS, axis=1)) + pltpu.roll(r, LANES - 2 * N_FOX_HEADS, axis=1)
    cum = jnp.where(valid, cum + carry_ref[...], 0.0)
    carry_ref[...] = cum[cum.shape[0] - 1:, :]

    rope(1, kd_ref, 1.0)

    f_pack = _pack_parts(_split3(cum * LOG2E))
    qb_ref[...] = (cq_ref[...] + _dot(f_pack, eq_ref[...])).astype(qb_ref.dtype)
    kb_ref[...] = (ck_ref[...] + _dot(f_pack, ek_ref[...])).astype(kb_ref.dtype)

    qf_ref[...] = (_dot(hb, w_ref[2]) * scale).astype(qf_ref.dtype)
    kf_ref[...] = _dot(hb, w_ref[3]).astype(kf_ref.dtype)
    vdt_ref[0] = _dot_nt(wt_ref[0], hb).astype(vdt_ref.dtype)
    vft_ref[0] = _dot_nt(wt_ref[1], hb).astype(vft_ref.dtype)


def _bias_placement():
    width = (N_FOX_HEADS // 2) * LANES
    eq = np.zeros((LANES, width), np.float32)
    ek = np.zeros((LANES, width), np.float32)
    cq = np.zeros((1, width), np.float32)
    ck = np.zeros((1, width), np.float32)
    for h in range(N_FOX_HEADS):
        base = (h // 2) * LANES + (h % 2) * N_BIAS
        for part in range(3):
            eq[part * N_FOX_HEADS + h, base + part] = 1.0
            ek[part * N_FOX_HEADS + h, base + 3 + part] = -1.0
        cq[0, base + 3:base + 6] = 1.0
        ck[0, base:base + 3] = 1.0
    return jnp.asarray(eq, BF16), jnp.asarray(ek, BF16), jnp.asarray(cq), jnp.asarray(ck)


def _in_proj(x2, g, cos2, sin2, w_stack, wt_stack, wf_cat, bf_pad, *, batch, seq, tm):
    n, d = x2.shape
    tiles_per_seq = seq // tm
    tri = jnp.tril(jnp.ones((tm, tm), F32)).astype(BF16)
    eq, ek, cq, ck = _bias_placement()
    wcols = w_stack.shape[2]
    const = lambda *shape: pl.BlockSpec(shape, lambda i: (0,) * len(shape))
    row_blk = lambda cols: pl.BlockSpec((tm, cols), lambda i: (i, 0))
    pos_blk = pl.BlockSpec((tm, LANES), lambda i: (i % tiles_per_seq, 0))
    vt_blk = pl.BlockSpec((1, wcols, tm), lambda i: (i // tiles_per_seq, 0, i % tiles_per_seq))
    out_bf = jax.ShapeDtypeStruct((n, wcols), BF16)
    out_vt = jax.ShapeDtypeStruct((batch, wcols, seq), BF16)
    return pl.pallas_call(
        functools.partial(_in_proj_kernel, tiles_per_seq=tiles_per_seq),
        grid=(n // tm,),
        in_specs=[row_blk(d), const(1, d), pos_blk, pos_blk, const(*w_stack.shape), const(*wt_stack.shape),
                  const(*wf_cat.shape), const(1, LANES), const(tm, tm),
                  const(*eq.shape), const(*ek.shape), const(*cq.shape), const(*ck.shape)],
        out_specs=[row_blk(wcols), row_blk(wcols), vt_blk, row_blk(wcols), row_blk(wcols), vt_blk,
                   row_blk(eq.shape[1]), row_blk(ek.shape[1])],
        out_shape=[out_bf, out_bf, out_vt, out_bf, out_bf, out_vt,
                   jax.ShapeDtypeStruct((n, eq.shape[1]), BF16), jax.ShapeDtypeStruct((n, ek.shape[1]), BF16)],
        scratch_shapes=[pltpu.VMEM((1, LANES), F32)],
        compiler_params=pltpu.CompilerParams(dimension_semantics=("arbitrary",),
                                             vmem_limit_bytes=VMEM_LIMIT),
        name="in_proj",
    )(x2, g, cos2, sin2, w_stack, wt_stack, wf_cat, bf_pad, tri, eq, ek, cq, ck)


def _flash_sweep(nq, tq, tabs, logits_fn, values_fn, mask, scratch):
    s0, s1, c0, c1, m_ref, acc_ref = scratch
    row_tab, k_tab = tabs
    n_maps = m_ref.shape[1]
    n_off = nq * (nq - 1) // 2
    ones = jnp.ones((ONES_ROWS, tq), BF16)

    def stage_a(row, tile, s_buf, c_buf, masked):
        sts = logits_fn(row, tile, masked)
        for a in range(n_maps):
            st = jnp.where(mask, sts[a], NEG_BIG) if masked else sts[a]
            s_buf[a] = st
            c_buf[a] = jnp.max(st, axis=0, keepdims=True)

    def stage_b(row, tile, s_buf, c_buf, first):
        vts = values_fn(tile)
        for a in range(n_maps):
            vt = jnp.concatenate([vts[a], ones], axis=0)
            if first:
                m_new = c_buf[a]
                acc_ref[row, a] = _dot(vt, jnp.exp2(s_buf[a] - m_new).astype(BF16))
            else:
                m = m_ref[row, a]
                m_new = jnp.maximum(m, c_buf[a])
                alpha = jnp.exp2(m - m_new)
                acc_ref[row, a] = alpha * acc_ref[row, a] + _dot(vt, jnp.exp2(s_buf[a] - m_new).astype(BF16))
            m_ref[row, a] = m_new

    stage_a(0, 0, s0, c0, True)

    def diag_pair(jj, _):
        r = 2 * jj
        stage_a(r + 1, r + 1, s1, c1, True)
        stage_b(r, r, s0, c0, True)
        stage_a(r + 2, r + 2, s0, c0, True)
        stage_b(r + 1, r + 1, s1, c1, True)
        return 0

    lax.fori_loop(0, (nq - 2) // 2, diag_pair, 0)
    stage_a(nq - 1, nq - 1, s1, c1, True)
    stage_b(nq - 2, nq - 2, s0, c0, True)

    stage_a(row_tab[0], k_tab[0], s0, c0, False)
    stage_b(nq - 1, nq - 1, s1, c1, True)

    def off_pair(jj, _):
        t = 2 * jj
        stage_a(row_tab[t + 1], k_tab[t + 1], s1, c1, False)
        stage_b(row_tab[t], k_tab[t], s0, c0, False)
        stage_a(row_tab[t + 2], k_tab[t + 2], s0, c0, False)
        stage_b(row_tab[t + 1], k_tab[t + 1], s1, c1, False)
        return 0

    lax.fori_loop(0, (n_off - 2) // 2, off_pair, 0)
    stage_a(row_tab[n_off - 1], k_tab[n_off - 1], s1, c1, False)
    stage_b(row_tab[n_off - 2], k_tab[n_off - 2], s0, c0, False)
    stage_b(row_tab[n_off - 1], k_tab[n_off - 1], s1, c1, False)


def _flash_scratch(nq, n_maps, dv, tq):
    s_buf = pltpu.VMEM((n_maps, tq, tq), F32)
    c_buf = pltpu.VMEM((n_maps, 1, tq), F32)
    return [s_buf, s_buf, c_buf, c_buf, pltpu.VMEM((nq, n_maps, 1, tq), F32),
            pltpu.VMEM((nq, n_maps, dv + ONES_ROWS, tq), F32)]


def _off_diagonal_tables(nq):
    pairs = [(r, k) for r in range(nq) for k in range(r)]
    return (jnp.asarray([p[0] for p in pairs], jnp.int32), jnp.asarray([p[1] for p in pairs], jnp.int32))


def _flash_outputs(acc_ref, row, dv):
    return [acc_ref[row, a, :dv, :] / acc_ref[row, a, dv:dv + 1, :] for a in range(acc_ref.shape[1])]


def _head_blocks(block):
    return [block[:, k * LANES:(k + 1) * LANES] for k in range(BLOCKS_PER_STEP)]


def _tile_logits(k, q, diagonal):
    if not diagonal:
        return _dot_nt(k, q)
    half = q.shape[0] // 2
    assert half % CHUNK == 0
    left = jnp.concatenate([_dot_nt(k[:half], q[:half]), jnp.full((k.shape[0] - half, half), NEG_BIG, F32)],
                           axis=0)
    return jnp.concatenate([left, _dot_nt(k, q[half:])], axis=1)


def _tile_rows(ref, tile, tq):
    return ref[0, pl.ds(pl.multiple_of(tile * tq, tq), tq), :]


def _diff_attn_kernel(row_tab, k_tab, q_ref, k_ref, vt_ref, lam_ref, g_ref, o_ref, *scratch, tq, lam_init):
    nq = q_ref.shape[1] // tq

    def logits(row, tile, diagonal):
        qs = []
        for q in _head_blocks(_tile_rows(q_ref, row, tq)):
            lane = lax.broadcasted_iota(jnp.int32, q.shape, 1)
            zero = jnp.zeros_like(q)
            qs += [jnp.where(lane < HEAD_DIM, q, zero), jnp.where(lane >= HEAD_DIM, q, zero)]
        kts = _head_blocks(_tile_rows(k_ref, tile, tq))
        return [_tile_logits(kts[a // 2], qs[a], diagonal) for a in range(len(qs))]

    def values(tile):
        vt = vt_ref[0, :, pl.ds(pl.multiple_of(tile * tq, tq), tq)]
        return [vt[(a // 2) * LANES:(a // 2 + 1) * LANES] for a in range(2 * BLOCKS_PER_STEP)]

    key = lax.broadcasted_iota(jnp.int32, (tq, tq), 0)
    qry = lax.broadcasted_iota(jnp.int32, (tq, tq), 1)
    _flash_sweep(nq, tq, (row_tab, k_tab), logits, values, (key // CHUNK) <= (qry // CHUNK), scratch)

    lp = lam_ref[...]
    lam = (jnp.exp(jnp.sum(lp[0:1] * lp[1:2], axis=1, keepdims=True))
           - jnp.exp(jnp.sum(lp[2:3] * lp[3:4], axis=1, keepdims=True)) + lam_init)

    def finish(row, _):
        outs = _flash_outputs(scratch[-1], row, LANES)
        ys = []
        for k in range(BLOCKS_PER_STEP):
            o = outs[2 * k] - lam * outs[2 * k + 1]
            y = o * lax.rsqrt(jnp.mean(o * o, axis=0, keepdims=True) + EPS) * g_ref[...] * (1.0 - lam_init)
            ys.append(y.T.astype(o_ref.dtype))
        o_ref[0, pl.ds(pl.multiple_of(row * tq, tq), tq), :] = jnp.concatenate(ys, axis=1)
        return 0

    lax.fori_loop(0, nq, finish, 0)


def _attn_call(kernel_fn, name, n_maps, dv, inputs, in_specs, b, s, w, tq):
    nq = s // tq
    assert nq % 2 == 0 and nq >= 4, "the pipelined sweeps handle tiles in pairs"
    wb = BLOCKS_PER_STEP * LANES
    return pl.pallas_call(
        kernel_fn,
        grid_spec=pltpu.PrefetchScalarGridSpec(
            num_scalar_prefetch=2,
            grid=(b, w // wb),
            in_specs=in_specs,
            out_specs=pl.BlockSpec((1, s, wb), lambda bi, h, *_: (bi, 0, h)),
            scratch_shapes=_flash_scratch(nq, n_maps, dv, tq)),
        out_shape=jax.ShapeDtypeStruct((b, s, w), BF16),
        compiler_params=pltpu.CompilerParams(dimension_semantics=("arbitrary",) * 2,
                                             vmem_limit_bytes=VMEM_LIMIT),
        name=name,
    )(*_off_diagonal_tables(nq), *inputs)


def _attn_specs(s):
    wb = BLOCKS_PER_STEP * LANES
    row_blk = pl.BlockSpec((1, s, wb), lambda bi, h, *_: (bi, 0, h))
    vt_blk = pl.BlockSpec((1, wb, s), lambda bi, h, *_: (bi, h, 0))
    return row_blk, vt_blk


def _diff_attn(qd, kd, vdt, lam_params, gnorm_col, *, tq, lam_init):
    b, s, w = qd.shape
    row_blk, vt_blk = _attn_specs(s)
    small = lambda arr: pl.BlockSpec(arr.shape, lambda bi, h, *_: (0, 0))
    return _attn_call(functools.partial(_diff_attn_kernel, tq=tq, lam_init=lam_init), "diff_attn",
                      2 * BLOCKS_PER_STEP, LANES, (qd, kd, vdt, lam_params, gnorm_col),
                      [row_blk, row_blk, vt_blk, small(lam_params), small(gnorm_col)], b, s, w, tq)


def _fox_attn_kernel(row_tab, k_tab, q_ref, k_ref, vt_ref, qb_ref, kb_ref, o_ref, *scratch, tq):
    nq = q_ref.shape[1] // tq

    def logits(row, tile, diagonal):
        qs = []
        for q, qb in zip(_head_blocks(_tile_rows(q_ref, row, tq)), _head_blocks(_tile_rows(qb_ref, row, tq))):
            lane = lax.broadcasted_iota(jnp.int32, q.shape, 1)
            zero = jnp.zeros_like(q)
            qs += [jnp.concatenate([jnp.where(lane < HEAD_DIM, q, zero), jnp.where(lane < N_BIAS, qb, zero)],
                                   axis=1),
                   jnp.concatenate([jnp.where(lane >= HEAD_DIM, q, zero),
                                    jnp.where((lane >= N_BIAS) & (lane < 2 * N_BIAS), qb, zero)], axis=1)]
        kts = [jnp.concatenate([k, kb], axis=1) for k, kb in zip(_head_blocks(_tile_rows(k_ref, tile, tq)),
                                                                   _head_blocks(_tile_rows(kb_ref, tile, tq)))]
        return [_tile_logits(kts[a // 2], qs[a], diagonal) for a in range(len(qs))]

    def values(tile):
        vt = vt_ref[0, :, pl.ds(pl.multiple_of(tile * tq, tq), tq)]
        return [vt[a * HEAD_DIM:(a + 1) * HEAD_DIM] for a in range(2 * BLOCKS_PER_STEP)]

    key = lax.broadcasted_iota(jnp.int32, (tq, tq), 0)
    qry = lax.broadcasted_iota(jnp.int32, (tq, tq), 1)
    _flash_sweep(nq, tq, (row_tab, k_tab), logits, values, key <= qry, scratch)

    def finish(row, _):
        o = jnp.concatenate(_flash_outputs(scratch[-1], row, HEAD_DIM), axis=0)
        o_ref[0, pl.ds(pl.multiple_of(row * tq, tq), tq), :] = o.T.astype(o_ref.dtype)
        return 0

    lax.fori_loop(0, nq, finish, 0)


def _fox_attn(qf, kf, vft, qb, kb, *, tq):
    b, s, w = qf.shape
    row_blk, vt_blk = _attn_specs(s)
    return _attn_call(functools.partial(_fox_attn_kernel, tq=tq), "fox_attn", 2 * BLOCKS_PER_STEP, HEAD_DIM,
                      (qf, kf, vft, qb, kb), [row_blk, row_blk, vt_blk, row_blk, row_blk], b, s, w, tq)


def _route(lt):
    tm = lt.shape[1]
    g8 = lt[0:8]
    r8 = lax.broadcasted_iota(jnp.int32, (8, tm), 0)
    g8 = jnp.where(r8 < N_GROUPS, g8, NEG_BIG)
    gmax = jnp.max(g8, axis=0, keepdims=True)
    gidx = jnp.min(jnp.where(g8 == gmax, r8, N_GROUPS), axis=0, keepdims=True)
    g_w = 1.0 / jnp.sum(jnp.exp(g8 - gmax), axis=0, keepdims=True)

    e16 = lt[8:8 + N_EXPERTS]
    r16 = lax.broadcasted_iota(jnp.int32, (N_EXPERTS, tm), 0)
    in_group = (r16 // EXPERTS_PER_GROUP) == gidx
    e_sel = jnp.where(in_group, e16, NEG_BIG)
    top1 = jnp.max(e_sel, axis=0, keepdims=True)
    id1 = jnp.min(jnp.where(e_sel == top1, r16, N_EXPERTS), axis=0, keepdims=True)
    e_rest = jnp.where(r16 == id1, NEG_BIG, e_sel)
    top2 = jnp.max(e_rest, axis=0, keepdims=True)
    id2 = jnp.min(jnp.where(e_rest == top2, r16, N_EXPERTS), axis=0, keepdims=True)
    t = jnp.exp(top2 - top1)
    w1 = g_w / (1.0 + t)
    w2 = w1 * t
    r128 = lax.broadcasted_iota(jnp.int32, (LANES, tm), 0)
    return (jnp.where(r128 == id1, w1, 0.0) + jnp.where(r128 == id2, w2, 0.0)
            + jnp.where(r128 == GID_LANE, gidx.astype(F32), 0.0))


def _out_proj_kernel(x_ref, od_ref, of_ref, wo_ref, g_ref, wr_cat_ref, br_ref,
                     x1_ref, hm_ref, comb_ref):
    tm = x_ref.shape[0]
    halves = [slice(k * tm // OUT_PROJ_SPLIT, (k + 1) * tm // OUT_PROJ_SPLIT) for k in range(OUT_PROJ_SPLIT)]
    x1s = []
    for rows in halves:
        x1 = x_ref[rows, :] + (_dot(od_ref[rows, :], wo_ref[0]) + _dot(of_ref[rows, :], wo_ref[1]))
        x1_ref[rows, :] = x1
        x1s.append(x1)
    wr_cat = wr_cat_ref[...]
    for rows, x1 in zip(halves, x1s):
        hb, h_lo = _split2(_rms(x1, g_ref[...]))
        hm_ref[rows, :] = hb
        both = _dot_nt(wr_cat, hb)
        lt = ((both[:LANES] + both[LANES:]) + _dot_nt(wr_cat[:LANES], h_lo)) + br_ref[...]
        comb_ref[rows, :] = _route(lt).T


def _out_proj(x2, od, of, wo, g, wr_cat, br, *, tm):
    n, d = x2.shape
    const = lambda *shape: pl.BlockSpec(shape, lambda i: (0,) * len(shape))
    row_blk = lambda cols: pl.BlockSpec((tm, cols), lambda i: (i, 0))
    return pl.pallas_call(
        _out_proj_kernel,
        grid=(n // tm,),
        in_specs=[row_blk(d), row_blk(od.shape[1]), row_blk(of.shape[1]), const(*wo.shape), const(1, d),
                  const(*wr_cat.shape), const(*br.shape)],
        out_specs=[row_blk(d), row_blk(d), row_blk(LANES)],
        out_shape=[jax.ShapeDtypeStruct((n, d), F32), jax.ShapeDtypeStruct((n, d), BF16),
                   jax.ShapeDtypeStruct((n, LANES), F32)],
        compiler_params=pltpu.CompilerParams(dimension_semantics=("arbitrary",),
                                             vmem_limit_bytes=VMEM_LIMIT),
        name="out_proj",
    )(x2, od, of, wo, g, wr_cat, br)


def _moe_kernel(hm_ref, comb_ref, x1_hbm, wg_ref, wu_ref, wd_ref, gf_ref, o_hbm,
                tri_ref, hs_ref, cs_ref, posc_ref, tab_ref, xbuf_ref, obuf_ref, sem_ref, *, apply_final):
    g = pl.program_id(1)
    k = pl.program_id(2)
    t = hm_ref.shape[0]
    t_pad = hs_ref.shape[1]
    rb = ROW_BLOCK

    @pl.when((pl.program_id(0) == 0) & (g == 0) & (k == 0))
    def _():
        row = lax.broadcasted_iota(jnp.int32, (t, t), 0)
        col = lax.broadcasted_iota(jnp.int32, (t, t), 1)
        tri_ref[...] = jnp.where(row > col, 1.0, 0.0).astype(BF16)

    @pl.when(g == 0)
    def _():
        comb = comb_ref[...]
        lane = lax.broadcasted_iota(jnp.int32, comb.shape, 1)
        onehot_c = jnp.where(lane == comb[:, GID_LANE:GID_LANE + 1].astype(jnp.int32), 1.0, 0.0)
        rank_c = _dot(tri_ref[...], onehot_c.astype(BF16))
        counts = jnp.sum(onehot_c, axis=0, keepdims=True)
        lane1 = lax.broadcasted_iota(jnp.int32, (1, LANES), 1)
        start_c = jnp.zeros((1, LANES), F32)
        first = jnp.int32(0)
        for grp in range(N_GROUPS):
            cnt = jnp.sum(jnp.where(lane1 == grp, counts, 0.0)).astype(jnp.int32)
            nblk = lax.shift_right_logical(cnt + (rb - 1), int(np.log2(rb)))
            tab_ref[k, grp] = first
            tab_ref[k, N_GROUPS + grp] = nblk
            start_c = jnp.where(lane1 == grp, (first * rb).astype(F32), start_c)
            first = first + nblk
        dest = onehot_c * (rank_c + start_c)
        posc_ref[k] = jnp.sum(dest, axis=1, keepdims=True).astype(jnp.int32)
        pos_r = jnp.sum(dest.T, axis=0, keepdims=True).astype(jnp.int32)
        c_parts = _split3(jnp.where(lane < N_EXPERTS, comb, 0.0))
        c_pack = (c_parts[0].astype(F32) + pltpu.roll(c_parts[1].astype(F32), N_EXPERTS, axis=1)
                  + pltpu.roll(c_parts[2].astype(F32), 2 * N_EXPERTS, axis=1)).astype(BF16)
        hm = hm_ref[...]
        for r0 in range(0, t_pad, SORT_ROWS):
            dst = lax.broadcasted_iota(jnp.int32, (SORT_ROWS, t), 0) + r0
            perm = jnp.where(dst == pos_r, 1.0, 0.0).astype(BF16)
            hs_ref[k, r0:r0 + SORT_ROWS, :] = _dot(perm, hm).astype(hs_ref.dtype)
            cp = _dot(perm, c_pack)
            cs_ref[k, r0:r0 + SORT_ROWS, :] = (cp + pltpu.roll(cp, LANES - N_EXPERTS, axis=1)
                                               + pltpu.roll(cp, LANES - 2 * N_EXPERTS, axis=1))

    b0 = tab_ref[k, g]

    def expert_block(b, _):
        off = pl.multiple_of(b * rb, rb)
        rows = hs_ref[k, pl.ds(off, rb), :]
        cblk = cs_ref[k, pl.ds(off, rb), :]
        lane = lax.broadcasted_iota(jnp.int32, cblk.shape, 1)
        gates = [_dot(rows, wg_ref[j]) for j in range(EXPERTS_PER_GROUP)]
        ups = [_dot(rows, wu_ref[j]) for j in range(EXPERTS_PER_GROUP)]
        y = None
        for j in range(EXPERTS_PER_GROUP):
            c = jnp.sum(jnp.where(lane == g * EXPERTS_PER_GROUP + j, cblk, 0.0), axis=1, keepdims=True)
            a = gates[j] * jax.nn.sigmoid(gates[j]) * ups[j] * c
            d = _dot(a.astype(BF16), wd_ref[j])
            y = d if y is None else y + d
        hs_ref[k, pl.ds(off, rb), :] = y.astype(hs_ref.dtype)
        return 0

    last_group = g == pl.num_programs(1) - 1
    tile = pl.program_id(0) * pl.num_programs(2) + k
    n_tiles = pl.num_programs(0) * pl.num_programs(2)

    def x1_copy():
        return pltpu.make_async_copy(x1_hbm.at[pl.ds(pl.multiple_of(tile * t, t), t), :], xbuf_ref, sem_ref.at[0])

    def out_copy(which):
        return pltpu.make_async_copy(obuf_ref, o_hbm.at[pl.ds(pl.multiple_of(which * t, t), t), :], sem_ref.at[1])

    @pl.when(last_group)
    def _():
        x1_copy().start()

    lax.fori_loop(b0, b0 + tab_ref[k, N_GROUPS + g], expert_block, 0)

    @pl.when(last_group)
    def _():
        x1_copy().wait()

        @pl.when(tile > 0)
        def _():
            out_copy(tile - 1).wait()

        ys = hs_ref[k]
        for r0 in range(0, t, SORT_ROWS):
            src = lax.broadcasted_iota(jnp.int32, (SORT_ROWS, t_pad), 1)
            unperm = jnp.where(src == posc_ref[k, r0:r0 + SORT_ROWS, :], 1.0, 0.0).astype(BF16)
            x2 = xbuf_ref[r0:r0 + SORT_ROWS, :] + _dot(unperm, ys)
            obuf_ref[r0:r0 + SORT_ROWS, :] = _rms(x2, gf_ref[...]) if apply_final else x2
        out_copy(tile).start()

        @pl.when(tile == n_tiles - 1)
        def _():
            out_copy(tile).wait()


def _moe(hm, comb, x1, wg, wu, wd, gfin, *, tm, apply_final):
    n, d = x1.shape
    ne, _, de = wg.shape
    epg = EXPERTS_PER_GROUP
    n_groups = ne // epg
    t_pad = tm + N_GROUPS * ROW_BLOCK
    assert t_pad % SORT_ROWS == 0 and tm % SORT_ROWS == 0 and n % (MOE_TILES * tm) == 0
    first_blk = lambda cols: pl.BlockSpec(
        (tm, cols), lambda p, g, k: (MOE_TILES * p + jnp.where(g == 0, k, MOE_TILES - 1), 0))
    w_blk = lambda *shape: pl.BlockSpec((epg,) + shape, lambda p, g, k: (g, 0, 0))
    hbm = pl.BlockSpec(memory_space=pl.ANY)
    return pl.pallas_call(
        functools.partial(_moe_kernel, apply_final=apply_final),
        grid=(n // (MOE_TILES * tm), n_groups, MOE_TILES),
        in_specs=[first_blk(d), first_blk(LANES), hbm, w_blk(d, de), w_blk(d, de), w_blk(de, d),
                  pl.BlockSpec((1, d), lambda p, g, k: (0, 0))],
        out_specs=hbm,
        out_shape=jax.ShapeDtypeStruct((n, d), F32),
        scratch_shapes=[pltpu.VMEM((tm, tm), BF16), pltpu.VMEM((MOE_TILES, t_pad, d), BF16),
                        pltpu.VMEM((MOE_TILES, t_pad, LANES), F32), pltpu.VMEM((MOE_TILES, tm, 1), jnp.int32),
                        pltpu.SMEM((MOE_TILES, 2 * N_GROUPS), jnp.int32),
                        pltpu.VMEM((tm, d), F32), pltpu.VMEM((tm, d), F32), pltpu.SemaphoreType.DMA((2,))],
        compiler_params=pltpu.CompilerParams(dimension_semantics=("arbitrary",) * 3,
                                             vmem_limit_bytes=VMEM_LIMIT),
        name="moe",
    )(hm, comb, x1, wg, wu, wd, gfin)


def kernel(x, norm_attn_g, w_in, b_forget, lambda_q1, lambda_k1, lambda_q2, lambda_k2, diff_norm_g, w_out,
           norm_ffn_g, router_group_w, router_group_b, router_expert_w, router_expert_b, w_gate, w_up, w_down,
           norm_final_g):
    b, s, d = x.shape
    depth = w_in.shape[0]
    n = b * s
    diff_w = N_DIFF_HEADS * 2 * HEAD_DIM
    fox_w = N_FOX_HEADS * HEAD_DIM

    inv_freq = 1.0 / (ROPE_THETA ** (np.arange(0, HEAD_DIM, 2, dtype=np.float64) / HEAD_DIM))
    ang = np.tile(np.arange(s, dtype=np.float64)[:, None] * inv_freq[None, :], (1, 2 * LANES // HEAD_DIM))
    first_half = (np.arange(LANES) % HEAD_DIM) < HEAD_DIM // 2
    cos2 = jnp.asarray(np.cos(ang), F32)
    sin2 = jnp.asarray(np.where(first_half, -np.sin(ang), np.sin(ang)), F32)

    x2 = x.reshape(n, d)
    for l in range(depth):
        lam_init = 0.8 - 0.6 * float(np.exp(-0.3 * l))
        w = w_in[l]
        offs = np.cumsum([0, diff_w, diff_w, diff_w, fox_w, fox_w, fox_w, N_FOX_HEADS])
        seg = [w[:, offs[k]:offs[k + 1]] for k in range(7)]
        w_stack = jnp.stack([seg[0], seg[1], seg[3], seg[4]]).astype(BF16)
        wt_stack = jnp.stack([seg[2].T, seg[5].T]).astype(BF16)
        wf = jnp.pad(seg[6], ((0, 0), (0, LANES - N_FOX_HEADS)))
        wf_cat = jnp.concatenate(_split2(wf), axis=1)
        bf_pad = jnp.pad(b_forget[l], (0, LANES - N_FOX_HEADS)).reshape(1, LANES)

        qd, kd, vdt, qf, kf, vft, qb, kb = _in_proj(
            x2, norm_attn_g[l].reshape(1, d), cos2, sin2, w_stack, wt_stack, wf_cat, bf_pad,
            batch=b, seq=s, tm=IN_PROJ_ROWS)

        to3 = lambda t: t.reshape(b, s, t.shape[-1])
        lam_params = jnp.stack([lambda_q1[l], lambda_k1[l], lambda_q2[l], lambda_k2[l]])
        od = _diff_attn(to3(qd), to3(kd), vdt, lam_params, diff_norm_g[l].reshape(-1, 1),
                        tq=ATTN_TILE, lam_init=lam_init)
        of = _fox_attn(to3(qf), to3(kf), vft, to3(qb), to3(kb), tq=ATTN_TILE)

        wo = w_out[l].astype(BF16).reshape(2, -1, d)
        wr = jnp.zeros((d, LANES), F32)
        wr = wr.at[:, :N_GROUPS].set(router_group_w[l]).at[:, 8:8 + N_EXPERTS].set(router_expert_w[l])
        wr_cat = jnp.concatenate(_split2(wr.T), axis=0)
        br = jnp.zeros((LANES,), F32)
        br = br.at[:N_GROUPS].set(router_group_b[l]).at[8:8 + N_EXPERTS].set(router_expert_b[l]).reshape(LANES, 1)
        x1, hm, comb = _out_proj(x2, od.reshape(n, -1), of.reshape(n, -1), wo, norm_ffn_g[l].reshape(1, d),
                                 wr_cat, br, tm=OUT_PROJ_ROWS)

        x2 = _moe(hm, comb, x1, w_gate[l].astype(BF16), w_up[l].astype(BF16), w_down[l].astype(BF16),
                  norm_final_g.reshape(1, d), tm=MOE_ROWS, apply_final=(l == depth - 1))
    return x2.reshape(b, s, d)
```

```python
import functools

import numpy as np
import jax
import jax.numpy as jnp
from jax import lax
from jax.experimental import pallas as pl
from jax.experimental.pallas import tpu as pltpu

CHUNK = 64
HEAD_DIM = 64
ROPE_THETA = 10000.0
EPS = 1e-6
LOG2E = 1.4426950408889634
N_DIFF_HEADS = 4
N_FOX_HEADS = 8
N_GROUPS = 4
EXPERTS_PER_GROUP = 4
N_EXPERTS = N_GROUPS * EXPERTS_PER_GROUP

LANES = 128
NEG_BIG = -1e30
VMEM_LIMIT = 56 * 1024 * 1024
IN_PROJ_ROWS = 512
ATTN_TILE = 512
OUT_PROJ_ROWS = 1024
MOE_ROWS = 1024
N_BIAS = 6
SUBLANES = 8
GID_LANE = N_EXPERTS
ROW_BLOCK = 128
ONES_ROWS = 16
BLOCKS_PER_STEP = 2
MOE_TILES = 2
OUT_PROJ_SPLIT = 2
SORT_ROWS = 512

BF16 = jnp.bfloat16
F32 = jnp.float32
_NT = (((1,), (1,)), ((), ()))


def _dot(a, b):
    return jnp.dot(a, b, preferred_element_type=F32)


def _dot_nt(a, b):
    return lax.dot_general(a, b, _NT, preferred_element_type=F32)


def _split2(x):
    hi = x.astype(BF16)
    lo = (x - hi.astype(F32)).astype(BF16)
    return hi, lo


def _split3(x):
    hi = x.astype(BF16)
    r = x - hi.astype(F32)
    mid = r.astype(BF16)
    lo = (r - mid.astype(F32)).astype(BF16)
    return hi, mid, lo


def _rms(x, g):
    return x * lax.rsqrt(jnp.mean(x * x, axis=-1, keepdims=True) + EPS) * g


def _pack_parts(parts):
    packed = parts[0].astype(F32)
    for k in (1, 2):
        packed = packed + pltpu.roll(parts[k].astype(F32), k * N_FOX_HEADS, axis=1)
    return packed.astype(BF16)


def _in_proj_kernel(x_ref, g_ref, cos_ref, sin_ref, w_ref, wt_ref, wf_cat_ref, bf_ref, tri_ref,
                    eq_ref, ek_ref, cq_ref, ck_ref,
                    qd_ref, kd_ref, vdt_ref, qf_ref, kf_ref, vft_ref, qb_ref, kb_ref, carry_ref,
                    *, tiles_per_seq):
    i = pl.program_id(0)
    h = _rms(x_ref[...], g_ref[...])
    hb = h.astype(BF16)
    cos = cos_ref[...]
    sin = sin_ref[...]
    scale = HEAD_DIM ** -0.5 * LOG2E
    lane = lax.broadcasted_iota(jnp.int32, cos.shape, 1)
    first_half = (lane % HEAD_DIM) < HEAD_DIM // 2

    def rope(w_idx, out_ref, mul):
        a = _dot(hb, w_ref[w_idx])
        for c in range(a.shape[1] // LANES):
            sl = slice(c * LANES, (c + 1) * LANES)
            x = a[:, sl]
            rot = jnp.where(first_half, pltpu.roll(x, LANES - HEAD_DIM // 2, axis=1),
                            pltpu.roll(x, HEAD_DIM // 2, axis=1))
            out_ref[:, sl] = ((x * cos + rot * sin) * mul).astype(out_ref.dtype)

    @pl.when(i % tiles_per_seq == 0)
    def _():
        carry_ref[...] = jnp.zeros_like(carry_ref)


    zz = _dot(hb, wf_cat_ref[...])
    z = (zz[:, :LANES] + zz[:, LANES:]) + bf_ref[...]
    log_f = jnp.minimum(z, 0.0) - jnp.log1p(jnp.exp(-jnp.abs(z)))
    valid = lane < N_FOX_HEADS
    log_f = jnp.where(valid, log_f, 0.0)

    rope(0, qd_ref, scale)

    r = _dot(tri_ref[...], _pack_parts(_split3(log_f)))
    cum = (r + pltpu.roll(r, LANES - N_FOX_HEADS, axis=1)) + pltpu.roll(r, LANES - 2 * N_FOX_HEADS, axis=1)
    cum = jnp.where(valid, cum + carry_ref[...], 0.0)
    carry_ref[...] = cum[cum.shape[0] - 1:, :]

    rope(1, kd_ref, 1.0)

    f_pack = _pack_parts(_split3(cum * LOG2E))
    qb_ref[...] = (cq_ref[...] + _dot(f_pack, eq_ref[...])).astype(qb_ref.dtype)
    kb_ref[...] = (ck_ref[...] + _dot(f_pack, ek_ref[...])).astype(kb_ref.dtype)

    qf_ref[...] = (_dot(hb, w_ref[2]) * scale).astype(qf_ref.dtype)
    kf_ref[...] = _dot(hb, w_ref[3]).astype(kf_ref.dtype)
    vdt_ref[0] = _dot_nt(wt_ref[0], hb).astype(vdt_ref.dtype)
    vft_ref[0] = _dot_nt(wt_ref[1], hb).astype(vft_ref.dtype)


def _bias_placement():
    width = (N_FOX_HEADS // 2) * LANES
    eq = np.zeros((LANES, width), np.float32)
    ek = np.zeros((LANES, width), np.float32)
    cq = np.zeros((1, width), np.float32)
    ck = np.zeros((1, width), np.float32)
    for h in range(N_FOX_HEADS):
        base = (h // 2) * LANES + (h % 2) * N_BIAS
        for part in range(3):
            eq[part * N_FOX_HEADS + h, base + part] = 1.0
            ek[part * N_FOX_HEADS + h, base + 3 + part] = -1.0
        cq[0, base + 3:base + 6] = 1.0
        ck[0, base:base + 3] = 1.0
    return jnp.asarray(eq, BF16), jnp.asarray(ek, BF16), jnp.asarray(cq), jnp.asarray(ck)


def _in_proj(x2, g, cos2, sin2, w_stack, wt_stack, wf_cat, bf_pad, *, batch, seq, tm):
    n, d = x2.shape
    tiles_per_seq = seq // tm
    tri = jnp.tril(jnp.ones((tm, tm), F32)).astype(BF16)
    eq, ek, cq, ck = _bias_placement()
    wcols = w_stack.shape[2]
    const = lambda *shape: pl.BlockSpec(shape, lambda i: (0,) * len(shape))
    row_blk = lambda cols: pl.BlockSpec((tm, cols), lambda i: (i, 0))
    pos_blk = pl.BlockSpec((tm, LANES), lambda i: (i % tiles_per_seq, 0))
    vt_blk = pl.BlockSpec((1, wcols, tm), lambda i: (i // tiles_per_seq, 0, i % tiles_per_seq))
    out_bf = jax.ShapeDtypeStruct((n, wcols), BF16)
    out_vt = jax.ShapeDtypeStruct((batch, wcols, seq), BF16)
    return pl.pallas_call(
        functools.partial(_in_proj_kernel, tiles_per_seq=tiles_per_seq),
        grid=(n // tm,),
        in_specs=[row_blk(d), const(1, d), pos_blk, pos_blk, const(*w_stack.shape), const(*wt_stack.shape),
                  const(*wf_cat.shape), const(1, LANES), const(tm, tm),
                  const(*eq.shape), const(*ek.shape), const(*cq.shape), const(*ck.shape)],
        out_specs=[row_blk(wcols), row_blk(wcols), vt_blk, row_blk(wcols), row_blk(wcols), vt_blk,
                   row_blk(eq.shape[1]), row_blk(ek.shape[1])],
        out_shape=[out_bf, out_bf, out_vt, out_bf, out_bf, out_vt,
                   jax.ShapeDtypeStruct((n, eq.shape[1]), BF16), jax.ShapeDtypeStruct((n, ek.shape[1]), BF16)],
        scratch_shapes=[pltpu.VMEM((1, LANES), F32)],
        compiler_params=pltpu.CompilerParams(dimension_semantics=("arbitrary",),
                                             vmem_limit_bytes=VMEM_LIMIT),
        name="in_proj",
    )(x2, g, cos2, sin2, w_stack, wt_stack, wf_cat, bf_pad, tri, eq, ek, cq, ck)


def _flash_sweep(nq, tq, tabs, logits_fn, values_fn, mask, scratch):
    s0, s1, c0, c1, m_ref, acc_ref = scratch
    row_tab, k_tab = tabs
    n_maps = m_ref.shape[1]
    n_off = nq * (nq - 1) // 2
    ones = jnp.ones((ONES_ROWS, tq), BF16)

    def stage_a(row, tile, s_buf, c_buf, masked):
        sts = logits_fn(row, tile, masked)
        for a in range(n_maps):
            st = jnp.where(mask, sts[a], NEG_BIG) if masked else sts[a]
            s_buf[a] = st
            c_buf[a] = jnp.max(st, axis=0, keepdims=True)

    def stage_b(row, tile, s_buf, c_buf, first):
        vts = values_fn(tile)
        for a in range(n_maps):
            vt = jnp.concatenate([vts[a], ones], axis=0)
            if first:
                m_new = c_buf[a]
                half = tq // 2
                p_left = jnp.exp2(s_buf[a, :half, :half] - m_new[:, :half]).astype(BF16)
                p_right = jnp.exp2(s_buf[a, :, half:] - m_new[:, half:]).astype(BF16)
                acc_ref[row, a] = jnp.concatenate([_dot(vt[:, :half], p_left), _dot(vt, p_right)], axis=1)
            else:
                m = m_ref[row, a]
                m_new = jnp.maximum(m, c_buf[a])
                alpha = jnp.exp2(m - m_new)
                acc_ref[row, a] = alpha * acc_ref[row, a] + _dot(vt, jnp.exp2(s_buf[a] - m_new).astype(BF16))
            m_ref[row, a] = m_new

    stage_a(0, 0, s0, c0, True)

    def diag_pair(jj, _):
        r = 2 * jj
        stage_a(r + 1, r + 1, s1, c1, True)
        stage_b(r, r, s0, c0, True)
        stage_a(r + 2, r + 2, s0, c0, True)
        stage_b(r + 1, r + 1, s1, c1, True)
        return 0

    lax.fori_loop(0, (nq - 2) // 2, diag_pair, 0)
    stage_a(nq - 1, nq - 1, s1, c1, True)
    stage_b(nq - 2, nq - 2, s0, c0, True)

    stage_a(row_tab[0], k_tab[0], s0, c0, False)
    stage_b(nq - 1, nq - 1, s1, c1, True)

    def off_pair(jj, _):
        t = 2 * jj
        stage_a(row_tab[t + 1], k_tab[t + 1], s1, c1, False)
        stage_b(row_tab[t], k_tab[t], s0, c0, False)
        stage_a(row_tab[t + 2], k_tab[t + 2], s0, c0, False)
        stage_b(row_tab[t + 1], k_tab[t + 1], s1, c1, False)
        return 0

    lax.fori_loop(0, (n_off - 2) // 2, off_pair, 0)
    stage_a(row_tab[n_off - 1], k_tab[n_off - 1], s1, c1, False)
    stage_b(row_tab[n_off - 2], k_tab[n_off - 2], s0, c0, False)
    stage_b(row_tab[n_off - 1], k_tab[n_off - 1], s1, c1, False)


def _flash_scratch(nq, n_maps, dv, tq):
    s_buf = pltpu.VMEM((n_maps, tq, tq), F32)
    c_buf = pltpu.VMEM((n_maps, 1, tq), F32)
    return [s_buf, s_buf, c_buf, c_buf, pltpu.VMEM((nq, n_maps, 1, tq), F32),
            pltpu.VMEM((nq, n_maps, dv + ONES_ROWS, tq), F32)]


def _off_diagonal_tables(nq):
    pairs = [(r, k) for r in range(nq) for k in range(r)]
    return (jnp.asarray([p[0] for p in pairs], jnp.int32), jnp.asarray([p[1] for p in pairs], jnp.int32))


def _flash_outputs(acc_ref, row, dv):
    return [acc_ref[row, a, :dv, :] / acc_ref[row, a, dv:dv + 1, :] for a in range(acc_ref.shape[1])]


def _head_blocks(block):
    return [block[:, k * LANES:(k + 1) * LANES] for k in range(BLOCKS_PER_STEP)]


def _tile_logits(k, q, diagonal):
    if not diagonal:
        return _dot_nt(k, q)
    half = q.shape[0] // 2
    assert half % CHUNK == 0
    left = jnp.concatenate([_dot_nt(k[:half], q[:half]), jnp.full((k.shape[0] - half, half), NEG_BIG, F32)],
                           axis=0)
    return jnp.concatenate([left, _dot_nt(k, q[half:])], axis=1)


def _tile_rows(ref, tile, tq):
    return ref[0, pl.ds(pl.multiple_of(tile * tq, tq), tq), :]


def _diff_attn_kernel(row_tab, k_tab, q_ref, k_ref, vt_ref, lam_ref, g_ref, o_ref, *scratch, tq, lam_init):
    nq = q_ref.shape[1] // tq

    def logits(row, tile, diagonal):
        qs = []
        for q in _head_blocks(_tile_rows(q_ref, row, tq)):
            lane = lax.broadcasted_iota(jnp.int32, q.shape, 1)
            zero = jnp.zeros_like(q)
            qs += [jnp.where(lane < HEAD_DIM, q, zero), jnp.where(lane >= HEAD_DIM, q, zero)]
        kts = _head_blocks(_tile_rows(k_ref, tile, tq))
        return [_tile_logits(kts[a // 2], qs[a], diagonal) for a in range(len(qs))]

    def values(tile):
        vt = vt_ref[0, :, pl.ds(pl.multiple_of(tile * tq, tq), tq)]
        return [vt[(a // 2) * LANES:(a // 2 + 1) * LANES] for a in range(2 * BLOCKS_PER_STEP)]

    key = lax.broadcasted_iota(jnp.int32, (tq, tq), 0)
    qry = lax.broadcasted_iota(jnp.int32, (tq, tq), 1)
    _flash_sweep(nq, tq, (row_tab, k_tab), logits, values, (key // CHUNK) <= (qry // CHUNK), scratch)

    lp = lam_ref[...]
    lam = (jnp.exp(jnp.sum(lp[0:1] * lp[1:2], axis=1, keepdims=True))
           - jnp.exp(jnp.sum(lp[2:3] * lp[3:4], axis=1, keepdims=True)) + lam_init)

    def finish(row, _):
        outs = _flash_outputs(scratch[-1], row, LANES)
        ys = []
        for k in range(BLOCKS_PER_STEP):
            o = outs[2 * k] - lam * outs[2 * k + 1]
            y = o * lax.rsqrt(jnp.mean(o * o, axis=0, keepdims=True) + EPS) * g_ref[...] * (1.0 - lam_init)
            ys.append(y.T.astype(o_ref.dtype))
        o_ref[0, pl.ds(pl.multiple_of(row * tq, tq), tq), :] = jnp.concatenate(ys, axis=1)
        return 0

    lax.fori_loop(0, nq, finish, 0)


def _attn_call(kernel_fn, name, n_maps, dv, inputs, in_specs, b, s, w, tq):
    nq = s // tq
    assert nq % 2 == 0 and nq >= 4, "the pipelined sweeps handle tiles in pairs"
    wb = BLOCKS_PER_STEP * LANES
    return pl.pallas_call(
        kernel_fn,
        grid_spec=pltpu.PrefetchScalarGridSpec(
            num_scalar_prefetch=2,
            grid=(b, w // wb),
            in_specs=in_specs,
            out_specs=pl.BlockSpec((1, s, wb), lambda bi, h, *_: (bi, 0, h)),
            scratch_shapes=_flash_scratch(nq, n_maps, dv, tq)),
        out_shape=jax.ShapeDtypeStruct((b, s, w), BF16),
        compiler_params=pltpu.CompilerParams(dimension_semantics=("arbitrary",) * 2,
                                             vmem_limit_bytes=VMEM_LIMIT),
        name=name,
    )(*_off_diagonal_tables(nq), *inputs)


def _attn_specs(s):
    wb = BLOCKS_PER_STEP * LANES
    row_blk = pl.BlockSpec((1, s, wb), lambda bi, h, *_: (bi, 0, h))
    vt_blk = pl.BlockSpec((1, wb, s), lambda bi, h, *_: (bi, h, 0))
    return row_blk, vt_blk


def _diff_attn(qd, kd, vdt, lam_params, gnorm_col, *, tq, lam_init):
    b, s, w = qd.shape
    row_blk, vt_blk = _attn_specs(s)
    small = lambda arr: pl.BlockSpec(arr.shape, lambda bi, h, *_: (0, 0))
    return _attn_call(functools.partial(_diff_attn_kernel, tq=tq, lam_init=lam_init), "diff_attn",
                      2 * BLOCKS_PER_STEP, LANES, (qd, kd, vdt, lam_params, gnorm_col),
                      [row_blk, row_blk, vt_blk, small(lam_params), small(gnorm_col)], b, s, w, tq)


def _fox_attn_kernel(row_tab, k_tab, q_ref, k_ref, vt_ref, qb_ref, kb_ref, o_ref, *scratch, tq):
    nq = q_ref.shape[1] // tq

    def logits(row, tile, diagonal):
        qs = []
        for q, qb in zip(_head_blocks(_tile_rows(q_ref, row, tq)), _head_blocks(_tile_rows(qb_ref, row, tq))):
            lane = lax.broadcasted_iota(jnp.int32, q.shape, 1)
            zero = jnp.zeros_like(q)
            qs += [jnp.concatenate([jnp.where(lane < HEAD_DIM, q, zero), jnp.where(lane < N_BIAS, qb, zero)],
                                   axis=1),
                   jnp.concatenate([jnp.where(lane >= HEAD_DIM, q, zero),
                                    jnp.where((lane >= N_BIAS) & (lane < 2 * N_BIAS), qb, zero)], axis=1)]
        kts = [jnp.concatenate([k, kb], axis=1) for k, kb in zip(_head_blocks(_tile_rows(k_ref, tile, tq)),
                                                                   _head_blocks(_tile_rows(kb_ref, tile, tq)))]
        return [_tile_logits(kts[a // 2], qs[a], diagonal) for a in range(len(qs))]

    def values(tile):
        vt = vt_ref[0, :, pl.ds(pl.multiple_of(tile * tq, tq), tq)]
        return [vt[a * HEAD_DIM:(a + 1) * HEAD_DIM] for a in range(2 * BLOCKS_PER_STEP)]

    key = lax.broadcasted_iota(jnp.int32, (tq, tq), 0)
    qry = lax.broadcasted_iota(jnp.int32, (tq, tq), 1)
    _flash_sweep(nq, tq, (row_tab, k_tab), logits, values, key <= qry, scratch)

    def finish(row, _):
        o = jnp.concatenate(_flash_outputs(scratch[-1], row, HEAD_DIM), axis=0)
        o_ref[0, pl.ds(pl.multiple_of(row * tq, tq), tq), :] = o.T.astype(o_ref.dtype)
        return 0

    lax.fori_loop(0, nq, finish, 0)


def _fox_attn(qf, kf, vft, qb, kb, *, tq):
    b, s, w = qf.shape
    row_blk, vt_blk = _attn_specs(s)
    return _attn_call(functools.partial(_fox_attn_kernel, tq=tq), "fox_attn", 2 * BLOCKS_PER_STEP, HEAD_DIM,
                      (qf, kf, vft, qb, kb), [row_blk, row_blk, vt_blk, row_blk, row_blk], b, s, w, tq)


def _route(lt):
    tm = lt.shape[1]
    g8 = lt[0:8]
    r8 = lax.broadcasted_iota(jnp.int32, (8, tm), 0)
    g8 = jnp.where(r8 < N_GROUPS, g8, NEG_BIG)
    gmax = jnp.max(g8, axis=0, keepdims=True)
    gidx = jnp.min(jnp.where(g8 == gmax, r8, N_GROUPS), axis=0, keepdims=True)
    g_w = 1.0 / jnp.sum(jnp.exp(g8 - gmax), axis=0, keepdims=True)

    e16 = lt[8:8 + N_EXPERTS]
    r16 = lax.broadcasted_iota(jnp.int32, (N_EXPERTS, tm), 0)
    in_group = (r16 // EXPERTS_PER_GROUP) == gidx
    e_sel = jnp.where(in_group, e16, NEG_BIG)
    top1 = jnp.max(e_sel, axis=0, keepdims=True)
    id1 = jnp.min(jnp.where(e_sel == top1, r16, N_EXPERTS), axis=0, keepdims=True)
    e_rest = jnp.where(r16 == id1, NEG_BIG, e_sel)
    top2 = jnp.max(e_rest, axis=0, keepdims=True)
    id2 = jnp.min(jnp.where(e_rest == top2, r16, N_EXPERTS), axis=0, keepdims=True)
    t = jnp.exp(top2 - top1)
    w1 = g_w / (1.0 + t)
    w2 = w1 * t
    r128 = lax.broadcasted_iota(jnp.int32, (LANES, tm), 0)
    return (jnp.where(r128 == id1, w1, 0.0) + jnp.where(r128 == id2, w2, 0.0)
            + jnp.where(r128 == GID_LANE, gidx.astype(F32), 0.0))


def _out_proj_kernel(x_ref, od_ref, of_ref, wo_ref, g_ref, wr_cat_ref, br_ref,
                     x1_ref, hm_ref, comb_ref):
    tm = x_ref.shape[0]
    halves = [slice(k * tm // OUT_PROJ_SPLIT, (k + 1) * tm // OUT_PROJ_SPLIT) for k in range(OUT_PROJ_SPLIT)]
    x1s = []
    for rows in halves:
        x1 = x_ref[rows, :] + (_dot(od_ref[rows, :], wo_ref[0]) + _dot(of_ref[rows, :], wo_ref[1]))
        x1_ref[rows, :] = x1
        x1s.append(x1)
    wr_cat = wr_cat_ref[...]
    for rows, x1 in zip(halves, x1s):
        hb, h_lo = _split2(_rms(x1, g_ref[...]))
        hm_ref[rows, :] = hb
        both = _dot_nt(wr_cat, hb)
        lt = ((both[:LANES] + both[LANES:]) + _dot_nt(wr_cat[:LANES], h_lo)) + br_ref[...]
        comb_ref[rows, :] = _route(lt).T


def _out_proj(x2, od, of, wo, g, wr_cat, br, *, tm):
    n, d = x2.shape
    const = lambda *shape: pl.BlockSpec(shape, lambda i: (0,) * len(shape))
    row_blk = lambda cols: pl.BlockSpec((tm, cols), lambda i: (i, 0))
    return pl.pallas_call(
        _out_proj_kernel,
        grid=(n // tm,),
        in_specs=[row_blk(d), row_blk(od.shape[1]), row_blk(of.shape[1]), const(*wo.shape), const(1, d),
                  const(*wr_cat.shape), const(*br.shape)],
        out_specs=[row_blk(d), row_blk(d), row_blk(LANES)],
        out_shape=[jax.ShapeDtypeStruct((n, d), F32), jax.ShapeDtypeStruct((n, d), BF16),
                   jax.ShapeDtypeStruct((n, LANES), F32)],
        compiler_params=pltpu.CompilerParams(dimension_semantics=("arbitrary",),
                                             vmem_limit_bytes=VMEM_LIMIT),
        name="out_proj",
    )(x2, od, of, wo, g, wr_cat, br)


def _moe_kernel(hm_ref, comb_ref, x1_hbm, wg_ref, wu_ref, wd_ref, gf_ref, o_hbm,
                tri_ref, hs_ref, cs_ref, posc_ref, tab_ref, xbuf_ref, obuf_ref, sem_ref, *, apply_final):
    g = pl.program_id(1)
    k = pl.program_id(2)
    t = hm_ref.shape[0]
    t_pad = hs_ref.shape[1]
    rb = ROW_BLOCK

    @pl.when((pl.program_id(0) == 0) & (g == 0) & (k == 0))
    def _():
        row = lax.broadcasted_iota(jnp.int32, (t, t), 0)
        col = lax.broadcasted_iota(jnp.int32, (t, t), 1)
        tri_ref[...] = jnp.where(row > col, 1.0, 0.0).astype(BF16)

    @pl.when(g == 0)
    def _():
        comb = comb_ref[...]
        lane = lax.broadcasted_iota(jnp.int32, comb.shape, 1)
        onehot_c = jnp.where(lane == comb[:, GID_LANE:GID_LANE + 1].astype(jnp.int32), 1.0, 0.0)
        rank_c = _dot(tri_ref[...], onehot_c.astype(BF16))
        counts = jnp.sum(onehot_c, axis=0, keepdims=True)
        lane1 = lax.broadcasted_iota(jnp.int32, (1, LANES), 1)
        start_c = jnp.zeros((1, LANES), F32)
        first = jnp.int32(0)
        for grp in range(N_GROUPS):
            cnt = jnp.sum(jnp.where(lane1 == grp, counts, 0.0)).astype(jnp.int32)
            nblk = lax.shift_right_logical(cnt + (rb - 1), int(np.log2(rb)))
            tab_ref[k, grp] = first
            tab_ref[k, N_GROUPS + grp] = nblk
            start_c = jnp.where(lane1 == grp, (first * rb).astype(F32), start_c)
            first = first + nblk
        dest = onehot_c * (rank_c + start_c)
        posc_ref[k] = jnp.sum(dest, axis=1, keepdims=True).astype(jnp.int32)
        pos_r = jnp.sum(dest.T, axis=0, keepdims=True).astype(jnp.int32)
        c_parts = _split3(jnp.where(lane < N_EXPERTS, comb, 0.0))
        c_pack = (c_parts[0].astype(F32) + pltpu.roll(c_parts[1].astype(F32), N_EXPERTS, axis=1)
                  + pltpu.roll(c_parts[2].astype(F32), 2 * N_EXPERTS, axis=1)).astype(BF16)
        hm = hm_ref[...]
        for r0 in range(0, t_pad, SORT_ROWS):
            dst = lax.broadcasted_iota(jnp.int32, (SORT_ROWS, t), 0) + r0
            perm = jnp.where(dst == pos_r, 1.0, 0.0).astype(BF16)
            hs_ref[k, r0:r0 + SORT_ROWS, :] = _dot(perm, hm).astype(hs_ref.dtype)
            cp = _dot(perm, c_pack)
            cs_ref[k, r0:r0 + SORT_ROWS, :] = (cp + pltpu.roll(cp, LANES - N_EXPERTS, axis=1)
                                               + pltpu.roll(cp, LANES - 2 * N_EXPERTS, axis=1))

    b0 = tab_ref[k, g]

    def expert_block(b, _):
        off = pl.multiple_of(b * rb, rb)
        rows = hs_ref[k, pl.ds(off, rb), :]
        cblk = cs_ref[k, pl.ds(off, rb), :]
        lane = lax.broadcasted_iota(jnp.int32, cblk.shape, 1)
        gates = [_dot(rows, wg_ref[j]) for j in range(EXPERTS_PER_GROUP)]
        ups = [_dot(rows, wu_ref[j]) for j in range(EXPERTS_PER_GROUP)]
        y = None
        for j in range(EXPERTS_PER_GROUP):
            c = jnp.sum(jnp.where(lane == g * EXPERTS_PER_GROUP + j, cblk, 0.0), axis=1, keepdims=True)
            a = gates[j] * jax.nn.sigmoid(gates[j]) * ups[j] * c
            d = _dot(a.astype(BF16), wd_ref[j])
            y = d if y is None else y + d
        hs_ref[k, pl.ds(off, rb), :] = y.astype(hs_ref.dtype)
        return 0

    last_group = g == pl.num_programs(1) - 1
    tile = pl.program_id(0) * pl.num_programs(2) + k
    n_tiles = pl.num_programs(0) * pl.num_programs(2)

    def x1_copy():
        return pltpu.make_async_copy(x1_hbm.at[pl.ds(pl.multiple_of(tile * t, t), t), :], xbuf_ref, sem_ref.at[0])

    def out_copy(which):
        return pltpu.make_async_copy(obuf_ref, o_hbm.at[pl.ds(pl.multiple_of(which * t, t), t), :], sem_ref.at[1])

    @pl.when(last_group)
    def _():
        x1_copy().start()

    lax.fori_loop(b0, b0 + tab_ref[k, N_GROUPS + g], expert_block, 0)

    @pl.when(last_group)
    def _():
        x1_copy().wait()

        @pl.when(tile > 0)
        def _():
            out_copy(tile - 1).wait()

        ys = hs_ref[k]
        for r0 in range(0, t, SORT_ROWS):
            src = lax.broadcasted_iota(jnp.int32, (SORT_ROWS, t_pad), 1)
            unperm = jnp.where(src == posc_ref[k, r0:r0 + SORT_ROWS, :], 1.0, 0.0).astype(BF16)
            x2 = xbuf_ref[r0:r0 + SORT_ROWS, :] + _dot(unperm, ys)
            obuf_ref[r0:r0 + SORT_ROWS, :] = _rms(x2, gf_ref[...]) if apply_final else x2
        out_copy(tile).start()

        @pl.when(tile == n_tiles - 1)
        def _():
            out_copy(tile).wait()


def _moe(hm, comb, x1, wg, wu, wd, gfin, *, tm, apply_final):
    n, d = x1.shape
    ne, _, de = wg.shape
    epg = EXPERTS_PER_GROUP
    n_groups = ne // epg
    t_pad = tm + N_GROUPS * ROW_BLOCK
    assert t_pad % SORT_ROWS == 0 and tm % SORT_ROWS == 0 and n % (MOE_TILES * tm) == 0
    first_blk = lambda cols: pl.BlockSpec(
        (tm, cols), lambda p, g, k: (MOE_TILES * p + jnp.where(g == 0, k, MOE_TILES - 1), 0))
    w_blk = lambda *shape: pl.BlockSpec((epg,) + shape, lambda p, g, k: (g, 0, 0))
    hbm = pl.BlockSpec(memory_space=pl.ANY)
    return pl.pallas_call(
        functools.partial(_moe_kernel, apply_final=apply_final),
        grid=(n // (MOE_TILES * tm), n_groups, MOE_TILES),
        in_specs=[first_blk(d), first_blk(LANES), hbm, w_blk(d, de), w_blk(d, de), w_blk(de, d),
                  pl.BlockSpec((1, d), lambda p, g, k: (0, 0))],
        out_specs=hbm,
        out_shape=jax.ShapeDtypeStruct((n, d), F32),
        scratch_shapes=[pltpu.VMEM((tm, tm), BF16), pltpu.VMEM((MOE_TILES, t_pad, d), BF16),
                        pltpu.VMEM((MOE_TILES, t_pad, LANES), F32), pltpu.VMEM((MOE_TILES, tm, 1), jnp.int32),
                        pltpu.SMEM((MOE_TILES, 2 * N_GROUPS), jnp.int32),
                        pltpu.VMEM((tm, d), F32), pltpu.VMEM((tm, d), F32), pltpu.SemaphoreType.DMA((2,))],
        compiler_params=pltpu.CompilerParams(dimension_semantics=("arbitrary",) * 3,
                                             vmem_limit_bytes=VMEM_LIMIT),
        name="moe",
    )(hm, comb, x1, wg, wu, wd, gfin)


def kernel(x, norm_attn_g, w_in, b_forget, lambda_q1, lambda_k1, lambda_q2, lambda_k2, diff_norm_g, w_out,
           norm_ffn_g, router_group_w, router_group_b, router_expert_w, router_expert_b, w_gate, w_up, w_down,
           norm_final_g):
    b, s, d = x.shape
    depth = w_in.shape[0]
    n = b * s
    diff_w = N_DIFF_HEADS * 2 * HEAD_DIM
    fox_w = N_FOX_HEADS * HEAD_DIM

    inv_freq = 1.0 / (ROPE_THETA ** (np.arange(0, HEAD_DIM, 2, dtype=np.float64) / HEAD_DIM))
    ang = np.tile(np.arange(s, dtype=np.float64)[:, None] * inv_freq[None, :], (1, 2 * LANES // HEAD_DIM))
    first_half = (np.arange(LANES) % HEAD_DIM) < HEAD_DIM // 2
    cos2 = jnp.asarray(np.cos(ang), F32)
    sin2 = jnp.asarray(np.where(first_half, -np.sin(ang), np.sin(ang)), F32)

    x2 = x.reshape(n, d)
    for l in range(depth):
        lam_init = 0.8 - 0.6 * float(np.exp(-0.3 * l))
        w = w_in[l]
        offs = np.cumsum([0, diff_w, diff_w, diff_w, fox_w, fox_w, fox_w, N_FOX_HEADS])
        seg = [w[:, offs[k]:offs[k + 1]] for k in range(7)]
        w_stack = jnp.stack([seg[0], seg[1], seg[3], seg[4]]).astype(BF16)
        wt_stack = jnp.stack([seg[2].T, seg[5].T]).astype(BF16)
        wf = jnp.pad(seg[6], ((0, 0), (0, LANES - N_FOX_HEADS)))
        wf_cat = jnp.concatenate(_split2(wf), axis=1)
        bf_pad = jnp.pad(b_forget[l], (0, LANES - N_FOX_HEADS)).reshape(1, LANES)

        qd, kd, vdt, qf, kf, vft, qb, kb = _in_proj(
            x2, norm_attn_g[l].reshape(1, d), cos2, sin2, w_stack, wt_stack, wf_cat, bf_pad,
            batch=b, seq=s, tm=IN_PROJ_ROWS)

        to3 = lambda t: t.reshape(b, s, t.shape[-1])
        lam_params = jnp.stack([lambda_q1[l], lambda_k1[l], lambda_q2[l], lambda_k2[l]])
        od = _diff_attn(to3(qd), to3(kd), vdt, lam_params, diff_norm_g[l].reshape(-1, 1),
                        tq=ATTN_TILE, lam_init=lam_init)
        of = _fox_attn(to3(qf), to3(kf), vft, to3(qb), to3(kb), tq=ATTN_TILE)

        wo = w_out[l].astype(BF16).reshape(2, -1, d)
        wr = jnp.zeros((d, LANES), F32)
        wr = wr.at[:, :N_GROUPS].set(router_group_w[l]).at[:, 8:8 + N_EXPERTS].set(router_expert_w[l])
        wr_cat = jnp.concatenate(_split2(wr.T), axis=0)
        br = jnp.zeros((LANES,), F32)
        br = br.at[:N_GROUPS].set(router_group_b[l]).at[8:8 + N_EXPERTS].set(router_expert_b[l]).reshape(LANES, 1)
        x1, hm, comb = _out_proj(x2, od.reshape(n, -1), of.reshape(n, -1), wo, norm_ffn_g[l].reshape(1, d),
                                 wr_cat, br, tm=OUT_PROJ_ROWS)

        x2 = _moe(hm, comb, x1, w_gate[l].astype(BF16), w_up[l].astype(BF16), w_down[l].astype(BF16),
                  norm_final_g.reshape(1, d), tm=MOE_ROWS, apply_final=(l == depth - 1))
    return x2.reshape(b, s, d)
```

```python
import functools

import numpy as np
import jax
import jax.numpy as jnp
from jax import lax
from jax.experimental import pallas as pl
from jax.experimental.pallas import tpu as pltpu

CHUNK = 64
HEAD_DIM = 64
ROPE_THETA = 10000.0
EPS = 1e-6
LOG2E = 1.4426950408889634
N_DIFF_HEADS = 4
N_FOX_HEADS = 8
N_GROUPS = 4
EXPERTS_PER_GROUP = 4
N_EXPERTS = N_GROUPS * EXPERTS_PER_GROUP

LANES = 128
NEG_BIG = -1e30
VMEM_LIMIT = 56 * 1024 * 1024
IN_PROJ_ROWS = 512
ATTN_TILE = 512
OUT_PROJ_ROWS = 1024
MOE_ROWS = 1024
N_BIAS = 6
SUBLANES = 8
GID_LANE = N_EXPERTS
ROW_BLOCK = 128
ONES_ROWS = 16
BLOCKS_PER_STEP = 2
MOE_TILES = 2
OUT_PROJ_SPLIT = 2
CAST_COLS = 512
SORT_ROWS = 512

BF16 = jnp.bfloat16
F32 = jnp.float32
_NT = (((1,), (1,)), ((), ()))


def _dot(a, b):
    return jnp.dot(a, b, preferred_element_type=F32)


def _dot_nt(a, b):
    return lax.dot_general(a, b, _NT, preferred_element_type=F32)


def _split2(x):
    hi = x.astype(BF16)
    lo = (x - hi.astype(F32)).astype(BF16)
    return hi, lo


def _split3(x):
    hi = x.astype(BF16)
    r = x - hi.astype(F32)
    mid = r.astype(BF16)
    lo = (r - mid.astype(F32)).astype(BF16)
    return hi, mid, lo


def _rms(x, g):
    return x * lax.rsqrt(jnp.mean(x * x, axis=-1, keepdims=True) + EPS) * g


def _pack_parts(parts):
    packed = parts[0].astype(F32)
    for k in (1, 2):
        packed = packed + pltpu.roll(parts[k].astype(F32), k * N_FOX_HEADS, axis=1)
    return packed.astype(BF16)


def _in_proj_kernel(x_ref, g_ref, cos_ref, sin_ref, w_ref, wt_ref, wf_cat_ref, bf_ref, tri_ref,
                    eq_ref, ek_ref, cq_ref, ck_ref, ew0_ref, ew1_ref, ew2_ref,
                    qd_ref, kd_ref, vdt_ref, qf_ref, kf_ref, vft_ref, qb_ref, kb_ref, eb0_ref, eb1_ref, eb2_ref,
                    carry_ref, *, tiles_per_seq):
    i = pl.program_id(0)
    for src, dst in ((ew0_ref, eb0_ref), (ew1_ref, eb1_ref), (ew2_ref, eb2_ref)):
        dst[...] = src[...].astype(dst.dtype)
    h = _rms(x_ref[...], g_ref[...])
    hb = h.astype(BF16)
    cos = cos_ref[...]
    sin = sin_ref[...]
    scale = HEAD_DIM ** -0.5 * LOG2E
    lane = lax.broadcasted_iota(jnp.int32, cos.shape, 1)
    first_half = (lane % HEAD_DIM) < HEAD_DIM // 2

    def rope(w_idx, out_ref, mul):
        a = _dot(hb, w_ref[w_idx])
        for c in range(a.shape[1] // LANES):
            sl = slice(c * LANES, (c + 1) * LANES)
            x = a[:, sl]
            rot = jnp.where(first_half, pltpu.roll(x, LANES - HEAD_DIM // 2, axis=1),
                            pltpu.roll(x, HEAD_DIM // 2, axis=1))
            out_ref[:, sl] = ((x * cos + rot * sin) * mul).astype(out_ref.dtype)

    @pl.when(i % tiles_per_seq == 0)
    def _():
        carry_ref[...] = jnp.zeros_like(carry_ref)


    zz = _dot(hb, wf_cat_ref[...])
    z = (zz[:, :LANES] + zz[:, LANES:]) + bf_ref[...]
    log_f = jnp.minimum(z, 0.0) - jnp.log1p(jnp.exp(-jnp.abs(z)))
    valid = lane < N_FOX_HEADS
    log_f = jnp.where(valid, log_f, 0.0)

    rope(0, qd_ref, scale)

    r = _dot(tri_ref[...], _pack_parts(_split3(log_f)))
    cum = (r + pltpu.roll(r, LANES - N_FOX_HEADS, axis=1)) + pltpu.roll(r, LANES - 2 * N_FOX_HEADS, axis=1)
    cum = jnp.where(valid, cum + carry_ref[...], 0.0)
    carry_ref[...] = cum[cum.shape[0] - 1:, :]

    rope(1, kd_ref, 1.0)

    f_pack = _pack_parts(_split3(cum * LOG2E))
    qb_ref[...] = (cq_ref[...] + _dot(f_pack, eq_ref[...])).astype(qb_ref.dtype)
    kb_ref[...] = (ck_ref[...] + _dot(f_pack, ek_ref[...])).astype(kb_ref.dtype)

    qf_ref[...] = (_dot(hb, w_ref[2]) * scale).astype(qf_ref.dtype)
    kf_ref[...] = _dot(hb, w_ref[3]).astype(kf_ref.dtype)
    vdt_ref[0] = _dot_nt(wt_ref[0], hb).astype(vdt_ref.dtype)
    vft_ref[0] = _dot_nt(wt_ref[1], hb).astype(vft_ref.dtype)


def _bias_placement():
    width = (N_FOX_HEADS // 2) * LANES
    eq = np.zeros((LANES, width), np.float32)
    ek = np.zeros((LANES, width), np.float32)
    cq = np.zeros((1, width), np.float32)
    ck = np.zeros((1, width), np.float32)
    for h in range(N_FOX_HEADS):
        base = (h // 2) * LANES + (h % 2) * N_BIAS
        for part in range(3):
            eq[part * N_FOX_HEADS + h, base + part] = 1.0
            ek[part * N_FOX_HEADS + h, base + 3 + part] = -1.0
        cq[0, base + 3:base + 6] = 1.0
        ck[0, base:base + 3] = 1.0
    return jnp.asarray(eq, BF16), jnp.asarray(ek, BF16), jnp.asarray(cq), jnp.asarray(ck)


def _in_proj(x2, g, cos2, sin2, w_stack, wt_stack, wf_cat, bf_pad, expert_ws, *, batch, seq, tm):
    n, d = x2.shape
    tiles_per_seq = seq // tm
    steps = n // tm
    ew = [w.reshape(steps, -1, CAST_COLS) for w in expert_ws]
    ew_blk = [pl.BlockSpec((1,) + w.shape[1:], lambda i: (i, 0, 0)) for w in ew]
    tri = jnp.tril(jnp.ones((tm, tm), F32)).astype(BF16)
    eq, ek, cq, ck = _bias_placement()
    wcols = w_stack.shape[2]
    const = lambda *shape: pl.BlockSpec(shape, lambda i: (0,) * len(shape))
    row_blk = lambda cols: pl.BlockSpec((tm, cols), lambda i: (i, 0))
    pos_blk = pl.BlockSpec((tm, LANES), lambda i: (i % tiles_per_seq, 0))
    vt_blk = pl.BlockSpec((1, wcols, tm), lambda i: (i // tiles_per_seq, 0, i % tiles_per_seq))
    out_bf = jax.ShapeDtypeStruct((n, wcols), BF16)
    out_vt = jax.ShapeDtypeStruct((batch, wcols, seq), BF16)
    return pl.pallas_call(
        functools.partial(_in_proj_kernel, tiles_per_seq=tiles_per_seq),
        grid=(n // tm,),
        in_specs=[row_blk(d), const(1, d), pos_blk, pos_blk, const(*w_stack.shape), const(*wt_stack.shape),
                  const(*wf_cat.shape), const(1, LANES), const(tm, tm),
                  const(*eq.shape), const(*ek.shape), const(*cq.shape), const(*ck.shape)] + ew_blk,
        out_specs=[row_blk(wcols), row_blk(wcols), vt_blk, row_blk(wcols), row_blk(wcols), vt_blk,
                   row_blk(eq.shape[1]), row_blk(ek.shape[1])] + ew_blk,
        out_shape=[out_bf, out_bf, out_vt, out_bf, out_bf, out_vt,
                   jax.ShapeDtypeStruct((n, eq.shape[1]), BF16), jax.ShapeDtypeStruct((n, ek.shape[1]), BF16)]
        + [jax.ShapeDtypeStruct(w.shape, BF16) for w in ew],
        scratch_shapes=[pltpu.VMEM((1, LANES), F32)],
        compiler_params=pltpu.CompilerParams(dimension_semantics=("arbitrary",),
                                             vmem_limit_bytes=VMEM_LIMIT),
        name="in_proj",
    )(x2, g, cos2, sin2, w_stack, wt_stack, wf_cat, bf_pad, tri, eq, ek, cq, ck, *ew)


def _flash_sweep(nq, tq, tabs, logits_fn, values_fn, mask, scratch):
    s0, s1, c0, c1, m_ref, acc_ref = scratch
    row_tab, k_tab = tabs
    n_maps = m_ref.shape[1]
    n_off = nq * (nq - 1) // 2
    ones = jnp.ones((ONES_ROWS, tq), BF16)

    def stage_a(row, tile, s_buf, c_buf, masked):
        sts = logits_fn(row, tile, masked)
        for a in range(n_maps):
            st = jnp.where(mask, sts[a], NEG_BIG) if masked else sts[a]
            s_buf[a] = st
            c_buf[a] = jnp.max(st, axis=0, keepdims=True)

    def stage_b(row, tile, s_buf, c_buf, first):
        vts = values_fn(tile)
        for a in range(n_maps):
            vt = jnp.concatenate([vts[a], ones], axis=0)
            if first:
                m_new = c_buf[a]
                half = tq // 2
                p_left = jnp.exp2(s_buf[a, :half, :half] - m_new[:, :half]).astype(BF16)
                p_right = jnp.exp2(s_buf[a, :, half:] - m_new[:, half:]).astype(BF16)
                acc_ref[row, a] = jnp.concatenate([_dot(vt[:, :half], p_left), _dot(vt, p_right)], axis=1)
            else:
                m = m_ref[row, a]
                m_new = jnp.maximum(m, c_buf[a])
                alpha = jnp.exp2(m - m_new)
                acc_ref[row, a] = alpha * acc_ref[row, a] + _dot(vt, jnp.exp2(s_buf[a] - m_new).astype(BF16))
            m_ref[row, a] = m_new

    stage_a(0, 0, s0, c0, True)

    def diag_pair(jj, _):
        r = 2 * jj
        stage_a(r + 1, r + 1, s1, c1, True)
        stage_b(r, r, s0, c0, True)
        stage_a(r + 2, r + 2, s0, c0, True)
        stage_b(r + 1, r + 1, s1, c1, True)
        return 0

    lax.fori_loop(0, (nq - 2) // 2, diag_pair, 0)
    stage_a(nq - 1, nq - 1, s1, c1, True)
    stage_b(nq - 2, nq - 2, s0, c0, True)

    stage_a(row_tab[0], k_tab[0], s0, c0, False)
    stage_b(nq - 1, nq - 1, s1, c1, True)

    def off_pair(jj, _):
        t = 2 * jj
        stage_a(row_tab[t + 1], k_tab[t + 1], s1, c1, False)
        stage_b(row_tab[t], k_tab[t], s0, c0, False)
        stage_a(row_tab[t + 2], k_tab[t + 2], s0, c0, False)
        stage_b(row_tab[t + 1], k_tab[t + 1], s1, c1, False)
        return 0

    lax.fori_loop(0, (n_off - 2) // 2, off_pair, 0)
    stage_a(row_tab[n_off - 1], k_tab[n_off - 1], s1, c1, False)
    stage_b(row_tab[n_off - 2], k_tab[n_off - 2], s0, c0, False)
    stage_b(row_tab[n_off - 1], k_tab[n_off - 1], s1, c1, False)


def _flash_scratch(nq, n_maps, dv, tq):
    s_buf = pltpu.VMEM((n_maps, tq, tq), F32)
    c_buf = pltpu.VMEM((n_maps, 1, tq), F32)
    return [s_buf, s_buf, c_buf, c_buf, pltpu.VMEM((nq, n_maps, 1, tq), F32),
            pltpu.VMEM((nq, n_maps, dv + ONES_ROWS, tq), F32)]


def _off_diagonal_tables(nq):
    pairs = [(r, k) for r in range(nq) for k in range(r)]
    return (jnp.asarray([p[0] for p in pairs], jnp.int32), jnp.asarray([p[1] for p in pairs], jnp.int32))


def _flash_outputs(acc_ref, row, dv):
    return [acc_ref[row, a, :dv, :] / acc_ref[row, a, dv:dv + 1, :] for a in range(acc_ref.shape[1])]


def _head_blocks(block):
    return [block[:, k * LANES:(k + 1) * LANES] for k in range(BLOCKS_PER_STEP)]


def _tile_logits(k, q, diagonal):
    if not diagonal:
        return _dot_nt(k, q)
    half = q.shape[0] // 2
    assert half % CHUNK == 0
    left = jnp.concatenate([_dot_nt(k[:half], q[:half]), jnp.full((k.shape[0] - half, half), NEG_BIG, F32)],
                           axis=0)
    return jnp.concatenate([left, _dot_nt(k, q[half:])], axis=1)


def _tile_rows(ref, tile, tq):
    return ref[0, pl.ds(pl.multiple_of(tile * tq, tq), tq), :]


def _diff_attn_kernel(row_tab, k_tab, q_ref, k_ref, vt_ref, lam_ref, g_ref, o_ref, *scratch, tq, lam_init):
    nq = q_ref.shape[1] // tq

    def logits(row, tile, diagonal):
        qs = []
        for q in _head_blocks(_tile_rows(q_ref, row, tq)):
            lane = lax.broadcasted_iota(jnp.int32, q.shape, 1)
            zero = jnp.zeros_like(q)
            qs += [jnp.where(lane < HEAD_DIM, q, zero), jnp.where(lane >= HEAD_DIM, q, zero)]
        kts = _head_blocks(_tile_rows(k_ref, tile, tq))
        return [_tile_logits(kts[a // 2], qs[a], diagonal) for a in range(len(qs))]

    def values(tile):
        vt = vt_ref[0, :, pl.ds(pl.multiple_of(tile * tq, tq), tq)]
        return [vt[(a // 2) * LANES:(a // 2 + 1) * LANES] for a in range(2 * BLOCKS_PER_STEP)]

    key = lax.broadcasted_iota(jnp.int32, (tq, tq), 0)
    qry = lax.broadcasted_iota(jnp.int32, (tq, tq), 1)
    _flash_sweep(nq, tq, (row_tab, k_tab), logits, values, (key // CHUNK) <= (qry // CHUNK), scratch)

    lp = lam_ref[...]
    lam = (jnp.exp(jnp.sum(lp[0:1] * lp[1:2], axis=1, keepdims=True))
           - jnp.exp(jnp.sum(lp[2:3] * lp[3:4], axis=1, keepdims=True)) + lam_init)

    def finish(row, _):
        outs = _flash_outputs(scratch[-1], row, LANES)
        ys = []
        for k in range(BLOCKS_PER_STEP):
            o = outs[2 * k] - lam * outs[2 * k + 1]
            y = o * lax.rsqrt(jnp.mean(o * o, axis=0, keepdims=True) + EPS) * g_ref[...] * (1.0 - lam_init)
            ys.append(y.T.astype(o_ref.dtype))
        o_ref[0, pl.ds(pl.multiple_of(row * tq, tq), tq), :] = jnp.concatenate(ys, axis=1)
        return 0

    lax.fori_loop(0, nq, finish, 0)


def _attn_call(kernel_fn, name, n_maps, dv, inputs, in_specs, b, s, w, tq):
    nq = s // tq
    assert nq % 2 == 0 and nq >= 4, "the pipelined sweeps handle tiles in pairs"
    wb = BLOCKS_PER_STEP * LANES
    return pl.pallas_call(
        kernel_fn,
        grid_spec=pltpu.PrefetchScalarGridSpec(
            num_scalar_prefetch=2,
            grid=(b, w // wb),
            in_specs=in_specs,
            out_specs=pl.BlockSpec((1, s, wb), lambda bi, h, *_: (bi, 0, h)),
            scratch_shapes=_flash_scratch(nq, n_maps, dv, tq)),
        out_shape=jax.ShapeDtypeStruct((b, s, w), BF16),
        compiler_params=pltpu.CompilerParams(dimension_semantics=("arbitrary",) * 2,
                                             vmem_limit_bytes=VMEM_LIMIT),
        name=name,
    )(*_off_diagonal_tables(nq), *inputs)


def _attn_specs(s):
    wb = BLOCKS_PER_STEP * LANES
    row_blk = pl.BlockSpec((1, s, wb), lambda bi, h, *_: (bi, 0, h))
    vt_blk = pl.BlockSpec((1, wb, s), lambda bi, h, *_: (bi, h, 0))
    return row_blk, vt_blk


def _diff_attn(qd, kd, vdt, lam_params, gnorm_col, *, tq, lam_init):
    b, s, w = qd.shape
    row_blk, vt_blk = _attn_specs(s)
    small = lambda arr: pl.BlockSpec(arr.shape, lambda bi, h, *_: (0, 0))
    return _attn_call(functools.partial(_diff_attn_kernel, tq=tq, lam_init=lam_init), "diff_attn",
                      2 * BLOCKS_PER_STEP, LANES, (qd, kd, vdt, lam_params, gnorm_col),
                      [row_blk, row_blk, vt_blk, small(lam_params), small(gnorm_col)], b, s, w, tq)


def _fox_attn_kernel(row_tab, k_tab, q_ref, k_ref, vt_ref, qb_ref, kb_ref, o_ref, *scratch, tq):
    nq = q_ref.shape[1] // tq

    def logits(row, tile, diagonal):
        qs = []
        for q, qb in zip(_head_blocks(_tile_rows(q_ref, row, tq)), _head_blocks(_tile_rows(qb_ref, row, tq))):
            lane = lax.broadcasted_iota(jnp.int32, q.shape, 1)
            zero = jnp.zeros_like(q)
            qs += [jnp.concatenate([jnp.where(lane < HEAD_DIM, q, zero), jnp.where(lane < N_BIAS, qb, zero)],
                                   axis=1),
                   jnp.concatenate([jnp.where(lane >= HEAD_DIM, q, zero),
                                    jnp.where((lane >= N_BIAS) & (lane < 2 * N_BIAS), qb, zero)], axis=1)]
        kts = [jnp.concatenate([k, kb], axis=1) for k, kb in zip(_head_blocks(_tile_rows(k_ref, tile, tq)),
                                                                   _head_blocks(_tile_rows(kb_ref, tile, tq)))]
        return [_tile_logits(kts[a // 2], qs[a], diagonal) for a in range(len(qs))]

    def values(tile):
        vt = vt_ref[0, :, pl.ds(pl.multiple_of(tile * tq, tq), tq)]
        return [vt[a * HEAD_DIM:(a + 1) * HEAD_DIM] for a in range(2 * BLOCKS_PER_STEP)]

    key = lax.broadcasted_iota(jnp.int32, (tq, tq), 0)
    qry = lax.broadcasted_iota(jnp.int32, (tq, tq), 1)
    _flash_sweep(nq, tq, (row_tab, k_tab), logits, values, key <= qry, scratch)

    def finish(row, _):
        o = jnp.concatenate(_flash_outputs(scratch[-1], row, HEAD_DIM), axis=0)
        o_ref[0, pl.ds(pl.multiple_of(row * tq, tq), tq), :] = o.T.astype(o_ref.dtype)
        return 0

    lax.fori_loop(0, nq, finish, 0)


def _fox_attn(qf, kf, vft, qb, kb, *, tq):
    b, s, w = qf.shape
    row_blk, vt_blk = _attn_specs(s)
    return _attn_call(functools.partial(_fox_attn_kernel, tq=tq), "fox_attn", 2 * BLOCKS_PER_STEP, HEAD_DIM,
                      (qf, kf, vft, qb, kb), [row_blk, row_blk, vt_blk, row_blk, row_blk], b, s, w, tq)


def _route(lt):
    tm = lt.shape[1]
    g8 = lt[0:8]
    r8 = lax.broadcasted_iota(jnp.int32, (8, tm), 0)
    g8 = jnp.where(r8 < N_GROUPS, g8, NEG_BIG)
    gmax = jnp.max(g8, axis=0, keepdims=True)
    gidx = jnp.min(jnp.where(g8 == gmax, r8, N_GROUPS), axis=0, keepdims=True)
    g_w = 1.0 / jnp.sum(jnp.exp(g8 - gmax), axis=0, keepdims=True)

    e16 = lt[8:8 + N_EXPERTS]
    r16 = lax.broadcasted_iota(jnp.int32, (N_EXPERTS, tm), 0)
    in_group = (r16 // EXPERTS_PER_GROUP) == gidx
    e_sel = jnp.where(in_group, e16, NEG_BIG)
    top1 = jnp.max(e_sel, axis=0, keepdims=True)
    id1 = jnp.min(jnp.where(e_sel == top1, r16, N_EXPERTS), axis=0, keepdims=True)
    e_rest = jnp.where(r16 == id1, NEG_BIG, e_sel)
    top2 = jnp.max(e_rest, axis=0, keepdims=True)
    id2 = jnp.min(jnp.where(e_rest == top2, r16, N_EXPERTS), axis=0, keepdims=True)
    t = jnp.exp(top2 - top1)
    w1 = g_w / (1.0 + t)
    w2 = w1 * t
    r128 = lax.broadcasted_iota(jnp.int32, (LANES, tm), 0)
    return (jnp.where(r128 == id1, w1, 0.0) + jnp.where(r128 == id2, w2, 0.0)
            + jnp.where(r128 == GID_LANE, gidx.astype(F32), 0.0))


def _out_proj_kernel(x_ref, od_ref, of_ref, wo_ref, g_ref, wr_cat_ref, br_ref,
                     x1_ref, hm_ref, comb_ref):
    tm = x_ref.shape[0]
    halves = [slice(k * tm // OUT_PROJ_SPLIT, (k + 1) * tm // OUT_PROJ_SPLIT) for k in range(OUT_PROJ_SPLIT)]
    x1s = []
    for rows in halves:
        x1 = x_ref[rows, :] + (_dot(od_ref[rows, :], wo_ref[0]) + _dot(of_ref[rows, :], wo_ref[1]))
        x1_ref[rows, :] = x1
        x1s.append(x1)
    wr_cat = wr_cat_ref[...]
    for rows, x1 in zip(halves, x1s):
        hb, h_lo = _split2(_rms(x1, g_ref[...]))
        hm_ref[rows, :] = hb
        both = _dot_nt(wr_cat, hb)
        lt = ((both[:LANES] + both[LANES:]) + _dot_nt(wr_cat[:LANES], h_lo)) + br_ref[...]
        comb_ref[rows, :] = _route(lt).T


def _out_proj(x2, od, of, wo, g, wr_cat, br, *, tm):
    n, d = x2.shape
    const = lambda *shape: pl.BlockSpec(shape, lambda i: (0,) * len(shape))
    row_blk = lambda cols: pl.BlockSpec((tm, cols), lambda i: (i, 0))
    return pl.pallas_call(
        _out_proj_kernel,
        grid=(n // tm,),
        in_specs=[row_blk(d), row_blk(od.shape[1]), row_blk(of.shape[1]), const(*wo.shape), const(1, d),
                  const(*wr_cat.shape), const(*br.shape)],
        out_specs=[row_blk(d), row_blk(d), row_blk(LANES)],
        out_shape=[jax.ShapeDtypeStruct((n, d), F32), jax.ShapeDtypeStruct((n, d), BF16),
                   jax.ShapeDtypeStruct((n, LANES), F32)],
        compiler_params=pltpu.CompilerParams(dimension_semantics=("arbitrary",),
                                             vmem_limit_bytes=VMEM_LIMIT),
        name="out_proj",
    )(x2, od, of, wo, g, wr_cat, br)


def _moe_kernel(hm_ref, comb_ref, x1_hbm, wg_ref, wu_ref, wd_ref, gf_ref, o_hbm,
                tri_ref, hs_ref, cs_ref, posc_ref, tab_ref, xbuf_ref, obuf_ref, sem_ref, *, apply_final):
    g = pl.program_id(1)
    k = pl.program_id(2)
    t = hm_ref.shape[0]
    t_pad = hs_ref.shape[1]
    rb = ROW_BLOCK

    @pl.when((pl.program_id(0) == 0) & (g == 0) & (k == 0))
    def _():
        row = lax.broadcasted_iota(jnp.int32, (t, t), 0)
        col = lax.broadcasted_iota(jnp.int32, (t, t), 1)
        tri_ref[...] = jnp.where(row > col, 1.0, 0.0).astype(BF16)

    @pl.when(g == 0)
    def _():
        comb = comb_ref[...]
        lane = lax.broadcasted_iota(jnp.int32, comb.shape, 1)
        onehot_c = jnp.where(lane == comb[:, GID_LANE:GID_LANE + 1].astype(jnp.int32), 1.0, 0.0)
        rank_c = _dot(tri_ref[...], onehot_c.astype(BF16))
        counts = jnp.sum(onehot_c, axis=0, keepdims=True)
        lane1 = lax.broadcasted_iota(jnp.int32, (1, LANES), 1)
        start_c = jnp.zeros((1, LANES), F32)
        first = jnp.int32(0)
        for grp in range(N_GROUPS):
            cnt = jnp.sum(jnp.where(lane1 == grp, counts, 0.0)).astype(jnp.int32)
            nblk = lax.shift_right_logical(cnt + (rb - 1), int(np.log2(rb)))
            tab_ref[k, grp] = first
            tab_ref[k, N_GROUPS + grp] = nblk
            start_c = jnp.where(lane1 == grp, (first * rb).astype(F32), start_c)
            first = first + nblk
        dest = onehot_c * (rank_c + start_c)
        posc_ref[k] = jnp.sum(dest, axis=1, keepdims=True).astype(jnp.int32)
        pos_r = jnp.sum(dest.T, axis=0, keepdims=True).astype(jnp.int32)
        c_parts = _split3(jnp.where(lane < N_EXPERTS, comb, 0.0))
        c_pack = (c_parts[0].astype(F32) + pltpu.roll(c_parts[1].astype(F32), N_EXPERTS, axis=1)
                  + pltpu.roll(c_parts[2].astype(F32), 2 * N_EXPERTS, axis=1)).astype(BF16)
        hm = hm_ref[...]
        for r0 in range(0, t_pad, SORT_ROWS):
            dst = lax.broadcasted_iota(jnp.int32, (SORT_ROWS, t), 0) + r0
            perm = jnp.where(dst == pos_r, 1.0, 0.0).astype(BF16)
            hs_ref[k, r0:r0 + SORT_ROWS, :] = _dot(perm, hm).astype(hs_ref.dtype)
            cp = _dot(perm, c_pack)
            cs_ref[k, r0:r0 + SORT_ROWS, :] = (cp + pltpu.roll(cp, LANES - N_EXPERTS, axis=1)
                                               + pltpu.roll(cp, LANES - 2 * N_EXPERTS, axis=1))

    b0 = tab_ref[k, g]

    def expert_block(b, _):
        off = pl.multiple_of(b * rb, rb)
        rows = hs_ref[k, pl.ds(off, rb), :]
        cblk = cs_ref[k, pl.ds(off, rb), :]
        lane = lax.broadcasted_iota(jnp.int32, cblk.shape, 1)
        gates = [_dot(rows, wg_ref[j]) for j in range(EXPERTS_PER_GROUP)]
        ups = [_dot(rows, wu_ref[j]) for j in range(EXPERTS_PER_GROUP)]
        y = None
        for j in range(EXPERTS_PER_GROUP):
            c = jnp.sum(jnp.where(lane == g * EXPERTS_PER_GROUP + j, cblk, 0.0), axis=1, keepdims=True)
            a = gates[j] * jax.nn.sigmoid(gates[j]) * ups[j] * c
            d = _dot(a.astype(BF16), wd_ref[j])
            y = d if y is None else y + d
        hs_ref[k, pl.ds(off, rb), :] = y.astype(hs_ref.dtype)
        return 0

    last_group = g == pl.num_programs(1) - 1
    tile = pl.program_id(0) * pl.num_programs(2) + k
    n_tiles = pl.num_programs(0) * pl.num_programs(2)

    def x1_copy():
        return pltpu.make_async_copy(x1_hbm.at[pl.ds(pl.multiple_of(tile * t, t), t), :], xbuf_ref, sem_ref.at[0])

    def out_copy(which):
        return pltpu.make_async_copy(obuf_ref, o_hbm.at[pl.ds(pl.multiple_of(which * t, t), t), :], sem_ref.at[1])

    @pl.when(last_group)
    def _():
        x1_copy().start()

    lax.fori_loop(b0, b0 + tab_ref[k, N_GROUPS + g], expert_block, 0)

    @pl.when(last_group)
    def _():
        x1_copy().wait()

        @pl.when(tile > 0)
        def _():
            out_copy(tile - 1).wait()

        ys = hs_ref[k]
        for r0 in range(0, t, SORT_ROWS):
            src = lax.broadcasted_iota(jnp.int32, (SORT_ROWS, t_pad), 1)
            unperm = jnp.where(src == posc_ref[k, r0:r0 + SORT_ROWS, :], 1.0, 0.0).astype(BF16)
            x2 = xbuf_ref[r0:r0 + SORT_ROWS, :] + _dot(unperm, ys)
            obuf_ref[r0:r0 + SORT_ROWS, :] = _rms(x2, gf_ref[...]) if apply_final else x2
        out_copy(tile).start()

        @pl.when(tile == n_tiles - 1)
        def _():
            out_copy(tile).wait()


def _moe(hm, comb, x1, wg, wu, wd, gfin, *, tm, apply_final):
    n, d = x1.shape
    ne, _, de = wg.shape
    epg = EXPERTS_PER_GROUP
    n_groups = ne // epg
    t_pad = tm + N_GROUPS * ROW_BLOCK
    assert t_pad % SORT_ROWS == 0 and tm % SORT_ROWS == 0 and n % (MOE_TILES * tm) == 0
    first_blk = lambda cols: pl.BlockSpec(
        (tm, cols), lambda p, g, k: (MOE_TILES * p + jnp.where(g == 0, k, MOE_TILES - 1), 0))
    w_blk = lambda *shape: pl.BlockSpec((epg,) + shape, lambda p, g, k: (g, 0, 0))
    hbm = pl.BlockSpec(memory_space=pl.ANY)
    return pl.pallas_call(
        functools.partial(_moe_kernel, apply_final=apply_final),
        grid=(n // (MOE_TILES * tm), n_groups, MOE_TILES),
        in_specs=[first_blk(d), first_blk(LANES), hbm, w_blk(d, de), w_blk(d, de), w_blk(de, d),
                  pl.BlockSpec((1, d), lambda p, g, k: (0, 0))],
        out_specs=hbm,
        out_shape=jax.ShapeDtypeStruct((n, d), F32),
        scratch_shapes=[pltpu.VMEM((tm, tm), BF16), pltpu.VMEM((MOE_TILES, t_pad, d), BF16),
                        pltpu.VMEM((MOE_TILES, t_pad, LANES), F32), pltpu.VMEM((MOE_TILES, tm, 1), jnp.int32),
                        pltpu.SMEM((MOE_TILES, 2 * N_GROUPS), jnp.int32),
                        pltpu.VMEM((tm, d), F32), pltpu.VMEM((tm, d), F32), pltpu.SemaphoreType.DMA((2,))],
        compiler_params=pltpu.CompilerParams(dimension_semantics=("arbitrary",) * 3,
                                             vmem_limit_bytes=VMEM_LIMIT),
        name="moe",
    )(hm, comb, x1, wg, wu, wd, gfin)


def kernel(x, norm_attn_g, w_in, b_forget, lambda_q1, lambda_k1, lambda_q2, lambda_k2, diff_norm_g, w_out,
           norm_ffn_g, router_group_w, router_group_b, router_expert_w, router_expert_b, w_gate, w_up, w_down,
           norm_final_g):
    b, s, d = x.shape
    depth = w_in.shape[0]
    n = b * s
    diff_w = N_DIFF_HEADS * 2 * HEAD_DIM
    fox_w = N_FOX_HEADS * HEAD_DIM

    inv_freq = 1.0 / (ROPE_THETA ** (np.arange(0, HEAD_DIM, 2, dtype=np.float64) / HEAD_DIM))
    ang = np.tile(np.arange(s, dtype=np.float64)[:, None] * inv_freq[None, :], (1, 2 * LANES // HEAD_DIM))
    first_half = (np.arange(LANES) % HEAD_DIM) < HEAD_DIM // 2
    cos2 = jnp.asarray(np.cos(ang), F32)
    sin2 = jnp.asarray(np.where(first_half, -np.sin(ang), np.sin(ang)), F32)

    x2 = x.reshape(n, d)
    for l in range(depth):
        lam_init = 0.8 - 0.6 * float(np.exp(-0.3 * l))
        w = w_in[l]
        offs = np.cumsum([0, diff_w, diff_w, diff_w, fox_w, fox_w, fox_w, N_FOX_HEADS])
        seg = [w[:, offs[k]:offs[k + 1]] for k in range(7)]
        w_stack = jnp.stack([seg[0], seg[1], seg[3], seg[4]]).astype(BF16)
        wt_stack = jnp.stack([seg[2].T, seg[5].T]).astype(BF16)
        wf = jnp.pad(seg[6], ((0, 0), (0, LANES - N_FOX_HEADS)))
        wf_cat = jnp.concatenate(_split2(wf), axis=1)
        bf_pad = jnp.pad(b_forget[l], (0, LANES - N_FOX_HEADS)).reshape(1, LANES)

        qd, kd, vdt, qf, kf, vft, qb, kb, wg_bf, wu_bf, wd_bf = _in_proj(
            x2, norm_attn_g[l].reshape(1, d), cos2, sin2, w_stack, wt_stack, wf_cat, bf_pad,
            (w_gate[l], w_up[l], w_down[l]), batch=b, seq=s, tm=IN_PROJ_ROWS)

        to3 = lambda t: t.reshape(b, s, t.shape[-1])
        lam_params = jnp.stack([lambda_q1[l], lambda_k1[l], lambda_q2[l], lambda_k2[l]])
        od = _diff_attn(to3(qd), to3(kd), vdt, lam_params, diff_norm_g[l].reshape(-1, 1),
                        tq=ATTN_TILE, lam_init=lam_init)
        of = _fox_attn(to3(qf), to3(kf), vft, to3(qb), to3(kb), tq=ATTN_TILE)

        wo = w_out[l].astype(BF16).reshape(2, -1, d)
        wr = jnp.zeros((d, LANES), F32)
        wr = wr.at[:, :N_GROUPS].set(router_group_w[l]).at[:, 8:8 + N_EXPERTS].set(router_expert_w[l])
        wr_cat = jnp.concatenate(_split2(wr.T), axis=0)
        br = jnp.zeros((LANES,), F32)
        br = br.at[:N_GROUPS].set(router_group_b[l]).at[8:8 + N_EXPERTS].set(router_expert_b[l]).reshape(LANES, 1)
        x1, hm, comb = _out_proj(x2, od.reshape(n, -1), of.reshape(n, -1), wo, norm_ffn_g[l].reshape(1, d),
                                 wr_cat, br, tm=OUT_PROJ_ROWS)

        x2 = _moe(hm, comb, x1, wg_bf.reshape(w_gate[l].shape), wu_bf.reshape(w_up[l].shape),
                  wd_bf.reshape(w_down[l].shape),
                  norm_final_g.reshape(1, d), tm=MOE_ROWS, apply_final=(l == depth - 1))
    return x2.reshape(b, s, d)
```

```python
import functools

import numpy as np
import jax
import jax.numpy as jnp
from jax import lax
from jax.experimental import pallas as pl
from jax.experimental.pallas import tpu as pltpu

CHUNK = 64
HEAD_DIM = 64
ROPE_THETA = 10000.0
EPS = 1e-6
LOG2E = 1.4426950408889634
N_DIFF_HEADS = 4
N_FOX_HEADS = 8
N_GROUPS = 4
EXPERTS_PER_GROUP = 4
N_EXPERTS = N_GROUPS * EXPERTS_PER_GROUP

LANES = 128
NEG_BIG = -1e30
VMEM_LIMIT = 56 * 1024 * 1024
IN_PROJ_ROWS = 512
ATTN_TILE = 512
OUT_PROJ_ROWS = 1024
MOE_ROWS = 1024
N_BIAS = 6
SUBLANES = 8
GID_LANE = N_EXPERTS
ROW_BLOCK = 128
ONES_ROWS = 16
BLOCKS_PER_STEP = 2
MOE_TILES = 2
OUT_PROJ_SPLIT = 2
SORT_ROWS = 512

BF16 = jnp.bfloat16
F32 = jnp.float32
_NT = (((1,), (1,)), ((), ()))


def _dot(a, b):
    return jnp.dot(a, b, preferred_element_type=F32)


def _dot_nt(a, b):
    return lax.dot_general(a, b, _NT, preferred_element_type=F32)


def _split2(x):
    hi = x.astype(BF16)
    lo = (x - hi.astype(F32)).astype(BF16)
    return hi, lo


def _split3(x):
    hi = x.astype(BF16)
    r = x - hi.astype(F32)
    mid = r.astype(BF16)
    lo = (r - mid.astype(F32)).astype(BF16)
    return hi, mid, lo


def _rms(x, g):
    return x * lax.rsqrt(jnp.mean(x * x, axis=-1, keepdims=True) + EPS) * g


def _pack_parts(parts):
    packed = parts[0].astype(F32)
    for k in (1, 2):
        packed = packed + pltpu.roll(parts[k].astype(F32), k * N_FOX_HEADS, axis=1)
    return packed.astype(BF16)


def _in_proj_kernel(x_ref, g_ref, cos_ref, sin_ref, w_ref, wt_ref, wf_cat_ref, bf_ref, tri_ref,
                    eq_ref, ek_ref, cq_ref, ck_ref, ew0_ref, ew1_ref, ew2_ref,
                    qd_ref, kd_ref, vdt_ref, qf_ref, kf_ref, vft_ref, qb_ref, kb_ref, eb0_ref, eb1_ref, eb2_ref,
                    carry_ref, *, tiles_per_seq):
    i = pl.program_id(0)
    for src, dst in ((ew0_ref, eb0_ref), (ew1_ref, eb1_ref), (ew2_ref, eb2_ref)):
        dst[0] = src[0, 0].astype(dst.dtype)
    h = _rms(x_ref[...], g_ref[...])
    hb = h.astype(BF16)
    cos = cos_ref[...]
    sin = sin_ref[...]
    scale = HEAD_DIM ** -0.5 * LOG2E
    lane = lax.broadcasted_iota(jnp.int32, cos.shape, 1)
    first_half = (lane % HEAD_DIM) < HEAD_DIM // 2

    def rope(w_idx, out_ref, mul):
        a = _dot(hb, w_ref[w_idx])
        for c in range(a.shape[1] // LANES):
            sl = slice(c * LANES, (c + 1) * LANES)
            x = a[:, sl]
            rot = jnp.where(first_half, pltpu.roll(x, LANES - HEAD_DIM // 2, axis=1),
                            pltpu.roll(x, HEAD_DIM // 2, axis=1))
            out_ref[:, sl] = ((x * cos + rot * sin) * mul).astype(out_ref.dtype)

    @pl.when(i % tiles_per_seq == 0)
    def _():
        carry_ref[...] = jnp.zeros_like(carry_ref)


    zz = _dot(hb, wf_cat_ref[...])
    z = (zz[:, :LANES] + zz[:, LANES:]) + bf_ref[...]
    log_f = jnp.minimum(z, 0.0) - jnp.log1p(jnp.exp(-jnp.abs(z)))
    valid = lane < N_FOX_HEADS
    log_f = jnp.where(valid, log_f, 0.0)

    rope(0, qd_ref, scale)

    r = _dot(tri_ref[...], _pack_parts(_split3(log_f)))
    cum = (r + pltpu.roll(r, LANES - N_FOX_HEADS, axis=1)) + pltpu.roll(r, LANES - 2 * N_FOX_HEADS, axis=1)
    cum = jnp.where(valid, cum + carry_ref[...], 0.0)
    carry_ref[...] = cum[cum.shape[0] - 1:, :]

    rope(1, kd_ref, 1.0)

    f_pack = _pack_parts(_split3(cum * LOG2E))
    qb_ref[...] = (cq_ref[...] + _dot(f_pack, eq_ref[...])).astype(qb_ref.dtype)
    kb_ref[...] = (ck_ref[...] + _dot(f_pack, ek_ref[...])).astype(kb_ref.dtype)

    qf_ref[...] = (_dot(hb, w_ref[2]) * scale).astype(qf_ref.dtype)
    kf_ref[...] = _dot(hb, w_ref[3]).astype(kf_ref.dtype)
    vdt_ref[0] = _dot_nt(wt_ref[0], hb).astype(vdt_ref.dtype)
    vft_ref[0] = _dot_nt(wt_ref[1], hb).astype(vft_ref.dtype)


def _bias_placement():
    width = (N_FOX_HEADS // 2) * LANES
    eq = np.zeros((LANES, width), np.float32)
    ek = np.zeros((LANES, width), np.float32)
    cq = np.zeros((1, width), np.float32)
    ck = np.zeros((1, width), np.float32)
    for h in range(N_FOX_HEADS):
        base = (h // 2) * LANES + (h % 2) * N_BIAS
        for part in range(3):
            eq[part * N_FOX_HEADS + h, base + part] = 1.0
            ek[part * N_FOX_HEADS + h, base + 3 + part] = -1.0
        cq[0, base + 3:base + 6] = 1.0
        ck[0, base:base + 3] = 1.0
    return jnp.asarray(eq, BF16), jnp.asarray(ek, BF16), jnp.asarray(cq), jnp.asarray(ck)


def _in_proj(x2, g, cos2, sin2, w_stack, wt_stack, wf_cat, bf_pad, expert_ws, *, layer, batch, seq, tm):
    n, d = x2.shape
    tiles_per_seq = seq // tm
    steps = n // tm
    ew = list(expert_ws)
    per_expert = steps // ew[0].shape[1]
    assert per_expert * ew[0].shape[1] == steps and all(w.shape[2] % (per_expert * 8) == 0 for w in ew)
    ew_blk = [pl.BlockSpec((1, 1, w.shape[2] // per_expert, w.shape[3]),
                           lambda i: (layer, i // per_expert, i % per_expert, 0)) for w in ew]
    eb_blk = [pl.BlockSpec((1, w.shape[2] // per_expert, w.shape[3]),
                           lambda i: (i // per_expert, i % per_expert, 0)) for w in ew]
    tri = jnp.tril(jnp.ones((tm, tm), F32)).astype(BF16)
    eq, ek, cq, ck = _bias_placement()
    wcols = w_stack.shape[2]
    const = lambda *shape: pl.BlockSpec(shape, lambda i: (0,) * len(shape))
    row_blk = lambda cols: pl.BlockSpec((tm, cols), lambda i: (i, 0))
    pos_blk = pl.BlockSpec((tm, LANES), lambda i: (i % tiles_per_seq, 0))
    vt_blk = pl.BlockSpec((1, wcols, tm), lambda i: (i // tiles_per_seq, 0, i % tiles_per_seq))
    out_bf = jax.ShapeDtypeStruct((n, wcols), BF16)
    out_vt = jax.ShapeDtypeStruct((batch, wcols, seq), BF16)
    return pl.pallas_call(
        functools.partial(_in_proj_kernel, tiles_per_seq=tiles_per_seq),
        grid=(n // tm,),
        in_specs=[row_blk(d), const(1, d), pos_blk, pos_blk, const(*w_stack.shape), const(*wt_stack.shape),
                  const(*wf_cat.shape), const(1, LANES), const(tm, tm),
                  const(*eq.shape), const(*ek.shape), const(*cq.shape), const(*ck.shape)] + ew_blk,
        out_specs=[row_blk(wcols), row_blk(wcols), vt_blk, row_blk(wcols), row_blk(wcols), vt_blk,
                   row_blk(eq.shape[1]), row_blk(ek.shape[1])] + eb_blk,
        out_shape=[out_bf, out_bf, out_vt, out_bf, out_bf, out_vt,
                   jax.ShapeDtypeStruct((n, eq.shape[1]), BF16), jax.ShapeDtypeStruct((n, ek.shape[1]), BF16)]
        + [jax.ShapeDtypeStruct(w.shape[1:], BF16) for w in ew],
        scratch_shapes=[pltpu.VMEM((1, LANES), F32)],
        compiler_params=pltpu.CompilerParams(dimension_semantics=("arbitrary",),
                                             vmem_limit_bytes=VMEM_LIMIT),
        name="in_proj",
    )(x2, g, cos2, sin2, w_stack, wt_stack, wf_cat, bf_pad, tri, eq, ek, cq, ck, *ew)


def _flash_sweep(nq, tq, tabs, logits_fn, values_fn, mask, scratch):
    s0, s1, c0, c1, m_ref, acc_ref = scratch
    row_tab, k_tab = tabs
    n_maps = m_ref.shape[1]
    n_off = nq * (nq - 1) // 2
    ones = jnp.ones((ONES_ROWS, tq), BF16)

    def stage_a(row, tile, s_buf, c_buf, masked):
        sts = logits_fn(row, tile, masked)
        for a in range(n_maps):
            st = jnp.where(mask, sts[a], NEG_BIG) if masked else sts[a]
            s_buf[a] = st
            c_buf[a] = jnp.max(st, axis=0, keepdims=True)

    def stage_b(row, tile, s_buf, c_buf, first):
        vts = values_fn(tile)
        for a in range(n_maps):
            vt = jnp.concatenate([vts[a], ones], axis=0)
            if first:
                m_new = c_buf[a]
                half = tq // 2
                p_left = jnp.exp2(s_buf[a, :half, :half] - m_new[:, :half]).astype(BF16)
                p_right = jnp.exp2(s_buf[a, :, half:] - m_new[:, half:]).astype(BF16)
                acc_ref[row, a] = jnp.concatenate([_dot(vt[:, :half], p_left), _dot(vt, p_right)], axis=1)
            else:
                m = m_ref[row, a]
                m_new = jnp.maximum(m, c_buf[a])
                alpha = jnp.exp2(m - m_new)
                acc_ref[row, a] = alpha * acc_ref[row, a] + _dot(vt, jnp.exp2(s_buf[a] - m_new).astype(BF16))
            m_ref[row, a] = m_new

    stage_a(0, 0, s0, c0, True)

    def diag_pair(jj, _):
        r = 2 * jj
        stage_a(r + 1, r + 1, s1, c1, True)
        stage_b(r, r, s0, c0, True)
        stage_a(r + 2, r + 2, s0, c0, True)
        stage_b(r + 1, r + 1, s1, c1, True)
        return 0

    lax.fori_loop(0, (nq - 2) // 2, diag_pair, 0)
    stage_a(nq - 1, nq - 1, s1, c1, True)
    stage_b(nq - 2, nq - 2, s0, c0, True)

    stage_a(row_tab[0], k_tab[0], s0, c0, False)
    stage_b(nq - 1, nq - 1, s1, c1, True)

    def off_pair(jj, _):
        t = 2 * jj
        stage_a(row_tab[t + 1], k_tab[t + 1], s1, c1, False)
        stage_b(row_tab[t], k_tab[t], s0, c0, False)
        stage_a(row_tab[t + 2], k_tab[t + 2], s0, c0, False)
        stage_b(row_tab[t + 1], k_tab[t + 1], s1, c1, False)
        return 0

    lax.fori_loop(0, (n_off - 2) // 2, off_pair, 0)
    stage_a(row_tab[n_off - 1], k_tab[n_off - 1], s1, c1, False)
    stage_b(row_tab[n_off - 2], k_tab[n_off - 2], s0, c0, False)
    stage_b(row_tab[n_off - 1], k_tab[n_off - 1], s1, c1, False)


def _flash_scratch(nq, n_maps, dv, tq):
    s_buf = pltpu.VMEM((n_maps, tq, tq), F32)
    c_buf = pltpu.VMEM((n_maps, 1, tq), F32)
    return [s_buf, s_buf, c_buf, c_buf, pltpu.VMEM((nq, n_maps, 1, tq), F32),
            pltpu.VMEM((nq, n_maps, dv + ONES_ROWS, tq), F32)]


def _off_diagonal_tables(nq):
    pairs = [(r, k) for r in range(nq) for k in range(r)]
    return (jnp.asarray([p[0] for p in pairs], jnp.int32), jnp.asarray([p[1] for p in pairs], jnp.int32))


def _flash_outputs(acc_ref, row, dv):
    return [acc_ref[row, a, :dv, :] / acc_ref[row, a, dv:dv + 1, :] for a in range(acc_ref.shape[1])]


def _head_blocks(block):
    return [block[:, k * LANES:(k + 1) * LANES] for k in range(BLOCKS_PER_STEP)]


def _tile_logits(k, q, diagonal):
    if not diagonal:
        return _dot_nt(k, q)
    half = q.shape[0] // 2
    assert half % CHUNK == 0
    left = jnp.concatenate([_dot_nt(k[:half], q[:half]), jnp.full((k.shape[0] - half, half), NEG_BIG, F32)],
                           axis=0)
    return jnp.concatenate([left, _dot_nt(k, q[half:])], axis=1)


def _tile_rows(ref, tile, tq):
    return ref[0, pl.ds(pl.multiple_of(tile * tq, tq), tq), :]


def _diff_attn_kernel(row_tab, k_tab, q_ref, k_ref, vt_ref, lam_ref, g_ref, o_ref, *scratch, tq, lam_init):
    nq = q_ref.shape[1] // tq

    def logits(row, tile, diagonal):
        qs = []
        for q in _head_blocks(_tile_rows(q_ref, row, tq)):
            lane = lax.broadcasted_iota(jnp.int32, q.shape, 1)
            zero = jnp.zeros_like(q)
            qs += [jnp.where(lane < HEAD_DIM, q, zero), jnp.where(lane >= HEAD_DIM, q, zero)]
        kts = _head_blocks(_tile_rows(k_ref, tile, tq))
        return [_tile_logits(kts[a // 2], qs[a], diagonal) for a in range(len(qs))]

    def values(tile):
        vt = vt_ref[0, :, pl.ds(pl.multiple_of(tile * tq, tq), tq)]
        return [vt[(a // 2) * LANES:(a // 2 + 1) * LANES] for a in range(2 * BLOCKS_PER_STEP)]

    key = lax.broadcasted_iota(jnp.int32, (tq, tq), 0)
    qry = lax.broadcasted_iota(jnp.int32, (tq, tq), 1)
    _flash_sweep(nq, tq, (row_tab, k_tab), logits, values, (key // CHUNK) <= (qry // CHUNK), scratch)

    lp = lam_ref[...]
    lam = (jnp.exp(jnp.sum(lp[0:1] * lp[1:2], axis=1, keepdims=True))
           - jnp.exp(jnp.sum(lp[2:3] * lp[3:4], axis=1, keepdims=True)) + lam_init)

    def finish(row, _):
        outs = _flash_outputs(scratch[-1], row, LANES)
        ys = []
        for k in range(BLOCKS_PER_STEP):
            o = outs[2 * k] - lam * outs[2 * k + 1]
            y = o * lax.rsqrt(jnp.mean(o * o, axis=0, keepdims=True) + EPS) * g_ref[...] * (1.0 - lam_init)
            ys.append(y.T.astype(o_ref.dtype))
        o_ref[0, pl.ds(pl.multiple_of(row * tq, tq), tq), :] = jnp.concatenate(ys, axis=1)
        return 0

    lax.fori_loop(0, nq, finish, 0)


def _attn_call(kernel_fn, name, n_maps, dv, inputs, in_specs, b, s, w, tq):
    nq = s // tq
    assert nq % 2 == 0 and nq >= 4, "the pipelined sweeps handle tiles in pairs"
    wb = BLOCKS_PER_STEP * LANES
    return pl.pallas_call(
        kernel_fn,
        grid_spec=pltpu.PrefetchScalarGridSpec(
            num_scalar_prefetch=2,
            grid=(b, w // wb),
            in_specs=in_specs,
            out_specs=pl.BlockSpec((1, s, wb), lambda bi, h, *_: (bi, 0, h)),
            scratch_shapes=_flash_scratch(nq, n_maps, dv, tq)),
        out_shape=jax.ShapeDtypeStruct((b, s, w), BF16),
        compiler_params=pltpu.CompilerParams(dimension_semantics=("arbitrary",) * 2,
                                             vmem_limit_bytes=VMEM_LIMIT),
        name=name,
    )(*_off_diagonal_tables(nq), *inputs)


def _attn_specs(s):
    wb = BLOCKS_PER_STEP * LANES
    row_blk = pl.BlockSpec((1, s, wb), lambda bi, h, *_: (bi, 0, h))
    vt_blk = pl.BlockSpec((1, wb, s), lambda bi, h, *_: (bi, h, 0))
    return row_blk, vt_blk


def _diff_attn(qd, kd, vdt, lam_params, gnorm_col, *, tq, lam_init):
    b, s, w = qd.shape
    row_blk, vt_blk = _attn_specs(s)
    small = lambda arr: pl.BlockSpec(arr.shape, lambda bi, h, *_: (0, 0))
    return _attn_call(functools.partial(_diff_attn_kernel, tq=tq, lam_init=lam_init), "diff_attn",
                      2 * BLOCKS_PER_STEP, LANES, (qd, kd, vdt, lam_params, gnorm_col),
                      [row_blk, row_blk, vt_blk, small(lam_params), small(gnorm_col)], b, s, w, tq)


def _fox_attn_kernel(row_tab, k_tab, q_ref, k_ref, vt_ref, qb_ref, kb_ref, o_ref, *scratch, tq):
    nq = q_ref.shape[1] // tq

    def logits(row, tile, diagonal):
        qs = []
        for q, qb in zip(_head_blocks(_tile_rows(q_ref, row, tq)), _head_blocks(_tile_rows(qb_ref, row, tq))):
            lane = lax.broadcasted_iota(jnp.int32, q.shape, 1)
            zero = jnp.zeros_like(q)
            qs += [jnp.concatenate([jnp.where(lane < HEAD_DIM, q, zero), jnp.where(lane < N_BIAS, qb, zero)],
                                   axis=1),
                   jnp.concatenate([jnp.where(lane >= HEAD_DIM, q, zero),
                                    jnp.where((lane >= N_BIAS) & (lane < 2 * N_BIAS), qb, zero)], axis=1)]
        kts = [jnp.concatenate([k, kb], axis=1) for k, kb in zip(_head_blocks(_tile_rows(k_ref, tile, tq)),
                                                                   _head_blocks(_tile_rows(kb_ref, tile, tq)))]
        return [_tile_logits(kts[a // 2], qs[a], diagonal) for a in range(len(qs))]

    def values(tile):
        vt = vt_ref[0, :, pl.ds(pl.multiple_of(tile * tq, tq), tq)]
        return [vt[a * HEAD_DIM:(a + 1) * HEAD_DIM] for a in range(2 * BLOCKS_PER_STEP)]

    key = lax.broadcasted_iota(jnp.int32, (tq, tq), 0)
    qry = lax.broadcasted_iota(jnp.int32, (tq, tq), 1)
    _flash_sweep(nq, tq, (row_tab, k_tab), logits, values, key <= qry, scratch)

    def finish(row, _):
        o = jnp.concatenate(_flash_outputs(scratch[-1], row, HEAD_DIM), axis=0)
        o_ref[0, pl.ds(pl.multiple_of(row * tq, tq), tq), :] = o.T.astype(o_ref.dtype)
        return 0

    lax.fori_loop(0, nq, finish, 0)


def _fox_attn(qf, kf, vft, qb, kb, *, tq):
    b, s, w = qf.shape
    row_blk, vt_blk = _attn_specs(s)
    return _attn_call(functools.partial(_fox_attn_kernel, tq=tq), "fox_attn", 2 * BLOCKS_PER_STEP, HEAD_DIM,
                      (qf, kf, vft, qb, kb), [row_blk, row_blk, vt_blk, row_blk, row_blk], b, s, w, tq)


def _route(lt):
    tm = lt.shape[1]
    g8 = lt[0:8]
    r8 = lax.broadcasted_iota(jnp.int32, (8, tm), 0)
    g8 = jnp.where(r8 < N_GROUPS, g8, NEG_BIG)
    gmax = jnp.max(g8, axis=0, keepdims=True)
    gidx = jnp.min(jnp.where(g8 == gmax, r8, N_GROUPS), axis=0, keepdims=True)
    g_w = 1.0 / jnp.sum(jnp.exp(g8 - gmax), axis=0, keepdims=True)

    e16 = lt[8:8 + N_EXPERTS]
    r16 = lax.broadcasted_iota(jnp.int32, (N_EXPERTS, tm), 0)
    in_group = (r16 // EXPERTS_PER_GROUP) == gidx
    e_sel = jnp.where(in_group, e16, NEG_BIG)
    top1 = jnp.max(e_sel, axis=0, keepdims=True)
    id1 = jnp.min(jnp.where(e_sel == top1, r16, N_EXPERTS), axis=0, keepdims=True)
    e_rest = jnp.where(r16 == id1, NEG_BIG, e_sel)
    top2 = jnp.max(e_rest, axis=0, keepdims=True)
    id2 = jnp.min(jnp.where(e_rest == top2, r16, N_EXPERTS), axis=0, keepdims=True)
    t = jnp.exp(top2 - top1)
    w1 = g_w / (1.0 + t)
    w2 = w1 * t
    r128 = lax.broadcasted_iota(jnp.int32, (LANES, tm), 0)
    return (jnp.where(r128 == id1, w1, 0.0) + jnp.where(r128 == id2, w2, 0.0)
            + jnp.where(r128 == GID_LANE, gidx.astype(F32), 0.0))


def _out_proj_kernel(x_ref, od_ref, of_ref, wo_ref, g_ref, wr_cat_ref, br_ref,
                     x1_ref, hm_ref, comb_ref):
    tm = x_ref.shape[0]
    halves = [slice(k * tm // OUT_PROJ_SPLIT, (k + 1) * tm // OUT_PROJ_SPLIT) for k in range(OUT_PROJ_SPLIT)]
    x1s = []
    for rows in halves:
        x1 = x_ref[rows, :] + (_dot(od_ref[rows, :], wo_ref[0]) + _dot(of_ref[rows, :], wo_ref[1]))
        x1_ref[rows, :] = x1
        x1s.append(x1)
    wr_cat = wr_cat_ref[...]
    for rows, x1 in zip(halves, x1s):
        hb, h_lo = _split2(_rms(x1, g_ref[...]))
        hm_ref[rows, :] = hb
        both = _dot_nt(wr_cat, hb)
        lt = ((both[:LANES] + both[LANES:]) + _dot_nt(wr_cat[:LANES], h_lo)) + br_ref[...]
        comb_ref[rows, :] = _route(lt).T


def _out_proj(x2, od, of, wo, g, wr_cat, br, *, tm):
    n, d = x2.shape
    const = lambda *shape: pl.BlockSpec(shape, lambda i: (0,) * len(shape))
    row_blk = lambda cols: pl.BlockSpec((tm, cols), lambda i: (i, 0))
    return pl.pallas_call(
        _out_proj_kernel,
        grid=(n // tm,),
        in_specs=[row_blk(d), row_blk(od.shape[1]), row_blk(of.shape[1]), const(*wo.shape), const(1, d),
                  const(*wr_cat.shape), const(*br.shape)],
        out_specs=[row_blk(d), row_blk(d), row_blk(LANES)],
        out_shape=[jax.ShapeDtypeStruct((n, d), F32), jax.ShapeDtypeStruct((n, d), BF16),
                   jax.ShapeDtypeStruct((n, LANES), F32)],
        compiler_params=pltpu.CompilerParams(dimension_semantics=("arbitrary",),
                                             vmem_limit_bytes=VMEM_LIMIT),
        name="out_proj",
    )(x2, od, of, wo, g, wr_cat, br)


def _moe_kernel(hm_ref, comb_ref, x1_hbm, wg_ref, wu_ref, wd_ref, gf_ref, o_hbm,
                tri_ref, hs_ref, cs_ref, posc_ref, tab_ref, xbuf_ref, obuf_ref, sem_ref, *, apply_final):
    g = pl.program_id(1)
    k = pl.program_id(2)
    t = hm_ref.shape[0]
    t_pad = hs_ref.shape[1]
    rb = ROW_BLOCK

    @pl.when((pl.program_id(0) == 0) & (g == 0) & (k == 0))
    def _():
        row = lax.broadcasted_iota(jnp.int32, (t, t), 0)
        col = lax.broadcasted_iota(jnp.int32, (t, t), 1)
        tri_ref[...] = jnp.where(row > col, 1.0, 0.0).astype(BF16)

    @pl.when(g == 0)
    def _():
        comb = comb_ref[...]
        lane = lax.broadcasted_iota(jnp.int32, comb.shape, 1)
        onehot_c = jnp.where(lane == comb[:, GID_LANE:GID_LANE + 1].astype(jnp.int32), 1.0, 0.0)
        rank_c = _dot(tri_ref[...], onehot_c.astype(BF16))
        counts = jnp.sum(onehot_c, axis=0, keepdims=True)
        lane1 = lax.broadcasted_iota(jnp.int32, (1, LANES), 1)
        start_c = jnp.zeros((1, LANES), F32)
        first = jnp.int32(0)
        for grp in range(N_GROUPS):
            cnt = jnp.sum(jnp.where(lane1 == grp, counts, 0.0)).astype(jnp.int32)
            nblk = lax.shift_right_logical(cnt + (rb - 1), int(np.log2(rb)))
            tab_ref[k, grp] = first
            tab_ref[k, N_GROUPS + grp] = nblk
            start_c = jnp.where(lane1 == grp, (first * rb).astype(F32), start_c)
            first = first + nblk
        dest = onehot_c * (rank_c + start_c)
        posc_ref[k] = jnp.sum(dest, axis=1, keepdims=True).astype(jnp.int32)
        pos_r = jnp.sum(dest.T, axis=0, keepdims=True).astype(jnp.int32)
        c_parts = _split3(jnp.where(lane < N_EXPERTS, comb, 0.0))
        c_pack = (c_parts[0].astype(F32) + pltpu.roll(c_parts[1].astype(F32), N_EXPERTS, axis=1)
                  + pltpu.roll(c_parts[2].astype(F32), 2 * N_EXPERTS, axis=1)).astype(BF16)
        hm = hm_ref[...]
        for r0 in range(0, t_pad, SORT_ROWS):
            dst = lax.broadcasted_iota(jnp.int32, (SORT_ROWS, t), 0) + r0
            perm = jnp.where(dst == pos_r, 1.0, 0.0).astype(BF16)
            hs_ref[k, r0:r0 + SORT_ROWS, :] = _dot(perm, hm).astype(hs_ref.dtype)
            cp = _dot(perm, c_pack)
            cs_ref[k, r0:r0 + SORT_ROWS, :] = (cp + pltpu.roll(cp, LANES - N_EXPERTS, axis=1)
                                               + pltpu.roll(cp, LANES - 2 * N_EXPERTS, axis=1))

    b0 = tab_ref[k, g]

    def expert_block(b, _):
        off = pl.multiple_of(b * rb, rb)
        rows = hs_ref[k, pl.ds(off, rb), :]
        cblk = cs_ref[k, pl.ds(off, rb), :]
        lane = lax.broadcasted_iota(jnp.int32, cblk.shape, 1)
        gates = [_dot(rows, wg_ref[j]) for j in range(EXPERTS_PER_GROUP)]
        ups = [_dot(rows, wu_ref[j]) for j in range(EXPERTS_PER_GROUP)]
        y = None
        for j in range(EXPERTS_PER_GROUP):
            c = jnp.sum(jnp.where(lane == g * EXPERTS_PER_GROUP + j, cblk, 0.0), axis=1, keepdims=True)
            a = gates[j] * jax.nn.sigmoid(gates[j]) * ups[j] * c
            d = _dot(a.astype(BF16), wd_ref[j])
            y = d if y is None else y + d
        hs_ref[k, pl.ds(off, rb), :] = y.astype(hs_ref.dtype)
        return 0

    last_group = g == pl.num_programs(1) - 1
    tile = pl.program_id(0) * pl.num_programs(2) + k
    n_tiles = pl.num_programs(0) * pl.num_programs(2)

    def x1_copy():
        return pltpu.make_async_copy(x1_hbm.at[pl.ds(pl.multiple_of(tile * t, t), t), :], xbuf_ref, sem_ref.at[0])

    def out_copy(which):
        return pltpu.make_async_copy(obuf_ref, o_hbm.at[pl.ds(pl.multiple_of(which * t, t), t), :], sem_ref.at[1])

    @pl.when(last_group)
    def _():
        x1_copy().start()

    lax.fori_loop(b0, b0 + tab_ref[k, N_GROUPS + g], expert_block, 0)

    @pl.when(last_group)
    def _():
        x1_copy().wait()

        @pl.when(tile > 0)
        def _():
            out_copy(tile - 1).wait()

        ys = hs_ref[k]
        for r0 in range(0, t, SORT_ROWS):
            src = lax.broadcasted_iota(jnp.int32, (SORT_ROWS, t_pad), 1)
            unperm = jnp.where(src == posc_ref[k, r0:r0 + SORT_ROWS, :], 1.0, 0.0).astype(BF16)
            x2 = xbuf_ref[r0:r0 + SORT_ROWS, :] + _dot(unperm, ys)
            obuf_ref[r0:r0 + SORT_ROWS, :] = _rms(x2, gf_ref[...]) if apply_final else x2
        out_copy(tile).start()

        @pl.when(tile == n_tiles - 1)
        def _():
            out_copy(tile).wait()


def _moe(hm, comb, x1, wg, wu, wd, gfin, *, tm, apply_final):
    n, d = x1.shape
    ne, _, de = wg.shape
    epg = EXPERTS_PER_GROUP
    n_groups = ne // epg
    t_pad = tm + N_GROUPS * ROW_BLOCK
    assert t_pad % SORT_ROWS == 0 and tm % SORT_ROWS == 0 and n % (MOE_TILES * tm) == 0
    first_blk = lambda cols: pl.BlockSpec(
        (tm, cols), lambda p, g, k: (MOE_TILES * p + jnp.where(g == 0, k, MOE_TILES - 1), 0))
    w_blk = lambda *shape: pl.BlockSpec((epg,) + shape, lambda p, g, k: (g, 0, 0))
    hbm = pl.BlockSpec(memory_space=pl.ANY)
    return pl.pallas_call(
        functools.partial(_moe_kernel, apply_final=apply_final),
        grid=(n // (MOE_TILES * tm), n_groups, MOE_TILES),
        in_specs=[first_blk(d), first_blk(LANES), hbm, w_blk(d, de), w_blk(d, de), w_blk(de, d),
                  pl.BlockSpec((1, d), lambda p, g, k: (0, 0))],
        out_specs=hbm,
        out_shape=jax.ShapeDtypeStruct((n, d), F32),
        scratch_shapes=[pltpu.VMEM((tm, tm), BF16), pltpu.VMEM((MOE_TILES, t_pad, d), BF16),
                        pltpu.VMEM((MOE_TILES, t_pad, LANES), F32), pltpu.VMEM((MOE_TILES, tm, 1), jnp.int32),
                        pltpu.SMEM((MOE_TILES, 2 * N_GROUPS), jnp.int32),
                        pltpu.VMEM((tm, d), F32), pltpu.VMEM((tm, d), F32), pltpu.SemaphoreType.DMA((2,))],
        compiler_params=pltpu.CompilerParams(dimension_semantics=("arbitrary",) * 3,
                                             vmem_limit_bytes=VMEM_LIMIT),
        name="moe",
    )(hm, comb, x1, wg, wu, wd, gfin)


def kernel(x, norm_attn_g, w_in, b_forget, lambda_q1, lambda_k1, lambda_q2, lambda_k2, diff_norm_g, w_out,
           norm_ffn_g, router_group_w, router_group_b, router_expert_w, router_expert_b, w_gate, w_up, w_down,
           norm_final_g):
    b, s, d = x.shape
    depth = w_in.shape[0]
    n = b * s
    diff_w = N_DIFF_HEADS * 2 * HEAD_DIM
    fox_w = N_FOX_HEADS * HEAD_DIM

    inv_freq = 1.0 / (ROPE_THETA ** (np.arange(0, HEAD_DIM, 2, dtype=np.float64) / HEAD_DIM))
    ang = np.tile(np.arange(s, dtype=np.float64)[:, None] * inv_freq[None, :], (1, 2 * LANES // HEAD_DIM))
    first_half = (np.arange(LANES) % HEAD_DIM) < HEAD_DIM // 2
    cos2 = jnp.asarray(np.cos(ang), F32)
    sin2 = jnp.asarray(np.where(first_half, -np.sin(ang), np.sin(ang)), F32)

    x2 = x.reshape(n, d)
    for l in range(depth):
        lam_init = 0.8 - 0.6 * float(np.exp(-0.3 * l))
        w = w_in[l]
        offs = np.cumsum([0, diff_w, diff_w, diff_w, fox_w, fox_w, fox_w, N_FOX_HEADS])
        seg = [w[:, offs[k]:offs[k + 1]] for k in range(7)]
        w_stack = jnp.stack([seg[0], seg[1], seg[3], seg[4]]).astype(BF16)
        wt_stack = jnp.stack([seg[2].T, seg[5].T]).astype(BF16)
        wf = jnp.pad(seg[6], ((0, 0), (0, LANES - N_FOX_HEADS)))
        wf_cat = jnp.concatenate(_split2(wf), axis=1)
        bf_pad = jnp.pad(b_forget[l], (0, LANES - N_FOX_HEADS)).reshape(1, LANES)

        qd, kd, vdt, qf, kf, vft, qb, kb, wg_bf, wu_bf, wd_bf = _in_proj(
            x2, norm_attn_g[l].reshape(1, d), cos2, sin2, w_stack, wt_stack, wf_cat, bf_pad,
            (w_gate, w_up, w_down), layer=l, batch=b, seq=s, tm=IN_PROJ_ROWS)

        to3 = lambda t: t.reshape(b, s, t.shape[-1])
        lam_params = jnp.stack([lambda_q1[l], lambda_k1[l], lambda_q2[l], lambda_k2[l]])
        od = _diff_attn(to3(qd), to3(kd), vdt, lam_params, diff_norm_g[l].reshape(-1, 1),
                        tq=ATTN_TILE, lam_init=lam_init)
        of = _fox_attn(to3(qf), to3(kf), vft, to3(qb), to3(kb), tq=ATTN_TILE)

        wo = w_out[l].astype(BF16).reshape(2, -1, d)
        wr = jnp.zeros((d, LANES), F32)
        wr = wr.at[:, :N_GROUPS].set(router_group_w[l]).at[:, 8:8 + N_EXPERTS].set(router_expert_w[l])
        wr_cat = jnp.concatenate(_split2(wr.T), axis=0)
        br = jnp.zeros((LANES,), F32)
        br = br.at[:N_GROUPS].set(router_group_b[l]).at[8:8 + N_EXPERTS].set(router_expert_b[l]).reshape(LANES, 1)
        x1, hm, comb = _out_proj(x2, od.reshape(n, -1), of.reshape(n, -1), wo, norm_ffn_g[l].reshape(1, d),
                                 wr_cat, br, tm=OUT_PROJ_ROWS)

        x2 = _moe(hm, comb, x1, wg_bf, wu_bf, wd_bf, norm_final_g.reshape(1, d), tm=MOE_ROWS, apply_final=(l == depth - 1))
    return x2.reshape(b, s, d)
```

```python
import functools

import numpy as np
import jax
import jax.numpy as jnp
from jax import lax
from jax.experimental import pallas as pl
from jax.experimental.pallas import tpu as pltpu

CHUNK = 64
HEAD_DIM = 64
ROPE_THETA = 10000.0
EPS = 1e-6
LOG2E = 1.4426950408889634
N_DIFF_HEADS = 4
N_FOX_HEADS = 8
N_GROUPS = 4
EXPERTS_PER_GROUP = 4
N_EXPERTS = N_GROUPS * EXPERTS_PER_GROUP

LANES = 128
NEG_BIG = -1e30
VMEM_LIMIT = 56 * 1024 * 1024
IN_PROJ_ROWS = 512
ATTN_TILE = 512
OUT_PROJ_ROWS = 1024
MOE_ROWS = 1024
N_BIAS = 6
SUBLANES = 8
GID_LANE = N_EXPERTS
ROW_BLOCK = 128
ONES_ROWS = 16
BLOCKS_PER_STEP = 2
MOE_TILES = 2
OUT_PROJ_SPLIT = 2
SORT_ROWS = 512

BF16 = jnp.bfloat16
F32 = jnp.float32
_NT = (((1,), (1,)), ((), ()))


def _dot(a, b):
    return jnp.dot(a, b, preferred_element_type=F32)


def _dot_nt(a, b):
    return lax.dot_general(a, b, _NT, preferred_element_type=F32)


def _split2(x):
    hi = x.astype(BF16)
    lo = (x - hi.astype(F32)).astype(BF16)
    return hi, lo


def _split3(x):
    hi = x.astype(BF16)
    r = x - hi.astype(F32)
    mid = r.astype(BF16)
    lo = (r - mid.astype(F32)).astype(BF16)
    return hi, mid, lo


def _rms(x, g):
    return x * lax.rsqrt(jnp.mean(x * x, axis=-1, keepdims=True) + EPS) * g


def _pack_parts(parts):
    packed = parts[0].astype(F32)
    for k in (1, 2):
        packed = packed + pltpu.roll(parts[k].astype(F32), k * N_FOX_HEADS, axis=1)
    return packed.astype(BF16)


def _in_proj_kernel(x_ref, g_ref, cos_ref, sin_ref, win_ref, wf_cat_ref, bf_ref, tri_ref,
                    eq_ref, ek_ref, cq_ref, ck_ref, ew0_ref, ew1_ref, ew2_ref,
                    qd_ref, kd_ref, vdt_ref, qf_ref, kf_ref, vft_ref, qb_ref, kb_ref, eb0_ref, eb1_ref, eb2_ref,
                    carry_ref, w_ref, wt_ref, *, tiles_per_seq, col_offsets):
    i = pl.program_id(0)

    @pl.when(i == 0)
    def _():
        width = w_ref.shape[2]
        for dst, off in enumerate(col_offsets[0]):
            w_ref[dst] = win_ref[0, :, off:off + width].astype(w_ref.dtype)
        for dst, off in enumerate(col_offsets[1]):
            wt_ref[dst] = win_ref[0, :, off:off + width].T.astype(wt_ref.dtype)

    for src, dst in ((ew0_ref, eb0_ref), (ew1_ref, eb1_ref), (ew2_ref, eb2_ref)):
        dst[0] = src[0, 0].astype(dst.dtype)
    h = _rms(x_ref[...], g_ref[...])
    hb = h.astype(BF16)
    cos = cos_ref[...]
    sin = sin_ref[...]
    scale = HEAD_DIM ** -0.5 * LOG2E
    lane = lax.broadcasted_iota(jnp.int32, cos.shape, 1)
    first_half = (lane % HEAD_DIM) < HEAD_DIM // 2

    def rope(w_idx, out_ref, mul):
        a = _dot(hb, w_ref[w_idx])
        for c in range(a.shape[1] // LANES):
            sl = slice(c * LANES, (c + 1) * LANES)
            x = a[:, sl]
            rot = jnp.where(first_half, pltpu.roll(x, LANES - HEAD_DIM // 2, axis=1),
                            pltpu.roll(x, HEAD_DIM // 2, axis=1))
            out_ref[:, sl] = ((x * cos + rot * sin) * mul).astype(out_ref.dtype)

    @pl.when(i % tiles_per_seq == 0)
    def _():
        carry_ref[...] = jnp.zeros_like(carry_ref)


    zz = _dot(hb, wf_cat_ref[...])
    z = (zz[:, :LANES] + zz[:, LANES:]) + bf_ref[...]
    log_f = jnp.minimum(z, 0.0) - jnp.log1p(jnp.exp(-jnp.abs(z)))
    valid = lane < N_FOX_HEADS
    log_f = jnp.where(valid, log_f, 0.0)

    rope(0, qd_ref, scale)

    r = _dot(tri_ref[...], _pack_parts(_split3(log_f)))
    cum = (r + pltpu.roll(r, LANES - N_FOX_HEADS, axis=1)) + pltpu.roll(r, LANES - 2 * N_FOX_HEADS, axis=1)
    cum = jnp.where(valid, cum + carry_ref[...], 0.0)
    carry_ref[...] = cum[cum.shape[0] - 1:, :]

    rope(1, kd_ref, 1.0)

    f_pack = _pack_parts(_split3(cum * LOG2E))
    qb_ref[...] = (cq_ref[...] + _dot(f_pack, eq_ref[...])).astype(qb_ref.dtype)
    kb_ref[...] = (ck_ref[...] + _dot(f_pack, ek_ref[...])).astype(kb_ref.dtype)

    qf_ref[...] = (_dot(hb, w_ref[2]) * scale).astype(qf_ref.dtype)
    kf_ref[...] = _dot(hb, w_ref[3]).astype(kf_ref.dtype)
    vdt_ref[0] = _dot_nt(wt_ref[0], hb).astype(vdt_ref.dtype)
    vft_ref[0] = _dot_nt(wt_ref[1], hb).astype(vft_ref.dtype)


def _bias_placement():
    width = (N_FOX_HEADS // 2) * LANES
    eq = np.zeros((LANES, width), np.float32)
    ek = np.zeros((LANES, width), np.float32)
    cq = np.zeros((1, width), np.float32)
    ck = np.zeros((1, width), np.float32)
    for h in range(N_FOX_HEADS):
        base = (h // 2) * LANES + (h % 2) * N_BIAS
        for part in range(3):
            eq[part * N_FOX_HEADS + h, base + part] = 1.0
            ek[part * N_FOX_HEADS + h, base + 3 + part] = -1.0
        cq[0, base + 3:base + 6] = 1.0
        ck[0, base:base + 3] = 1.0
    return jnp.asarray(eq, BF16), jnp.asarray(ek, BF16), jnp.asarray(cq), jnp.asarray(ck)


def _in_proj(x2, g, cos2, sin2, w_in, col_offsets, wcols, wf_cat, bf_pad, expert_ws, *, layer, batch, seq, tm):
    n, d = x2.shape
    tiles_per_seq = seq // tm
    steps = n // tm
    ew = list(expert_ws)
    per_expert = steps // ew[0].shape[1]
    assert per_expert * ew[0].shape[1] == steps and all(w.shape[2] % (per_expert * 8) == 0 for w in ew)
    ew_blk = [pl.BlockSpec((1, 1, w.shape[2] // per_expert, w.shape[3]),
                           lambda i: (layer, i // per_expert, i % per_expert, 0)) for w in ew]
    eb_blk = [pl.BlockSpec((1, w.shape[2] // per_expert, w.shape[3]),
                           lambda i: (i // per_expert, i % per_expert, 0)) for w in ew]
    tri = jnp.tril(jnp.ones((tm, tm), F32)).astype(BF16)
    eq, ek, cq, ck = _bias_placement()
    const = lambda *shape: pl.BlockSpec(shape, lambda i: (0,) * len(shape))
    row_blk = lambda cols: pl.BlockSpec((tm, cols), lambda i: (i, 0))
    pos_blk = pl.BlockSpec((tm, LANES), lambda i: (i % tiles_per_seq, 0))
    vt_blk = pl.BlockSpec((1, wcols, tm), lambda i: (i // tiles_per_seq, 0, i % tiles_per_seq))
    out_bf = jax.ShapeDtypeStruct((n, wcols), BF16)
    out_vt = jax.ShapeDtypeStruct((batch, wcols, seq), BF16)
    return pl.pallas_call(
        functools.partial(_in_proj_kernel, tiles_per_seq=tiles_per_seq, col_offsets=col_offsets),
        grid=(n // tm,),
        in_specs=[row_blk(d), const(1, d), pos_blk, pos_blk,
                  pl.BlockSpec((1,) + w_in.shape[1:], lambda i: (layer, 0, 0), pipeline_mode=pl.Buffered(1)),
                  const(*wf_cat.shape), const(1, LANES), const(tm, tm),
                  const(*eq.shape), const(*ek.shape), const(*cq.shape), const(*ck.shape)] + ew_blk,
        out_specs=[row_blk(wcols), row_blk(wcols), vt_blk, row_blk(wcols), row_blk(wcols), vt_blk,
                   row_blk(eq.shape[1]), row_blk(ek.shape[1])] + eb_blk,
        out_shape=[out_bf, out_bf, out_vt, out_bf, out_bf, out_vt,
                   jax.ShapeDtypeStruct((n, eq.shape[1]), BF16), jax.ShapeDtypeStruct((n, ek.shape[1]), BF16)]
        + [jax.ShapeDtypeStruct(w.shape[1:], BF16) for w in ew],
        scratch_shapes=[pltpu.VMEM((1, LANES), F32), pltpu.VMEM((len(col_offsets[0]), d, wcols), BF16),
                        pltpu.VMEM((len(col_offsets[1]), wcols, d), BF16)],
        compiler_params=pltpu.CompilerParams(dimension_semantics=("arbitrary",),
                                             vmem_limit_bytes=VMEM_LIMIT),
        name="in_proj",
    )(x2, g, cos2, sin2, w_in, wf_cat, bf_pad, tri, eq, ek, cq, ck, *ew)


def _flash_sweep(nq, tq, tabs, logits_fn, values_fn, mask, scratch):
    s0, s1, c0, c1, m_ref, acc_ref = scratch
    row_tab, k_tab = tabs
    n_maps = m_ref.shape[1]
    n_off = nq * (nq - 1) // 2
    ones = jnp.ones((ONES_ROWS, tq), BF16)

    def stage_a(row, tile, s_buf, c_buf, masked):
        sts = logits_fn(row, tile, masked)
        for a in range(n_maps):
            st = jnp.where(mask, sts[a], NEG_BIG) if masked else sts[a]
            s_buf[a] = st
            c_buf[a] = jnp.max(st, axis=0, keepdims=True)

    def stage_b(row, tile, s_buf, c_buf, first):
        vts = values_fn(tile)
        for a in range(n_maps):
            vt = jnp.concatenate([vts[a], ones], axis=0)
            if first:
                m_new = c_buf[a]
                half = tq // 2
                p_left = jnp.exp2(s_buf[a, :half, :half] - m_new[:, :half]).astype(BF16)
                p_right = jnp.exp2(s_buf[a, :, half:] - m_new[:, half:]).astype(BF16)
                acc_ref[row, a] = jnp.concatenate([_dot(vt[:, :half], p_left), _dot(vt, p_right)], axis=1)
            else:
                m = m_ref[row, a]
                m_new = jnp.maximum(m, c_buf[a])
                alpha = jnp.exp2(m - m_new)
                acc_ref[row, a] = alpha * acc_ref[row, a] + _dot(vt, jnp.exp2(s_buf[a] - m_new).astype(BF16))
            m_ref[row, a] = m_new

    stage_a(0, 0, s0, c0, True)

    def diag_pair(jj, _):
        r = 2 * jj
        stage_a(r + 1, r + 1, s1, c1, True)
        stage_b(r, r, s0, c0, True)
        stage_a(r + 2, r + 2, s0, c0, True)
        stage_b(r + 1, r + 1, s1, c1, True)
        return 0

    lax.fori_loop(0, (nq - 2) // 2, diag_pair, 0)
    stage_a(nq - 1, nq - 1, s1, c1, True)
    stage_b(nq - 2, nq - 2, s0, c0, True)

    stage_a(row_tab[0], k_tab[0], s0, c0, False)
    stage_b(nq - 1, nq - 1, s1, c1, True)

    def off_pair(jj, _):
        t = 2 * jj
        stage_a(row_tab[t + 1], k_tab[t + 1], s1, c1, False)
        stage_b(row_tab[t], k_tab[t], s0, c0, False)
        stage_a(row_tab[t + 2], k_tab[t + 2], s0, c0, False)
        stage_b(row_tab[t + 1], k_tab[t + 1], s1, c1, False)
        return 0

    lax.fori_loop(0, (n_off - 2) // 2, off_pair, 0)
    stage_a(row_tab[n_off - 1], k_tab[n_off - 1], s1, c1, False)
    stage_b(row_tab[n_off - 2], k_tab[n_off - 2], s0, c0, False)
    stage_b(row_tab[n_off - 1], k_tab[n_off - 1], s1, c1, False)


def _flash_scratch(nq, n_maps, dv, tq):
    s_buf = pltpu.VMEM((n_maps, tq, tq), F32)
    c_buf = pltpu.VMEM((n_maps, 1, tq), F32)
    return [s_buf, s_buf, c_buf, c_buf, pltpu.VMEM((nq, n_maps, 1, tq), F32),
            pltpu.VMEM((nq, n_maps, dv + ONES_ROWS, tq), F32)]


def _off_diagonal_tables(nq):
    pairs = [(r, k) for r in range(nq) for k in range(r)]
    return (jnp.asarray([p[0] for p in pairs], jnp.int32), jnp.asarray([p[1] for p in pairs], jnp.int32))


def _flash_outputs(acc_ref, row, dv):
    return [acc_ref[row, a, :dv, :] / acc_ref[row, a, dv:dv + 1, :] for a in range(acc_ref.shape[1])]


def _head_blocks(block):
    return [block[:, k * LANES:(k + 1) * LANES] for k in range(BLOCKS_PER_STEP)]


def _tile_logits(k, q, diagonal):
    if not diagonal:
        return _dot_nt(k, q)
    half = q.shape[0] // 2
    assert half % CHUNK == 0
    left = jnp.concatenate([_dot_nt(k[:half], q[:half]), jnp.full((k.shape[0] - half, half), NEG_BIG, F32)],
                           axis=0)
    return jnp.concatenate([left, _dot_nt(k, q[half:])], axis=1)


def _tile_rows(ref, tile, tq):
    return ref[0, pl.ds(pl.multiple_of(tile * tq, tq), tq), :]


def _diff_attn_kernel(row_tab, k_tab, q_ref, k_ref, vt_ref, lam_ref, g_ref, o_ref, *scratch, tq, lam_init):
    nq = q_ref.shape[1] // tq

    def logits(row, tile, diagonal):
        qs = []
        for q in _head_blocks(_tile_rows(q_ref, row, tq)):
            lane = lax.broadcasted_iota(jnp.int32, q.shape, 1)
            zero = jnp.zeros_like(q)
            qs += [jnp.where(lane < HEAD_DIM, q, zero), jnp.where(lane >= HEAD_DIM, q, zero)]
        kts = _head_blocks(_tile_rows(k_ref, tile, tq))
        return [_tile_logits(kts[a // 2], qs[a], diagonal) for a in range(len(qs))]

    def values(tile):
        vt = vt_ref[0, :, pl.ds(pl.multiple_of(tile * tq, tq), tq)]
        return [vt[(a // 2) * LANES:(a // 2 + 1) * LANES] for a in range(2 * BLOCKS_PER_STEP)]

    key = lax.broadcasted_iota(jnp.int32, (tq, tq), 0)
    qry = lax.broadcasted_iota(jnp.int32, (tq, tq), 1)
    _flash_sweep(nq, tq, (row_tab, k_tab), logits, values, (key // CHUNK) <= (qry // CHUNK), scratch)

    lp = lam_ref[...]
    lam = (jnp.exp(jnp.sum(lp[0:1] * lp[1:2], axis=1, keepdims=True))
           - jnp.exp(jnp.sum(lp[2:3] * lp[3:4], axis=1, keepdims=True)) + lam_init)

    def finish(row, _):
        outs = _flash_outputs(scratch[-1], row, LANES)
        ys = []
        for k in range(BLOCKS_PER_STEP):
            o = outs[2 * k] - lam * outs[2 * k + 1]
            y = o * lax.rsqrt(jnp.mean(o * o, axis=0, keepdims=True) + EPS) * g_ref[...] * (1.0 - lam_init)
            ys.append(y.T.astype(o_ref.dtype))
        o_ref[0, pl.ds(pl.multiple_of(row * tq, tq), tq), :] = jnp.concatenate(ys, axis=1)
        return 0

    lax.fori_loop(0, nq, finish, 0)


def _attn_call(kernel_fn, name, n_maps, dv, inputs, in_specs, b, s, w, tq):
    nq = s // tq
    assert nq % 2 == 0 and nq >= 4, "the pipelined sweeps handle tiles in pairs"
    wb = BLOCKS_PER_STEP * LANES
    return pl.pallas_call(
        kernel_fn,
        grid_spec=pltpu.PrefetchScalarGridSpec(
            num_scalar_prefetch=2,
            grid=(b, w // wb),
            in_specs=in_specs,
            out_specs=pl.BlockSpec((1, s, wb), lambda bi, h, *_: (bi, 0, h)),
            scratch_shapes=_flash_scratch(nq, n_maps, dv, tq)),
        out_shape=jax.ShapeDtypeStruct((b, s, w), BF16),
        compiler_params=pltpu.CompilerParams(dimension_semantics=("arbitrary",) * 2,
                                             vmem_limit_bytes=VMEM_LIMIT),
        name=name,
    )(*_off_diagonal_tables(nq), *inputs)


def _attn_specs(s):
    wb = BLOCKS_PER_STEP * LANES
    row_blk = pl.BlockSpec((1, s, wb), lambda bi, h, *_: (bi, 0, h))
    vt_blk = pl.BlockSpec((1, wb, s), lambda bi, h, *_: (bi, h, 0))
    return row_blk, vt_blk


def _diff_attn(qd, kd, vdt, lam_params, gnorm_col, *, tq, lam_init):
    b, s, w = qd.shape
    row_blk, vt_blk = _attn_specs(s)
    small = lambda arr: pl.BlockSpec(arr.shape, lambda bi, h, *_: (0, 0))
    return _attn_call(functools.partial(_diff_attn_kernel, tq=tq, lam_init=lam_init), "diff_attn",
                      2 * BLOCKS_PER_STEP, LANES, (qd, kd, vdt, lam_params, gnorm_col),
                      [row_blk, row_blk, vt_blk, small(lam_params), small(gnorm_col)], b, s, w, tq)


def _fox_attn_kernel(row_tab, k_tab, q_ref, k_ref, vt_ref, qb_ref, kb_ref, o_ref, *scratch, tq):
    nq = q_ref.shape[1] // tq

    def logits(row, tile, diagonal):
        qs = []
        for q, qb in zip(_head_blocks(_tile_rows(q_ref, row, tq)), _head_blocks(_tile_rows(qb_ref, row, tq))):
            lane = lax.broadcasted_iota(jnp.int32, q.shape, 1)
            zero = jnp.zeros_like(q)
            qs += [jnp.concatenate([jnp.where(lane < HEAD_DIM, q, zero), jnp.where(lane < N_BIAS, qb, zero)],
                                   axis=1),
                   jnp.concatenate([jnp.where(lane >= HEAD_DIM, q, zero),
                                    jnp.where((lane >= N_BIAS) & (lane < 2 * N_BIAS), qb, zero)], axis=1)]
        kts = [jnp.concatenate([k, kb], axis=1) for k, kb in zip(_head_blocks(_tile_rows(k_ref, tile, tq)),
                                                                   _head_blocks(_tile_rows(kb_ref, tile, tq)))]
        return [_tile_logits(kts[a // 2], qs[a], diagonal) for a in range(len(qs))]

    def values(tile):
        vt = vt_ref[0, :, pl.ds(pl.multiple_of(tile * tq, tq), tq)]
        return [vt[a * HEAD_DIM:(a + 1) * HEAD_DIM] for a in range(2 * BLOCKS_PER_STEP)]

    key = lax.broadcasted_iota(jnp.int32, (tq, tq), 0)
    qry = lax.broadcasted_iota(jnp.int32, (tq, tq), 1)
    _flash_sweep(nq, tq, (row_tab, k_tab), logits, values, key <= qry, scratch)

    def finish(row, _):
        o = jnp.concatenate(_flash_outputs(scratch[-1], row, HEAD_DIM), axis=0)
        o_ref[0, pl.ds(pl.multiple_of(row * tq, tq), tq), :] = o.T.astype(o_ref.dtype)
        return 0

    lax.fori_loop(0, nq, finish, 0)


def _fox_attn(qf, kf, vft, qb, kb, *, tq):
    b, s, w = qf.shape
    row_blk, vt_blk = _attn_specs(s)
    return _attn_call(functools.partial(_fox_attn_kernel, tq=tq), "fox_attn", 2 * BLOCKS_PER_STEP, HEAD_DIM,
                      (qf, kf, vft, qb, kb), [row_blk, row_blk, vt_blk, row_blk, row_blk], b, s, w, tq)


def _route(lt):
    tm = lt.shape[1]
    g8 = lt[0:8]
    r8 = lax.broadcasted_iota(jnp.int32, (8, tm), 0)
    g8 = jnp.where(r8 < N_GROUPS, g8, NEG_BIG)
    gmax = jnp.max(g8, axis=0, keepdims=True)
    gidx = jnp.min(jnp.where(g8 == gmax, r8, N_GROUPS), axis=0, keepdims=True)
    g_w = 1.0 / jnp.sum(jnp.exp(g8 - gmax), axis=0, keepdims=True)

    e16 = lt[8:8 + N_EXPERTS]
    r16 = lax.broadcasted_iota(jnp.int32, (N_EXPERTS, tm), 0)
    in_group = (r16 // EXPERTS_PER_GROUP) == gidx
    e_sel = jnp.where(in_group, e16, NEG_BIG)
    top1 = jnp.max(e_sel, axis=0, keepdims=True)
    id1 = jnp.min(jnp.where(e_sel == top1, r16, N_EXPERTS), axis=0, keepdims=True)
    e_rest = jnp.where(r16 == id1, NEG_BIG, e_sel)
    top2 = jnp.max(e_rest, axis=0, keepdims=True)
    id2 = jnp.min(jnp.where(e_rest == top2, r16, N_EXPERTS), axis=0, keepdims=True)
    t = jnp.exp(top2 - top1)
    w1 = g_w / (1.0 + t)
    w2 = w1 * t
    r128 = lax.broadcasted_iota(jnp.int32, (LANES, tm), 0)
    return (jnp.where(r128 == id1, w1, 0.0) + jnp.where(r128 == id2, w2, 0.0)
            + jnp.where(r128 == GID_LANE, gidx.astype(F32), 0.0))


def _out_proj_kernel(x_ref, od_ref, of_ref, wo_ref, g_ref, wr_cat_ref, br_ref,
                     x1_ref, hm_ref, comb_ref):
    tm = x_ref.shape[0]
    halves = [slice(k * tm // OUT_PROJ_SPLIT, (k + 1) * tm // OUT_PROJ_SPLIT) for k in range(OUT_PROJ_SPLIT)]
    x1s = []
    for rows in halves:
        x1 = x_ref[rows, :] + (_dot(od_ref[rows, :], wo_ref[0]) + _dot(of_ref[rows, :], wo_ref[1]))
        x1_ref[rows, :] = x1
        x1s.append(x1)
    wr_cat = wr_cat_ref[...]
    for rows, x1 in zip(halves, x1s):
        hb, h_lo = _split2(_rms(x1, g_ref[...]))
        hm_ref[rows, :] = hb
        both = _dot_nt(wr_cat, hb)
        lt = ((both[:LANES] + both[LANES:]) + _dot_nt(wr_cat[:LANES], h_lo)) + br_ref[...]
        comb_ref[rows, :] = _route(lt).T


def _out_proj(x2, od, of, wo, g, wr_cat, br, *, tm):
    n, d = x2.shape
    const = lambda *shape: pl.BlockSpec(shape, lambda i: (0,) * len(shape))
    row_blk = lambda cols: pl.BlockSpec((tm, cols), lambda i: (i, 0))
    return pl.pallas_call(
        _out_proj_kernel,
        grid=(n // tm,),
        in_specs=[row_blk(d), row_blk(od.shape[1]), row_blk(of.shape[1]), const(*wo.shape), const(1, d),
                  const(*wr_cat.shape), const(*br.shape)],
        out_specs=[row_blk(d), row_blk(d), row_blk(LANES)],
        out_shape=[jax.ShapeDtypeStruct((n, d), F32), jax.ShapeDtypeStruct((n, d), BF16),
                   jax.ShapeDtypeStruct((n, LANES), F32)],
        compiler_params=pltpu.CompilerParams(dimension_semantics=("arbitrary",),
                                             vmem_limit_bytes=VMEM_LIMIT),
        name="out_proj",
    )(x2, od, of, wo, g, wr_cat, br)


def _moe_kernel(hm_ref, comb_ref, x1_hbm, wg_ref, wu_ref, wd_ref, gf_ref, o_hbm,
                tri_ref, hs_ref, cs_ref, posc_ref, tab_ref, xbuf_ref, obuf_ref, sem_ref, *, apply_final):
    g = pl.program_id(1)
    k = pl.program_id(2)
    t = hm_ref.shape[0]
    t_pad = hs_ref.shape[1]
    rb = ROW_BLOCK

    @pl.when((pl.program_id(0) == 0) & (g == 0) & (k == 0))
    def _():
        row = lax.broadcasted_iota(jnp.int32, (t, t), 0)
        col = lax.broadcasted_iota(jnp.int32, (t, t), 1)
        tri_ref[...] = jnp.where(row > col, 1.0, 0.0).astype(BF16)

    @pl.when(g == 0)
    def _():
        comb = comb_ref[...]
        lane = lax.broadcasted_iota(jnp.int32, comb.shape, 1)
        onehot_c = jnp.where(lane == comb[:, GID_LANE:GID_LANE + 1].astype(jnp.int32), 1.0, 0.0)
        rank_c = _dot(tri_ref[...], onehot_c.astype(BF16))
        counts = jnp.sum(onehot_c, axis=0, keepdims=True)
        lane1 = lax.broadcasted_iota(jnp.int32, (1, LANES), 1)
        start_c = jnp.zeros((1, LANES), F32)
        first = jnp.int32(0)
        for grp in range(N_GROUPS):
            cnt = jnp.sum(jnp.where(lane1 == grp, counts, 0.0)).astype(jnp.int32)
            nblk = lax.shift_right_logical(cnt + (rb - 1), int(np.log2(rb)))
            tab_ref[k, grp] = first
            tab_ref[k, N_GROUPS + grp] = nblk
            start_c = jnp.where(lane1 == grp, (first * rb).astype(F32), start_c)
            first = first + nblk
        dest = onehot_c * (rank_c + start_c)
        posc_ref[k] = jnp.sum(dest, axis=1, keepdims=True).astype(jnp.int32)
        pos_r = jnp.sum(dest.T, axis=0, keepdims=True).astype(jnp.int32)
        c_parts = _split3(jnp.where(lane < N_EXPERTS, comb, 0.0))
        c_pack = (c_parts[0].astype(F32) + pltpu.roll(c_parts[1].astype(F32), N_EXPERTS, axis=1)
                  + pltpu.roll(c_parts[2].astype(F32), 2 * N_EXPERTS, axis=1)).astype(BF16)
        hm = hm_ref[...]
        for r0 in range(0, t_pad, SORT_ROWS):
            dst = lax.broadcasted_iota(jnp.int32, (SORT_ROWS, t), 0) + r0
            perm = jnp.where(dst == pos_r, 1.0, 0.0).astype(BF16)
            hs_ref[k, r0:r0 + SORT_ROWS, :] = _dot(perm, hm).astype(hs_ref.dtype)
            cp = _dot(perm, c_pack)
            cs_ref[k, r0:r0 + SORT_ROWS, :] = (cp + pltpu.roll(cp, LANES - N_EXPERTS, axis=1)
                                               + pltpu.roll(cp, LANES - 2 * N_EXPERTS, axis=1))

    b0 = tab_ref[k, g]

    def expert_block(b, _):
        off = pl.multiple_of(b * rb, rb)
        rows = hs_ref[k, pl.ds(off, rb), :]
        cblk = cs_ref[k, pl.ds(off, rb), :]
        lane = lax.broadcasted_iota(jnp.int32, cblk.shape, 1)
        gates = [_dot(rows, wg_ref[j]) for j in range(EXPERTS_PER_GROUP)]
        ups = [_dot(rows, wu_ref[j]) for j in range(EXPERTS_PER_GROUP)]
        y = None
        for j in range(EXPERTS_PER_GROUP):
            c = jnp.sum(jnp.where(lane == g * EXPERTS_PER_GROUP + j, cblk, 0.0), axis=1, keepdims=True)
            a = gates[j] * jax.nn.sigmoid(gates[j]) * ups[j] * c
            d = _dot(a.astype(BF16), wd_ref[j])
            y = d if y is None else y + d
        hs_ref[k, pl.ds(off, rb), :] = y.astype(hs_ref.dtype)
        return 0

    last_group = g == pl.num_programs(1) - 1
    tile = pl.program_id(0) * pl.num_programs(2) + k
    n_tiles = pl.num_programs(0) * pl.num_programs(2)

    def x1_copy():
        return pltpu.make_async_copy(x1_hbm.at[pl.ds(pl.multiple_of(tile * t, t), t), :], xbuf_ref, sem_ref.at[0])

    def out_copy(which):
        return pltpu.make_async_copy(obuf_ref, o_hbm.at[pl.ds(pl.multiple_of(which * t, t), t), :], sem_ref.at[1])

    @pl.when(last_group)
    def _():
        x1_copy().start()

    lax.fori_loop(b0, b0 + tab_ref[k, N_GROUPS + g], expert_block, 0)

    @pl.when(last_group)
    def _():
        x1_copy().wait()

        @pl.when(tile > 0)
        def _():
            out_copy(tile - 1).wait()

        ys = hs_ref[k]
        for r0 in range(0, t, SORT_ROWS):
            src = lax.broadcasted_iota(jnp.int32, (SORT_ROWS, t_pad), 1)
            unperm = jnp.where(src == posc_ref[k, r0:r0 + SORT_ROWS, :], 1.0, 0.0).astype(BF16)
            x2 = xbuf_ref[r0:r0 + SORT_ROWS, :] + _dot(unperm, ys)
            obuf_ref[r0:r0 + SORT_ROWS, :] = _rms(x2, gf_ref[...]) if apply_final else x2
        out_copy(tile).start()

        @pl.when(tile == n_tiles - 1)
        def _():
            out_copy(tile).wait()


def _moe(hm, comb, x1, wg, wu, wd, gfin, *, tm, apply_final):
    n, d = x1.shape
    ne, _, de = wg.shape
    epg = EXPERTS_PER_GROUP
    n_groups = ne // epg
    t_pad = tm + N_GROUPS * ROW_BLOCK
    assert t_pad % SORT_ROWS == 0 and tm % SORT_ROWS == 0 and n % (MOE_TILES * tm) == 0
    first_blk = lambda cols: pl.BlockSpec(
        (tm, cols), lambda p, g, k: (MOE_TILES * p + jnp.where(g == 0, k, MOE_TILES - 1), 0))
    w_blk = lambda *shape: pl.BlockSpec((epg,) + shape, lambda p, g, k: (g, 0, 0))
    hbm = pl.BlockSpec(memory_space=pl.ANY)
    return pl.pallas_call(
        functools.partial(_moe_kernel, apply_final=apply_final),
        grid=(n // (MOE_TILES * tm), n_groups, MOE_TILES),
        in_specs=[first_blk(d), first_blk(LANES), hbm, w_blk(d, de), w_blk(d, de), w_blk(de, d),
                  pl.BlockSpec((1, d), lambda p, g, k: (0, 0))],
        out_specs=hbm,
        out_shape=jax.ShapeDtypeStruct((n, d), F32),
        scratch_shapes=[pltpu.VMEM((tm, tm), BF16), pltpu.VMEM((MOE_TILES, t_pad, d), BF16),
                        pltpu.VMEM((MOE_TILES, t_pad, LANES), F32), pltpu.VMEM((MOE_TILES, tm, 1), jnp.int32),
                        pltpu.SMEM((MOE_TILES, 2 * N_GROUPS), jnp.int32),
                        pltpu.VMEM((tm, d), F32), pltpu.VMEM((tm, d), F32), pltpu.SemaphoreType.DMA((2,))],
        compiler_params=pltpu.CompilerParams(dimension_semantics=("arbitrary",) * 3,
                                             vmem_limit_bytes=VMEM_LIMIT),
        name="moe",
    )(hm, comb, x1, wg, wu, wd, gfin)


def kernel(x, norm_attn_g, w_in, b_forget, lambda_q1, lambda_k1, lambda_q2, lambda_k2, diff_norm_g, w_out,
           norm_ffn_g, router_group_w, router_group_b, router_expert_w, router_expert_b, w_gate, w_up, w_down,
           norm_final_g):
    b, s, d = x.shape
    depth = w_in.shape[0]
    n = b * s
    diff_w = N_DIFF_HEADS * 2 * HEAD_DIM
    fox_w = N_FOX_HEADS * HEAD_DIM

    inv_freq = 1.0 / (ROPE_THETA ** (np.arange(0, HEAD_DIM, 2, dtype=np.float64) / HEAD_DIM))
    ang = np.tile(np.arange(s, dtype=np.float64)[:, None] * inv_freq[None, :], (1, 2 * LANES // HEAD_DIM))
    first_half = (np.arange(LANES) % HEAD_DIM) < HEAD_DIM // 2
    cos2 = jnp.asarray(np.cos(ang), F32)
    sin2 = jnp.asarray(np.where(first_half, -np.sin(ang), np.sin(ang)), F32)

    x2 = x.reshape(n, d)
    for l in range(depth):
        lam_init = 0.8 - 0.6 * float(np.exp(-0.3 * l))
        offs = [int(o) for o in np.cumsum([0, diff_w, diff_w, diff_w, fox_w, fox_w, fox_w])]
        assert diff_w == fox_w
        col_offsets = ((offs[0], offs[1], offs[3], offs[4]), (offs[2], offs[5]))
        wf = jnp.pad(w_in[l][:, offs[6]:], ((0, 0), (0, LANES - N_FOX_HEADS)))
        wf_cat = jnp.concatenate(_split2(wf), axis=1)
        bf_pad = jnp.pad(b_forget[l], (0, LANES - N_FOX_HEADS)).reshape(1, LANES)

        qd, kd, vdt, qf, kf, vft, qb, kb, wg_bf, wu_bf, wd_bf = _in_proj(
            x2, norm_attn_g[l].reshape(1, d), cos2, sin2, w_in, col_offsets, diff_w, wf_cat, bf_pad,
            (w_gate, w_up, w_down), layer=l, batch=b, seq=s, tm=IN_PROJ_ROWS)

        to3 = lambda t: t.reshape(b, s, t.shape[-1])
        lam_params = jnp.stack([lambda_q1[l], lambda_k1[l], lambda_q2[l], lambda_k2[l]])
        od = _diff_attn(to3(qd), to3(kd), vdt, lam_params, diff_norm_g[l].reshape(-1, 1),
                        tq=ATTN_TILE, lam_init=lam_init)
        of = _fox_attn(to3(qf), to3(kf), vft, to3(qb), to3(kb), tq=ATTN_TILE)

        wo = w_out[l].astype(BF16).reshape(2, -1, d)
        wr = jnp.zeros((d, LANES), F32)
        wr = wr.at[:, :N_GROUPS].set(router_group_w[l]).at[:, 8:8 + N_EXPERTS].set(router_expert_w[l])
        wr_cat = jnp.concatenate(_split2(wr.T), axis=0)
        br = jnp.zeros((LANES,), F32)
        br = br.at[:N_GROUPS].set(router_group_b[l]).at[8:8 + N_EXPERTS].set(router_expert_b[l]).reshape(LANES, 1)
        x1, hm, comb = _out_proj(x2, od.reshape(n, -1), of.reshape(n, -1), wo, norm_ffn_g[l].reshape(1, d),
                                 wr_cat, br, tm=OUT_PROJ_ROWS)

        x2 = _moe(hm, comb, x1, wg_bf, wu_bf, wd_bf, norm_final_g.reshape(1, d), tm=MOE_ROWS, apply_final=(l == depth - 1))
    return x2.reshape(b, s, d)
```

```python
import functools

import numpy as np
import jax
import jax.numpy as jnp
from jax import lax
from jax.experimental import pallas as pl
from jax.experimental.pallas import tpu as pltpu

CHUNK = 64
HEAD_DIM = 64
ROPE_THETA = 10000.0
EPS = 1e-6
LOG2E = 1.4426950408889634
N_DIFF_HEADS = 4
N_FOX_HEADS = 8
N_GROUPS = 4
EXPERTS_PER_GROUP = 4
N_EXPERTS = N_GROUPS * EXPERTS_PER_GROUP

LANES = 128
NEG_BIG = -1e30
VMEM_LIMIT = 56 * 1024 * 1024
IN_PROJ_ROWS = 512
ATTN_TILE = 512
OUT_PROJ_ROWS = 1024
MOE_ROWS = 1024
N_BIAS = 6
SUBLANES = 8
GID_LANE = N_EXPERTS
ROW_BLOCK = 128
ONES_ROWS = 16
BLOCKS_PER_STEP = 2
MOE_TILES = 2
OUT_PROJ_SPLIT = 2
SORT_ROWS = 512

BF16 = jnp.bfloat16
F32 = jnp.float32
_NT = (((1,), (1,)), ((), ()))


def _dot(a, b):
    return jnp.dot(a, b, preferred_element_type=F32)


def _dot_nt(a, b):
    return lax.dot_general(a, b, _NT, preferred_element_type=F32)


def _split2(x):
    hi = x.astype(BF16)
    lo = (x - hi.astype(F32)).astype(BF16)
    return hi, lo


def _split3(x):
    hi = x.astype(BF16)
    r = x - hi.astype(F32)
    mid = r.astype(BF16)
    lo = (r - mid.astype(F32)).astype(BF16)
    return hi, mid, lo


def _rms(x, g):
    return x * lax.rsqrt(jnp.mean(x * x, axis=-1, keepdims=True) + EPS) * g


def _pack_parts(parts):
    packed = parts[0].astype(F32)
    for k in (1, 2):
        packed = packed + pltpu.roll(parts[k].astype(F32), k * N_FOX_HEADS, axis=1)
    return packed.astype(BF16)


def _in_proj_kernel(x_ref, g_ref, cos_ref, sin_ref, win_ref, wf_cat_ref, bf_ref, tri_ref,
                    eq_ref, ek_ref, cq_ref, ck_ref, ew0_ref, ew1_ref, ew2_ref,
                    qd_ref, kd_ref, vdt_ref, qf_ref, kf_ref, vft_ref, qb_ref, kb_ref, eb0_ref, eb1_ref, eb2_ref,
                    carry_ref, w_ref, wt_ref, *, tiles_per_seq, col_offsets):
    i = pl.program_id(0)

    @pl.when(i == 0)
    def _():
        width = w_ref.shape[2]
        for dst, off in enumerate(col_offsets[0]):
            w_ref[dst] = win_ref[0, :, off:off + width].astype(w_ref.dtype)
        for dst, off in enumerate(col_offsets[1]):
            wt_ref[dst] = win_ref[0, :, off:off + width].T.astype(wt_ref.dtype)

    for src, dst in ((ew0_ref, eb0_ref), (ew1_ref, eb1_ref), (ew2_ref, eb2_ref)):
        dst[0] = src[0, 0].astype(dst.dtype)
    h = _rms(x_ref[...], g_ref[...])
    hb = h.astype(BF16)
    cos = cos_ref[...]
    sin = sin_ref[...]
    scale = HEAD_DIM ** -0.5 * LOG2E
    lane = lax.broadcasted_iota(jnp.int32, cos.shape, 1)
    first_half = (lane % HEAD_DIM) < HEAD_DIM // 2

    def rope(w_idx, out_ref, mul):
        a = _dot(hb, w_ref[w_idx])
        for c in range(a.shape[1] // LANES):
            sl = slice(c * LANES, (c + 1) * LANES)
            x = a[:, sl]
            rot = jnp.where(first_half, pltpu.roll(x, LANES - HEAD_DIM // 2, axis=1),
                            pltpu.roll(x, HEAD_DIM // 2, axis=1))
            out_ref[:, sl] = ((x * cos + rot * sin) * mul).astype(out_ref.dtype)

    @pl.when(i % tiles_per_seq == 0)
    def _():
        carry_ref[...] = jnp.zeros_like(carry_ref)


    zz = _dot(hb, wf_cat_ref[...])
    z = (zz[:, :LANES] + zz[:, LANES:]) + bf_ref[...]
    log_f = jnp.minimum(z, 0.0) - jnp.log1p(jnp.exp(-jnp.abs(z)))
    valid = lane < N_FOX_HEADS
    log_f = jnp.where(valid, log_f, 0.0)

    rope(0, qd_ref, scale)

    r = _dot(tri_ref[...], _pack_parts(_split3(log_f)))
    cum = (r + pltpu.roll(r, LANES - N_FOX_HEADS, axis=1)) + pltpu.roll(r, LANES - 2 * N_FOX_HEADS, axis=1)
    cum = jnp.where(valid, cum + carry_ref[...], 0.0)
    carry_ref[...] = cum[cum.shape[0] - 1:, :]

    rope(1, kd_ref, 1.0)

    f_pack = _pack_parts(_split3(cum * LOG2E))
    qb_ref[...] = (cq_ref[...] + _dot(f_pack, eq_ref[...])).astype(qb_ref.dtype)
    kb_ref[...] = (ck_ref[...] + _dot(f_pack, ek_ref[...])).astype(kb_ref.dtype)

    qf_ref[...] = (_dot(hb, w_ref[2]) * scale).astype(qf_ref.dtype)
    kf_ref[...] = _dot(hb, w_ref[3]).astype(kf_ref.dtype)
    vdt_ref[0] = _dot_nt(wt_ref[0], hb).astype(vdt_ref.dtype)
    vft_ref[0] = _dot_nt(wt_ref[1], hb).astype(vft_ref.dtype)


def _bias_placement():
    width = (N_FOX_HEADS // 2) * LANES
    eq = np.zeros((LANES, width), np.float32)
    ek = np.zeros((LANES, width), np.float32)
    cq = np.zeros((1, width), np.float32)
    ck = np.zeros((1, width), np.float32)
    for h in range(N_FOX_HEADS):
        base = (h // 2) * LANES + (h % 2) * N_BIAS
        for part in range(3):
            eq[part * N_FOX_HEADS + h, base + part] = 1.0
            ek[part * N_FOX_HEADS + h, base + 3 + part] = -1.0
        cq[0, base + 3:base + 6] = 1.0
        ck[0, base:base + 3] = 1.0
    return jnp.asarray(eq, BF16), jnp.asarray(ek, BF16), jnp.asarray(cq), jnp.asarray(ck)


def _in_proj(x2, g, cos2, sin2, w_in, col_offsets, wcols, wf_cat, bf_pad, expert_ws, *, layer, batch, seq, tm):
    n, d = x2.shape
    tiles_per_seq = seq // tm
    steps = n // tm
    ew = list(expert_ws)
    per_expert = steps // ew[0].shape[1]
    assert per_expert * ew[0].shape[1] == steps and all(w.shape[2] % (per_expert * 8) == 0 for w in ew)
    ew_blk = [pl.BlockSpec((1, 1, w.shape[2] // per_expert, w.shape[3]),
                           lambda i: (layer, i // per_expert, i % per_expert, 0)) for w in ew]
    eb_blk = [pl.BlockSpec((1, w.shape[2] // per_expert, w.shape[3]),
                           lambda i: (i // per_expert, i % per_expert, 0)) for w in ew]
    tri = jnp.tril(jnp.ones((tm, tm), F32)).astype(BF16)
    eq, ek, cq, ck = _bias_placement()
    const = lambda *shape: pl.BlockSpec(shape, lambda i: (0,) * len(shape))
    row_blk = lambda cols: pl.BlockSpec((tm, cols), lambda i: (i, 0))
    pos_blk = pl.BlockSpec((tm, LANES), lambda i: (i % tiles_per_seq, 0))
    vt_blk = pl.BlockSpec((1, wcols, tm), lambda i: (i // tiles_per_seq, 0, i % tiles_per_seq))
    out_bf = jax.ShapeDtypeStruct((n, wcols), BF16)
    out_vt = jax.ShapeDtypeStruct((batch, wcols, seq), BF16)
    return pl.pallas_call(
        functools.partial(_in_proj_kernel, tiles_per_seq=tiles_per_seq, col_offsets=col_offsets),
        grid=(n // tm,),
        in_specs=[row_blk(d), const(1, d), pos_blk, pos_blk,
                  pl.BlockSpec((1,) + w_in.shape[1:], lambda i: (layer, 0, 0), pipeline_mode=pl.Buffered(1)),
                  const(*wf_cat.shape), const(1, LANES), const(tm, tm),
                  const(*eq.shape), const(*ek.shape), const(*cq.shape), const(*ck.shape)] + ew_blk,
        out_specs=[row_blk(wcols), row_blk(wcols), vt_blk, row_blk(wcols), row_blk(wcols), vt_blk,
                   row_blk(eq.shape[1]), row_blk(ek.shape[1])] + eb_blk,
        out_shape=[out_bf, out_bf, out_vt, out_bf, out_bf, out_vt,
                   jax.ShapeDtypeStruct((n, eq.shape[1]), BF16), jax.ShapeDtypeStruct((n, ek.shape[1]), BF16)]
        + [jax.ShapeDtypeStruct(w.shape[1:], BF16) for w in ew],
        scratch_shapes=[pltpu.VMEM((1, LANES), F32), pltpu.VMEM((len(col_offsets[0]), d, wcols), BF16),
                        pltpu.VMEM((len(col_offsets[1]), wcols, d), BF16)],
        compiler_params=pltpu.CompilerParams(dimension_semantics=("arbitrary",),
                                             vmem_limit_bytes=VMEM_LIMIT),
        name="in_proj",
    )(x2, g, cos2, sin2, w_in, wf_cat, bf_pad, tri, eq, ek, cq, ck, *ew)


def _flash_sweep(nq, tq, tabs, logits_fn, values_fn, mask, scratch):
    s0, s1, c0, c1, m_ref, acc_ref = scratch
    row_tab, k_tab = tabs
    n_maps = m_ref.shape[1]
    n_off = nq * (nq - 1) // 2
    ones = jnp.ones((ONES_ROWS, tq), BF16)

    def stage_a(row, tile, s_buf, c_buf, masked):
        sts = logits_fn(row, tile, masked)
        for a in range(n_maps):
            st = jnp.where(mask, sts[a], NEG_BIG) if masked else sts[a]
            s_buf[a] = st
            c_buf[a] = jnp.max(st, axis=0, keepdims=True)

    def stage_b(row, tile, s_buf, c_buf, first):
        vts = values_fn(tile)
        for a in range(n_maps):
            vt = jnp.concatenate([vts[a], ones], axis=0)
            if first:
                m_new = c_buf[a]
                half = tq // 2
                p_left = jnp.exp2(s_buf[a, :half, :half] - m_new[:, :half]).astype(BF16)
                p_right = jnp.exp2(s_buf[a, :, half:] - m_new[:, half:]).astype(BF16)
                acc_ref[row, a] = jnp.concatenate([_dot(vt[:, :half], p_left), _dot(vt, p_right)], axis=1)
            else:
                m = m_ref[row, a]
                m_new = jnp.maximum(m, c_buf[a])
                alpha = jnp.exp2(m - m_new)
                acc_ref[row, a] = alpha * acc_ref[row, a] + _dot(vt, jnp.exp2(s_buf[a] - m_new).astype(BF16))
            m_ref[row, a] = m_new

    stage_a(0, 0, s0, c0, True)

    def diag_pair(jj, _):
        r = 2 * jj
        stage_a(r + 1, r + 1, s1, c1, True)
        stage_b(r, r, s0, c0, True)
        stage_a(r + 2, r + 2, s0, c0, True)
        stage_b(r + 1, r + 1, s1, c1, True)
        return 0

    lax.fori_loop(0, (nq - 2) // 2, diag_pair, 0)
    stage_a(nq - 1, nq - 1, s1, c1, True)
    stage_b(nq - 2, nq - 2, s0, c0, True)

    stage_a(row_tab[0], k_tab[0], s0, c0, False)
    stage_b(nq - 1, nq - 1, s1, c1, True)

    def off_pair(jj, _):
        t = 2 * jj
        stage_a(row_tab[t + 1], k_tab[t + 1], s1, c1, False)
        stage_b(row_tab[t], k_tab[t], s0, c0, False)
        stage_a(row_tab[t + 2], k_tab[t + 2], s0, c0, False)
        stage_b(row_tab[t + 1], k_tab[t + 1], s1, c1, False)
        return 0

    lax.fori_loop(0, (n_off - 2) // 2, off_pair, 0)
    stage_a(row_tab[n_off - 1], k_tab[n_off - 1], s1, c1, False)
    stage_b(row_tab[n_off - 2], k_tab[n_off - 2], s0, c0, False)
    stage_b(row_tab[n_off - 1], k_tab[n_off - 1], s1, c1, False)


def _flash_scratch(nq, n_maps, dv, tq):
    s_buf = pltpu.VMEM((n_maps, tq, tq), F32)
    c_buf = pltpu.VMEM((n_maps, 1, tq), F32)
    return [s_buf, s_buf, c_buf, c_buf, pltpu.VMEM((nq, n_maps, 1, tq), F32),
            pltpu.VMEM((nq, n_maps, dv + ONES_ROWS, tq), F32)]


def _off_diagonal_tables(nq):
    pairs = [(r, k) for k in range(nq) for r in range(k + 1, nq)]
    return (jnp.asarray([p[0] for p in pairs], jnp.int32), jnp.asarray([p[1] for p in pairs], jnp.int32))


def _flash_outputs(acc_ref, row, dv):
    return [acc_ref[row, a, :dv, :] / acc_ref[row, a, dv:dv + 1, :] for a in range(acc_ref.shape[1])]


def _head_blocks(block):
    return [block[:, k * LANES:(k + 1) * LANES] for k in range(BLOCKS_PER_STEP)]


def _tile_logits(k, q, diagonal):
    if not diagonal:
        return _dot_nt(k, q)
    half = q.shape[0] // 2
    assert half % CHUNK == 0
    left = jnp.concatenate([_dot_nt(k[:half], q[:half]), jnp.full((k.shape[0] - half, half), NEG_BIG, F32)],
                           axis=0)
    return jnp.concatenate([left, _dot_nt(k, q[half:])], axis=1)


def _tile_rows(ref, tile, tq):
    return ref[0, pl.ds(pl.multiple_of(tile * tq, tq), tq), :]


def _diff_attn_kernel(row_tab, k_tab, q_ref, k_ref, vt_ref, lam_ref, g_ref, o_ref, *scratch, tq, lam_init):
    nq = q_ref.shape[1] // tq

    def logits(row, tile, diagonal):
        qs = []
        for q in _head_blocks(_tile_rows(q_ref, row, tq)):
            lane = lax.broadcasted_iota(jnp.int32, q.shape, 1)
            zero = jnp.zeros_like(q)
            qs += [jnp.where(lane < HEAD_DIM, q, zero), jnp.where(lane >= HEAD_DIM, q, zero)]
        kts = _head_blocks(_tile_rows(k_ref, tile, tq))
        return [_tile_logits(kts[a // 2], qs[a], diagonal) for a in range(len(qs))]

    def values(tile):
        vt = vt_ref[0, :, pl.ds(pl.multiple_of(tile * tq, tq), tq)]
        return [vt[(a // 2) * LANES:(a // 2 + 1) * LANES] for a in range(2 * BLOCKS_PER_STEP)]

    key = lax.broadcasted_iota(jnp.int32, (tq, tq), 0)
    qry = lax.broadcasted_iota(jnp.int32, (tq, tq), 1)
    _flash_sweep(nq, tq, (row_tab, k_tab), logits, values, (key // CHUNK) <= (qry // CHUNK), scratch)

    lp = lam_ref[...]
    lam = (jnp.exp(jnp.sum(lp[0:1] * lp[1:2], axis=1, keepdims=True))
           - jnp.exp(jnp.sum(lp[2:3] * lp[3:4], axis=1, keepdims=True)) + lam_init)

    def finish(row, _):
        outs = _flash_outputs(scratch[-1], row, LANES)
        ys = []
        for k in range(BLOCKS_PER_STEP):
            o = outs[2 * k] - lam * outs[2 * k + 1]
            y = o * lax.rsqrt(jnp.mean(o * o, axis=0, keepdims=True) + EPS) * g_ref[...] * (1.0 - lam_init)
            ys.append(y.T.astype(o_ref.dtype))
        o_ref[0, pl.ds(pl.multiple_of(row * tq, tq), tq), :] = jnp.concatenate(ys, axis=1)
        return 0

    lax.fori_loop(0, nq, finish, 0)


def _attn_call(kernel_fn, name, n_maps, dv, inputs, in_specs, b, s, w, tq):
    nq = s // tq
    assert nq % 2 == 0 and nq >= 4, "the pipelined sweeps handle tiles in pairs"
    wb = BLOCKS_PER_STEP * LANES
    return pl.pallas_call(
        kernel_fn,
        grid_spec=pltpu.PrefetchScalarGridSpec(
            num_scalar_prefetch=2,
            grid=(b, w // wb),
            in_specs=in_specs,
            out_specs=pl.BlockSpec((1, s, wb), lambda bi, h, *_: (bi, 0, h)),
            scratch_shapes=_flash_scratch(nq, n_maps, dv, tq)),
        out_shape=jax.ShapeDtypeStruct((b, s, w), BF16),
        compiler_params=pltpu.CompilerParams(dimension_semantics=("arbitrary",) * 2,
                                             vmem_limit_bytes=VMEM_LIMIT),
        name=name,
    )(*_off_diagonal_tables(nq), *inputs)


def _attn_specs(s):
    wb = BLOCKS_PER_STEP * LANES
    row_blk = pl.BlockSpec((1, s, wb), lambda bi, h, *_: (bi, 0, h))
    vt_blk = pl.BlockSpec((1, wb, s), lambda bi, h, *_: (bi, h, 0))
    return row_blk, vt_blk


def _diff_attn(qd, kd, vdt, lam_params, gnorm_col, *, tq, lam_init):
    b, s, w = qd.shape
    row_blk, vt_blk = _attn_specs(s)
    small = lambda arr: pl.BlockSpec(arr.shape, lambda bi, h, *_: (0, 0))
    return _attn_call(functools.partial(_diff_attn_kernel, tq=tq, lam_init=lam_init), "diff_attn",
                      2 * BLOCKS_PER_STEP, LANES, (qd, kd, vdt, lam_params, gnorm_col),
                      [row_blk, row_blk, vt_blk, small(lam_params), small(gnorm_col)], b, s, w, tq)


def _fox_attn_kernel(row_tab, k_tab, q_ref, k_ref, vt_ref, qb_ref, kb_ref, o_ref, *scratch, tq):
    nq = q_ref.shape[1] // tq

    def logits(row, tile, diagonal):
        qs = []
        for q, qb in zip(_head_blocks(_tile_rows(q_ref, row, tq)), _head_blocks(_tile_rows(qb_ref, row, tq))):
            lane = lax.broadcasted_iota(jnp.int32, q.shape, 1)
            zero = jnp.zeros_like(q)
            qs += [jnp.concatenate([jnp.where(lane < HEAD_DIM, q, zero), jnp.where(lane < N_BIAS, qb, zero)],
                                   axis=1),
                   jnp.concatenate([jnp.where(lane >= HEAD_DIM, q, zero),
                                    jnp.where((lane >= N_BIAS) & (lane < 2 * N_BIAS), qb, zero)], axis=1)]
        kts = [jnp.concatenate([k, kb], axis=1) for k, kb in zip(_head_blocks(_tile_rows(k_ref, tile, tq)),
                                                                   _head_blocks(_tile_rows(kb_ref, tile, tq)))]
        return [_tile_logits(kts[a // 2], qs[a], diagonal) for a in range(len(qs))]

    def values(tile):
        vt = vt_ref[0, :, pl.ds(pl.multiple_of(tile * tq, tq), tq)]
        return [vt[a * HEAD_DIM:(a + 1) * HEAD_DIM] for a in range(2 * BLOCKS_PER_STEP)]

    key = lax.broadcasted_iota(jnp.int32, (tq, tq), 0)
    qry = lax.broadcasted_iota(jnp.int32, (tq, tq), 1)
    _flash_sweep(nq, tq, (row_tab, k_tab), logits, values, key <= qry, scratch)

    def finish(row, _):
        o = jnp.concatenate(_flash_outputs(scratch[-1], row, HEAD_DIM), axis=0)
        o_ref[0, pl.ds(pl.multiple_of(row * tq, tq), tq), :] = o.T.astype(o_ref.dtype)
        return 0

    lax.fori_loop(0, nq, finish, 0)


def _fox_attn(qf, kf, vft, qb, kb, *, tq):
    b, s, w = qf.shape
    row_blk, vt_blk = _attn_specs(s)
    return _attn_call(functools.partial(_fox_attn_kernel, tq=tq), "fox_attn", 2 * BLOCKS_PER_STEP, HEAD_DIM,
                      (qf, kf, vft, qb, kb), [row_blk, row_blk, vt_blk, row_blk, row_blk], b, s, w, tq)


def _route(lt):
    tm = lt.shape[1]
    g8 = lt[0:8]
    r8 = lax.broadcasted_iota(jnp.int32, (8, tm), 0)
    g8 = jnp.where(r8 < N_GROUPS, g8, NEG_BIG)
    gmax = jnp.max(g8, axis=0, keepdims=True)
    gidx = jnp.min(jnp.where(g8 == gmax, r8, N_GROUPS), axis=0, keepdims=True)
    g_w = 1.0 / jnp.sum(jnp.exp(g8 - gmax), axis=0, keepdims=True)

    e16 = lt[8:8 + N_EXPERTS]
    r16 = lax.broadcasted_iota(jnp.int32, (N_EXPERTS, tm), 0)
    in_group = (r16 // EXPERTS_PER_GROUP) == gidx
    e_sel = jnp.where(in_group, e16, NEG_BIG)
    top1 = jnp.max(e_sel, axis=0, keepdims=True)
    id1 = jnp.min(jnp.where(e_sel == top1, r16, N_EXPERTS), axis=0, keepdims=True)
    e_rest = jnp.where(r16 == id1, NEG_BIG, e_sel)
    top2 = jnp.max(e_rest, axis=0, keepdims=True)
    id2 = jnp.min(jnp.where(e_rest == top2, r16, N_EXPERTS), axis=0, keepdims=True)
    t = jnp.exp(top2 - top1)
    w1 = g_w / (1.0 + t)
    w2 = w1 * t
    r128 = lax.broadcasted_iota(jnp.int32, (LANES, tm), 0)
    return (jnp.where(r128 == id1, w1, 0.0) + jnp.where(r128 == id2, w2, 0.0)
            + jnp.where(r128 == GID_LANE, gidx.astype(F32), 0.0))


def _out_proj_kernel(x_ref, od_ref, of_ref, wo_ref, g_ref, wr_cat_ref, br_ref,
                     x1_ref, hm_ref, comb_ref):
    tm = x_ref.shape[0]
    halves = [slice(k * tm // OUT_PROJ_SPLIT, (k + 1) * tm // OUT_PROJ_SPLIT) for k in range(OUT_PROJ_SPLIT)]
    x1s = []
    for rows in halves:
        x1 = x_ref[rows, :] + (_dot(od_ref[rows, :], wo_ref[0]) + _dot(of_ref[rows, :], wo_ref[1]))
        x1_ref[rows, :] = x1
        x1s.append(x1)
    wr_cat = wr_cat_ref[...]
    for rows, x1 in zip(halves, x1s):
        hb, h_lo = _split2(_rms(x1, g_ref[...]))
        hm_ref[rows, :] = hb
        both = _dot_nt(wr_cat, hb)
        lt = ((both[:LANES] + both[LANES:]) + _dot_nt(wr_cat[:LANES], h_lo)) + br_ref[...]
        comb_ref[rows, :] = _route(lt).T


def _out_proj(x2, od, of, wo, g, wr_cat, br, *, tm):
    n, d = x2.shape
    const = lambda *shape: pl.BlockSpec(shape, lambda i: (0,) * len(shape))
    row_blk = lambda cols: pl.BlockSpec((tm, cols), lambda i: (i, 0))
    return pl.pallas_call(
        _out_proj_kernel,
        grid=(n // tm,),
        in_specs=[row_blk(d), row_blk(od.shape[1]), row_blk(of.shape[1]), const(*wo.shape), const(1, d),
                  const(*wr_cat.shape), const(*br.shape)],
        out_specs=[row_blk(d), row_blk(d), row_blk(LANES)],
        out_shape=[jax.ShapeDtypeStruct((n, d), F32), jax.ShapeDtypeStruct((n, d), BF16),
                   jax.ShapeDtypeStruct((n, LANES), F32)],
        compiler_params=pltpu.CompilerParams(dimension_semantics=("arbitrary",),
                                             vmem_limit_bytes=VMEM_LIMIT),
        name="out_proj",
    )(x2, od, of, wo, g, wr_cat, br)


def _moe_kernel(hm_ref, comb_ref, x1_hbm, wg_ref, wu_ref, wd_ref, gf_ref, o_hbm,
                tri_ref, hs_ref, cs_ref, posc_ref, tab_ref, xbuf_ref, obuf_ref, sem_ref, *, apply_final):
    g = pl.program_id(1)
    k = pl.program_id(2)
    t = hm_ref.shape[0]
    t_pad = hs_ref.shape[1]
    rb = ROW_BLOCK

    @pl.when((pl.program_id(0) == 0) & (g == 0) & (k == 0))
    def _():
        row = lax.broadcasted_iota(jnp.int32, (t, t), 0)
        col = lax.broadcasted_iota(jnp.int32, (t, t), 1)
        tri_ref[...] = jnp.where(row > col, 1.0, 0.0).astype(BF16)

    @pl.when(g == 0)
    def _():
        comb = comb_ref[...]
        lane = lax.broadcasted_iota(jnp.int32, comb.shape, 1)
        onehot_c = jnp.where(lane == comb[:, GID_LANE:GID_LANE + 1].astype(jnp.int32), 1.0, 0.0)
        rank_c = _dot(tri_ref[...], onehot_c.astype(BF16))
        counts = jnp.sum(onehot_c, axis=0, keepdims=True)
        lane1 = lax.broadcasted_iota(jnp.int32, (1, LANES), 1)
        start_c = jnp.zeros((1, LANES), F32)
        first = jnp.int32(0)
        for grp in range(N_GROUPS):
            cnt = jnp.sum(jnp.where(lane1 == grp, counts, 0.0)).astype(jnp.int32)
            nblk = lax.shift_right_logical(cnt + (rb - 1), int(np.log2(rb)))
            tab_ref[k, grp] = first
            tab_ref[k, N_GROUPS + grp] = nblk
            start_c = jnp.where(lane1 == grp, (first * rb).astype(F32), start_c)
            first = first + nblk
        dest = onehot_c * (rank_c + start_c)
        posc_ref[k] = jnp.sum(dest, axis=1, keepdims=True).astype(jnp.int32)
        pos_r = jnp.sum(dest.T, axis=0, keepdims=True).astype(jnp.int32)
        c_parts = _split3(jnp.where(lane < N_EXPERTS, comb, 0.0))
        c_pack = (c_parts[0].astype(F32) + pltpu.roll(c_parts[1].astype(F32), N_EXPERTS, axis=1)
                  + pltpu.roll(c_parts[2].astype(F32), 2 * N_EXPERTS, axis=1)).astype(BF16)
        hm = hm_ref[...]
        for r0 in range(0, t_pad, SORT_ROWS):
            dst = lax.broadcasted_iota(jnp.int32, (SORT_ROWS, t), 0) + r0
            perm = jnp.where(dst == pos_r, 1.0, 0.0).astype(BF16)
            hs_ref[k, r0:r0 + SORT_ROWS, :] = _dot(perm, hm).astype(hs_ref.dtype)
            cp = _dot(perm, c_pack)
            cs_ref[k, r0:r0 + SORT_ROWS, :] = (cp + pltpu.roll(cp, LANES - N_EXPERTS, axis=1)
                                               + pltpu.roll(cp, LANES - 2 * N_EXPERTS, axis=1))

    b0 = tab_ref[k, g]

    def expert_block(b, _):
        off = pl.multiple_of(b * rb, rb)
        rows = hs_ref[k, pl.ds(off, rb), :]
        cblk = cs_ref[k, pl.ds(off, rb), :]
        lane = lax.broadcasted_iota(jnp.int32, cblk.shape, 1)
        gates = [_dot(rows, wg_ref[j]) for j in range(EXPERTS_PER_GROUP)]
        ups = [_dot(rows, wu_ref[j]) for j in range(EXPERTS_PER_GROUP)]
        y = None
        for j in range(EXPERTS_PER_GROUP):
            c = jnp.sum(jnp.where(lane == g * EXPERTS_PER_GROUP + j, cblk, 0.0), axis=1, keepdims=True)
            a = gates[j] * jax.nn.sigmoid(gates[j]) * ups[j] * c
            d = _dot(a.astype(BF16), wd_ref[j])
            y = d if y is None else y + d
        hs_ref[k, pl.ds(off, rb), :] = y.astype(hs_ref.dtype)
        return 0

    last_group = g == pl.num_programs(1) - 1
    tile = pl.program_id(0) * pl.num_programs(2) + k
    n_tiles = pl.num_programs(0) * pl.num_programs(2)

    def x1_copy():
        return pltpu.make_async_copy(x1_hbm.at[pl.ds(pl.multiple_of(tile * t, t), t), :], xbuf_ref, sem_ref.at[0])

    def out_copy(which):
        return pltpu.make_async_copy(obuf_ref, o_hbm.at[pl.ds(pl.multiple_of(which * t, t), t), :], sem_ref.at[1])

    @pl.when(last_group)
    def _():
        x1_copy().start()

    lax.fori_loop(b0, b0 + tab_ref[k, N_GROUPS + g], expert_block, 0)

    @pl.when(last_group)
    def _():
        x1_copy().wait()

        @pl.when(tile > 0)
        def _():
            out_copy(tile - 1).wait()

        ys = hs_ref[k]
        for r0 in range(0, t, SORT_ROWS):
            src = lax.broadcasted_iota(jnp.int32, (SORT_ROWS, t_pad), 1)
            unperm = jnp.where(src == posc_ref[k, r0:r0 + SORT_ROWS, :], 1.0, 0.0).astype(BF16)
            x2 = xbuf_ref[r0:r0 + SORT_ROWS, :] + _dot(unperm, ys)
            obuf_ref[r0:r0 + SORT_ROWS, :] = _rms(x2, gf_ref[...]) if apply_final else x2
        out_copy(tile).start()

        @pl.when(tile == n_tiles - 1)
        def _():
            out_copy(tile).wait()


def _moe(hm, comb, x1, wg, wu, wd, gfin, *, tm, apply_final):
    n, d = x1.shape
    ne, _, de = wg.shape
    epg = EXPERTS_PER_GROUP
    n_groups = ne // epg
    t_pad = tm + N_GROUPS * ROW_BLOCK
    assert t_pad % SORT_ROWS == 0 and tm % SORT_ROWS == 0 and n % (MOE_TILES * tm) == 0
    first_blk = lambda cols: pl.BlockSpec(
        (tm, cols), lambda p, g, k: (MOE_TILES * p + jnp.where(g == 0, k, MOE_TILES - 1), 0))
    w_blk = lambda *shape: pl.BlockSpec((epg,) + shape, lambda p, g, k: (g, 0, 0))
    hbm = pl.BlockSpec(memory_space=pl.ANY)
    return pl.pallas_call(
        functools.partial(_moe_kernel, apply_final=apply_final),
        grid=(n // (MOE_TILES * tm), n_groups, MOE_TILES),
        in_specs=[first_blk(d), first_blk(LANES), hbm, w_blk(d, de), w_blk(d, de), w_blk(de, d),
                  pl.BlockSpec((1, d), lambda p, g, k: (0, 0))],
        out_specs=hbm,
        out_shape=jax.ShapeDtypeStruct((n, d), F32),
        scratch_shapes=[pltpu.VMEM((tm, tm), BF16), pltpu.VMEM((MOE_TILES, t_pad, d), BF16),
                        pltpu.VMEM((MOE_TILES, t_pad, LANES), F32), pltpu.VMEM((MOE_TILES, tm, 1), jnp.int32),
                        pltpu.SMEM((MOE_TILES, 2 * N_GROUPS), jnp.int32),
                        pltpu.VMEM((tm, d), F32), pltpu.VMEM((tm, d), F32), pltpu.SemaphoreType.DMA((2,))],
        compiler_params=pltpu.CompilerParams(dimension_semantics=("arbitrary",) * 3,
                                             vmem_limit_bytes=VMEM_LIMIT),
        name="moe",
    )(hm, comb, x1, wg, wu, wd, gfin)


def kernel(x, norm_attn_g, w_in, b_forget, lambda_q1, lambda_k1, lambda_q2, lambda_k2, diff_norm_g, w_out,
           norm_ffn_g, router_group_w, router_group_b, router_expert_w, router_expert_b, w_gate, w_up, w_down,
           norm_final_g):
    b, s, d = x.shape
    depth = w_in.shape[0]
    n = b * s
    diff_w = N_DIFF_HEADS * 2 * HEAD_DIM
    fox_w = N_FOX_HEADS * HEAD_DIM

    inv_freq = 1.0 / (ROPE_THETA ** (np.arange(0, HEAD_DIM, 2, dtype=np.float64) / HEAD_DIM))
    ang = np.tile(np.arange(s, dtype=np.float64)[:, None] * inv_freq[None, :], (1, 2 * LANES // HEAD_DIM))
    first_half = (np.arange(LANES) % HEAD_DIM) < HEAD_DIM // 2
    cos2 = jnp.asarray(np.cos(ang), F32)
    sin2 = jnp.asarray(np.where(first_half, -np.sin(ang), np.sin(ang)), F32)

    x2 = x.reshape(n, d)
    for l in range(depth):
        lam_init = 0.8 - 0.6 * float(np.exp(-0.3 * l))
        offs = [int(o) for o in np.cumsum([0, diff_w, diff_w, diff_w, fox_w, fox_w, fox_w])]
        assert diff_w == fox_w
        col_offsets = ((offs[0], offs[1], offs[3], offs[4]), (offs[2], offs[5]))
        wf = jnp.pad(w_in[l][:, offs[6]:], ((0, 0), (0, LANES - N_FOX_HEADS)))
        wf_cat = jnp.concatenate(_split2(wf), axis=1)
        bf_pad = jnp.pad(b_forget[l], (0, LANES - N_FOX_HEADS)).reshape(1, LANES)

        qd, kd, vdt, qf, kf, vft, qb, kb, wg_bf, wu_bf, wd_bf = _in_proj(
            x2, norm_attn_g[l].reshape(1, d), cos2, sin2, w_in, col_offsets, diff_w, wf_cat, bf_pad,
            (w_gate, w_up, w_down), layer=l, batch=b, seq=s, tm=IN_PROJ_ROWS)

        to3 = lambda t: t.reshape(b, s, t.shape[-1])
        lam_params = jnp.stack([lambda_q1[l], lambda_k1[l], lambda_q2[l], lambda_k2[l]])
        od = _diff_attn(to3(qd), to3(kd), vdt, lam_params, diff_norm_g[l].reshape(-1, 1),
                        tq=ATTN_TILE, lam_init=lam_init)
        of = _fox_attn(to3(qf), to3(kf), vft, to3(qb), to3(kb), tq=ATTN_TILE)

        wo = w_out[l].astype(BF16).reshape(2, -1, d)
        wr = jnp.zeros((d, LANES), F32)
        wr = wr.at[:, :N_GROUPS].set(router_group_w[l]).at[:, 8:8 + N_EXPERTS].set(router_expert_w[l])
        wr_cat = jnp.concatenate(_split2(wr.T), axis=0)
        br = jnp.zeros((LANES,), F32)
        br = br.at[:N_GROUPS].set(router_group_b[l]).at[8:8 + N_EXPERTS].set(router_expert_b[l]).reshape(LANES, 1)
        x1, hm, comb = _out_proj(x2, od.reshape(n, -1), of.reshape(n, -1), wo, norm_ffn_g[l].reshape(1, d),
                                 wr_cat, br, tm=OUT_PROJ_ROWS)

        x2 = _moe(hm, comb, x1, wg_bf, wu_bf, wd_bf, norm_final_g.reshape(1, d), tm=MOE_ROWS, apply_final=(l == depth - 1))
    return x2.reshape(b, s, d)
```

```python
import functools

import numpy as np
import jax
import jax.numpy as jnp
from jax import lax
from jax.experimental import pallas as pl
from jax.experimental.pallas import tpu as pltpu

CHUNK = 64
HEAD_DIM = 64
ROPE_THETA = 10000.0
EPS = 1e-6
LOG2E = 1.4426950408889634
N_DIFF_HEADS = 4
N_FOX_HEADS = 8
N_GROUPS = 4
EXPERTS_PER_GROUP = 4
N_EXPERTS = N_GROUPS * EXPERTS_PER_GROUP

LANES = 128
NEG_BIG = -1e30
VMEM_LIMIT = 56 * 1024 * 1024
IN_PROJ_ROWS = 512
ATTN_TILE = 512
OUT_PROJ_ROWS = 1024
MOE_ROWS = 1024
N_BIAS = 6
SUBLANES = 8
GID_LANE = N_EXPERTS
ROW_BLOCK = 128
ONES_ROWS = 16
BLOCKS_PER_STEP = 2
MOE_TILES = 2
OUT_PROJ_SPLIT = 2
SORT_ROWS = 512

BF16 = jnp.bfloat16
F32 = jnp.float32
_NT = (((1,), (1,)), ((), ()))


def _dot(a, b):
    return jnp.dot(a, b, preferred_element_type=F32)


def _dot_nt(a, b):
    return lax.dot_general(a, b, _NT, preferred_element_type=F32)


def _split2(x):
    hi = x.astype(BF16)
    lo = (x - hi.astype(F32)).astype(BF16)
    return hi, lo


def _split3(x):
    hi = x.astype(BF16)
    r = x - hi.astype(F32)
    mid = r.astype(BF16)
    lo = (r - mid.astype(F32)).astype(BF16)
    return hi, mid, lo


def _rms(x, g):
    return x * lax.rsqrt(jnp.mean(x * x, axis=-1, keepdims=True) + EPS) * g


def _pack_parts(parts):
    packed = parts[0].astype(F32)
    for k in (1, 2):
        packed = packed + pltpu.roll(parts[k].astype(F32), k * N_FOX_HEADS, axis=1)
    return packed.astype(BF16)


def _in_proj_kernel(x_ref, g_ref, cos_ref, sin_ref, win_ref, wf_cat_ref, bf_ref, tri_ref,
                    eq_ref, ek_ref, cq_ref, ck_ref, ew0_ref, ew1_ref, ew2_ref,
                    qd_ref, kd_ref, vdt_ref, qf_ref, kf_ref, vft_ref, qb_ref, kb_ref, eb0_ref, eb1_ref, eb2_ref,
                    carry_ref, w_ref, wt_ref, *, tiles_per_seq, col_offsets):
    i = pl.program_id(0)

    @pl.when(i == 0)
    def _():
        width = w_ref.shape[2]
        for dst, off in enumerate(col_offsets[0]):
            w_ref[dst] = win_ref[0, :, off:off + width].astype(w_ref.dtype)
        for dst, off in enumerate(col_offsets[1]):
            wt_ref[dst] = win_ref[0, :, off:off + width].T.astype(wt_ref.dtype)

    for src, dst in ((ew0_ref, eb0_ref), (ew1_ref, eb1_ref), (ew2_ref, eb2_ref)):
        dst[0] = src[0, 0].astype(dst.dtype)
    h = _rms(x_ref[...], g_ref[...])
    hb = h.astype(BF16)
    cos = cos_ref[...]
    sin = sin_ref[...]
    scale = HEAD_DIM ** -0.5 * LOG2E
    lane = lax.broadcasted_iota(jnp.int32, cos.shape, 1)
    first_half = (lane % HEAD_DIM) < HEAD_DIM // 2

    def rope(w_idx, out_ref, mul):
        a = _dot(hb, w_ref[w_idx])
        for c in range(a.shape[1] // LANES):
            sl = slice(c * LANES, (c + 1) * LANES)
            x = a[:, sl]
            rot = jnp.where(first_half, pltpu.roll(x, LANES - HEAD_DIM // 2, axis=1),
                            pltpu.roll(x, HEAD_DIM // 2, axis=1))
            out_ref[:, sl] = ((x * cos + rot * sin) * mul).astype(out_ref.dtype)

    @pl.when(i % tiles_per_seq == 0)
    def _():
        carry_ref[...] = jnp.zeros_like(carry_ref)


    zz = _dot(hb, wf_cat_ref[...])
    z = (zz[:, :LANES] + zz[:, LANES:]) + bf_ref[...]
    log_f = jnp.minimum(z, 0.0) - jnp.log1p(jnp.exp(-jnp.abs(z)))
    valid = lane < N_FOX_HEADS
    log_f = jnp.where(valid, log_f, 0.0)

    rope(0, qd_ref, scale)

    r = _dot(tri_ref[...], _pack_parts(_split3(log_f)))
    cum = (r + pltpu.roll(r, LANES - N_FOX_HEADS, axis=1)) + pltpu.roll(r, LANES - 2 * N_FOX_HEADS, axis=1)
    cum = jnp.where(valid, cum + carry_ref[...], 0.0)
    carry_ref[...] = cum[cum.shape[0] - 1:, :]

    rope(1, kd_ref, 1.0)

    f_pack = _pack_parts(_split3(cum * LOG2E))
    qb_ref[...] = (cq_ref[...] + _dot(f_pack, eq_ref[...])).astype(qb_ref.dtype)
    kb_ref[...] = (ck_ref[...] + _dot(f_pack, ek_ref[...])).astype(kb_ref.dtype)

    qf_ref[...] = (_dot(hb, w_ref[2]) * scale).astype(qf_ref.dtype)
    kf_ref[...] = _dot(hb, w_ref[3]).astype(kf_ref.dtype)
    vdt_ref[0] = _dot_nt(wt_ref[0], hb).astype(vdt_ref.dtype)
    vft_ref[0] = _dot_nt(wt_ref[1], hb).astype(vft_ref.dtype)


def _bias_placement():
    width = (N_FOX_HEADS // 2) * LANES
    eq = np.zeros((LANES, width), np.float32)
    ek = np.zeros((LANES, width), np.float32)
    cq = np.zeros((1, width), np.float32)
    ck = np.zeros((1, width), np.float32)
    for h in range(N_FOX_HEADS):
        base = (h // 2) * LANES + (h % 2) * N_BIAS
        for part in range(3):
            eq[part * N_FOX_HEADS + h, base + part] = 1.0
            ek[part * N_FOX_HEADS + h, base + 3 + part] = -1.0
        cq[0, base + 3:base + 6] = 1.0
        ck[0, base:base + 3] = 1.0
    return jnp.asarray(eq, BF16), jnp.asarray(ek, BF16), jnp.asarray(cq), jnp.asarray(ck)


def _in_proj(x2, g, cos2, sin2, w_in, col_offsets, wcols, wf_cat, bf_pad, expert_ws, *, layer, batch, seq, tm):
    n, d = x2.shape
    tiles_per_seq = seq // tm
    steps = n // tm
    ew = list(expert_ws)
    per_expert = steps // ew[0].shape[1]
    assert per_expert * ew[0].shape[1] == steps and all(w.shape[2] % (per_expert * 8) == 0 for w in ew)
    ew_blk = [pl.BlockSpec((1, 1, w.shape[2] // per_expert, w.shape[3]),
                           lambda i: (layer, i // per_expert, i % per_expert, 0)) for w in ew]
    eb_blk = [pl.BlockSpec((1, w.shape[2] // per_expert, w.shape[3]),
                           lambda i: (i // per_expert, i % per_expert, 0)) for w in ew]
    tri = jnp.tril(jnp.ones((tm, tm), F32)).astype(BF16)
    eq, ek, cq, ck = _bias_placement()
    const = lambda *shape: pl.BlockSpec(shape, lambda i: (0,) * len(shape))
    row_blk = lambda cols: pl.BlockSpec((tm, cols), lambda i: (i, 0))
    pos_blk = pl.BlockSpec((tm, LANES), lambda i: (i % tiles_per_seq, 0))
    vt_blk = pl.BlockSpec((1, wcols, tm), lambda i: (i // tiles_per_seq, 0, i % tiles_per_seq))
    out_bf = jax.ShapeDtypeStruct((n, wcols), BF16)
    out_vt = jax.ShapeDtypeStruct((batch, wcols, seq), BF16)
    return pl.pallas_call(
        functools.partial(_in_proj_kernel, tiles_per_seq=tiles_per_seq, col_offsets=col_offsets),
        grid=(n // tm,),
        in_specs=[row_blk(d), const(1, d), pos_blk, pos_blk,
                  pl.BlockSpec((1,) + w_in.shape[1:], lambda i: (layer, 0, 0), pipeline_mode=pl.Buffered(1)),
                  const(*wf_cat.shape), const(1, LANES), const(tm, tm),
                  const(*eq.shape), const(*ek.shape), const(*cq.shape), const(*ck.shape)] + ew_blk,
        out_specs=[row_blk(wcols), row_blk(wcols), vt_blk, row_blk(wcols), row_blk(wcols), vt_blk,
                   row_blk(eq.shape[1]), row_blk(ek.shape[1])] + eb_blk,
        out_shape=[out_bf, out_bf, out_vt, out_bf, out_bf, out_vt,
                   jax.ShapeDtypeStruct((n, eq.shape[1]), BF16), jax.ShapeDtypeStruct((n, ek.shape[1]), BF16)]
        + [jax.ShapeDtypeStruct(w.shape[1:], BF16) for w in ew],
        scratch_shapes=[pltpu.VMEM((1, LANES), F32), pltpu.VMEM((len(col_offsets[0]), d, wcols), BF16),
                        pltpu.VMEM((len(col_offsets[1]), wcols, d), BF16)],
        compiler_params=pltpu.CompilerParams(dimension_semantics=("arbitrary",),
                                             vmem_limit_bytes=VMEM_LIMIT),
        name="in_proj",
    )(x2, g, cos2, sin2, w_in, wf_cat, bf_pad, tri, eq, ek, cq, ck, *ew)


def _flash_sweep(nq, tq, tabs, logits_fn, values_fn, mask, scratch):
    s0, s1, c0, c1, m_ref, acc_ref = scratch
    row_tab, k_tab = tabs
    n_maps = m_ref.shape[1]
    n_off = nq * (nq - 1) // 2
    ones = jnp.ones((ONES_ROWS, tq), BF16)

    all_maps = tuple(range(n_maps))
    halves = (all_maps[:n_maps // 2], all_maps[n_maps // 2:])

    def stage_a(row, tile, s_buf, c_buf, masked, maps=all_maps):
        sts = logits_fn(row, tile, masked, maps)
        for idx, a in enumerate(maps):
            st = jnp.where(mask, sts[idx], NEG_BIG) if masked else sts[idx]
            s_buf[a] = st
            c_buf[a] = jnp.max(st, axis=0, keepdims=True)

    def stage_b(row, tile, s_buf, c_buf, first, maps=all_maps):
        vts = values_fn(tile)
        for a in maps:
            vt = jnp.concatenate([vts[a], ones], axis=0)
            if first:
                m_new = c_buf[a]
                half = tq // 2
                p_left = jnp.exp2(s_buf[a, :half, :half] - m_new[:, :half]).astype(BF16)
                p_right = jnp.exp2(s_buf[a, :, half:] - m_new[:, half:]).astype(BF16)
                acc_ref[row, a] = jnp.concatenate([_dot(vt[:, :half], p_left), _dot(vt, p_right)], axis=1)
            else:
                m = m_ref[row, a]
                m_new = jnp.maximum(m, c_buf[a])
                alpha = jnp.exp2(m - m_new)
                acc_ref[row, a] = alpha * acc_ref[row, a] + _dot(vt, jnp.exp2(s_buf[a] - m_new).astype(BF16))
            m_ref[row, a] = m_new

    stage_a(0, 0, s0, c0, True)

    def diag_pair(jj, _):
        r = 2 * jj
        for maps in halves:
            stage_a(r + 1, r + 1, s1, c1, True, maps)
            stage_b(r, r, s0, c0, True, maps)
        for maps in halves:
            stage_a(r + 2, r + 2, s0, c0, True, maps)
            stage_b(r + 1, r + 1, s1, c1, True, maps)
        return 0

    lax.fori_loop(0, (nq - 2) // 2, diag_pair, 0)
    stage_a(nq - 1, nq - 1, s1, c1, True)
    stage_b(nq - 2, nq - 2, s0, c0, True)

    stage_a(row_tab[0], k_tab[0], s0, c0, False)
    stage_b(nq - 1, nq - 1, s1, c1, True)

    def off_pair(jj, _):
        t = 2 * jj
        for maps in halves:
            stage_a(row_tab[t + 1], k_tab[t + 1], s1, c1, False, maps)
            stage_b(row_tab[t], k_tab[t], s0, c0, False, maps)
        for maps in halves:
            stage_a(row_tab[t + 2], k_tab[t + 2], s0, c0, False, maps)
            stage_b(row_tab[t + 1], k_tab[t + 1], s1, c1, False, maps)
        return 0

    lax.fori_loop(0, (n_off - 2) // 2, off_pair, 0)
    stage_a(row_tab[n_off - 1], k_tab[n_off - 1], s1, c1, False)
    stage_b(row_tab[n_off - 2], k_tab[n_off - 2], s0, c0, False)
    stage_b(row_tab[n_off - 1], k_tab[n_off - 1], s1, c1, False)


def _flash_scratch(nq, n_maps, dv, tq):
    s_buf = pltpu.VMEM((n_maps, tq, tq), F32)
    c_buf = pltpu.VMEM((n_maps, 1, tq), F32)
    return [s_buf, s_buf, c_buf, c_buf, pltpu.VMEM((nq, n_maps, 1, tq), F32),
            pltpu.VMEM((nq, n_maps, dv + ONES_ROWS, tq), F32)]


def _off_diagonal_tables(nq):
    pairs = [(r, k) for k in range(nq) for r in range(k + 1, nq)]
    return (jnp.asarray([p[0] for p in pairs], jnp.int32), jnp.asarray([p[1] for p in pairs], jnp.int32))


def _flash_outputs(acc_ref, row, dv):
    return [acc_ref[row, a, :dv, :] / acc_ref[row, a, dv:dv + 1, :] for a in range(acc_ref.shape[1])]


def _head_blocks(block):
    return [block[:, k * LANES:(k + 1) * LANES] for k in range(BLOCKS_PER_STEP)]


def _tile_logits(k, q, diagonal):
    if not diagonal:
        return _dot_nt(k, q)
    half = q.shape[0] // 2
    assert half % CHUNK == 0
    left = jnp.concatenate([_dot_nt(k[:half], q[:half]), jnp.full((k.shape[0] - half, half), NEG_BIG, F32)],
                           axis=0)
    return jnp.concatenate([left, _dot_nt(k, q[half:])], axis=1)


def _tile_rows(ref, tile, tq):
    return ref[0, pl.ds(pl.multiple_of(tile * tq, tq), tq), :]


def _diff_attn_kernel(row_tab, k_tab, q_ref, k_ref, vt_ref, lam_ref, g_ref, o_ref, *scratch, tq, lam_init):
    nq = q_ref.shape[1] // tq

    def logits(row, tile, diagonal, maps):
        qs = []
        for q in _head_blocks(_tile_rows(q_ref, row, tq)):
            lane = lax.broadcasted_iota(jnp.int32, q.shape, 1)
            zero = jnp.zeros_like(q)
            qs += [jnp.where(lane < HEAD_DIM, q, zero), jnp.where(lane >= HEAD_DIM, q, zero)]
        kts = _head_blocks(_tile_rows(k_ref, tile, tq))
        return [_tile_logits(kts[a // 2], qs[a], diagonal) for a in maps]

    def values(tile):
        vt = vt_ref[0, :, pl.ds(pl.multiple_of(tile * tq, tq), tq)]
        return [vt[(a // 2) * LANES:(a // 2 + 1) * LANES] for a in range(2 * BLOCKS_PER_STEP)]

    key = lax.broadcasted_iota(jnp.int32, (tq, tq), 0)
    qry = lax.broadcasted_iota(jnp.int32, (tq, tq), 1)
    _flash_sweep(nq, tq, (row_tab, k_tab), logits, values, (key // CHUNK) <= (qry // CHUNK), scratch)

    lp = lam_ref[...]
    lam = (jnp.exp(jnp.sum(lp[0:1] * lp[1:2], axis=1, keepdims=True))
           - jnp.exp(jnp.sum(lp[2:3] * lp[3:4], axis=1, keepdims=True)) + lam_init)

    def finish(row, _):
        outs = _flash_outputs(scratch[-1], row, LANES)
        ys = []
        for k in range(BLOCKS_PER_STEP):
            o = outs[2 * k] - lam * outs[2 * k + 1]
            y = o * lax.rsqrt(jnp.mean(o * o, axis=0, keepdims=True) + EPS) * g_ref[...] * (1.0 - lam_init)
            ys.append(y.T.astype(o_ref.dtype))
        o_ref[0, pl.ds(pl.multiple_of(row * tq, tq), tq), :] = jnp.concatenate(ys, axis=1)
        return 0

    lax.fori_loop(0, nq, finish, 0)


def _attn_call(kernel_fn, name, n_maps, dv, inputs, in_specs, b, s, w, tq):
    nq = s // tq
    assert nq % 2 == 0 and nq >= 4, "the pipelined sweeps handle tiles in pairs"
    wb = BLOCKS_PER_STEP * LANES
    return pl.pallas_call(
        kernel_fn,
        grid_spec=pltpu.PrefetchScalarGridSpec(
            num_scalar_prefetch=2,
            grid=(b, w // wb),
            in_specs=in_specs,
            out_specs=pl.BlockSpec((1, s, wb), lambda bi, h, *_: (bi, 0, h)),
            scratch_shapes=_flash_scratch(nq, n_maps, dv, tq)),
        out_shape=jax.ShapeDtypeStruct((b, s, w), BF16),
        compiler_params=pltpu.CompilerParams(dimension_semantics=("arbitrary",) * 2,
                                             vmem_limit_bytes=VMEM_LIMIT),
        name=name,
    )(*_off_diagonal_tables(nq), *inputs)


def _attn_specs(s):
    wb = BLOCKS_PER_STEP * LANES
    row_blk = pl.BlockSpec((1, s, wb), lambda bi, h, *_: (bi, 0, h))
    vt_blk = pl.BlockSpec((1, wb, s), lambda bi, h, *_: (bi, h, 0))
    return row_blk, vt_blk


def _diff_attn(qd, kd, vdt, lam_params, gnorm_col, *, tq, lam_init):
    b, s, w = qd.shape
    row_blk, vt_blk = _attn_specs(s)
    small = lambda arr: pl.BlockSpec(arr.shape, lambda bi, h, *_: (0, 0))
    return _attn_call(functools.partial(_diff_attn_kernel, tq=tq, lam_init=lam_init), "diff_attn",
                      2 * BLOCKS_PER_STEP, LANES, (qd, kd, vdt, lam_params, gnorm_col),
                      [row_blk, row_blk, vt_blk, small(lam_params), small(gnorm_col)], b, s, w, tq)


def _fox_attn_kernel(row_tab, k_tab, q_ref, k_ref, vt_ref, qb_ref, kb_ref, o_ref, *scratch, tq):
    nq = q_ref.shape[1] // tq

    def logits(row, tile, diagonal, maps):
        qs = []
        for q, qb in zip(_head_blocks(_tile_rows(q_ref, row, tq)), _head_blocks(_tile_rows(qb_ref, row, tq))):
            lane = lax.broadcasted_iota(jnp.int32, q.shape, 1)
            zero = jnp.zeros_like(q)
            qs += [jnp.concatenate([jnp.where(lane < HEAD_DIM, q, zero), jnp.where(lane < N_BIAS, qb, zero)],
                                   axis=1),
                   jnp.concatenate([jnp.where(lane >= HEAD_DIM, q, zero),
                                    jnp.where((lane >= N_BIAS) & (lane < 2 * N_BIAS), qb, zero)], axis=1)]
        kts = [jnp.concatenate([k, kb], axis=1) for k, kb in zip(_head_blocks(_tile_rows(k_ref, tile, tq)),
                                                                   _head_blocks(_tile_rows(kb_ref, tile, tq)))]
        return [_tile_logits(kts[a // 2], qs[a], diagonal) for a in maps]

    def values(tile):
        vt = vt_ref[0, :, pl.ds(pl.multiple_of(tile * tq, tq), tq)]
        return [vt[a * HEAD_DIM:(a + 1) * HEAD_DIM] for a in range(2 * BLOCKS_PER_STEP)]

    key = lax.broadcasted_iota(jnp.int32, (tq, tq), 0)
    qry = lax.broadcasted_iota(jnp.int32, (tq, tq), 1)
    _flash_sweep(nq, tq, (row_tab, k_tab), logits, values, key <= qry, scratch)

    def finish(row, _):
        o = jnp.concatenate(_flash_outputs(scratch[-1], row, HEAD_DIM), axis=0)
        o_ref[0, pl.ds(pl.multiple_of(row * tq, tq), tq), :] = o.T.astype(o_ref.dtype)
        return 0

    lax.fori_loop(0, nq, finish, 0)


def _fox_attn(qf, kf, vft, qb, kb, *, tq):
    b, s, w = qf.shape
    row_blk, vt_blk = _attn_specs(s)
    return _attn_call(functools.partial(_fox_attn_kernel, tq=tq), "fox_attn", 2 * BLOCKS_PER_STEP, HEAD_DIM,
                      (qf, kf, vft, qb, kb), [row_blk, row_blk, vt_blk, row_blk, row_blk], b, s, w, tq)


def _route(lt):
    tm = lt.shape[1]
    g8 = lt[0:8]
    r8 = lax.broadcasted_iota(jnp.int32, (8, tm), 0)
    g8 = jnp.where(r8 < N_GROUPS, g8, NEG_BIG)
    gmax = jnp.max(g8, axis=0, keepdims=True)
    gidx = jnp.min(jnp.where(g8 == gmax, r8, N_GROUPS), axis=0, keepdims=True)
    g_w = 1.0 / jnp.sum(jnp.exp(g8 - gmax), axis=0, keepdims=True)

    e16 = lt[8:8 + N_EXPERTS]
    r16 = lax.broadcasted_iota(jnp.int32, (N_EXPERTS, tm), 0)
    in_group = (r16 // EXPERTS_PER_GROUP) == gidx
    e_sel = jnp.where(in_group, e16, NEG_BIG)
    top1 = jnp.max(e_sel, axis=0, keepdims=True)
    id1 = jnp.min(jnp.where(e_sel == top1, r16, N_EXPERTS), axis=0, keepdims=True)
    e_rest = jnp.where(r16 == id1, NEG_BIG, e_sel)
    top2 = jnp.max(e_rest, axis=0, keepdims=True)
    id2 = jnp.min(jnp.where(e_rest == top2, r16, N_EXPERTS), axis=0, keepdims=True)
    t = jnp.exp(top2 - top1)
    w1 = g_w / (1.0 + t)
    w2 = w1 * t
    r128 = lax.broadcasted_iota(jnp.int32, (LANES, tm), 0)
    return (jnp.where(r128 == id1, w1, 0.0) + jnp.where(r128 == id2, w2, 0.0)
            + jnp.where(r128 == GID_LANE, gidx.astype(F32), 0.0))


def _out_proj_kernel(x_ref, od_ref, of_ref, wo_ref, g_ref, wr_cat_ref, br_ref,
                     x1_ref, hm_ref, comb_ref):
    tm = x_ref.shape[0]
    halves = [slice(k * tm // OUT_PROJ_SPLIT, (k + 1) * tm // OUT_PROJ_SPLIT) for k in range(OUT_PROJ_SPLIT)]
    x1s = []
    for rows in halves:
        x1 = x_ref[rows, :] + (_dot(od_ref[rows, :], wo_ref[0]) + _dot(of_ref[rows, :], wo_ref[1]))
        x1_ref[rows, :] = x1
        x1s.append(x1)
    wr_cat = wr_cat_ref[...]
    for rows, x1 in zip(halves, x1s):
        hb, h_lo = _split2(_rms(x1, g_ref[...]))
        hm_ref[rows, :] = hb
        both = _dot_nt(wr_cat, hb)
        lt = ((both[:LANES] + both[LANES:]) + _dot_nt(wr_cat[:LANES], h_lo)) + br_ref[...]
        comb_ref[rows, :] = _route(lt).T


def _out_proj(x2, od, of, wo, g, wr_cat, br, *, tm):
    n, d = x2.shape
    const = lambda *shape: pl.BlockSpec(shape, lambda i: (0,) * len(shape))
    row_blk = lambda cols: pl.BlockSpec((tm, cols), lambda i: (i, 0))
    return pl.pallas_call(
        _out_proj_kernel,
        grid=(n // tm,),
        in_specs=[row_blk(d), row_blk(od.shape[1]), row_blk(of.shape[1]), const(*wo.shape), const(1, d),
                  const(*wr_cat.shape), const(*br.shape)],
        out_specs=[row_blk(d), row_blk(d), row_blk(LANES)],
        out_shape=[jax.ShapeDtypeStruct((n, d), F32), jax.ShapeDtypeStruct((n, d), BF16),
                   jax.ShapeDtypeStruct((n, LANES), F32)],
        compiler_params=pltpu.CompilerParams(dimension_semantics=("arbitrary",),
                                             vmem_limit_bytes=VMEM_LIMIT),
        name="out_proj",
    )(x2, od, of, wo, g, wr_cat, br)


def _moe_kernel(hm_ref, comb_ref, x1_hbm, wg_ref, wu_ref, wd_ref, gf_ref, o_hbm,
                tri_ref, hs_ref, cs_ref, posc_ref, tab_ref, xbuf_ref, obuf_ref, sem_ref, *, apply_final):
    g = pl.program_id(1)
    k = pl.program_id(2)
    t = hm_ref.shape[0]
    t_pad = hs_ref.shape[1]
    rb = ROW_BLOCK

    @pl.when((pl.program_id(0) == 0) & (g == 0) & (k == 0))
    def _():
        row = lax.broadcasted_iota(jnp.int32, (t, t), 0)
        col = lax.broadcasted_iota(jnp.int32, (t, t), 1)
        tri_ref[...] = jnp.where(row > col, 1.0, 0.0).astype(BF16)

    @pl.when(g == 0)
    def _():
        comb = comb_ref[...]
        lane = lax.broadcasted_iota(jnp.int32, comb.shape, 1)
        onehot_c = jnp.where(lane == comb[:, GID_LANE:GID_LANE + 1].astype(jnp.int32), 1.0, 0.0)
        rank_c = _dot(tri_ref[...], onehot_c.astype(BF16))
        counts = jnp.sum(onehot_c, axis=0, keepdims=True)
        lane1 = lax.broadcasted_iota(jnp.int32, (1, LANES), 1)
        start_c = jnp.zeros((1, LANES), F32)
        first = jnp.int32(0)
        for grp in range(N_GROUPS):
            cnt = jnp.sum(jnp.where(lane1 == grp, counts, 0.0)).astype(jnp.int32)
            nblk = lax.shift_right_logical(cnt + (rb - 1), int(np.log2(rb)))
            tab_ref[k, grp] = first
            tab_ref[k, N_GROUPS + grp] = nblk
            start_c = jnp.where(lane1 == grp, (first * rb).astype(F32), start_c)
            first = first + nblk
        dest = onehot_c * (rank_c + start_c)
        posc_ref[k] = jnp.sum(dest, axis=1, keepdims=True).astype(jnp.int32)
        pos_r = jnp.sum(dest.T, axis=0, keepdims=True).astype(jnp.int32)
        c_parts = _split3(jnp.where(lane < N_EXPERTS, comb, 0.0))
        c_pack = (c_parts[0].astype(F32) + pltpu.roll(c_parts[1].astype(F32), N_EXPERTS, axis=1)
                  + pltpu.roll(c_parts[2].astype(F32), 2 * N_EXPERTS, axis=1)).astype(BF16)
        hm = hm_ref[...]
        for r0 in range(0, t_pad, SORT_ROWS):
            dst = lax.broadcasted_iota(jnp.int32, (SORT_ROWS, t), 0) + r0
            perm = jnp.where(dst == pos_r, 1.0, 0.0).astype(BF16)
            hs_ref[k, r0:r0 + SORT_ROWS, :] = _dot(perm, hm).astype(hs_ref.dtype)
            cp = _dot(perm, c_pack)
            cs_ref[k, r0:r0 + SORT_ROWS, :] = (cp + pltpu.roll(cp, LANES - N_EXPERTS, axis=1)
                                               + pltpu.roll(cp, LANES - 2 * N_EXPERTS, axis=1))

    b0 = tab_ref[k, g]

    def expert_block(b, _):
        off = pl.multiple_of(b * rb, rb)
        rows = hs_ref[k, pl.ds(off, rb), :]
        cblk = cs_ref[k, pl.ds(off, rb), :]
        lane = lax.broadcasted_iota(jnp.int32, cblk.shape, 1)
        gates = [_dot(rows, wg_ref[j]) for j in range(EXPERTS_PER_GROUP)]
        ups = [_dot(rows, wu_ref[j]) for j in range(EXPERTS_PER_GROUP)]
        y = None
        for j in range(EXPERTS_PER_GROUP):
            c = jnp.sum(jnp.where(lane == g * EXPERTS_PER_GROUP + j, cblk, 0.0), axis=1, keepdims=True)
            a = gates[j] * jax.nn.sigmoid(gates[j]) * ups[j] * c
            d = _dot(a.astype(BF16), wd_ref[j])
            y = d if y is None else y + d
        hs_ref[k, pl.ds(off, rb), :] = y.astype(hs_ref.dtype)
        return 0

    last_group = g == pl.num_programs(1) - 1
    tile = pl.program_id(0) * pl.num_programs(2) + k
    n_tiles = pl.num_programs(0) * pl.num_programs(2)

    def x1_copy():
        return pltpu.make_async_copy(x1_hbm.at[pl.ds(pl.multiple_of(tile * t, t), t), :], xbuf_ref, sem_ref.at[0])

    def out_copy(which):
        return pltpu.make_async_copy(obuf_ref, o_hbm.at[pl.ds(pl.multiple_of(which * t, t), t), :], sem_ref.at[1])

    @pl.when(last_group)
    def _():
        x1_copy().start()

    lax.fori_loop(b0, b0 + tab_ref[k, N_GROUPS + g], expert_block, 0)

    @pl.when(last_group)
    def _():
        x1_copy().wait()

        @pl.when(tile > 0)
        def _():
            out_copy(tile - 1).wait()

        ys = hs_ref[k]
        for r0 in range(0, t, SORT_ROWS):
            src = lax.broadcasted_iota(jnp.int32, (SORT_ROWS, t_pad), 1)
            unperm = jnp.where(src == posc_ref[k, r0:r0 + SORT_ROWS, :], 1.0, 0.0).astype(BF16)
            x2 = xbuf_ref[r0:r0 + SORT_ROWS, :] + _dot(unperm, ys)
            obuf_ref[r0:r0 + SORT_ROWS, :] = _rms(x2, gf_ref[...]) if apply_final else x2
        out_copy(tile).start()

        @pl.when(tile == n_tiles - 1)
        def _():
            out_copy(tile).wait()


def _moe(hm, comb, x1, wg, wu, wd, gfin, *, tm, apply_final):
    n, d = x1.shape
    ne, _, de = wg.shape
    epg = EXPERTS_PER_GROUP
    n_groups = ne // epg
    t_pad = tm + N_GROUPS * ROW_BLOCK
    assert t_pad % SORT_ROWS == 0 and tm % SORT_ROWS == 0 and n % (MOE_TILES * tm) == 0
    first_blk = lambda cols: pl.BlockSpec(
        (tm, cols), lambda p, g, k: (MOE_TILES * p + jnp.where(g == 0, k, MOE_TILES - 1), 0))
    w_blk = lambda *shape: pl.BlockSpec((epg,) + shape, lambda p, g, k: (g, 0, 0))
    hbm = pl.BlockSpec(memory_space=pl.ANY)
    return pl.pallas_call(
        functools.partial(_moe_kernel, apply_final=apply_final),
        grid=(n // (MOE_TILES * tm), n_groups, MOE_TILES),
        in_specs=[first_blk(d), first_blk(LANES), hbm, w_blk(d, de), w_blk(d, de), w_blk(de, d),
                  pl.BlockSpec((1, d), lambda p, g, k: (0, 0))],
        out_specs=hbm,
        out_shape=jax.ShapeDtypeStruct((n, d), F32),
        scratch_shapes=[pltpu.VMEM((tm, tm), BF16), pltpu.VMEM((MOE_TILES, t_pad, d), BF16),
                        pltpu.VMEM((MOE_TILES, t_pad, LANES), F32), pltpu.VMEM((MOE_TILES, tm, 1), jnp.int32),
                        pltpu.SMEM((MOE_TILES, 2 * N_GROUPS), jnp.int32),
                        pltpu.VMEM((tm, d), F32), pltpu.VMEM((tm, d), F32), pltpu.SemaphoreType.DMA((2,))],
        compiler_params=pltpu.CompilerParams(dimension_semantics=("arbitrary",) * 3,
                                             vmem_limit_bytes=VMEM_LIMIT),
        name="moe",
    )(hm, comb, x1, wg, wu, wd, gfin)


def kernel(x, norm_attn_g, w_in, b_forget, lambda_q1, lambda_k1, lambda_q2, lambda_k2, diff_norm_g, w_out,
           norm_ffn_g, router_group_w, router_group_b, router_expert_w, router_expert_b, w_gate, w_up, w_down,
           norm_final_g):
    b, s, d = x.shape
    depth = w_in.shape[0]
    n = b * s
    diff_w = N_DIFF_HEADS * 2 * HEAD_DIM
    fox_w = N_FOX_HEADS * HEAD_DIM

    inv_freq = 1.0 / (ROPE_THETA ** (np.arange(0, HEAD_DIM, 2, dtype=np.float64) / HEAD_DIM))
    ang = np.tile(np.arange(s, dtype=np.float64)[:, None] * inv_freq[None, :], (1, 2 * LANES // HEAD_DIM))
    first_half = (np.arange(LANES) % HEAD_DIM) < HEAD_DIM // 2
    cos2 = jnp.asarray(np.cos(ang), F32)
    sin2 = jnp.asarray(np.where(first_half, -np.sin(ang), np.sin(ang)), F32)

    x2 = x.reshape(n, d)
    for l in range(depth):
        lam_init = 0.8 - 0.6 * float(np.exp(-0.3 * l))
        offs = [int(o) for o in np.cumsum([0, diff_w, diff_w, diff_w, fox_w, fox_w, fox_w])]
        assert diff_w == fox_w
        col_offsets = ((offs[0], offs[1], offs[3], offs[4]), (offs[2], offs[5]))
        wf = jnp.pad(w_in[l][:, offs[6]:], ((0, 0), (0, LANES - N_FOX_HEADS)))
        wf_cat = jnp.concatenate(_split2(wf), axis=1)
        bf_pad = jnp.pad(b_forget[l], (0, LANES - N_FOX_HEADS)).reshape(1, LANES)

        qd, kd, vdt, qf, kf, vft, qb, kb, wg_bf, wu_bf, wd_bf = _in_proj(
            x2, norm_attn_g[l].reshape(1, d), cos2, sin2, w_in, col_offsets, diff_w, wf_cat, bf_pad,
            (w_gate, w_up, w_down), layer=l, batch=b, seq=s, tm=IN_PROJ_ROWS)

        to3 = lambda t: t.reshape(b, s, t.shape[-1])
        lam_params = jnp.stack([lambda_q1[l], lambda_k1[l], lambda_q2[l], lambda_k2[l]])
        od = _diff_attn(to3(qd), to3(kd), vdt, lam_params, diff_norm_g[l].reshape(-1, 1),
                        tq=ATTN_TILE, lam_init=lam_init)
        of = _fox_attn(to3(qf), to3(kf), vft, to3(qb), to3(kb), tq=ATTN_TILE)

        wo = w_out[l].astype(BF16).reshape(2, -1, d)
        wr = jnp.zeros((d, LANES), F32)
        wr = wr.at[:, :N_GROUPS].set(router_group_w[l]).at[:, 8:8 + N_EXPERTS].set(router_expert_w[l])
        wr_cat = jnp.concatenate(_split2(wr.T), axis=0)
        br = jnp.zeros((LANES,), F32)
        br = br.at[:N_GROUPS].set(router_group_b[l]).at[8:8 + N_EXPERTS].set(router_expert_b[l]).reshape(LANES, 1)
        x1, hm, comb = _out_proj(x2, od.reshape(n, -1), of.reshape(n, -1), wo, norm_ffn_g[l].reshape(1, d),
                                 wr_cat, br, tm=OUT_PROJ_ROWS)

        x2 = _moe(hm, comb, x1, wg_bf, wu_bf, wd_bf, norm_final_g.reshape(1, d), tm=MOE_ROWS, apply_final=(l == depth - 1))
    return x2.reshape(b, s, d)
```

```python
import functools

import numpy as np
import jax
import jax.numpy as jnp
from jax import lax
from jax.experimental import pallas as pl
from jax.experimental.pallas import tpu as pltpu

CHUNK = 64
HEAD_DIM = 64
ROPE_THETA = 10000.0
EPS = 1e-6
LOG2E = 1.4426950408889634
N_DIFF_HEADS = 4
N_FOX_HEADS = 8
N_GROUPS = 4
EXPERTS_PER_GROUP = 4
N_EXPERTS = N_GROUPS * EXPERTS_PER_GROUP

LANES = 128
NEG_BIG = -1e30
VMEM_LIMIT = 56 * 1024 * 1024
IN_PROJ_ROWS = 512
ATTN_TILE = 512
OUT_PROJ_ROWS = 1024
MOE_ROWS = 1024
N_BIAS = 6
SUBLANES = 8
GID_LANE = N_EXPERTS
ROW_BLOCK = 128
ONES_ROWS = 16
BLOCKS_PER_STEP = 2
MOE_TILES = 2
OUT_PROJ_SPLIT = 2
SORT_ROWS = 512

BF16 = jnp.bfloat16
F32 = jnp.float32
_NT = (((1,), (1,)), ((), ()))


def _dot(a, b):
    return jnp.dot(a, b, preferred_element_type=F32)


def _dot_nt(a, b):
    return lax.dot_general(a, b, _NT, preferred_element_type=F32)


def _split2(x):
    hi = x.astype(BF16)
    lo = (x - hi.astype(F32)).astype(BF16)
    return hi, lo


def _split3(x):
    hi = x.astype(BF16)
    r = x - hi.astype(F32)
    mid = r.astype(BF16)
    lo = (r - mid.astype(F32)).astype(BF16)
    return hi, mid, lo


def _rms(x, g):
    return x * lax.rsqrt(jnp.mean(x * x, axis=-1, keepdims=True) + EPS) * g


def _pack_parts(parts):
    packed = parts[0].astype(F32)
    for k in (1, 2):
        packed = packed + pltpu.roll(parts[k].astype(F32), k * N_FOX_HEADS, axis=1)
    return packed.astype(BF16)


def _in_proj_kernel(x_ref, g_ref, cos_ref, sin_ref, win_ref, wf_cat_ref, bf_ref, tri_ref,
                    eq_ref, ek_ref, cq_ref, ck_ref, ew0_ref, ew1_ref, ew2_ref,
                    qd_ref, kd_ref, vdt_ref, qf_ref, kf_ref, vft_ref, qb_ref, kb_ref, eb0_ref, eb1_ref, eb2_ref,
                    carry_ref, w_ref, wt_ref, *, tiles_per_seq, col_offsets):
    i = pl.program_id(0)

    @pl.when(i == 0)
    def _():
        width = w_ref.shape[2]
        for dst, off in enumerate(col_offsets[0]):
            w_ref[dst] = win_ref[0, :, off:off + width].astype(w_ref.dtype)
        for dst, off in enumerate(col_offsets[1]):
            wt_ref[dst] = win_ref[0, :, off:off + width].T.astype(wt_ref.dtype)

    for src, dst in ((ew0_ref, eb0_ref), (ew1_ref, eb1_ref), (ew2_ref, eb2_ref)):
        dst[0] = src[0, 0].astype(dst.dtype)
    h = _rms(x_ref[...], g_ref[...])
    hb = h.astype(BF16)
    cos = cos_ref[...]
    sin = sin_ref[...]
    scale = HEAD_DIM ** -0.5 * LOG2E
    lane = lax.broadcasted_iota(jnp.int32, cos.shape, 1)
    first_half = (lane % HEAD_DIM) < HEAD_DIM // 2

    def rope(w_idx, out_ref, mul):
        a = _dot(hb, w_ref[w_idx])
        for c in range(a.shape[1] // LANES):
            sl = slice(c * LANES, (c + 1) * LANES)
            x = a[:, sl]
            rot = jnp.where(first_half, pltpu.roll(x, LANES - HEAD_DIM // 2, axis=1),
                            pltpu.roll(x, HEAD_DIM // 2, axis=1))
            out_ref[:, sl] = ((x * cos + rot * sin) * mul).astype(out_ref.dtype)

    @pl.when(i % tiles_per_seq == 0)
    def _():
        carry_ref[...] = jnp.zeros_like(carry_ref)


    zz = _dot(hb, wf_cat_ref[...])
    z = (zz[:, :LANES] + zz[:, LANES:]) + bf_ref[...]
    log_f = jnp.minimum(z, 0.0) - jnp.log1p(jnp.exp(-jnp.abs(z)))
    valid = lane < N_FOX_HEADS
    log_f = jnp.where(valid, log_f, 0.0)

    rope(0, qd_ref, scale)

    r = _dot(tri_ref[...], _pack_parts(_split3(log_f)))
    cum = (r + pltpu.roll(r, LANES - N_FOX_HEADS, axis=1)) + pltpu.roll(r, LANES - 2 * N_FOX_HEADS, axis=1)
    cum = jnp.where(valid, cum + carry_ref[...], 0.0)
    carry_ref[...] = cum[cum.shape[0] - 1:, :]

    rope(1, kd_ref, 1.0)

    f_pack = _pack_parts(_split3(cum * LOG2E))
    qb_ref[...] = (cq_ref[...] + _dot(f_pack, eq_ref[...])).astype(qb_ref.dtype)
    kb_ref[...] = (ck_ref[...] + _dot(f_pack, ek_ref[...])).astype(kb_ref.dtype)

    qf_ref[...] = (_dot(hb, w_ref[2]) * scale).astype(qf_ref.dtype)
    kf_ref[...] = _dot(hb, w_ref[3]).astype(kf_ref.dtype)
    vdt_ref[0] = _dot_nt(wt_ref[0], hb).astype(vdt_ref.dtype)
    vft_ref[0] = _dot_nt(wt_ref[1], hb).astype(vft_ref.dtype)


def _bias_placement():
    width = (N_FOX_HEADS // 2) * LANES
    eq = np.zeros((LANES, width), np.float32)
    ek = np.zeros((LANES, width), np.float32)
    cq = np.zeros((1, width), np.float32)
    ck = np.zeros((1, width), np.float32)
    for h in range(N_FOX_HEADS):
        base = (h // 2) * LANES + (h % 2) * N_BIAS
        for part in range(3):
            eq[part * N_FOX_HEADS + h, base + part] = 1.0
            ek[part * N_FOX_HEADS + h, base + 3 + part] = -1.0
        cq[0, base + 3:base + 6] = 1.0
        ck[0, base:base + 3] = 1.0
    return jnp.asarray(eq, BF16), jnp.asarray(ek, BF16), jnp.asarray(cq), jnp.asarray(ck)


def _in_proj(x2, g, cos2, sin2, w_in, col_offsets, wcols, wf_cat, bf_pad, expert_ws, *, layer, batch, seq, tm):
    n, d = x2.shape
    tiles_per_seq = seq // tm
    steps = n // tm
    ew = list(expert_ws)
    per_expert = steps // ew[0].shape[1]
    assert per_expert * ew[0].shape[1] == steps and all(w.shape[2] % (per_expert * 8) == 0 for w in ew)
    ew_blk = [pl.BlockSpec((1, 1, w.shape[2] // per_expert, w.shape[3]),
                           lambda i: (layer, i // per_expert, i % per_expert, 0)) for w in ew]
    eb_blk = [pl.BlockSpec((1, w.shape[2] // per_expert, w.shape[3]),
                           lambda i: (i // per_expert, i % per_expert, 0)) for w in ew]
    tri = jnp.tril(jnp.ones((tm, tm), F32)).astype(BF16)
    eq, ek, cq, ck = _bias_placement()
    const = lambda *shape: pl.BlockSpec(shape, lambda i: (0,) * len(shape))
    row_blk = lambda cols: pl.BlockSpec((tm, cols), lambda i: (i, 0))
    pos_blk = pl.BlockSpec((tm, LANES), lambda i: (i % tiles_per_seq, 0))
    vt_blk = pl.BlockSpec((1, wcols, tm), lambda i: (i // tiles_per_seq, 0, i % tiles_per_seq))
    out_bf = jax.ShapeDtypeStruct((n, wcols), BF16)
    out_vt = jax.ShapeDtypeStruct((batch, wcols, seq), BF16)
    return pl.pallas_call(
        functools.partial(_in_proj_kernel, tiles_per_seq=tiles_per_seq, col_offsets=col_offsets),
        grid=(n // tm,),
        in_specs=[row_blk(d), const(1, d), pos_blk, pos_blk,
                  pl.BlockSpec((1,) + w_in.shape[1:], lambda i: (layer, 0, 0), pipeline_mode=pl.Buffered(1)),
                  const(*wf_cat.shape), const(1, LANES), const(tm, tm),
                  const(*eq.shape), const(*ek.shape), const(*cq.shape), const(*ck.shape)] + ew_blk,
        out_specs=[row_blk(wcols), row_blk(wcols), vt_blk, row_blk(wcols), row_blk(wcols), vt_blk,
                   row_blk(eq.shape[1]), row_blk(ek.shape[1])] + eb_blk,
        out_shape=[out_bf, out_bf, out_vt, out_bf, out_bf, out_vt,
                   jax.ShapeDtypeStruct((n, eq.shape[1]), BF16), jax.ShapeDtypeStruct((n, ek.shape[1]), BF16)]
        + [jax.ShapeDtypeStruct(w.shape[1:], BF16) for w in ew],
        scratch_shapes=[pltpu.VMEM((1, LANES), F32), pltpu.VMEM((len(col_offsets[0]), d, wcols), BF16),
                        pltpu.VMEM((len(col_offsets[1]), wcols, d), BF16)],
        compiler_params=pltpu.CompilerParams(dimension_semantics=("arbitrary",),
                                             vmem_limit_bytes=VMEM_LIMIT),
        name="in_proj",
    )(x2, g, cos2, sin2, w_in, wf_cat, bf_pad, tri, eq, ek, cq, ck, *ew)


def _flash_sweep(nq, tq, tabs, logits_fn, values_fn, mask, scratch, split_maps):
    s0, s1, c0, c1, m_ref, acc_ref = scratch
    row_tab, k_tab = tabs
    n_maps = m_ref.shape[1]
    n_off = nq * (nq - 1) // 2
    ones = jnp.ones((ONES_ROWS, tq), BF16)

    all_maps = tuple(range(n_maps))
    halves = (all_maps[:n_maps // 2], all_maps[n_maps // 2:]) if split_maps else (all_maps,)

    def stage_a(row, tile, s_buf, c_buf, masked, maps=all_maps):
        sts = logits_fn(row, tile, masked, maps)
        for idx, a in enumerate(maps):
            st = jnp.where(mask, sts[idx], NEG_BIG) if masked else sts[idx]
            s_buf[a] = st
            c_buf[a] = jnp.max(st, axis=0, keepdims=True)

    def stage_b(row, tile, s_buf, c_buf, first, maps=all_maps):
        vts = values_fn(tile)
        for a in maps:
            vt = jnp.concatenate([vts[a], ones], axis=0)
            if first:
                m_new = c_buf[a]
                half = tq // 2
                p_left = jnp.exp2(s_buf[a, :half, :half] - m_new[:, :half]).astype(BF16)
                p_right = jnp.exp2(s_buf[a, :, half:] - m_new[:, half:]).astype(BF16)
                acc_ref[row, a] = jnp.concatenate([_dot(vt[:, :half], p_left), _dot(vt, p_right)], axis=1)
            else:
                m = m_ref[row, a]
                m_new = jnp.maximum(m, c_buf[a])
                alpha = jnp.exp2(m - m_new)
                acc_ref[row, a] = alpha * acc_ref[row, a] + _dot(vt, jnp.exp2(s_buf[a] - m_new).astype(BF16))
            m_ref[row, a] = m_new

    stage_a(0, 0, s0, c0, True)

    def diag_pair(jj, _):
        r = 2 * jj
        for maps in halves:
            stage_a(r + 1, r + 1, s1, c1, True, maps)
            stage_b(r, r, s0, c0, True, maps)
        for maps in halves:
            stage_a(r + 2, r + 2, s0, c0, True, maps)
            stage_b(r + 1, r + 1, s1, c1, True, maps)
        return 0

    lax.fori_loop(0, (nq - 2) // 2, diag_pair, 0)
    stage_a(nq - 1, nq - 1, s1, c1, True)
    stage_b(nq - 2, nq - 2, s0, c0, True)

    stage_a(row_tab[0], k_tab[0], s0, c0, False)
    stage_b(nq - 1, nq - 1, s1, c1, True)

    def off_pair(jj, _):
        t = 2 * jj
        for maps in halves:
            stage_a(row_tab[t + 1], k_tab[t + 1], s1, c1, False, maps)
            stage_b(row_tab[t], k_tab[t], s0, c0, False, maps)
        for maps in halves:
            stage_a(row_tab[t + 2], k_tab[t + 2], s0, c0, False, maps)
            stage_b(row_tab[t + 1], k_tab[t + 1], s1, c1, False, maps)
        return 0

    lax.fori_loop(0, (n_off - 2) // 2, off_pair, 0)
    stage_a(row_tab[n_off - 1], k_tab[n_off - 1], s1, c1, False)
    stage_b(row_tab[n_off - 2], k_tab[n_off - 2], s0, c0, False)
    stage_b(row_tab[n_off - 1], k_tab[n_off - 1], s1, c1, False)


def _flash_scratch(nq, n_maps, dv, tq):
    s_buf = pltpu.VMEM((n_maps, tq, tq), F32)
    c_buf = pltpu.VMEM((n_maps, 1, tq), F32)
    return [s_buf, s_buf, c_buf, c_buf, pltpu.VMEM((nq, n_maps, 1, tq), F32),
            pltpu.VMEM((nq, n_maps, dv + ONES_ROWS, tq), F32)]


def _off_diagonal_tables(nq):
    pairs = [(r, k) for k in range(nq) for r in range(k + 1, nq)]
    return (jnp.asarray([p[0] for p in pairs], jnp.int32), jnp.asarray([p[1] for p in pairs], jnp.int32))


def _flash_outputs(acc_ref, row, dv):
    return [acc_ref[row, a, :dv, :] / acc_ref[row, a, dv:dv + 1, :] for a in range(acc_ref.shape[1])]


def _head_blocks(block):
    return [block[:, k * LANES:(k + 1) * LANES] for k in range(BLOCKS_PER_STEP)]


def _tile_logits(k, q, diagonal):
    if not diagonal:
        return _dot_nt(k, q)
    half = q.shape[0] // 2
    assert half % CHUNK == 0
    left = jnp.concatenate([_dot_nt(k[:half], q[:half]), jnp.full((k.shape[0] - half, half), NEG_BIG, F32)],
                           axis=0)
    return jnp.concatenate([left, _dot_nt(k, q[half:])], axis=1)


def _tile_rows(ref, tile, tq):
    return ref[0, pl.ds(pl.multiple_of(tile * tq, tq), tq), :]


def _diff_attn_kernel(row_tab, k_tab, q_ref, k_ref, vt_ref, lam_ref, g_ref, o_ref, *scratch, tq, lam_init):
    nq = q_ref.shape[1] // tq

    def logits(row, tile, diagonal, maps):
        qs = []
        for q in _head_blocks(_tile_rows(q_ref, row, tq)):
            lane = lax.broadcasted_iota(jnp.int32, q.shape, 1)
            zero = jnp.zeros_like(q)
            qs += [jnp.where(lane < HEAD_DIM, q, zero), jnp.where(lane >= HEAD_DIM, q, zero)]
        kts = _head_blocks(_tile_rows(k_ref, tile, tq))
        return [_tile_logits(kts[a // 2], qs[a], diagonal) for a in maps]

    def values(tile):
        vt = vt_ref[0, :, pl.ds(pl.multiple_of(tile * tq, tq), tq)]
        return [vt[(a // 2) * LANES:(a // 2 + 1) * LANES] for a in range(2 * BLOCKS_PER_STEP)]

    key = lax.broadcasted_iota(jnp.int32, (tq, tq), 0)
    qry = lax.broadcasted_iota(jnp.int32, (tq, tq), 1)
    _flash_sweep(nq, tq, (row_tab, k_tab), logits, values, (key // CHUNK) <= (qry // CHUNK), scratch,
                 split_maps=True)

    lp = lam_ref[...]
    lam = (jnp.exp(jnp.sum(lp[0:1] * lp[1:2], axis=1, keepdims=True))
           - jnp.exp(jnp.sum(lp[2:3] * lp[3:4], axis=1, keepdims=True)) + lam_init)

    def finish(row, _):
        outs = _flash_outputs(scratch[-1], row, LANES)
        ys = []
        for k in range(BLOCKS_PER_STEP):
            o = outs[2 * k] - lam * outs[2 * k + 1]
            y = o * lax.rsqrt(jnp.mean(o * o, axis=0, keepdims=True) + EPS) * g_ref[...] * (1.0 - lam_init)
            ys.append(y.T.astype(o_ref.dtype))
        o_ref[0, pl.ds(pl.multiple_of(row * tq, tq), tq), :] = jnp.concatenate(ys, axis=1)
        return 0

    lax.fori_loop(0, nq, finish, 0)


def _attn_call(kernel_fn, name, n_maps, dv, inputs, in_specs, b, s, w, tq):
    nq = s // tq
    assert nq % 2 == 0 and nq >= 4, "the pipelined sweeps handle tiles in pairs"
    wb = BLOCKS_PER_STEP * LANES
    return pl.pallas_call(
        kernel_fn,
        grid_spec=pltpu.PrefetchScalarGridSpec(
            num_scalar_prefetch=2,
            grid=(b, w // wb),
            in_specs=in_specs,
            out_specs=pl.BlockSpec((1, s, wb), lambda bi, h, *_: (bi, 0, h)),
            scratch_shapes=_flash_scratch(nq, n_maps, dv, tq)),
        out_shape=jax.ShapeDtypeStruct((b, s, w), BF16),
        compiler_params=pltpu.CompilerParams(dimension_semantics=("arbitrary",) * 2,
                                             vmem_limit_bytes=VMEM_LIMIT),
        name=name,
    )(*_off_diagonal_tables(nq), *inputs)


def _attn_specs(s):
    wb = BLOCKS_PER_STEP * LANES
    row_blk = pl.BlockSpec((1, s, wb), lambda bi, h, *_: (bi, 0, h))
    vt_blk = pl.BlockSpec((1, wb, s), lambda bi, h, *_: (bi, h, 0))
    return row_blk, vt_blk


def _diff_attn(qd, kd, vdt, lam_params, gnorm_col, *, tq, lam_init):
    b, s, w = qd.shape
    row_blk, vt_blk = _attn_specs(s)
    small = lambda arr: pl.BlockSpec(arr.shape, lambda bi, h, *_: (0, 0))
    return _attn_call(functools.partial(_diff_attn_kernel, tq=tq, lam_init=lam_init), "diff_attn",
                      2 * BLOCKS_PER_STEP, LANES, (qd, kd, vdt, lam_params, gnorm_col),
                      [row_blk, row_blk, vt_blk, small(lam_params), small(gnorm_col)], b, s, w, tq)


def _fox_attn_kernel(row_tab, k_tab, q_ref, k_ref, vt_ref, qb_ref, kb_ref, o_ref, *scratch, tq):
    nq = q_ref.shape[1] // tq

    def logits(row, tile, diagonal, maps):
        qs = []
        for q, qb in zip(_head_blocks(_tile_rows(q_ref, row, tq)), _head_blocks(_tile_rows(qb_ref, row, tq))):
            lane = lax.broadcasted_iota(jnp.int32, q.shape, 1)
            zero = jnp.zeros_like(q)
            qs += [jnp.concatenate([jnp.where(lane < HEAD_DIM, q, zero), jnp.where(lane < N_BIAS, qb, zero)],
                                   axis=1),
                   jnp.concatenate([jnp.where(lane >= HEAD_DIM, q, zero),
                                    jnp.where((lane >= N_BIAS) & (lane < 2 * N_BIAS), qb, zero)], axis=1)]
        kts = [jnp.concatenate([k, kb], axis=1) for k, kb in zip(_head_blocks(_tile_rows(k_ref, tile, tq)),
                                                                   _head_blocks(_tile_rows(kb_ref, tile, tq)))]
        return [_tile_logits(kts[a // 2], qs[a], diagonal) for a in maps]

    def values(tile):
        vt = vt_ref[0, :, pl.ds(pl.multiple_of(tile * tq, tq), tq)]
        return [vt[a * HEAD_DIM:(a + 1) * HEAD_DIM] for a in range(2 * BLOCKS_PER_STEP)]

    key = lax.broadcasted_iota(jnp.int32, (tq, tq), 0)
    qry = lax.broadcasted_iota(jnp.int32, (tq, tq), 1)
    _flash_sweep(nq, tq, (row_tab, k_tab), logits, values, key <= qry, scratch, split_maps=False)

    def finish(row, _):
        o = jnp.concatenate(_flash_outputs(scratch[-1], row, HEAD_DIM), axis=0)
        o_ref[0, pl.ds(pl.multiple_of(row * tq, tq), tq), :] = o.T.astype(o_ref.dtype)
        return 0

    lax.fori_loop(0, nq, finish, 0)


def _fox_attn(qf, kf, vft, qb, kb, *, tq):
    b, s, w = qf.shape
    row_blk, vt_blk = _attn_specs(s)
    return _attn_call(functools.partial(_fox_attn_kernel, tq=tq), "fox_attn", 2 * BLOCKS_PER_STEP, HEAD_DIM,
                      (qf, kf, vft, qb, kb), [row_blk, row_blk, vt_blk, row_blk, row_blk], b, s, w, tq)


def _route(lt):
    tm = lt.shape[1]
    g8 = lt[0:8]
    r8 = lax.broadcasted_iota(jnp.int32, (8, tm), 0)
    g8 = jnp.where(r8 < N_GROUPS, g8, NEG_BIG)
    gmax = jnp.max(g8, axis=0, keepdims=True)
    gidx = jnp.min(jnp.where(g8 == gmax, r8, N_GROUPS), axis=0, keepdims=True)
    g_w = 1.0 / jnp.sum(jnp.exp(g8 - gmax), axis=0, keepdims=True)

    e16 = lt[8:8 + N_EXPERTS]
    r16 = lax.broadcasted_iota(jnp.int32, (N_EXPERTS, tm), 0)
    in_group = (r16 // EXPERTS_PER_GROUP) == gidx
    e_sel = jnp.where(in_group, e16, NEG_BIG)
    top1 = jnp.max(e_sel, axis=0, keepdims=True)
    id1 = jnp.min(jnp.where(e_sel == top1, r16, N_EXPERTS), axis=0, keepdims=True)
    e_rest = jnp.where(r16 == id1, NEG_BIG, e_sel)
    top2 = jnp.max(e_rest, axis=0, keepdims=True)
    id2 = jnp.min(jnp.where(e_rest == top2, r16, N_EXPERTS), axis=0, keepdims=True)
    t = jnp.exp(top2 - top1)
    w1 = g_w / (1.0 + t)
    w2 = w1 * t
    r128 = lax.broadcasted_iota(jnp.int32, (LANES, tm), 0)
    return (jnp.where(r128 == id1, w1, 0.0) + jnp.where(r128 == id2, w2, 0.0)
            + jnp.where(r128 == GID_LANE, gidx.astype(F32), 0.0))


def _out_proj_kernel(x_ref, od_ref, of_ref, wo_ref, g_ref, wr_cat_ref, br_ref,
                     x1_ref, hm_ref, comb_ref):
    tm = x_ref.shape[0]
    halves = [slice(k * tm // OUT_PROJ_SPLIT, (k + 1) * tm // OUT_PROJ_SPLIT) for k in range(OUT_PROJ_SPLIT)]
    x1s = []
    for rows in halves:
        x1 = x_ref[rows, :] + (_dot(od_ref[rows, :], wo_ref[0]) + _dot(of_ref[rows, :], wo_ref[1]))
        x1_ref[rows, :] = x1
        x1s.append(x1)
    wr_cat = wr_cat_ref[...]
    for rows, x1 in zip(halves, x1s):
        hb, h_lo = _split2(_rms(x1, g_ref[...]))
        hm_ref[rows, :] = hb
        both = _dot_nt(wr_cat, hb)
        lt = ((both[:LANES] + both[LANES:]) + _dot_nt(wr_cat[:LANES], h_lo)) + br_ref[...]
        comb_ref[rows, :] = _route(lt).T


def _out_proj(x2, od, of, wo, g, wr_cat, br, *, tm):
    n, d = x2.shape
    const = lambda *shape: pl.BlockSpec(shape, lambda i: (0,) * len(shape))
    row_blk = lambda cols: pl.BlockSpec((tm, cols), lambda i: (i, 0))
    return pl.pallas_call(
        _out_proj_kernel,
        grid=(n // tm,),
        in_specs=[row_blk(d), row_blk(od.shape[1]), row_blk(of.shape[1]), const(*wo.shape), const(1, d),
                  const(*wr_cat.shape), const(*br.shape)],
        out_specs=[row_blk(d), row_blk(d), row_blk(LANES)],
        out_shape=[jax.ShapeDtypeStruct((n, d), F32), jax.ShapeDtypeStruct((n, d), BF16),
                   jax.ShapeDtypeStruct((n, LANES), F32)],
        compiler_params=pltpu.CompilerParams(dimension_semantics=("arbitrary",),
                                             vmem_limit_bytes=VMEM_LIMIT),
        name="out_proj",
    )(x2, od, of, wo, g, wr_cat, br)


def _moe_kernel(hm_ref, comb_ref, x1_hbm, wg_ref, wu_ref, wd_ref, gf_ref, o_hbm,
                tri_ref, hs_ref, cs_ref, posc_ref, tab_ref, xbuf_ref, obuf_ref, sem_ref, *, apply_final):
    g = pl.program_id(1)
    k = pl.program_id(2)
    t = hm_ref.shape[0]
    t_pad = hs_ref.shape[1]
    rb = ROW_BLOCK

    @pl.when((pl.program_id(0) == 0) & (g == 0) & (k == 0))
    def _():
        row = lax.broadcasted_iota(jnp.int32, (t, t), 0)
        col = lax.broadcasted_iota(jnp.int32, (t, t), 1)
        tri_ref[...] = jnp.where(row > col, 1.0, 0.0).astype(BF16)

    @pl.when(g == 0)
    def _():
        comb = comb_ref[...]
        lane = lax.broadcasted_iota(jnp.int32, comb.shape, 1)
        onehot_c = jnp.where(lane == comb[:, GID_LANE:GID_LANE + 1].astype(jnp.int32), 1.0, 0.0)
        rank_c = _dot(tri_ref[...], onehot_c.astype(BF16))
        counts = jnp.sum(onehot_c, axis=0, keepdims=True)
        lane1 = lax.broadcasted_iota(jnp.int32, (1, LANES), 1)
        start_c = jnp.zeros((1, LANES), F32)
        first = jnp.int32(0)
        for grp in range(N_GROUPS):
            cnt = jnp.sum(jnp.where(lane1 == grp, counts, 0.0)).astype(jnp.int32)
            nblk = lax.shift_right_logical(cnt + (rb - 1), int(np.log2(rb)))
            tab_ref[k, grp] = first
            tab_ref[k, N_GROUPS + grp] = nblk
            start_c = jnp.where(lane1 == grp, (first * rb).astype(F32), start_c)
            first = first + nblk
        dest = onehot_c * (rank_c + start_c)
        posc_ref[k] = jnp.sum(dest, axis=1, keepdims=True).astype(jnp.int32)
        pos_r = jnp.sum(dest.T, axis=0, keepdims=True).astype(jnp.int32)
        c_parts = _split3(jnp.where(lane < N_EXPERTS, comb, 0.0))
        c_pack = (c_parts[0].astype(F32) + pltpu.roll(c_parts[1].astype(F32), N_EXPERTS, axis=1)
                  + pltpu.roll(c_parts[2].astype(F32), 2 * N_EXPERTS, axis=1)).astype(BF16)
        hm = hm_ref[...]
        for r0 in range(0, t_pad, SORT_ROWS):
            dst = lax.broadcasted_iota(jnp.int32, (SORT_ROWS, t), 0) + r0
            perm = jnp.where(dst == pos_r, 1.0, 0.0).astype(BF16)
            hs_ref[k, r0:r0 + SORT_ROWS, :] = _dot(perm, hm).astype(hs_ref.dtype)
            cp = _dot(perm, c_pack)
            cs_ref[k, r0:r0 + SORT_ROWS, :] = (cp + pltpu.roll(cp, LANES - N_EXPERTS, axis=1)
                                               + pltpu.roll(cp, LANES - 2 * N_EXPERTS, axis=1))

    b0 = tab_ref[k, g]

    def expert_block(b, _):
        off = pl.multiple_of(b * rb, rb)
        rows = hs_ref[k, pl.ds(off, rb), :]
        cblk = cs_ref[k, pl.ds(off, rb), :]
        lane = lax.broadcasted_iota(jnp.int32, cblk.shape, 1)
        gates = [_dot(rows, wg_ref[j]) for j in range(EXPERTS_PER_GROUP)]
        ups = [_dot(rows, wu_ref[j]) for j in range(EXPERTS_PER_GROUP)]
        y = None
        for j in range(EXPERTS_PER_GROUP):
            c = jnp.sum(jnp.where(lane == g * EXPERTS_PER_GROUP + j, cblk, 0.0), axis=1, keepdims=True)
            a = gates[j] * jax.nn.sigmoid(gates[j]) * ups[j] * c
            d = _dot(a.astype(BF16), wd_ref[j])
            y = d if y is None else y + d
        hs_ref[k, pl.ds(off, rb), :] = y.astype(hs_ref.dtype)
        return 0

    last_group = g == pl.num_programs(1) - 1
    tile = pl.program_id(0) * pl.num_programs(2) + k
    n_tiles = pl.num_programs(0) * pl.num_programs(2)

    def x1_copy():
        return pltpu.make_async_copy(x1_hbm.at[pl.ds(pl.multiple_of(tile * t, t), t), :], xbuf_ref, sem_ref.at[0])

    def out_copy(which):
        return pltpu.make_async_copy(obuf_ref, o_hbm.at[pl.ds(pl.multiple_of(which * t, t), t), :], sem_ref.at[1])

    @pl.when(last_group)
    def _():
        x1_copy().start()

    lax.fori_loop(b0, b0 + tab_ref[k, N_GROUPS + g], expert_block, 0)

    @pl.when(last_group)
    def _():
        x1_copy().wait()

        @pl.when(tile > 0)
        def _():
            out_copy(tile - 1).wait()

        ys = hs_ref[k]
        for r0 in range(0, t, SORT_ROWS):
            src = lax.broadcasted_iota(jnp.int32, (SORT_ROWS, t_pad), 1)
            unperm = jnp.where(src == posc_ref[k, r0:r0 + SORT_ROWS, :], 1.0, 0.0).astype(BF16)
            x2 = xbuf_ref[r0:r0 + SORT_ROWS, :] + _dot(unperm, ys)
            obuf_ref[r0:r0 + SORT_ROWS, :] = _rms(x2, gf_ref[...]) if apply_final else x2
        out_copy(tile).start()

        @pl.when(tile == n_tiles - 1)
        def _():
            out_copy(tile).wait()


def _moe(hm, comb, x1, wg, wu, wd, gfin, *, tm, apply_final):
    n, d = x1.shape
    ne, _, de = wg.shape
    epg = EXPERTS_PER_GROUP
    n_groups = ne // epg
    t_pad = tm + N_GROUPS * ROW_BLOCK
    assert t_pad % SORT_ROWS == 0 and tm % SORT_ROWS == 0 and n % (MOE_TILES * tm) == 0
    first_blk = lambda cols: pl.BlockSpec(
        (tm, cols), lambda p, g, k: (MOE_TILES * p + jnp.where(g == 0, k, MOE_TILES - 1), 0))
    w_blk = lambda *shape: pl.BlockSpec((epg,) + shape, lambda p, g, k: (g, 0, 0))
    hbm = pl.BlockSpec(memory_space=pl.ANY)
    return pl.pallas_call(
        functools.partial(_moe_kernel, apply_final=apply_final),
        grid=(n // (MOE_TILES * tm), n_groups, MOE_TILES),
        in_specs=[first_blk(d), first_blk(LANES), hbm, w_blk(d, de), w_blk(d, de), w_blk(de, d),
                  pl.BlockSpec((1, d), lambda p, g, k: (0, 0))],
        out_specs=hbm,
        out_shape=jax.ShapeDtypeStruct((n, d), F32),
        scratch_shapes=[pltpu.VMEM((tm, tm), BF16), pltpu.VMEM((MOE_TILES, t_pad, d), BF16),
                        pltpu.VMEM((MOE_TILES, t_pad, LANES), F32), pltpu.VMEM((MOE_TILES, tm, 1), jnp.int32),
                        pltpu.SMEM((MOE_TILES, 2 * N_GROUPS), jnp.int32),
                        pltpu.VMEM((tm, d), F32), pltpu.VMEM((tm, d), F32), pltpu.SemaphoreType.DMA((2,))],
        compiler_params=pltpu.CompilerParams(dimension_semantics=("arbitrary",) * 3,
                                             vmem_limit_bytes=VMEM_LIMIT),
        name="moe",
    )(hm, comb, x1, wg, wu, wd, gfin)


def kernel(x, norm_attn_g, w_in, b_forget, lambda_q1, lambda_k1, lambda_q2, lambda_k2, diff_norm_g, w_out,
           norm_ffn_g, router_group_w, router_group_b, router_expert_w, router_expert_b, w_gate, w_up, w_down,
           norm_final_g):
    b, s, d = x.shape
    depth = w_in.shape[0]
    n = b * s
    diff_w = N_DIFF_HEADS * 2 * HEAD_DIM
    fox_w = N_FOX_HEADS * HEAD_DIM

    inv_freq = 1.0 / (ROPE_THETA ** (np.arange(0, HEAD_DIM, 2, dtype=np.float64) / HEAD_DIM))
    ang = np.tile(np.arange(s, dtype=np.float64)[:, None] * inv_freq[None, :], (1, 2 * LANES // HEAD_DIM))
    first_half = (np.arange(LANES) % HEAD_DIM) < HEAD_DIM // 2
    cos2 = jnp.asarray(np.cos(ang), F32)
    sin2 = jnp.asarray(np.where(first_half, -np.sin(ang), np.sin(ang)), F32)

    x2 = x.reshape(n, d)
    for l in range(depth):
        lam_init = 0.8 - 0.6 * float(np.exp(-0.3 * l))
        offs = [int(o) for o in np.cumsum([0, diff_w, diff_w, diff_w, fox_w, fox_w, fox_w])]
        assert diff_w == fox_w
        col_offsets = ((offs[0], offs[1], offs[3], offs[4]), (offs[2], offs[5]))
        wf = jnp.pad(w_in[l][:, offs[6]:], ((0, 0), (0, LANES - N_FOX_HEADS)))
        wf_cat = jnp.concatenate(_split2(wf), axis=1)
        bf_pad = jnp.pad(b_forget[l], (0, LANES - N_FOX_HEADS)).reshape(1, LANES)

        qd, kd, vdt, qf, kf, vft, qb, kb, wg_bf, wu_bf, wd_bf = _in_proj(
            x2, norm_attn_g[l].reshape(1, d), cos2, sin2, w_in, col_offsets, diff_w, wf_cat, bf_pad,
            (w_gate, w_up, w_down), layer=l, batch=b, seq=s, tm=IN_PROJ_ROWS)

        to3 = lambda t: t.reshape(b, s, t.shape[-1])
        lam_params = jnp.stack([lambda_q1[l], lambda_k1[l], lambda_q2[l], lambda_k2[l]])
        od = _diff_attn(to3(qd), to3(kd), vdt, lam_params, diff_norm_g[l].reshape(-1, 1),
                        tq=ATTN_TILE, lam_init=lam_init)
        of = _fox_attn(to3(qf), to3(kf), vft, to3(qb), to3(kb), tq=ATTN_TILE)

        wo = w_out[l].astype(BF16).reshape(2, -1, d)
        wr = jnp.zeros((d, LANES), F32)
        wr = wr.at[:, :N_GROUPS].set(router_group_w[l]).at[:, 8:8 + N_EXPERTS].set(router_expert_w[l])
        wr_cat = jnp.concatenate(_split2(wr.T), axis=0)
        br = jnp.zeros((LANES,), F32)
        br = br.at[:N_GROUPS].set(router_group_b[l]).at[8:8 + N_EXPERTS].set(router_expert_b[l]).reshape(LANES, 1)
        x1, hm, comb = _out_proj(x2, od.reshape(n, -1), of.reshape(n, -1), wo, norm_ffn_g[l].reshape(1, d),
                                 wr_cat, br, tm=OUT_PROJ_ROWS)

        x2 = _moe(hm, comb, x1, wg_bf, wu_bf, wd_bf, norm_final_g.reshape(1, d), tm=MOE_ROWS, apply_final=(l == depth - 1))
    return x2.reshape(b, s, d)
```
